```python
import jax
import jax.numpy as jnp
from jax import lax
import numpy as np

D_MODEL = 2048
BATCH = 4
SEQ = 4096
DEPTH = 2

CTX_LEN = 256
GRID_W = 64
W_BRANCH = D_MODEL // 2
N_BRANCH = 3
NA_HEADS = 16
NA_HEAD_DIM = W_BRANCH // NA_HEADS
NA_WIN_H = 8
NA_WIN_W = 16
POOL_WINDOWS = (2, 4, 8, 16)
POOL_GROUPS = len(POOL_WINDOWS)
POOL_GROUP_DIM = W_BRANCH // POOL_GROUPS
RWKV_HEAD_DIM = 64
RWKV_HEADS = W_BRANCH // RWKV_HEAD_DIM
RWKV_LORA = 64
RMS_EPS = 1e-6
LNX_EPS = 64e-5
NEG_INF = -1e30
IN_SIZES = (W_BRANCH,) * 10 + (RWKV_LORA, RWKV_LORA, N_BRANCH * D_MODEL)
IN_SPLIT_POINTS = tuple(int(s) for s in np.cumsum(IN_SIZES)[:-1])
D_IN = int(sum(IN_SIZES))

kernel_name = "hybrid_natten_pool_rwkv7_prefix_block"


def rms_norm(x, g):
    xf = x.astype(jnp.float32)
    y = xf * lax.rsqrt(jnp.mean(xf * xf, axis=-1, keepdims=True) + RMS_EPS)
    return (y * g).astype(x.dtype)


def adaln_modulation(cond, w_mod, b_mod):
    m = jax.nn.silu(cond) @ w_mod + b_mod
    return jnp.split(m, 3, axis=-1)


def to_heads(z, n_heads):
    return z.reshape(z.shape[:-1] + (n_heads, z.shape[-1] // n_heads))


def neighbourhood_attention(q, k, v, k_ctx, v_ctx, rpb):
    B, T, H, Dh = q.shape
    rows = T // GRID_W
    win_h = min(NA_WIN_H, rows)
    scale = Dh ** -0.5
    qg = q.reshape(B, rows, GRID_W, H, Dh)
    kg = k.reshape(B, rows, GRID_W, H, Dh)
    vg = v.reshape(B, rows, GRID_W, H, Dh)
    col = jnp.arange(GRID_W)
    col_start = jnp.clip(col - NA_WIN_W // 2, 0, GRID_W - NA_WIN_W)
    col_mask = (col[None, :] >= col_start[:, None]) & (col[None, :] < col_start[:, None] + NA_WIN_W)
    col_off = jnp.clip(col[None, :] - col[:, None] + NA_WIN_W - 1, 0, 2 * NA_WIN_W - 2)
    n_loc = win_h * GRID_W

    def one_row(r):
        r0 = jnp.clip(r - win_h // 2, 0, rows - win_h)
        kb = lax.dynamic_slice_in_dim(kg, r0, win_h, axis=1)
        vb = lax.dynamic_slice_in_dim(vg, r0, win_h, axis=1)
        qr = lax.dynamic_index_in_dim(qg, r, axis=1, keepdims=False)
        row_off = r0 + jnp.arange(win_h) - r + NA_WIN_H - 1
        bias = jnp.take(jnp.take(rpb, row_off, axis=1), col_off, axis=2)
        bias = bias.transpose(0, 2, 1, 3).astype(jnp.float32)
        s_loc = jnp.einsum('bqhd,bkwhd->bhqkw', qr, kb, preferred_element_type=jnp.float32) * scale + bias[None]
        s_loc = jnp.where(col_mask[None, None, :, None, :], s_loc, NEG_INF)
        s_ctx = jnp.einsum('bqhd,bchd->bhqc', qr, k_ctx, preferred_element_type=jnp.float32) * scale
        s = jnp.concatenate([s_loc.reshape(B, H, GRID_W, n_loc), s_ctx], axis=-1)
        p = jax.nn.softmax(s, axis=-1).astype(v.dtype)
        p_loc = p[..., :n_loc].reshape(B, H, GRID_W, win_h, GRID_W)
        return (jnp.einsum('bhqkw,bkwhd->bqhd', p_loc, vb)
                + jnp.einsum('bhqc,bchd->bqhd', p[..., n_loc:], v_ctx))

    out = lax.map(one_row, jnp.arange(rows))
    return out.transpose(1, 0, 2, 3, 4).reshape(B, T, H * Dh)


def context_attention(q, k, v):
    s = jnp.einsum('bqhd,bkhd->bhqk', q, k, preferred_element_type=jnp.float32) * q.shape[-1] ** -0.5
    p = jax.nn.softmax(s, axis=-1).astype(v.dtype)
    o = jnp.einsum('bhqk,bkhd->bqhd', p, v)
    return o.reshape(o.shape[:2] + (-1,))


def multiscale_pool(u, pool_w, pool_scale):
    B, T, _ = u.shape
    uf = u.astype(jnp.float32).reshape(B, T, POOL_GROUPS, POOL_GROUP_DIM)
    csum = jnp.concatenate([jnp.zeros_like(uf[:, :1]), jnp.cumsum(uf, axis=1)], axis=1)
    t = jnp.arange(T)
    pooled = []
    for g, win in enumerate(POOL_WINDOWS):
        lo = jnp.maximum(t - win // 2, 0)
        hi = jnp.minimum(t + win // 2, T)
        cg = csum[:, :, g]
        pooled.append((cg[:, hi] - cg[:, lo]) / (hi - lo).astype(jnp.float32)[None, :, None])
    diff = (jnp.stack(pooled, axis=2) - uf).astype(u.dtype)
    y = jnp.einsum('btgc,gcd->btgd', diff, pool_w)
    return y.reshape(B, T, -1) * pool_scale


def centred_neighbour_mean(z):
    zp = jnp.pad(z, ((0, 0), (1, 1), (0, 0)))
    return 0.5 * (zp[:, :-2] + zp[:, 2:])


def rwkv_features(r0, k0, v0, lw, la, mu, w0, w2, a0, a2, k_k, k_a):
    f32 = jnp.float32
    r, k, v = (z + m * (centred_neighbour_mean(z) - z)
               for z, m in ((r0.astype(f32), mu[0]), (k0.astype(f32), mu[1]), (v0.astype(f32), mu[2])))
    kk = to_heads(k * k_k, RWKV_HEADS)
    kk = kk / jnp.maximum(jnp.sqrt(jnp.sum(kk * kk, axis=-1, keepdims=True)), 1e-12)
    lw_t = jnp.tanh(lw.astype(f32))
    la_f = la.astype(f32)
    dirs = []
    for d in range(2):
        w_log = -jax.nn.softplus(-(w0[d] + lw_t @ w2[d])) - 0.5
        decay = jnp.exp(-jnp.exp(w_log))
        a = jax.nn.sigmoid(a0[d] + la_f @ a2[d])
        k_d = k * (1 + (a - 1) * k_a)
        dirs.append((to_heads(decay, RWKV_HEADS), to_heads(k_d, RWKV_HEADS), -kk, kk * to_heads(a, RWKV_HEADS)))
    return to_heads(r, RWKV_HEADS), to_heads(v, RWKV_HEADS), dirs


def wkv_scan(r, v, decay, k, a_in, b_in, s0, reverse):
    def step(S, inp):
        rt, vt, wt, kt, at, bt = inp
        sa = jnp.einsum('bhij,bhj->bhi', S, at)
        S = S * wt[:, :, None, :] + sa[..., None] * bt[:, :, None, :] + vt[..., None] * kt[:, :, None, :]
        return S, jnp.einsum('bhij,bhj->bhi', S, rt)
    xs = tuple(jnp.moveaxis(z, 1, 0) for z in (r, v, decay, k, a_in, b_in))
    s_end, ys = lax.scan(step, s0, xs, reverse=reverse)
    return jnp.moveaxis(ys, 0, 1), s_end


def rwkv_readout(r, v, ys, dirs, r_k, g, b):
    y = ys[0] + ys[1]
    mu = jnp.mean(y, axis=-1, keepdims=True)
    var = jnp.mean(jnp.square(y - mu), axis=-1, keepdims=True)
    y = (y - mu) * lax.rsqrt(var + LNX_EPS) * to_heads(g, RWKV_HEADS) + to_heads(b, RWKV_HEADS)
    bonus = (jnp.sum(r * dirs[0][1] * r_k, axis=-1, keepdims=True)
             + jnp.sum(r * dirs[1][1] * r_k, axis=-1, keepdims=True)) * v
    out = y + bonus
    return out.reshape(out.shape[:2] + (-1,))


def hybrid_layer(x_lat, x_ctx, c, c_ctx, norm_g, w_mod, b_mod, w_in, na_rpb, pool_w, pool_scale,
                 rw_mu, rw_w0, rw_w2, rw_a0, rw_a2, rw_k_k, rw_k_a, rw_r_k, rw_lnx_g, rw_lnx_b,
                 w_branch, w_out, compute_ctx):
    n_ctx = x_ctx.shape[1]
    shift_l, scale_l, gate_l = adaln_modulation(c, w_mod, b_mod)
    shift_c, scale_c, gate_c = adaln_modulation(c_ctx, w_mod, b_mod)
    h_lat = rms_norm(x_lat, norm_g) * (1 + scale_l[:, None]) + shift_l[:, None]
    h_ctx = rms_norm(x_ctx, norm_g) * (1 + scale_c) + shift_c
    h = jnp.concatenate([h_ctx, h_lat], axis=1)
    (na_q, na_k, na_v, na_gate, pool_u, pool_gate, rw_r, rw_k, rw_v, rw_gate,
     rw_lw, rw_la, merge_logits) = jnp.split(h @ w_in, IN_SPLIT_POINTS, axis=-1)

    def cpart(z):
        return z[:, :n_ctx]

    def lpart(z):
        return z[:, n_ctx:]

    keep = (lambda z: z) if compute_ctx else lpart

    q, k, v = (to_heads(z, NA_HEADS) for z in (na_q, na_k, na_v))
    y_na = neighbourhood_attention(lpart(q), lpart(k), lpart(v), cpart(k), cpart(v), na_rpb)
    if compute_ctx:
        y_na = jnp.concatenate([context_attention(cpart(q), cpart(k), cpart(v)), y_na], axis=1)

    y_pool = multiscale_pool(lpart(pool_u), pool_w, pool_scale)
    if compute_ctx:
        y_pool = jnp.concatenate([multiscale_pool(cpart(pool_u), pool_w, pool_scale), y_pool], axis=1)

    rw_params = (rw_mu, rw_w0, rw_w2, rw_a0, rw_a2, rw_k_k, rw_k_a)
    r_c, v_c, dirs_c = rwkv_features(*(cpart(z) for z in (rw_r, rw_k, rw_v, rw_lw, rw_la)), *rw_params)
    r_l, v_l, dirs_l = rwkv_features(*(lpart(z) for z in (rw_r, rw_k, rw_v, rw_lw, rw_la)), *rw_params)
    ys_c, ys_l = [], []
    for d in range(2):
        s0 = jnp.zeros((h.shape[0], RWKV_HEADS, RWKV_HEAD_DIM, RWKV_HEAD_DIM), jnp.float32)
        y_c, s_ctx_end = wkv_scan(r_c, v_c, *dirs_c[d], s0, reverse=(d == 1))
        y_l, _ = wkv_scan(r_l, v_l, *dirs_l[d], s_ctx_end, reverse=(d == 1))
        ys_c.append(y_c)
        ys_l.append(y_l)
    y_rw = rwkv_readout(r_l, v_l, ys_l, dirs_l, rw_r_k, rw_lnx_g, rw_lnx_b)
    if compute_ctx:
        y_rw = jnp.concatenate([rwkv_readout(r_c, v_c, ys_c, dirs_c, rw_r_k, rw_lnx_g, rw_lnx_b), y_rw], axis=1)

    dt = h.dtype
    b_na = y_na.astype(dt) * jax.nn.silu(keep(na_gate))
    b_pool = y_pool.astype(dt) * jax.nn.silu(keep(pool_gate))
    b_rw = y_rw.astype(dt) * jax.nn.silu(keep(rw_gate))
    g_na, g_pool, g_rw = jnp.split(jax.nn.sigmoid(keep(merge_logits)), N_BRANCH, axis=-1)
    merged = g_na * (b_na @ w_branch[0]) + g_pool * (b_pool @ w_branch[1]) + g_rw * (b_rw @ w_branch[2])
    out = merged @ w_out
    if compute_ctx:
        return x_lat + gate_l[:, None] * lpart(out), x_ctx + gate_c * cpart(out)
    return x_lat + gate_l[:, None] * out, None


def setup_inputs(seed: int = 0) -> dict:
    key = jax.random.key(seed)
    ks = jax.random.split(key, 24)
    f32 = jnp.float32
    D, L, W, R = D_MODEL, DEPTH, W_BRANCH, RWKV_LORA

    def nrm(k, shape, std):
        return jax.random.normal(k, shape, f32) * std

    return {
        "x": nrm(ks[0], (BATCH, SEQ, D), 1.0),
        "c": nrm(ks[1], (BATCH, D), 1.0),
        "ctx": nrm(ks[2], (BATCH, CTX_LEN, D), 1.0),
        "c_ctx": nrm(ks[3], (D,), 1.0),
        "norm_g": 1.0 + nrm(ks[4], (L, D), 0.05),
        "w_mod": nrm(ks[5], (L, D, 3 * D), 0.5 * D ** -0.5),
        "b_mod": nrm(ks[6], (L, 3 * D), 0.02),
        "w_in": nrm(ks[7], (L, D, D_IN), D ** -0.5),
        "na_rpb": nrm(ks[8], (L, NA_HEADS, 2 * NA_WIN_H - 1, 2 * NA_WIN_W - 1), 0.1),
        "pool_w": nrm(ks[9], (L, POOL_GROUPS, POOL_GROUP_DIM, POOL_GROUP_DIM), POOL_GROUP_DIM ** -0.5),
        "pool_scale": 1.0 + nrm(ks[10], (L, W), 0.1),
        "rw_mu": jax.random.uniform(ks[11], (L, 3, W), f32),
        "rw_w0": jax.random.uniform(ks[12], (L, 2, W), f32, -6.0, -1.0),
        "rw_w2": nrm(ks[13], (L, 2, R, W), 0.5 * R ** -0.5),
        "rw_a0": nrm(ks[14], (L, 2, W), 0.5),
        "rw_a2": nrm(ks[15], (L, 2, R, W), 0.5 * R ** -0.5),
        "rw_k_k": 0.85 + nrm(ks[16], (L, W), 0.05),
        "rw_k_a": 1.0 + nrm(ks[17], (L, W), 0.05),
        "rw_r_k": nrm(ks[18], (L, RWKV_HEADS, RWKV_HEAD_DIM), 0.1),
        "rw_lnx_g": 1.0 + nrm(ks[19], (L, W), 0.05),
        "rw_lnx_b": nrm(ks[20], (L, W), 0.02),
        "w_branch": nrm(ks[21], (L, N_BRANCH, W, D), W ** -0.5),
        "w_out": nrm(ks[22], (L, D, D), D ** -0.5),
        "final_g": 1.0 + nrm(ks[23], (D,), 0.05),
    }


def reference(x, c, ctx, c_ctx, norm_g, w_mod, b_mod, w_in, na_rpb, pool_w, pool_scale,
              rw_mu, rw_w0, rw_w2, rw_a0, rw_a2, rw_k_k, rw_k_a, rw_r_k, rw_lnx_g, rw_lnx_b,
              w_branch, w_out, final_g):
    x_lat, x_ctx = x, ctx
    for layer in range(DEPTH):
        x_lat, x_ctx = hybrid_layer(
            x_lat, x_ctx, c, c_ctx, norm_g[layer], w_mod[layer], b_mod[layer], w_in[layer],
            na_rpb[layer], pool_w[layer], pool_scale[layer], rw_mu[layer], rw_w0[layer], rw_w2[layer],
            rw_a0[layer], rw_a2[layer], rw_k_k[layer], rw_k_a[layer], rw_r_k[layer], rw_lnx_g[layer],
            rw_lnx_b[layer], w_branch[layer], w_out[layer], compute_ctx=(layer < DEPTH - 1))
    return rms_norm(x_lat, final_g)
```

```python
import functools

import numpy as np
import jax
import jax.numpy as jnp
from jax import lax
from jax.experimental import pallas as pl
from jax.experimental.pallas import tpu as pltpu

F32 = jnp.float32
BF16 = jnp.bfloat16

D_MODEL = 2048
W_BRANCH = D_MODEL // 2
N_BRANCH = 3
N_HEADS = 16
HEAD_DIM = 64
GRID_W = 64
NA_WIN_H = 8
NA_WIN_W = 16
POOL_WINDOWS = (2, 4, 8, 16)
POOL_GROUP_DIM = W_BRANCH // len(POOL_WINDOWS)
POOL_HALO = 8
RWKV_LORA = 64
RMS_EPS = 1e-6
LNX_EPS = 64e-5
NEG_INF = -1e30

LANES = 128
ROW_TILE = 256
NA_Q_ROWS = ROW_TILE // GRID_W
NA_K_ROWS = NA_Q_ROWS + NA_WIN_H
NA_K_TOK = NA_K_ROWS * GRID_W
CTX_MOD_ROW = 4
MOD_ROWS = 8
SCAN_STEPS = 16
VMEM_LIMIT = 56 << 20

COL_Q, COL_K, COL_V, COL_NA_GATE, COL_POOL_U, COL_POOL_GATE, COL_RW_R, COL_RW_K, COL_RW_V, COL_RW_GATE = range(10)
N_MAIN = 10 * W_BRANCH

P_MU_R, P_MU_K, P_MU_V, P_W0_F, P_W0_B, P_A0_F, P_A0_B, P_K_K, P_K_A, P_R_K = range(10)
P_ROWS = 16


def _params(*sem):
    return pltpu.CompilerParams(dimension_semantics=sem, vmem_limit_bytes=VMEM_LIMIT)


def _sigmoid(x):
    return 1.0 / (1.0 + jnp.exp(-x))


def _silu(x):
    return x * _sigmoid(x)


def _split3(x):
    hi = x.astype(BF16)
    r1 = x - hi.astype(F32)
    mid = r1.astype(BF16)
    lo = (r1 - mid.astype(F32)).astype(BF16)
    return hi, mid, lo


def _head_sum(x, ones):
    cols = []
    for c in range(x.shape[1] // LANES):
        hi, mid, lo = _split3(x[:, c * LANES:(c + 1) * LANES])
        cols.append(jnp.dot(hi, ones, preferred_element_type=F32)
                    + jnp.dot(mid, ones, preferred_element_type=F32)
                    + jnp.dot(lo, ones, preferred_element_type=F32))
    return jnp.concatenate(cols, axis=1)


def _mod_kernel(cond_ref, w_ref, b_ref, o_ref):
    s = _silu(cond_ref[...])
    o_ref[0] = jnp.dot(s.astype(BF16), w_ref[0], preferred_element_type=F32) + b_ref[0]


def _modulation(cond, w_mod, b_mod):
    n_layers = w_mod.shape[0]
    tn = 3 * D_MODEL // 4
    return pl.pallas_call(
        _mod_kernel,
        grid=(n_layers, 4),
        in_specs=[pl.BlockSpec((MOD_ROWS, D_MODEL), lambda l, j: (0, 0)),
                  pl.BlockSpec((1, D_MODEL, tn), lambda l, j: (l, 0, j)),
                  pl.BlockSpec((1, 1, tn), lambda l, j: (l, 0, j))],
        out_specs=pl.BlockSpec((1, MOD_ROWS, tn), lambda l, j: (l, 0, j)),
        out_shape=jax.ShapeDtypeStruct((n_layers, MOD_ROWS, 3 * D_MODEL), F32),
        compiler_params=_params("arbitrary", "arbitrary"),
        name="adaln_modulation",
    )(cond, w_mod, b_mod)


def _mod_row(mod_ref, tile, n_ctx_tiles):
    row = jnp.where(tile < n_ctx_tiles, CTX_MOD_ROW, pl.program_id(0))
    return mod_ref[pl.ds(row, 1), :]


def _rms(x, g):
    return x * lax.rsqrt(jnp.mean(x * x, axis=-1, keepdims=True) + RMS_EPS) * g


def _norm_mod_kernel(x_ref, g_ref, mod_ref, h_ref, *, n_ctx_tiles):
    m = _mod_row(mod_ref, pl.program_id(1), n_ctx_tiles)
    shift = m[:, :D_MODEL]
    scale = m[:, D_MODEL:2 * D_MODEL]
    h_ref[0] = (_rms(x_ref[0], g_ref[...]) * (1.0 + scale) + shift).astype(BF16)


def _norm_mod(x_all, norm_g, mod, n_ctx):
    B, R, _ = x_all.shape
    return pl.pallas_call(
        functools.partial(_norm_mod_kernel, n_ctx_tiles=n_ctx // ROW_TILE),
        grid=(B, R // ROW_TILE),
        in_specs=[pl.BlockSpec((1, ROW_TILE, D_MODEL), lambda b, i: (b, i, 0)),
                  pl.BlockSpec((1, D_MODEL), lambda b, i: (0, 0)),
                  pl.BlockSpec((MOD_ROWS, 3 * D_MODEL), lambda b, i: (0, 0))],
        out_specs=pl.BlockSpec((1, ROW_TILE, D_MODEL), lambda b, i: (b, i, 0)),
        out_shape=jax.ShapeDtypeStruct((B, R, D_MODEL), BF16),
        compiler_params=_params("parallel", "parallel"),
        name="norm_modulate",
    )(x_all, norm_g, mod)


def _mm_kernel(a_ref, w_ref, o_ref):
    o_ref[...] = jnp.dot(a_ref[...], w_ref[...], preferred_element_type=F32).astype(o_ref.dtype)


def _row_tile(m):
    for t in (1024, 512, 256):
        if m % t == 0:
            return t
    raise ValueError(f"row count {m} is not a multiple of {ROW_TILE}")


def _matmul(a, w, out_dtype, name):
    M, K = a.shape
    N = w.shape[1]
    tm = _row_tile(M)
    tn = min(N, 1024)
    return pl.pallas_call(
        _mm_kernel,
        grid=(N // tn, M // tm),
        in_specs=[pl.BlockSpec((tm, K), lambda j, i: (i, 0)),
                  pl.BlockSpec((K, tn), lambda j, i: (0, j))],
        out_specs=pl.BlockSpec((tm, tn), lambda j, i: (i, j)),
        out_shape=jax.ShapeDtypeStruct((M, N), out_dtype),
        compiler_params=_params("parallel", "parallel"),
        name=name,
    )(a, w)


def _na_bias_tables(rpb, rows):
    n_blocks = rows // NA_Q_ROWS
    qi = np.arange(NA_Q_ROWS * GRID_W)
    ki = np.arange(NA_K_TOK)
    qr, qc = qi // GRID_W, qi % GRID_W
    kr, kc = ki // GRID_W, ki % GRID_W
    tables = []
    for m in (0, 1, n_blocks - 1):
        q_row = NA_Q_ROWS * m + qr
        k_row = int(np.clip(NA_Q_ROWS * m - NA_Q_ROWS, 0, rows - NA_K_ROWS)) + kr
        r0 = np.clip(q_row - NA_WIN_H // 2, 0, rows - NA_WIN_H)
        c0 = np.clip(qc - NA_WIN_W // 2, 0, GRID_W - NA_WIN_W)
        valid = ((k_row[None, :] >= r0[:, None]) & (k_row[None, :] < r0[:, None] + NA_WIN_H)
                 & (kc[None, :] >= c0[:, None]) & (kc[None, :] < c0[:, None] + NA_WIN_W))
        row_off = np.clip(k_row[None, :] - q_row[:, None] + NA_WIN_H - 1, 0, 2 * NA_WIN_H - 2)
        col_off = np.clip(kc[None, :] - qc[:, None] + NA_WIN_W - 1, 0, 2 * NA_WIN_W - 2)
        bias = rpb[:, row_off, col_off].astype(F32)
        tables.append(jnp.where(valid[None], bias, NEG_INF))
    return jnp.stack(tables)


def _attend(qe, keys, vals, biases):
    dn = (((1,), (1,)), ((), ()))
    scores = []
    for kk, bias in zip(keys, biases):
        s = lax.dot_general(qe, kk, dn, preferred_element_type=F32)
        scores.append(s if bias is None else s + bias)
    m = scores[0].max(axis=-1, keepdims=True)
    for s in scores[1:]:
        m = jnp.maximum(m, s.max(axis=-1, keepdims=True))
    num, den = None, None
    for s, vv in zip(scores, vals):
        p = jnp.exp(s - m)
        l = p.sum(axis=-1, keepdims=True)
        o = jnp.dot(p.astype(BF16), vv, preferred_element_type=F32)
        num = o if num is None else num + o
        den = l if den is None else den + l
    return num / den


def _na_kernel(q_ref, k_ref, v_ref, g_ref, bias_ref, o_ref, *, n_ctx, rows):
    j = pl.program_id(2)
    lane = lax.broadcasted_iota(jnp.int32, (1, LANES), 1)
    in_head = (lane < HEAD_DIM, lane >= HEAD_DIM)
    q = q_ref[0] * (HEAD_DIM ** -0.5)
    kc = k_ref[0, 0:n_ctx, :].astype(BF16)
    vc = v_ref[0, 0:n_ctx, :].astype(BF16)

    def heads(q):
        return [jnp.where(in_head[e], q, 0.0).astype(BF16) for e in range(2)]

    def finish(o0, o1):
        o = jnp.where(in_head[0], o0, o1)
        o_ref[0] = (o * _silu(g_ref[0])).astype(o_ref.dtype)

    @pl.when(j == 0)
    def _():
        finish(*[_attend(qe, [kc], [vc], [None]) for qe in heads(q)])

    @pl.when(j > 0)
    def _():
        k_row = jnp.clip(NA_Q_ROWS * (j - 1) - NA_Q_ROWS, 0, rows - NA_K_ROWS)
        start = pl.multiple_of(n_ctx + k_row * GRID_W, GRID_W)
        kw = k_ref[0, pl.ds(start, NA_K_TOK), :].astype(BF16)
        vw = v_ref[0, pl.ds(start, NA_K_TOK), :].astype(BF16)
        finish(*[_attend(qe, [kw, kc], [vw, vc], [bias_ref[0, e], None])
                 for e, qe in enumerate(heads(q))])


def _na_attention(p_main, bias_tables, n_ctx):
    B, R, _ = p_main.shape
    rows = (R - n_ctx) // GRID_W
    n_blocks = rows // NA_Q_ROWS
    pairs = W_BRANCH // LANES
    assert n_ctx == ROW_TILE and rows >= NA_K_ROWS and rows % NA_Q_ROWS == 0

    def col(c):
        return lambda b, hp, j: (b, 0, c * pairs + hp)

    def bias_idx(b, hp, j):
        return (jnp.where(j <= 1, 0, jnp.where(j == n_blocks, 2, 1)), hp, 0, 0)

    return pl.pallas_call(
        functools.partial(_na_kernel, n_ctx=n_ctx, rows=rows),
        grid=(B, pairs, n_blocks + 1),
        in_specs=[pl.BlockSpec((1, ROW_TILE, LANES), lambda b, hp, j: (b, j, COL_Q * pairs + hp)),
                  pl.BlockSpec((1, R, LANES), col(COL_K)),
                  pl.BlockSpec((1, R, LANES), col(COL_V)),
                  pl.BlockSpec((1, ROW_TILE, LANES), lambda b, hp, j: (b, j, COL_NA_GATE * pairs + hp)),
                  pl.BlockSpec((1, 2, ROW_TILE, NA_K_TOK), bias_idx)],
        out_specs=pl.BlockSpec((1, ROW_TILE, LANES), lambda b, hp, j: (b, j, hp)),
        out_shape=jax.ShapeDtypeStruct((B, R, W_BRANCH), BF16),
        compiler_params=_params("parallel", "parallel", "arbitrary"),
        name="neighbourhood_attention",
    )(p_main, p_main, p_main, p_main, bias_tables)


def _pool_kernel(u_ref, g_ref, w_ref, sc_ref, o_ref, pad_ref, *, n_ctx, n_lat):
    grp = pl.program_id(1)
    w = w_ref[0]
    scale = sc_ref[...]

    def run(win):
        half = win // 2
        for seq_start, seq_len in ((0, n_ctx), (n_ctx, n_lat)):
            zeros = jnp.zeros((POOL_HALO, POOL_GROUP_DIM), F32)
            pad_ref[0:POOL_HALO, :] = zeros
            pad_ref[POOL_HALO:POOL_HALO + seq_len, :] = u_ref[0, seq_start:seq_start + seq_len, :]
            pad_ref[POOL_HALO + seq_len:2 * POOL_HALO + seq_len, :] = zeros

            def chunk(c, carry):
                base = pl.multiple_of(c * ROW_TILE, ROW_TILE)
                x = pad_ref[pl.ds(base, ROW_TILE + 2 * POOL_HALO), :]
                acc = x[POOL_HALO - half:POOL_HALO - half + ROW_TILE]
                for o in range(-half + 1, half):
                    acc = acc + x[POOL_HALO + o:POOL_HALO + o + ROW_TILE]
                t = base + lax.broadcasted_iota(jnp.int32, (ROW_TILE, 1), 0)
                cnt = jnp.minimum(t + half, seq_len) - jnp.maximum(t - half, 0)
                diff = acc / cnt.astype(F32) - x[POOL_HALO:POOL_HALO + ROW_TILE]
                y = jnp.dot(diff.astype(BF16), w, preferred_element_type=F32) * scale
                rows = pl.ds(seq_start + base, ROW_TILE)
                o_ref[0, rows, :] = (y * _silu(g_ref[0, rows, :])).astype(o_ref.dtype)
                return carry

            lax.fori_loop(0, seq_len // ROW_TILE, chunk, 0)

    for gi, win in enumerate(POOL_WINDOWS):
        pl.when(grp == gi)(functools.partial(run, win))


def _pool(p_main, pool_w, pool_scale, n_ctx):
    B, R, _ = p_main.shape
    groups = len(POOL_WINDOWS)
    return pl.pallas_call(
        functools.partial(_pool_kernel, n_ctx=n_ctx, n_lat=R - n_ctx),
        grid=(B, groups),
        in_specs=[pl.BlockSpec((1, R, POOL_GROUP_DIM), lambda b, g: (b, 0, COL_POOL_U * groups + g)),
                  pl.BlockSpec((1, R, POOL_GROUP_DIM), lambda b, g: (b, 0, COL_POOL_GATE * groups + g)),
                  pl.BlockSpec((1, POOL_GROUP_DIM, POOL_GROUP_DIM), lambda b, g: (g, 0, 0)),
                  pl.BlockSpec((1, POOL_GROUP_DIM), lambda b, g: (0, g))],
        out_specs=pl.BlockSpec((1, R, POOL_GROUP_DIM), lambda b, g: (b, 0, g)),
        out_shape=jax.ShapeDtypeStruct((B, R, W_BRANCH), BF16),
        scratch_shapes=[pltpu.VMEM((R - n_ctx + 2 * POOL_HALO, POOL_GROUP_DIM), F32)],
        compiler_params=_params("parallel", "arbitrary"),
        name="multiscale_pool",
    )(p_main, p_main, pool_w, pool_scale)


def _rwkv_feat_kernel(r_ref, rp_ref, rn_ref, k_ref, kp_ref, kn_ref, v_ref, vp_ref, vn_ref, lora_ref,
                      par_ref, w2_ref, a2_ref, ones_ref,
                      ro_ref, vo_ref, ao_ref, wf_ref, kf_ref, bf_ref, wb_ref, kb_ref, bb_ref, bonus_ref,
                      *, n_ctx_tiles, n_tiles):
    i = pl.program_id(1)
    first = (i == 0) | (i == n_ctx_tiles)
    last = (i == n_ctx_tiles - 1) | (i == n_tiles - 1)
    row = lax.broadcasted_iota(jnp.int32, (ROW_TILE, 1), 0)
    par = par_ref[...]
    ones = ones_ref[...]

    def prm(p):
        return par[p:p + 1, :]

    def mix(z_ref, prev_ref, next_ref, mu):
        z = z_ref[0]
        prev = jnp.where(first, 0.0, prev_ref[0, 7:8, :])
        nxt = jnp.where(last, 0.0, next_ref[0, 0:1, :])
        z_prev = jnp.where(row == 0, prev, pltpu.roll(z, 1, 0))
        z_next = jnp.where(row == ROW_TILE - 1, nxt, pltpu.roll(z, ROW_TILE - 1, 0))
        return z + mu * (0.5 * (z_prev + z_next) - z)

    r = mix(r_ref, rp_ref, rn_ref, prm(P_MU_R))
    k = mix(k_ref, kp_ref, kn_ref, prm(P_MU_K))
    v = mix(v_ref, vp_ref, vn_ref, prm(P_MU_V))
    ro_ref[0] = r
    vo_ref[0] = v

    kk = k * prm(P_K_K)
    kk = kk / jnp.maximum(jnp.sqrt(_head_sum(kk * kk, ones)), 1e-12)
    ao_ref[0] = -kk

    lora = lora_ref[0]
    lane = lax.broadcasted_iota(jnp.int32, (1, LANES), 1)
    lora = jnp.where(lane < RWKV_LORA, jnp.tanh(lora), lora).astype(BF16)
    k_sum = None
    for d, (w_out, k_out, b_out) in enumerate(((wf_ref, kf_ref, bf_ref), (wb_ref, kb_ref, bb_ref))):
        x = prm(P_W0_F + d) + jnp.dot(lora, w2_ref[d], preferred_element_type=F32)
        w_log = -(jnp.maximum(-x, 0.0) + jnp.log1p(jnp.exp(-jnp.abs(x)))) - 0.5
        w_out[0] = jnp.exp(-jnp.exp(w_log))
        a = _sigmoid(prm(P_A0_F + d) + jnp.dot(lora, a2_ref[d], preferred_element_type=F32))
        k_d = k * (1.0 + (a - 1.0) * prm(P_K_A))
        k_out[0] = k_d
        b_out[0] = kk * a
        k_sum = k_d if k_sum is None else k_sum + k_d
    bonus_ref[0] = _head_sum(r * k_sum * prm(P_R_K), ones) * v


def _rwkv_features(p_main, p_lora, par, w2, a2, ones, n_ctx):
    B, R, _ = p_main.shape
    n_tiles = R // ROW_TILE
    sub = ROW_TILE // 8

    def main(c):
        return pl.BlockSpec((1, ROW_TILE, W_BRANCH), lambda b, i: (b, i, c))

    def prev(c):
        return pl.BlockSpec((1, 8, W_BRANCH), lambda b, i: (b, jnp.maximum(i * sub - 1, 0), c))

    def nxt(c):
        return pl.BlockSpec((1, 8, W_BRANCH), lambda b, i: (b, jnp.minimum((i + 1) * sub, n_tiles * sub - 1), c))

    in_specs = []
    for c in (COL_RW_R, COL_RW_K, COL_RW_V):
        in_specs += [main(c), prev(c), nxt(c)]
    in_specs += [pl.BlockSpec((1, ROW_TILE, LANES), lambda b, i: (b, i, 0)),
                 pl.BlockSpec((P_ROWS, W_BRANCH), lambda b, i: (0, 0)),
                 pl.BlockSpec((2, LANES, W_BRANCH), lambda b, i: (0, 0, 0)),
                 pl.BlockSpec((2, LANES, W_BRANCH), lambda b, i: (0, 0, 0)),
                 pl.BlockSpec((LANES, LANES), lambda b, i: (0, 0))]
    out = jax.ShapeDtypeStruct((B, R, W_BRANCH), F32)
    return pl.pallas_call(
        functools.partial(_rwkv_feat_kernel, n_ctx_tiles=n_ctx // ROW_TILE, n_tiles=n_tiles),
        grid=(B, n_tiles),
        in_specs=in_specs,
        out_specs=[pl.BlockSpec((1, ROW_TILE, W_BRANCH), lambda b, i: (b, i, 0))] * 10,
        out_shape=[out] * 10,
        compiler_params=_params("parallel", "parallel"),
        name="rwkv_features",
    )(*([p_main] * 9), p_lora, par, w2, a2, ones)


def _scan_kernel(w_ref, k_ref, b_ref, a_ref, r_ref, v_ref, y_ref, s_ref):
    n = HEAD_DIM

    @pl.when(pl.program_id(0) == 0)
    def _():
        s_ref[...] = jnp.zeros_like(s_ref)

    def step(t, carry):
        def row(ref, j):
            return ref[t, pl.ds(j, 1), :]

        sa = [s_ref[j] * row(a_ref, j) for j in range(2)]
        for j in range(2, n):
            sa[j % 2] = sa[j % 2] + s_ref[j] * row(a_ref, j)
        sa = sa[0] + sa[1]
        vt = v_ref[t]
        y = [None, None]
        for j in range(n):
            sj = s_ref[j] * row(w_ref, j) + sa * row(b_ref, j) + vt * row(k_ref, j)
            s_ref[j] = sj
            yj = sj * row(r_ref, j)
            y[j % 2] = yj if y[j % 2] is None else y[j % 2] + yj
        y_ref[t] = y[0] + y[1]
        return carry

    lax.fori_loop(0, SCAN_STEPS, step, 0)


def _wkv_scan(w, k, b, a, r, v):
    R, n, chains = w.shape
    spec = pl.BlockSpec((SCAN_STEPS, n, chains), lambda s: (s, 0, 0))
    return pl.pallas_call(
        _scan_kernel,
        grid=(R // SCAN_STEPS,),
        in_specs=[spec] * 6,
        out_specs=spec,
        out_shape=jax.ShapeDtypeStruct((R, n, chains), F32),
        scratch_shapes=[pltpu.VMEM((n, n, chains), F32)],
        compiler_params=_params("arbitrary"),
        name="wkv_scan",
    )(w, k, b, a, r, v)


def _seq_flip(z, n_ctx):
    return jnp.concatenate([z[:, :n_ctx][:, ::-1], z[:, n_ctx:][:, ::-1]], axis=1)


def _to_scan(z_fwd, z_bwd, n_ctx):
    def tr(z):
        B, R, _ = z.shape
        return z.reshape(B, R, N_HEADS, HEAD_DIM).transpose(1, 3, 0, 2).reshape(R, HEAD_DIM, B * N_HEADS)
    return jnp.concatenate([tr(z_fwd), tr(_seq_flip(z_bwd, n_ctx))], axis=-1)


def _from_scan(y, n_batch, n_ctx):
    R = y.shape[0]
    half = n_batch * N_HEADS

    def tr(z):
        return z.reshape(R, HEAD_DIM, n_batch, N_HEADS).transpose(2, 0, 3, 1).reshape(n_batch, R, W_BRANCH)
    return tr(y[..., :half]), _seq_flip(tr(y[..., half:]), n_ctx)


def _rwkv_readout_kernel(yf_ref, yb_ref, bonus_ref, gate_ref, gb_ref, ones_ref, o_ref):
    ones = ones_ref[...]
    y = yf_ref[0] + yb_ref[0]
    mu = _head_sum(y, ones) * (1.0 / HEAD_DIM)
    yc = y - mu
    var = _head_sum(yc * yc, ones) * (1.0 / HEAD_DIM)
    gb = gb_ref[...]
    out = yc * lax.rsqrt(var + LNX_EPS) * gb[0:1, :] + gb[1:2, :] + bonus_ref[0]
    o_ref[0] = (out * _silu(gate_ref[0])).astype(o_ref.dtype)


def _rwkv_readout(y_fwd, y_bwd, bonus, p_main, lnx_gb, ones):
    B, R, _ = y_fwd.shape
    tile = pl.BlockSpec((1, ROW_TILE, W_BRANCH), lambda b, i: (b, i, 0))
    return pl.pallas_call(
        _rwkv_readout_kernel,
        grid=(B, R // ROW_TILE),
        in_specs=[tile, tile, tile,
                  pl.BlockSpec((1, ROW_TILE, W_BRANCH), lambda b, i: (b, i, COL_RW_GATE)),
                  pl.BlockSpec((8, W_BRANCH), lambda b, i: (0, 0)),
                  pl.BlockSpec((LANES, LANES), lambda b, i: (0, 0))],
        out_specs=tile,
        out_shape=jax.ShapeDtypeStruct((B, R, W_BRANCH), BF16),
        compiler_params=_params("parallel", "parallel"),
        name="rwkv_readout",
    )(y_fwd, y_bwd, bonus, p_main, lnx_gb, ones)


def _merge_kernel(na_ref, pool_ref, rw_ref, lna_ref, lpool_ref, lrw_ref, w_ref, o_ref):
    acc = None
    for br, (x_ref, l_ref) in enumerate(((na_ref, lna_ref), (pool_ref, lpool_ref), (rw_ref, lrw_ref))):
        t = _sigmoid(l_ref[...]) * jnp.dot(x_ref[...], w_ref[br], preferred_element_type=F32)
        acc = t if acc is None else acc + t
    o_ref[...] = acc.astype(o_ref.dtype)


def _merge(b_na, b_pool, b_rw, p_merge, w_branch):
    M = b_na.shape[0]
    tm, tn = min(_row_tile(M), 512), 1024
    nb = D_MODEL // tn
    x_spec = pl.BlockSpec((tm, W_BRANCH), lambda j, i: (i, 0))

    def logit(br):
        return pl.BlockSpec((tm, tn), lambda j, i: (i, br * nb + j))

    return pl.pallas_call(
        _merge_kernel,
        grid=(nb, M // tm),
        in_specs=[x_spec, x_spec, x_spec, logit(0), logit(1), logit(2),
                  pl.BlockSpec((N_BRANCH, W_BRANCH, tn), lambda j, i: (0, 0, j))],
        out_specs=pl.BlockSpec((tm, tn), lambda j, i: (i, j)),
        out_shape=jax.ShapeDtypeStruct((M, D_MODEL), BF16),
        compiler_params=_params("parallel", "parallel"),
        name="branch_merge",
    )(b_na, b_pool, b_rw, p_merge, p_merge, p_merge, w_branch)


def _out_kernel(m_ref, w_ref, x_ref, mod_ref, fg_ref, o_ref, *, n_ctx_tiles, tile_offset, final):
    gate = _mod_row(mod_ref, pl.program_id(1) + tile_offset, n_ctx_tiles)[:, 2 * D_MODEL:]
    x = x_ref[0] + gate * jnp.dot(m_ref[0], w_ref[...], preferred_element_type=F32)
    o_ref[0] = _rms(x, fg_ref[...]) if final else x


def _out_proj(merged, w_out, x_all, mod, final_g, n_ctx, final):
    B, R, _ = x_all.shape
    off = n_ctx // ROW_TILE if final else 0
    tile = pl.BlockSpec((1, ROW_TILE, D_MODEL), lambda b, i: (b, i + off, 0))
    return pl.pallas_call(
        functools.partial(_out_kernel, n_ctx_tiles=n_ctx // ROW_TILE, tile_offset=off, final=final),
        grid=(B, R // ROW_TILE - off),
        in_specs=[tile,
                  pl.BlockSpec((D_MODEL, D_MODEL), lambda b, i: (0, 0)),
                  tile,
                  pl.BlockSpec((MOD_ROWS, 3 * D_MODEL), lambda b, i: (0, 0)),
                  pl.BlockSpec((1, D_MODEL), lambda b, i: (0, 0))],
        out_specs=pl.BlockSpec((1, ROW_TILE, D_MODEL), lambda b, i: (b, i, 0)),
        out_shape=jax.ShapeDtypeStruct((B, R - off * ROW_TILE, D_MODEL), F32),
        compiler_params=_params("parallel", "parallel"),
        name="out_proj_final" if final else "out_proj",
    )(merged, w_out, x_all, mod, final_g)


def _head_ones():
    lane = np.arange(LANES)
    return jnp.asarray(lane[:, None] // HEAD_DIM == lane[None, :] // HEAD_DIM, BF16)


def _layer(x_all, mod, n_ctx, final, final_g, norm_g, w_in, na_rpb, pool_w, pool_scale, rw_mu, rw_w0, rw_w2,
           rw_a0, rw_a2, rw_k_k, rw_k_a, rw_r_k, rw_lnx_g, rw_lnx_b, w_branch, w_out):
    B, R, _ = x_all.shape
    rows = (R - n_ctx) // GRID_W
    ones = _head_ones()

    h = _norm_mod(x_all, norm_g[None], mod, n_ctx).reshape(B * R, D_MODEL)
    lo = N_MAIN + 2 * RWKV_LORA
    p_main = _matmul(h, w_in[:, :N_MAIN].astype(BF16), F32, "in_proj_main").reshape(B, R, N_MAIN)
    p_lora = _matmul(h, w_in[:, N_MAIN:lo].astype(BF16), F32, "in_proj_lora").reshape(B, R, 2 * RWKV_LORA)
    p_merge = _matmul(h, w_in[:, lo:].astype(BF16), F32, "in_proj_merge")

    b_na = _na_attention(p_main, _na_bias_tables(na_rpb, rows), n_ctx)
    b_pool = _pool(p_main, pool_w.astype(BF16), pool_scale[None], n_ctx)

    par = jnp.zeros((P_ROWS, W_BRANCH), F32)
    par = par.at[P_MU_R:P_MU_V + 1].set(rw_mu).at[P_W0_F:P_W0_B + 1].set(rw_w0).at[P_A0_F:P_A0_B + 1].set(rw_a0)
    par = par.at[P_K_K].set(rw_k_k).at[P_K_A].set(rw_k_a).at[P_R_K].set(rw_r_k.reshape(-1))
    zeros = jnp.zeros_like(rw_w2)
    w2 = jnp.concatenate([rw_w2, zeros], axis=1).astype(BF16)
    a2 = jnp.concatenate([zeros, rw_a2], axis=1).astype(BF16)
    r, v, a_in, w_f, k_f, b_f, w_b, k_b, b_b, bonus = _rwkv_features(p_main, p_lora, par, w2, a2, ones, n_ctx)
    y = _wkv_scan(_to_scan(w_f, w_b, n_ctx), _to_scan(k_f, k_b, n_ctx), _to_scan(b_f, b_b, n_ctx),
                  _to_scan(a_in, a_in, n_ctx), _to_scan(r, r, n_ctx), _to_scan(v, v, n_ctx))
    y_fwd, y_bwd = _from_scan(y, B, n_ctx)
    lnx_gb = jnp.zeros((8, W_BRANCH), F32).at[0].set(rw_lnx_g).at[1].set(rw_lnx_b)
    b_rw = _rwkv_readout(y_fwd, y_bwd, bonus, p_main, lnx_gb, ones)

    def flat(z):
        return z.reshape(B * R, W_BRANCH)

    merged = _merge(flat(b_na), flat(b_pool), flat(b_rw), p_merge, w_branch.astype(BF16))
    return _out_proj(merged.reshape(B, R, D_MODEL), w_out.astype(BF16), x_all, mod, final_g[None], n_ctx, final)


def kernel(x, c, ctx, c_ctx, norm_g, w_mod, b_mod, w_in, na_rpb, pool_w, pool_scale, rw_mu, rw_w0, rw_w2, rw_a0,
           rw_a2, rw_k_k, rw_k_a, rw_r_k, rw_lnx_g, rw_lnx_b, w_branch, w_out, final_g):
    B, T, _ = x.shape
    n_ctx = ctx.shape[1]
    depth = w_in.shape[0]
    assert B <= CTX_MOD_ROW and n_ctx % ROW_TILE == 0 and T % ROW_TILE == 0 and (n_ctx + T) % SCAN_STEPS == 0

    cond = jnp.zeros((MOD_ROWS, D_MODEL), F32).at[:B].set(c).at[CTX_MOD_ROW].set(c_ctx)
    mods = _modulation(cond, w_mod.astype(BF16), b_mod[:, None, :])
    x_all = jnp.concatenate([ctx, x], axis=1)
    for layer in range(depth):
        x_all = _layer(x_all, mods[layer], n_ctx, layer == depth - 1, final_g, norm_g[layer], w_in[layer],
                       na_rpb[layer], pool_w[layer], pool_scale[layer], rw_mu[layer], rw_w0[layer], rw_w2[layer],
                       rw_a0[layer], rw_a2[layer], rw_k_k[layer], rw_k_a[layer], rw_r_k[layer], rw_lnx_g[layer],
                       rw_lnx_b[layer], w_branch[layer], w_out[layer])
    return x_all
```

```python
import functools

import numpy as np
import jax
import jax.numpy as jnp
from jax import lax
from jax.experimental import pallas as pl
from jax.experimental.pallas import tpu as pltpu

F32 = jnp.float32
BF16 = jnp.bfloat16

D_MODEL = 2048
W_BRANCH = D_MODEL // 2
N_BRANCH = 3
N_HEADS = 16
HEAD_DIM = 64
GRID_W = 64
NA_WIN_H = 8
NA_WIN_W = 16
POOL_WINDOWS = (2, 4, 8, 16)
POOL_GROUP_DIM = W_BRANCH // len(POOL_WINDOWS)
POOL_HALO = 8
RWKV_LORA = 64
RMS_EPS = 1e-6
LNX_EPS = 64e-5
NEG_INF = -1e30

LANES = 128
ROW_TILE = 256
NA_Q_ROWS = ROW_TILE // GRID_W
NA_K_ROWS = NA_Q_ROWS + NA_WIN_H
NA_K_TOK = NA_K_ROWS * GRID_W
CTX_MOD_ROW = 4
MOD_ROWS = 8
SCAN_STEPS = 32
RELAYOUT_ROWS = 128
VMEM_LIMIT = 56 << 20

COL_Q, COL_K, COL_V, COL_NA_GATE, COL_POOL_U, COL_POOL_GATE, COL_RW_R, COL_RW_K, COL_RW_V, COL_RW_GATE = range(10)
N_MAIN = 10 * W_BRANCH

P_MU_R, P_MU_K, P_MU_V, P_W0_F, P_W0_B, P_A0_F, P_A0_B, P_K_K, P_K_A, P_R_K = range(10)
P_ROWS = 16


def _params(*sem):
    return pltpu.CompilerParams(dimension_semantics=sem, vmem_limit_bytes=VMEM_LIMIT)


def _sigmoid(x):
    return 1.0 / (1.0 + jnp.exp(-x))


def _silu(x):
    return x * _sigmoid(x)


def _split3(x):
    hi = x.astype(BF16)
    r1 = x - hi.astype(F32)
    mid = r1.astype(BF16)
    lo = (r1 - mid.astype(F32)).astype(BF16)
    return hi, mid, lo


def _head_sum(x, ones):
    cols = []
    for c in range(x.shape[1] // LANES):
        hi, mid, lo = _split3(x[:, c * LANES:(c + 1) * LANES])
        cols.append(jnp.dot(hi, ones, preferred_element_type=F32)
                    + jnp.dot(mid, ones, preferred_element_type=F32)
                    + jnp.dot(lo, ones, preferred_element_type=F32))
    return jnp.concatenate(cols, axis=1)


def _mod_kernel(cond_ref, w_ref, b_ref, o_ref):
    s = _silu(cond_ref[...])
    o_ref[0] = jnp.dot(s.astype(BF16), w_ref[0], preferred_element_type=F32) + b_ref[0]


def _modulation(cond, w_mod, b_mod):
    n_layers = w_mod.shape[0]
    tn = 3 * D_MODEL // 4
    return pl.pallas_call(
        _mod_kernel,
        grid=(n_layers, 4),
        in_specs=[pl.BlockSpec((MOD_ROWS, D_MODEL), lambda l, j: (0, 0)),
                  pl.BlockSpec((1, D_MODEL, tn), lambda l, j: (l, 0, j)),
                  pl.BlockSpec((1, 1, tn), lambda l, j: (l, 0, j))],
        out_specs=pl.BlockSpec((1, MOD_ROWS, tn), lambda l, j: (l, 0, j)),
        out_shape=jax.ShapeDtypeStruct((n_layers, MOD_ROWS, 3 * D_MODEL), F32),
        compiler_params=_params("arbitrary", "arbitrary"),
        name="adaln_modulation",
    )(cond, w_mod, b_mod)


def _mod_row(mod_ref, tile, n_ctx_tiles):
    row = jnp.where(tile < n_ctx_tiles, CTX_MOD_ROW, pl.program_id(0))
    return mod_ref[pl.ds(row, 1), :]


def _rms(x, g):
    return x * lax.rsqrt(jnp.mean(x * x, axis=-1, keepdims=True) + RMS_EPS) * g


def _norm_mod_kernel(x_ref, g_ref, mod_ref, h_ref, *, n_ctx_tiles):
    m = _mod_row(mod_ref, pl.program_id(1), n_ctx_tiles)
    shift = m[:, :D_MODEL]
    scale = m[:, D_MODEL:2 * D_MODEL]
    h_ref[0] = (_rms(x_ref[0], g_ref[...]) * (1.0 + scale) + shift).astype(BF16)


def _norm_mod(x_all, norm_g, mod, n_ctx):
    B, R, _ = x_all.shape
    return pl.pallas_call(
        functools.partial(_norm_mod_kernel, n_ctx_tiles=n_ctx // ROW_TILE),
        grid=(B, R // ROW_TILE),
        in_specs=[pl.BlockSpec((1, ROW_TILE, D_MODEL), lambda b, i: (b, i, 0)),
                  pl.BlockSpec((1, D_MODEL), lambda b, i: (0, 0)),
                  pl.BlockSpec((MOD_ROWS, 3 * D_MODEL), lambda b, i: (0, 0))],
        out_specs=pl.BlockSpec((1, ROW_TILE, D_MODEL), lambda b, i: (b, i, 0)),
        out_shape=jax.ShapeDtypeStruct((B, R, D_MODEL), BF16),
        compiler_params=_params("parallel", "parallel"),
        name="norm_modulate",
    )(x_all, norm_g, mod)


def _mm_kernel(a_ref, w_ref, o_ref):
    o_ref[...] = jnp.dot(a_ref[...], w_ref[...], preferred_element_type=F32).astype(o_ref.dtype)


def _row_tile(m):
    for t in (1024, 512, 256):
        if m % t == 0:
            return t
    raise ValueError(f"row count {m} is not a multiple of {ROW_TILE}")


def _matmul(a, w, out_dtype, name):
    M, K = a.shape
    N = w.shape[1]
    tm = _row_tile(M)
    tn = min(N, 1024)
    return pl.pallas_call(
        _mm_kernel,
        grid=(N // tn, M // tm),
        in_specs=[pl.BlockSpec((tm, K), lambda j, i: (i, 0)),
                  pl.BlockSpec((K, tn), lambda j, i: (0, j))],
        out_specs=pl.BlockSpec((tm, tn), lambda j, i: (i, j)),
        out_shape=jax.ShapeDtypeStruct((M, N), out_dtype),
        compiler_params=_params("parallel", "parallel"),
        name=name,
    )(a, w)


def _na_bias_tables(rpb, rows):
    n_blocks = rows // NA_Q_ROWS
    qi = np.arange(NA_Q_ROWS * GRID_W)
    ki = np.arange(NA_K_TOK)
    qr, qc = qi // GRID_W, qi % GRID_W
    kr, kc = ki // GRID_W, ki % GRID_W
    tables = []
    for m in (0, 1, n_blocks - 1):
        q_row = NA_Q_ROWS * m + qr
        k_row = int(np.clip(NA_Q_ROWS * m - NA_Q_ROWS, 0, rows - NA_K_ROWS)) + kr
        r0 = np.clip(q_row - NA_WIN_H // 2, 0, rows - NA_WIN_H)
        c0 = np.clip(qc - NA_WIN_W // 2, 0, GRID_W - NA_WIN_W)
        valid = ((k_row[None, :] >= r0[:, None]) & (k_row[None, :] < r0[:, None] + NA_WIN_H)
                 & (kc[None, :] >= c0[:, None]) & (kc[None, :] < c0[:, None] + NA_WIN_W))
        row_off = np.clip(k_row[None, :] - q_row[:, None] + NA_WIN_H - 1, 0, 2 * NA_WIN_H - 2)
        col_off = np.clip(kc[None, :] - qc[:, None] + NA_WIN_W - 1, 0, 2 * NA_WIN_W - 2)
        bias = rpb[:, row_off, col_off].astype(F32)
        tables.append(jnp.where(valid[None], bias, NEG_INF))
    return jnp.stack(tables)


def _attend(qe, keys, vals, biases):
    dn = (((1,), (1,)), ((), ()))
    scores = []
    for kk, bias in zip(keys, biases):
        s = lax.dot_general(qe, kk, dn, preferred_element_type=F32)
        scores.append(s if bias is None else s + bias)
    m = scores[0].max(axis=-1, keepdims=True)
    for s in scores[1:]:
        m = jnp.maximum(m, s.max(axis=-1, keepdims=True))
    num, den = None, None
    for s, vv in zip(scores, vals):
        p = jnp.exp(s - m)
        l = p.sum(axis=-1, keepdims=True)
        o = jnp.dot(p.astype(BF16), vv, preferred_element_type=F32)
        num = o if num is None else num + o
        den = l if den is None else den + l
    return num / den


def _na_kernel(q_ref, k_ref, v_ref, g_ref, bias_ref, o_ref, *, n_ctx, rows):
    j = pl.program_id(2)
    lane = lax.broadcasted_iota(jnp.int32, (1, LANES), 1)
    in_head = (lane < HEAD_DIM, lane >= HEAD_DIM)
    q = q_ref[0] * (HEAD_DIM ** -0.5)
    kc = k_ref[0, 0:n_ctx, :].astype(BF16)
    vc = v_ref[0, 0:n_ctx, :].astype(BF16)

    def heads(q):
        return [jnp.where(in_head[e], q, 0.0).astype(BF16) for e in range(2)]

    def finish(o0, o1):
        o = jnp.where(in_head[0], o0, o1)
        o_ref[0] = (o * _silu(g_ref[0])).astype(o_ref.dtype)

    @pl.when(j == 0)
    def _():
        finish(*[_attend(qe, [kc], [vc], [None]) for qe in heads(q)])

    @pl.when(j > 0)
    def _():
        k_row = jnp.clip(NA_Q_ROWS * (j - 1) - NA_Q_ROWS, 0, rows - NA_K_ROWS)
        start = pl.multiple_of(n_ctx + k_row * GRID_W, GRID_W)
        kw = k_ref[0, pl.ds(start, NA_K_TOK), :].astype(BF16)
        vw = v_ref[0, pl.ds(start, NA_K_TOK), :].astype(BF16)
        finish(*[_attend(qe, [kw, kc], [vw, vc], [bias_ref[0, e], None])
                 for e, qe in enumerate(heads(q))])


def _na_attention(p_main, bias_tables, n_ctx):
    B, R, _ = p_main.shape
    rows = (R - n_ctx) // GRID_W
    n_blocks = rows // NA_Q_ROWS
    pairs = W_BRANCH // LANES
    assert n_ctx == ROW_TILE and rows >= NA_K_ROWS and rows % NA_Q_ROWS == 0

    def col(c):
        return lambda b, hp, j: (b, 0, c * pairs + hp)

    def bias_idx(b, hp, j):
        return (jnp.where(j <= 1, 0, jnp.where(j == n_blocks, 2, 1)), hp, 0, 0)

    return pl.pallas_call(
        functools.partial(_na_kernel, n_ctx=n_ctx, rows=rows),
        grid=(B, pairs, n_blocks + 1),
        in_specs=[pl.BlockSpec((1, ROW_TILE, LANES), lambda b, hp, j: (b, j, COL_Q * pairs + hp)),
                  pl.BlockSpec((1, R, LANES), col(COL_K)),
                  pl.BlockSpec((1, R, LANES), col(COL_V)),
                  pl.BlockSpec((1, ROW_TILE, LANES), lambda b, hp, j: (b, j, COL_NA_GATE * pairs + hp)),
                  pl.BlockSpec((1, 2, ROW_TILE, NA_K_TOK), bias_idx)],
        out_specs=pl.BlockSpec((1, ROW_TILE, LANES), lambda b, hp, j: (b, j, hp)),
        out_shape=jax.ShapeDtypeStruct((B, R, W_BRANCH), BF16),
        compiler_params=_params("parallel", "parallel", "arbitrary"),
        name="neighbourhood_attention",
    )(p_main, p_main, p_main, p_main, bias_tables)


def _pool_kernel(u_ref, g_ref, w_ref, sc_ref, o_ref, pad_ref, *, n_ctx, n_lat):
    grp = pl.program_id(1)
    w = w_ref[0]
    scale = sc_ref[...]

    def run(win):
        half = win // 2
        for seq_start, seq_len in ((0, n_ctx), (n_ctx, n_lat)):
            zeros = jnp.zeros((POOL_HALO, POOL_GROUP_DIM), F32)
            pad_ref[0:POOL_HALO, :] = zeros
            pad_ref[POOL_HALO:POOL_HALO + seq_len, :] = u_ref[0, seq_start:seq_start + seq_len, :]
            pad_ref[POOL_HALO + seq_len:2 * POOL_HALO + seq_len, :] = zeros

            def chunk(c, carry):
                base = pl.multiple_of(c * ROW_TILE, ROW_TILE)
                x = pad_ref[pl.ds(base, ROW_TILE + 2 * POOL_HALO), :]
                acc = x[POOL_HALO - half:POOL_HALO - half + ROW_TILE]
                for o in range(-half + 1, half):
                    acc = acc + x[POOL_HALO + o:POOL_HALO + o + ROW_TILE]
                t = base + lax.broadcasted_iota(jnp.int32, (ROW_TILE, 1), 0)
                cnt = jnp.minimum(t + half, seq_len) - jnp.maximum(t - half, 0)
                diff = acc / cnt.astype(F32) - x[POOL_HALO:POOL_HALO + ROW_TILE]
                y = jnp.dot(diff.astype(BF16), w, preferred_element_type=F32) * scale
                rows = pl.ds(seq_start + base, ROW_TILE)
                o_ref[0, rows, :] = (y * _silu(g_ref[0, rows, :])).astype(o_ref.dtype)
                return carry

            lax.fori_loop(0, seq_len // ROW_TILE, chunk, 0)

    for gi, win in enumerate(POOL_WINDOWS):
        pl.when(grp == gi)(functools.partial(run, win))


def _pool(p_main, pool_w, pool_scale, n_ctx):
    B, R, _ = p_main.shape
    groups = len(POOL_WINDOWS)
    return pl.pallas_call(
        functools.partial(_pool_kernel, n_ctx=n_ctx, n_lat=R - n_ctx),
        grid=(B, groups),
        in_specs=[pl.BlockSpec((1, R, POOL_GROUP_DIM), lambda b, g: (b, 0, COL_POOL_U * groups + g)),
                  pl.BlockSpec((1, R, POOL_GROUP_DIM), lambda b, g: (b, 0, COL_POOL_GATE * groups + g)),
                  pl.BlockSpec((1, POOL_GROUP_DIM, POOL_GROUP_DIM), lambda b, g: (g, 0, 0)),
                  pl.BlockSpec((1, POOL_GROUP_DIM), lambda b, g: (0, g))],
        out_specs=pl.BlockSpec((1, R, POOL_GROUP_DIM), lambda b, g: (b, 0, g)),
        out_shape=jax.ShapeDtypeStruct((B, R, W_BRANCH), BF16),
        scratch_shapes=[pltpu.VMEM((R - n_ctx + 2 * POOL_HALO, POOL_GROUP_DIM), F32)],
        compiler_params=_params("parallel", "arbitrary"),
        name="multiscale_pool",
    )(p_main, p_main, pool_w, pool_scale)


def _rwkv_feat_kernel(r_ref, rp_ref, rn_ref, k_ref, kp_ref, kn_ref, v_ref, vp_ref, vn_ref, lora_ref,
                      par_ref, w2_ref, a2_ref, ones_ref,
                      ro_ref, vo_ref, ao_ref, wf_ref, kf_ref, bf_ref, wb_ref, kb_ref, bb_ref, bonus_ref,
                      *, n_ctx_tiles, n_tiles):
    i = pl.program_id(1)
    first = (i == 0) | (i == n_ctx_tiles)
    last = (i == n_ctx_tiles - 1) | (i == n_tiles - 1)
    row = lax.broadcasted_iota(jnp.int32, (ROW_TILE, 1), 0)
    par = par_ref[...]
    ones = ones_ref[...]

    def prm(p):
        return par[p:p + 1, :]

    def mix(z_ref, prev_ref, next_ref, mu):
        z = z_ref[0]
        prev = jnp.where(first, 0.0, prev_ref[0, 7:8, :])
        nxt = jnp.where(last, 0.0, next_ref[0, 0:1, :])
        z_prev = jnp.where(row == 0, prev, pltpu.roll(z, 1, 0))
        z_next = jnp.where(row == ROW_TILE - 1, nxt, pltpu.roll(z, ROW_TILE - 1, 0))
        return z + mu * (0.5 * (z_prev + z_next) - z)

    r = mix(r_ref, rp_ref, rn_ref, prm(P_MU_R))
    k = mix(k_ref, kp_ref, kn_ref, prm(P_MU_K))
    v = mix(v_ref, vp_ref, vn_ref, prm(P_MU_V))
    ro_ref[0] = r
    vo_ref[0] = v

    kk = k * prm(P_K_K)
    kk = kk / jnp.maximum(jnp.sqrt(_head_sum(kk * kk, ones)), 1e-12)
    ao_ref[0] = -kk

    lora = lora_ref[0]
    lane = lax.broadcasted_iota(jnp.int32, (1, LANES), 1)
    lora = jnp.where(lane < RWKV_LORA, jnp.tanh(lora), lora).astype(BF16)
    k_sum = None
    for d, (w_out, k_out, b_out) in enumerate(((wf_ref, kf_ref, bf_ref), (wb_ref, kb_ref, bb_ref))):
        x = prm(P_W0_F + d) + jnp.dot(lora, w2_ref[d], preferred_element_type=F32)
        w_log = -(jnp.maximum(-x, 0.0) + jnp.log1p(jnp.exp(-jnp.abs(x)))) - 0.5
        w_out[0] = jnp.exp(-jnp.exp(w_log))
        a = _sigmoid(prm(P_A0_F + d) + jnp.dot(lora, a2_ref[d], preferred_element_type=F32))
        k_d = k * (1.0 + (a - 1.0) * prm(P_K_A))
        k_out[0] = k_d
        b_out[0] = kk * a
        k_sum = k_d if k_sum is None else k_sum + k_d
    bonus_ref[0] = _head_sum(r * k_sum * prm(P_R_K), ones) * v


def _rwkv_features(p_main, p_lora, par, w2, a2, ones, n_ctx):
    B, R, _ = p_main.shape
    n_tiles = R // ROW_TILE
    sub = ROW_TILE // 8

    def main(c):
        return pl.BlockSpec((1, ROW_TILE, W_BRANCH), lambda b, i: (b, i, c))

    def prev(c):
        return pl.BlockSpec((1, 8, W_BRANCH), lambda b, i: (b, jnp.maximum(i * sub - 1, 0), c))

    def nxt(c):
        return pl.BlockSpec((1, 8, W_BRANCH), lambda b, i: (b, jnp.minimum((i + 1) * sub, n_tiles * sub - 1), c))

    in_specs = []
    for c in (COL_RW_R, COL_RW_K, COL_RW_V):
        in_specs += [main(c), prev(c), nxt(c)]
    in_specs += [pl.BlockSpec((1, ROW_TILE, LANES), lambda b, i: (b, i, 0)),
                 pl.BlockSpec((P_ROWS, W_BRANCH), lambda b, i: (0, 0)),
                 pl.BlockSpec((2, LANES, W_BRANCH), lambda b, i: (0, 0, 0)),
                 pl.BlockSpec((2, LANES, W_BRANCH), lambda b, i: (0, 0, 0)),
                 pl.BlockSpec((LANES, LANES), lambda b, i: (0, 0))]
    out = jax.ShapeDtypeStruct((B, R, W_BRANCH), F32)
    return pl.pallas_call(
        functools.partial(_rwkv_feat_kernel, n_ctx_tiles=n_ctx // ROW_TILE, n_tiles=n_tiles),
        grid=(B, n_tiles),
        in_specs=in_specs,
        out_specs=[pl.BlockSpec((1, ROW_TILE, W_BRANCH), lambda b, i: (b, i, 0))] * 10,
        out_shape=[out] * 10,
        compiler_params=_params("parallel", "parallel"),
        name="rwkv_features",
    )(*([p_main] * 9), p_lora, par, w2, a2, ones)


def _scan_kernel(w_ref, k_ref, b_ref, a_ref, r_ref, v_ref, y_ref, s_ref):
    n = HEAD_DIM

    @pl.when(pl.program_id(0) == 0)
    def _():
        s_ref[...] = jnp.zeros_like(s_ref)

    def step(t, carry):
        def row(ref, j):
            return ref[0, pl.ds(j * SCAN_STEPS + t, 1), :]

        sa = [s_ref[j] * row(a_ref, j) for j in range(2)]
        for j in range(2, n):
            sa[j % 2] = sa[j % 2] + s_ref[j] * row(a_ref, j)
        sa = sa[0] + sa[1]
        vt = v_ref[0, pl.ds(t, n, stride=SCAN_STEPS), :]
        y = [None, None]
        for j in range(n):
            sj = s_ref[j] * row(w_ref, j) + sa * row(b_ref, j) + vt * row(k_ref, j)
            s_ref[j] = sj
            yj = sj * row(r_ref, j)
            y[j % 2] = yj if y[j % 2] is None else y[j % 2] + yj
        y_ref[0, pl.ds(t, n, stride=SCAN_STEPS), :] = y[0] + y[1]
        return carry

    lax.fori_loop(0, SCAN_STEPS, step, 0)


def _wkv_scan(w, k, b, a, r, v):
    n_chunks, rows, chains = w.shape
    spec = pl.BlockSpec((1, rows, chains), lambda s: (s, 0, 0))
    return pl.pallas_call(
        _scan_kernel,
        grid=(n_chunks,),
        in_specs=[spec] * 6,
        out_specs=spec,
        out_shape=jax.ShapeDtypeStruct(w.shape, F32),
        scratch_shapes=[pltpu.VMEM((HEAD_DIM, HEAD_DIM, chains), F32)],
        compiler_params=_params("arbitrary"),
        name="wkv_scan",
    )(w, k, b, a, r, v)


def _flip_rows(x, flip):
    hi, mid, lo = _split3(x)
    return (jnp.dot(flip, hi, preferred_element_type=F32) + jnp.dot(flip, mid, preferred_element_type=F32)
            + jnp.dot(flip, lo, preferred_element_type=F32))


def _mirror_chunk(c, n_ctx_chunks, n_chunks):
    return jnp.where(c < n_ctx_chunks, n_ctx_chunks - 1 - c, n_ctx_chunks + n_chunks - 1 - c)


def _to_scan_kernel(zf_ref, zb_ref, flip_ref, o_ref, t_ref):
    nb = zf_ref.shape[0]
    for b in range(nb):
        t_ref[b] = zf_ref[b].T
        t_ref[nb + b] = _flip_rows(zb_ref[b], flip_ref[...]).T
    sub = RELAYOUT_ROWS // SCAN_STEPS

    def body(n, carry):
        slabs = [t_ref[g, pl.ds(n, N_HEADS, stride=HEAD_DIM), :] for g in range(2 * nb)]
        tile = jnp.concatenate(slabs, axis=0).T
        for q in range(sub):
            o_ref[q, pl.ds(pl.multiple_of(n * SCAN_STEPS, SCAN_STEPS), SCAN_STEPS), :] = (
                tile[q * SCAN_STEPS:(q + 1) * SCAN_STEPS])
        return carry

    lax.fori_loop(0, HEAD_DIM, body, 0)


def _to_scan(z_fwd, z_bwd, flip, n_ctx):
    B, R, _ = z_fwd.shape
    n_chunks = R // RELAYOUT_ROWS
    n_ctx_chunks = n_ctx // RELAYOUT_ROWS
    sub = RELAYOUT_ROWS // SCAN_STEPS
    chains = 2 * B * N_HEADS
    return pl.pallas_call(
        _to_scan_kernel,
        grid=(n_chunks,),
        in_specs=[pl.BlockSpec((B, RELAYOUT_ROWS, W_BRANCH), lambda c: (0, c, 0)),
                  pl.BlockSpec((B, RELAYOUT_ROWS, W_BRANCH),
                               lambda c: (0, _mirror_chunk(c, n_ctx_chunks, n_chunks), 0)),
                  pl.BlockSpec((RELAYOUT_ROWS, RELAYOUT_ROWS), lambda c: (0, 0))],
        out_specs=pl.BlockSpec((sub, HEAD_DIM * SCAN_STEPS, chains), lambda c: (c, 0, 0)),
        out_shape=jax.ShapeDtypeStruct((R // SCAN_STEPS, HEAD_DIM * SCAN_STEPS, chains), F32),
        scratch_shapes=[pltpu.VMEM((2 * B, W_BRANCH, RELAYOUT_ROWS), F32)],
        compiler_params=_params("parallel"),
        name="to_scan_layout",
    )(z_fwd, z_bwd, flip)


def _from_scan_kernel(y_ref, flip_ref, yf_ref, yb_ref, t_ref):
    nb = yf_ref.shape[0]
    sub = RELAYOUT_ROWS // SCAN_STEPS

    def body(n, carry):
        base = pl.multiple_of(n * SCAN_STEPS, SCAN_STEPS)
        tile = jnp.concatenate([y_ref[q, pl.ds(base, SCAN_STEPS), :] for q in range(sub)], axis=0)
        tile = tile.T
        for g in range(2 * nb):
            t_ref[g, pl.ds(n, N_HEADS, stride=HEAD_DIM), :] = tile[g * N_HEADS:(g + 1) * N_HEADS]
        return carry

    lax.fori_loop(0, HEAD_DIM, body, 0)
    for b in range(nb):
        yf_ref[b] = t_ref[b].T
        yb_ref[b] = _flip_rows(t_ref[nb + b].T, flip_ref[...])


def _from_scan(y, flip, n_batch, n_ctx):
    R = y.shape[0] * SCAN_STEPS
    n_chunks = R // RELAYOUT_ROWS
    n_ctx_chunks = n_ctx // RELAYOUT_ROWS
    sub = RELAYOUT_ROWS // SCAN_STEPS
    out = jax.ShapeDtypeStruct((n_batch, R, W_BRANCH), F32)
    return pl.pallas_call(
        _from_scan_kernel,
        grid=(n_chunks,),
        in_specs=[pl.BlockSpec((sub, HEAD_DIM * SCAN_STEPS, y.shape[2]), lambda c: (c, 0, 0)),
                  pl.BlockSpec((RELAYOUT_ROWS, RELAYOUT_ROWS), lambda c: (0, 0))],
        out_specs=[pl.BlockSpec((n_batch, RELAYOUT_ROWS, W_BRANCH), lambda c: (0, c, 0)),
                   pl.BlockSpec((n_batch, RELAYOUT_ROWS, W_BRANCH),
                                lambda c: (0, _mirror_chunk(c, n_ctx_chunks, n_chunks), 0))],
        out_shape=[out, out],
        scratch_shapes=[pltpu.VMEM((2 * n_batch, W_BRANCH, RELAYOUT_ROWS), F32)],
        compiler_params=_params("parallel"),
        name="from_scan_layout",
    )(y, flip)


def _rwkv_readout_kernel(yf_ref, yb_ref, bonus_ref, gate_ref, gb_ref, ones_ref, o_ref):
    ones = ones_ref[...]
    y = yf_ref[0] + yb_ref[0]
    mu = _head_sum(y, ones) * (1.0 / HEAD_DIM)
    yc = y - mu
    var = _head_sum(yc * yc, ones) * (1.0 / HEAD_DIM)
    gb = gb_ref[...]
    out = yc * lax.rsqrt(var + LNX_EPS) * gb[0:1, :] + gb[1:2, :] + bonus_ref[0]
    o_ref[0] = (out * _silu(gate_ref[0])).astype(o_ref.dtype)


def _rwkv_readout(y_fwd, y_bwd, bonus, p_main, lnx_gb, ones):
    B, R, _ = y_fwd.shape
    tile = pl.BlockSpec((1, ROW_TILE, W_BRANCH), lambda b, i: (b, i, 0))
    return pl.pallas_call(
        _rwkv_readout_kernel,
        grid=(B, R // ROW_TILE),
        in_specs=[tile, tile, tile,
                  pl.BlockSpec((1, ROW_TILE, W_BRANCH), lambda b, i: (b, i, COL_RW_GATE)),
                  pl.BlockSpec((8, W_BRANCH), lambda b, i: (0, 0)),
                  pl.BlockSpec((LANES, LANES), lambda b, i: (0, 0))],
        out_specs=tile,
        out_shape=jax.ShapeDtypeStruct((B, R, W_BRANCH), BF16),
        compiler_params=_params("parallel", "parallel"),
        name="rwkv_readout",
    )(y_fwd, y_bwd, bonus, p_main, lnx_gb, ones)


def _merge_kernel(na_ref, pool_ref, rw_ref, lna_ref, lpool_ref, lrw_ref, w_ref, o_ref):
    acc = None
    for br, (x_ref, l_ref) in enumerate(((na_ref, lna_ref), (pool_ref, lpool_ref), (rw_ref, lrw_ref))):
        t = _sigmoid(l_ref[...]) * jnp.dot(x_ref[...], w_ref[br], preferred_element_type=F32)
        acc = t if acc is None else acc + t
    o_ref[...] = acc.astype(o_ref.dtype)


def _merge(b_na, b_pool, b_rw, p_merge, w_branch):
    M = b_na.shape[0]
    tm, tn = min(_row_tile(M), 512), 1024
    nb = D_MODEL // tn
    x_spec = pl.BlockSpec((tm, W_BRANCH), lambda j, i: (i, 0))

    def logit(br):
        return pl.BlockSpec((tm, tn), lambda j, i: (i, br * nb + j))

    return pl.pallas_call(
        _merge_kernel,
        grid=(nb, M // tm),
        in_specs=[x_spec, x_spec, x_spec, logit(0), logit(1), logit(2),
                  pl.BlockSpec((N_BRANCH, W_BRANCH, tn), lambda j, i: (0, 0, j))],
        out_specs=pl.BlockSpec((tm, tn), lambda j, i: (i, j)),
        out_shape=jax.ShapeDtypeStruct((M, D_MODEL), BF16),
        compiler_params=_params("parallel", "parallel"),
        name="branch_merge",
    )(b_na, b_pool, b_rw, p_merge, p_merge, p_merge, w_branch)


def _out_kernel(m_ref, w_ref, x_ref, mod_ref, fg_ref, o_ref, *, n_ctx_tiles, tile_offset, final):
    gate = _mod_row(mod_ref, pl.program_id(1) + tile_offset, n_ctx_tiles)[:, 2 * D_MODEL:]
    x = x_ref[0] + gate * jnp.dot(m_ref[0], w_ref[...], preferred_element_type=F32)
    o_ref[0] = _rms(x, fg_ref[...]) if final else x


def _out_proj(merged, w_out, x_all, mod, final_g, n_ctx, final):
    B, R, _ = x_all.shape
    off = n_ctx // ROW_TILE if final else 0
    tile = pl.BlockSpec((1, ROW_TILE, D_MODEL), lambda b, i: (b, i + off, 0))
    return pl.pallas_call(
        functools.partial(_out_kernel, n_ctx_tiles=n_ctx // ROW_TILE, tile_offset=off, final=final),
        grid=(B, R // ROW_TILE - off),
        in_specs=[tile,
                  pl.BlockSpec((D_MODEL, D_MODEL), lambda b, i: (0, 0)),
                  tile,
                  pl.BlockSpec((MOD_ROWS, 3 * D_MODEL), lambda b, i: (0, 0)),
                  pl.BlockSpec((1, D_MODEL), lambda b, i: (0, 0))],
        out_specs=pl.BlockSpec((1, ROW_TILE, D_MODEL), lambda b, i: (b, i, 0)),
        out_shape=jax.ShapeDtypeStruct((B, R - off * ROW_TILE, D_MODEL), F32),
        compiler_params=_params("parallel", "parallel"),
        name="out_proj_final" if final else "out_proj",
    )(merged, w_out, x_all, mod, final_g)


def _head_ones():
    lane = np.arange(LANES)
    return jnp.asarray(lane[:, None] // HEAD_DIM == lane[None, :] // HEAD_DIM, BF16)


def _layer(x_all, mod, n_ctx, final, final_g, norm_g, w_in, na_rpb, pool_w, pool_scale, rw_mu, rw_w0, rw_w2,
           rw_a0, rw_a2, rw_k_k, rw_k_a, rw_r_k, rw_lnx_g, rw_lnx_b, w_branch, w_out):
    B, R, _ = x_all.shape
    rows = (R - n_ctx) // GRID_W
    ones = _head_ones()

    h = _norm_mod(x_all, norm_g[None], mod, n_ctx).reshape(B * R, D_MODEL)
    lo = N_MAIN + 2 * RWKV_LORA
    p_main = _matmul(h, w_in[:, :N_MAIN].astype(BF16), F32, "in_proj_main").reshape(B, R, N_MAIN)
    p_lora = _matmul(h, w_in[:, N_MAIN:lo].astype(BF16), F32, "in_proj_lora").reshape(B, R, 2 * RWKV_LORA)
    p_merge = _matmul(h, w_in[:, lo:].astype(BF16), F32, "in_proj_merge")

    b_na = _na_attention(p_main, _na_bias_tables(na_rpb, rows), n_ctx)
    b_pool = _pool(p_main, pool_w.astype(BF16), pool_scale[None], n_ctx)

    par = jnp.zeros((P_ROWS, W_BRANCH), F32)
    par = par.at[P_MU_R:P_MU_V + 1].set(rw_mu).at[P_W0_F:P_W0_B + 1].set(rw_w0).at[P_A0_F:P_A0_B + 1].set(rw_a0)
    par = par.at[P_K_K].set(rw_k_k).at[P_K_A].set(rw_k_a).at[P_R_K].set(rw_r_k.reshape(-1))
    zeros = jnp.zeros_like(rw_w2)
    w2 = jnp.concatenate([rw_w2, zeros], axis=1).astype(BF16)
    a2 = jnp.concatenate([zeros, rw_a2], axis=1).astype(BF16)
    r, v, a_in, w_f, k_f, b_f, w_b, k_b, b_b, bonus = _rwkv_features(p_main, p_lora, par, w2, a2, ones, n_ctx)
    flip = jnp.asarray(np.eye(RELAYOUT_ROWS)[::-1], BF16)
    y = _wkv_scan(*[_to_scan(zf, zb, flip, n_ctx)
                    for zf, zb in ((w_f, w_b), (k_f, k_b), (b_f, b_b), (a_in, a_in), (r, r), (v, v))])
    y_fwd, y_bwd = _from_scan(y, flip, B, n_ctx)
    lnx_gb = jnp.zeros((8, W_BRANCH), F32).at[0].set(rw_lnx_g).at[1].set(rw_lnx_b)
    b_rw = _rwkv_readout(y_fwd, y_bwd, bonus, p_main, lnx_gb, ones)

    def flat(z):
        return z.reshape(B * R, W_BRANCH)

    merged = _merge(flat(b_na), flat(b_pool), flat(b_rw), p_merge, w_branch.astype(BF16))
    return _out_proj(merged.reshape(B, R, D_MODEL), w_out.astype(BF16), x_all, mod, final_g[None], n_ctx, final)


def kernel(x, c, ctx, c_ctx, norm_g, w_mod, b_mod, w_in, na_rpb, pool_w, pool_scale, rw_mu, rw_w0, rw_w2, rw_a0,
           rw_a2, rw_k_k, rw_k_a, rw_r_k, rw_lnx_g, rw_lnx_b, w_branch, w_out, final_g):
    B, T, _ = x.shape
    n_ctx = ctx.shape[1]
    depth = w_in.shape[0]
    assert B <= CTX_MOD_ROW and n_ctx % ROW_TILE == 0 and T % ROW_TILE == 0
    assert ROW_TILE % RELAYOUT_ROWS == 0 and RELAYOUT_ROWS % SCAN_STEPS == 0

    cond = jnp.zeros((MOD_ROWS, D_MODEL), F32).at[:B].set(c).at[CTX_MOD_ROW].set(c_ctx)
    mods = _modulation(cond, w_mod.astype(BF16), b_mod[:, None, :])
    x_all = jnp.concatenate([ctx, x], axis=1)
    for layer in range(depth):
        x_all = _layer(x_all, mods[layer], n_ctx, layer == depth - 1, final_g, norm_g[layer], w_in[layer],
                       na_rpb[layer], pool_w[layer], pool_scale[layer], rw_mu[layer], rw_w0[layer], rw_w2[layer],
                       rw_a0[layer], rw_a2[layer], rw_k_k[layer], rw_k_a[layer], rw_r_k[layer], rw_lnx_g[layer],
                       rw_lnx_b[layer], w_branch[layer], w_out[layer])
    return x_all
```

```python
import functools

import numpy as np
import jax
import jax.numpy as jnp
from jax import lax
from jax.experimental import pallas as pl
from jax.experimental.pallas import tpu as pltpu

F32 = jnp.float32
BF16 = jnp.bfloat16

D_MODEL = 2048
W_BRANCH = D_MODEL // 2
N_BRANCH = 3
N_HEADS = 16
HEAD_DIM = 64
GRID_W = 64
NA_WIN_H = 8
NA_WIN_W = 16
POOL_WINDOWS = (2, 4, 8, 16)
POOL_GROUP_DIM = W_BRANCH // len(POOL_WINDOWS)
POOL_HALO = 8
RWKV_LORA = 64
RMS_EPS = 1e-6
LNX_EPS = 64e-5
NEG_INF = -1e30

LANES = 128
ROW_TILE = 256
NA_Q_ROWS = ROW_TILE // GRID_W
NA_K_ROWS = NA_Q_ROWS + NA_WIN_H
NA_K_TOK = NA_K_ROWS * GRID_W
CTX_MOD_ROW = 4
MOD_ROWS = 8
SCAN_STEPS = 32
SCAN_J_UNROLL = 8
RELAYOUT_UNROLL = 4
RELAYOUT_ROWS = 128
VMEM_LIMIT = 56 << 20

COL_Q, COL_K, COL_V, COL_NA_GATE, COL_POOL_U, COL_POOL_GATE, COL_RW_R, COL_RW_K, COL_RW_V, COL_RW_GATE = range(10)
N_MAIN = 10 * W_BRANCH

P_MU_R, P_MU_K, P_MU_V, P_W0_F, P_W0_B, P_A0_F, P_A0_B, P_K_K, P_K_A, P_R_K = range(10)
P_ROWS = 16


def _params(*sem):
    return pltpu.CompilerParams(dimension_semantics=sem, vmem_limit_bytes=VMEM_LIMIT)


def _sigmoid(x):
    return 1.0 / (1.0 + jnp.exp(-x))


def _silu(x):
    return x * _sigmoid(x)


def _split3(x):
    hi = x.astype(BF16)
    r1 = x - hi.astype(F32)
    mid = r1.astype(BF16)
    lo = (r1 - mid.astype(F32)).astype(BF16)
    return hi, mid, lo


def _head_sum(x, ones):
    cols = []
    for c in range(x.shape[1] // LANES):
        hi, mid, lo = _split3(x[:, c * LANES:(c + 1) * LANES])
        cols.append(jnp.dot(hi, ones, preferred_element_type=F32)
                    + jnp.dot(mid, ones, preferred_element_type=F32)
                    + jnp.dot(lo, ones, preferred_element_type=F32))
    return jnp.concatenate(cols, axis=1)


def _mod_kernel(cond_ref, w_ref, b_ref, o_ref):
    s = _silu(cond_ref[...])
    o_ref[0] = jnp.dot(s.astype(BF16), w_ref[0], preferred_element_type=F32) + b_ref[0]


def _modulation(cond, w_mod, b_mod):
    n_layers = w_mod.shape[0]
    tn = 3 * D_MODEL // 4
    return pl.pallas_call(
        _mod_kernel,
        grid=(n_layers, 4),
        in_specs=[pl.BlockSpec((MOD_ROWS, D_MODEL), lambda l, j: (0, 0)),
                  pl.BlockSpec((1, D_MODEL, tn), lambda l, j: (l, 0, j)),
                  pl.BlockSpec((1, 1, tn), lambda l, j: (l, 0, j))],
        out_specs=pl.BlockSpec((1, MOD_ROWS, tn), lambda l, j: (l, 0, j)),
        out_shape=jax.ShapeDtypeStruct((n_layers, MOD_ROWS, 3 * D_MODEL), F32),
        compiler_params=_params("arbitrary", "arbitrary"),
        name="adaln_modulation",
    )(cond, w_mod, b_mod)


def _mod_row(mod_ref, tile, n_ctx_tiles):
    row = jnp.where(tile < n_ctx_tiles, CTX_MOD_ROW, pl.program_id(0))
    return mod_ref[pl.ds(row, 1), :]


def _rms(x, g):
    return x * lax.rsqrt(jnp.mean(x * x, axis=-1, keepdims=True) + RMS_EPS) * g


def _norm_mod_kernel(x_ref, g_ref, mod_ref, h_ref, *, n_ctx_tiles):
    m = _mod_row(mod_ref, pl.program_id(1), n_ctx_tiles)
    shift = m[:, :D_MODEL]
    scale = m[:, D_MODEL:2 * D_MODEL]
    h_ref[0] = (_rms(x_ref[0], g_ref[...]) * (1.0 + scale) + shift).astype(BF16)


def _norm_mod(x_all, norm_g, mod, n_ctx):
    B, R, _ = x_all.shape
    return pl.pallas_call(
        functools.partial(_norm_mod_kernel, n_ctx_tiles=n_ctx // ROW_TILE),
        grid=(B, R // ROW_TILE),
        in_specs=[pl.BlockSpec((1, ROW_TILE, D_MODEL), lambda b, i: (b, i, 0)),
                  pl.BlockSpec((1, D_MODEL), lambda b, i: (0, 0)),
                  pl.BlockSpec((MOD_ROWS, 3 * D_MODEL), lambda b, i: (0, 0))],
        out_specs=pl.BlockSpec((1, ROW_TILE, D_MODEL), lambda b, i: (b, i, 0)),
        out_shape=jax.ShapeDtypeStruct((B, R, D_MODEL), BF16),
        compiler_params=_params("parallel", "parallel"),
        name="norm_modulate",
    )(x_all, norm_g, mod)


def _mm_kernel(a_ref, w_ref, o_ref):
    o_ref[...] = jnp.dot(a_ref[...], w_ref[...], preferred_element_type=F32).astype(o_ref.dtype)


def _row_tile(m):
    for t in (1024, 512, 256):
        if m % t == 0:
            return t
    raise ValueError(f"row count {m} is not a multiple of {ROW_TILE}")


def _matmul(a, w, out_dtype, name):
    M, K = a.shape
    N = w.shape[1]
    tm = _row_tile(M)
    tn = min(N, 1024)
    return pl.pallas_call(
        _mm_kernel,
        grid=(N // tn, M // tm),
        in_specs=[pl.BlockSpec((tm, K), lambda j, i: (i, 0)),
                  pl.BlockSpec((K, tn), lambda j, i: (0, j))],
        out_specs=pl.BlockSpec((tm, tn), lambda j, i: (i, j)),
        out_shape=jax.ShapeDtypeStruct((M, N), out_dtype),
        compiler_params=_params("parallel", "parallel"),
        name=name,
    )(a, w)


def _na_bias_tables(rpb, rows):
    n_blocks = rows // NA_Q_ROWS
    col = np.arange(GRID_W)
    c0 = np.clip(col - NA_WIN_W // 2, 0, GRID_W - NA_WIN_W)
    valid_c = (col[None, :] >= c0[:, None]) & (col[None, :] < c0[:, None] + NA_WIN_W)
    col_off = np.clip(col[None, :] - col[:, None] + NA_WIN_W - 1, 0, 2 * NA_WIN_W - 2)
    pick_c = jnp.asarray(np.eye(2 * NA_WIN_W - 1)[col_off], F32)
    tables = []
    for m in (0, 1, n_blocks - 1):
        q_row = NA_Q_ROWS * m + np.arange(NA_Q_ROWS)
        k_row = int(np.clip(NA_Q_ROWS * m - NA_Q_ROWS, 0, rows - NA_K_ROWS)) + np.arange(NA_K_ROWS)
        r0 = np.clip(q_row - NA_WIN_H // 2, 0, rows - NA_WIN_H)
        valid_r = (k_row[None, :] >= r0[:, None]) & (k_row[None, :] < r0[:, None] + NA_WIN_H)
        row_off = np.clip(k_row[None, :] - q_row[:, None] + NA_WIN_H - 1, 0, 2 * NA_WIN_H - 2)
        pick_r = jnp.asarray(np.eye(2 * NA_WIN_H - 1)[row_off], F32)
        bias = jnp.einsum("akr,hrc,qpc->haqkp", pick_r, rpb.astype(F32), pick_c, precision=lax.Precision.HIGHEST)
        valid = valid_r[:, None, :, None] & valid_c[None, :, None, :]
        tables.append(jnp.where(valid[None], bias, NEG_INF).reshape(N_HEADS, ROW_TILE, NA_K_TOK))
    return jnp.stack(tables)


def _attend(qe, keys, vals, biases):
    dn = (((1,), (1,)), ((), ()))
    scores = []
    for kk, bias in zip(keys, biases):
        s = lax.dot_general(qe, kk, dn, preferred_element_type=F32)
        scores.append(s if bias is None else s + bias)
    m = scores[0].max(axis=-1, keepdims=True)
    for s in scores[1:]:
        m = jnp.maximum(m, s.max(axis=-1, keepdims=True))
    num, den = None, None
    for s, vv in zip(scores, vals):
        p = jnp.exp(s - m)
        l = p.sum(axis=-1, keepdims=True)
        o = jnp.dot(p.astype(BF16), vv, preferred_element_type=F32)
        num = o if num is None else num + o
        den = l if den is None else den + l
    return num / den


def _na_kernel(q_ref, k_ref, v_ref, g_ref, bias_ref, o_ref, *, n_ctx, rows):
    j = pl.program_id(2)
    lane = lax.broadcasted_iota(jnp.int32, (1, LANES), 1)
    in_head = (lane < HEAD_DIM, lane >= HEAD_DIM)
    q = q_ref[0] * (HEAD_DIM ** -0.5)
    kc = k_ref[0, 0:n_ctx, :].astype(BF16)
    vc = v_ref[0, 0:n_ctx, :].astype(BF16)

    def heads(q):
        return [jnp.where(in_head[e], q, 0.0).astype(BF16) for e in range(2)]

    def finish(o0, o1):
        o = jnp.where(in_head[0], o0, o1)
        o_ref[0] = (o * _silu(g_ref[0])).astype(o_ref.dtype)

    @pl.when(j == 0)
    def _():
        finish(*[_attend(qe, [kc], [vc], [None]) for qe in heads(q)])

    @pl.when(j > 0)
    def _():
        k_row = jnp.clip(NA_Q_ROWS * (j - 1) - NA_Q_ROWS, 0, rows - NA_K_ROWS)
        start = pl.multiple_of(n_ctx + k_row * GRID_W, GRID_W)
        kw = k_ref[0, pl.ds(start, NA_K_TOK), :].astype(BF16)
        vw = v_ref[0, pl.ds(start, NA_K_TOK), :].astype(BF16)
        finish(*[_attend(qe, [kw, kc], [vw, vc], [bias_ref[0, e], None])
                 for e, qe in enumerate(heads(q))])


def _na_attention(p_main, bias_tables, n_ctx):
    B, R, _ = p_main.shape
    rows = (R - n_ctx) // GRID_W
    n_blocks = rows // NA_Q_ROWS
    pairs = W_BRANCH // LANES
    assert n_ctx == ROW_TILE and rows >= NA_K_ROWS and rows % NA_Q_ROWS == 0

    def col(c):
        return lambda b, hp, j: (b, 0, c * pairs + hp)

    def bias_idx(b, hp, j):
        return (jnp.where(j <= 1, 0, jnp.where(j == n_blocks, 2, 1)), hp, 0, 0)

    return pl.pallas_call(
        functools.partial(_na_kernel, n_ctx=n_ctx, rows=rows),
        grid=(B, pairs, n_blocks + 1),
        in_specs=[pl.BlockSpec((1, ROW_TILE, LANES), lambda b, hp, j: (b, j, COL_Q * pairs + hp)),
                  pl.BlockSpec((1, R, LANES), col(COL_K)),
                  pl.BlockSpec((1, R, LANES), col(COL_V)),
                  pl.BlockSpec((1, ROW_TILE, LANES), lambda b, hp, j: (b, j, COL_NA_GATE * pairs + hp)),
                  pl.BlockSpec((1, 2, ROW_TILE, NA_K_TOK), bias_idx)],
        out_specs=pl.BlockSpec((1, ROW_TILE, LANES), lambda b, hp, j: (b, j, hp)),
        out_shape=jax.ShapeDtypeStruct((B, R, W_BRANCH), BF16),
        compiler_params=_params("parallel", "parallel", "arbitrary"),
        name="neighbourhood_attention",
    )(p_main, p_main, p_main, p_main, bias_tables)


def _pool_kernel(u_ref, g_ref, w_ref, sc_ref, o_ref, pad_ref, *, n_ctx, n_lat):
    grp = pl.program_id(1)
    w = w_ref[0]
    scale = sc_ref[...]

    def run(win):
        half = win // 2
        for seq_start, seq_len in ((0, n_ctx), (n_ctx, n_lat)):
            zeros = jnp.zeros((POOL_HALO, POOL_GROUP_DIM), F32)
            pad_ref[0:POOL_HALO, :] = zeros
            pad_ref[POOL_HALO:POOL_HALO + seq_len, :] = u_ref[0, seq_start:seq_start + seq_len, :]
            pad_ref[POOL_HALO + seq_len:2 * POOL_HALO + seq_len, :] = zeros

            def chunk(c, carry):
                base = pl.multiple_of(c * ROW_TILE, ROW_TILE)
                x = pad_ref[pl.ds(base, ROW_TILE + 2 * POOL_HALO), :]
                acc = x[POOL_HALO - half:POOL_HALO - half + ROW_TILE]
                for o in range(-half + 1, half):
                    acc = acc + x[POOL_HALO + o:POOL_HALO + o + ROW_TILE]
                t = base + lax.broadcasted_iota(jnp.int32, (ROW_TILE, 1), 0)
                cnt = jnp.minimum(t + half, seq_len) - jnp.maximum(t - half, 0)
                diff = acc / cnt.astype(F32) - x[POOL_HALO:POOL_HALO + ROW_TILE]
                y = jnp.dot(diff.astype(BF16), w, preferred_element_type=F32) * scale
                rows = pl.ds(seq_start + base, ROW_TILE)
                o_ref[0, rows, :] = (y * _silu(g_ref[0, rows, :])).astype(o_ref.dtype)
                return carry

            lax.fori_loop(0, seq_len // ROW_TILE, chunk, 0)

    for gi, win in enumerate(POOL_WINDOWS):
        pl.when(grp == gi)(functools.partial(run, win))


def _pool(p_main, pool_w, pool_scale, n_ctx):
    B, R, _ = p_main.shape
    groups = len(POOL_WINDOWS)
    return pl.pallas_call(
        functools.partial(_pool_kernel, n_ctx=n_ctx, n_lat=R - n_ctx),
        grid=(B, groups),
        in_specs=[pl.BlockSpec((1, R, POOL_GROUP_DIM), lambda b, g: (b, 0, COL_POOL_U * groups + g)),
                  pl.BlockSpec((1, R, POOL_GROUP_DIM), lambda b, g: (b, 0, COL_POOL_GATE * groups + g)),
                  pl.BlockSpec((1, POOL_GROUP_DIM, POOL_GROUP_DIM), lambda b, g: (g, 0, 0)),
                  pl.BlockSpec((1, POOL_GROUP_DIM), lambda b, g: (0, g))],
        out_specs=pl.BlockSpec((1, R, POOL_GROUP_DIM), lambda b, g: (b, 0, g)),
        out_shape=jax.ShapeDtypeStruct((B, R, W_BRANCH), BF16),
        scratch_shapes=[pltpu.VMEM((R - n_ctx + 2 * POOL_HALO, POOL_GROUP_DIM), F32)],
        compiler_params=_params("parallel", "arbitrary"),
        name="multiscale_pool",
    )(p_main, p_main, pool_w, pool_scale)


def _rwkv_feat_kernel(r_ref, rp_ref, rn_ref, k_ref, kp_ref, kn_ref, v_ref, vp_ref, vn_ref, lora_ref,
                      par_ref, w2_ref, a2_ref, ones_ref,
                      ro_ref, vo_ref, ao_ref, wf_ref, kf_ref, bf_ref, wb_ref, kb_ref, bb_ref, bonus_ref,
                      *, n_ctx_tiles, n_tiles):
    i = pl.program_id(1)
    first = (i == 0) | (i == n_ctx_tiles)
    last = (i == n_ctx_tiles - 1) | (i == n_tiles - 1)
    row = lax.broadcasted_iota(jnp.int32, (ROW_TILE, 1), 0)
    par = par_ref[...]
    ones = ones_ref[...]

    def prm(p):
        return par[p:p + 1, :]

    def mix(z_ref, prev_ref, next_ref, mu):
        z = z_ref[0]
        prev = jnp.where(first, 0.0, prev_ref[0, 7:8, :])
        nxt = jnp.where(last, 0.0, next_ref[0, 0:1, :])
        z_prev = jnp.where(row == 0, prev, pltpu.roll(z, 1, 0))
        z_next = jnp.where(row == ROW_TILE - 1, nxt, pltpu.roll(z, ROW_TILE - 1, 0))
        return z + mu * (0.5 * (z_prev + z_next) - z)

    r = mix(r_ref, rp_ref, rn_ref, prm(P_MU_R))
    k = mix(k_ref, kp_ref, kn_ref, prm(P_MU_K))
    v = mix(v_ref, vp_ref, vn_ref, prm(P_MU_V))
    ro_ref[0] = r
    vo_ref[0] = v

    kk = k * prm(P_K_K)
    kk = kk / jnp.maximum(jnp.sqrt(_head_sum(kk * kk, ones)), 1e-12)
    ao_ref[0] = -kk

    lora = lora_ref[0]
    lane = lax.broadcasted_iota(jnp.int32, (1, LANES), 1)
    lora = jnp.where(lane < RWKV_LORA, jnp.tanh(lora), lora).astype(BF16)
    k_sum = None
    for d, (w_out, k_out, b_out) in enumerate(((wf_ref, kf_ref, bf_ref), (wb_ref, kb_ref, bb_ref))):
        x = prm(P_W0_F + d) + jnp.dot(lora, w2_ref[d], preferred_element_type=F32)
        w_log = -(jnp.maximum(-x, 0.0) + jnp.log1p(jnp.exp(-jnp.abs(x)))) - 0.5
        w_out[0] = jnp.exp(-jnp.exp(w_log))
        a = _sigmoid(prm(P_A0_F + d) + jnp.dot(lora, a2_ref[d], preferred_element_type=F32))
        k_d = k * (1.0 + (a - 1.0) * prm(P_K_A))
        k_out[0] = k_d
        b_out[0] = kk * a
        k_sum = k_d if k_sum is None else k_sum + k_d
    bonus_ref[0] = _head_sum(r * k_sum * prm(P_R_K), ones) * v


def _rwkv_features(p_main, p_lora, par, w2, a2, ones, n_ctx):
    B, R, _ = p_main.shape
    n_tiles = R // ROW_TILE
    sub = ROW_TILE // 8

    def main(c):
        return pl.BlockSpec((1, ROW_TILE, W_BRANCH), lambda b, i: (b, i, c))

    def prev(c):
        return pl.BlockSpec((1, 8, W_BRANCH), lambda b, i: (b, jnp.maximum(i * sub - 1, 0), c))

    def nxt(c):
        return pl.BlockSpec((1, 8, W_BRANCH), lambda b, i: (b, jnp.minimum((i + 1) * sub, n_tiles * sub - 1), c))

    in_specs = []
    for c in (COL_RW_R, COL_RW_K, COL_RW_V):
        in_specs += [main(c), prev(c), nxt(c)]
    in_specs += [pl.BlockSpec((1, ROW_TILE, LANES), lambda b, i: (b, i, 0)),
                 pl.BlockSpec((P_ROWS, W_BRANCH), lambda b, i: (0, 0)),
                 pl.BlockSpec((2, LANES, W_BRANCH), lambda b, i: (0, 0, 0)),
                 pl.BlockSpec((2, LANES, W_BRANCH), lambda b, i: (0, 0, 0)),
                 pl.BlockSpec((LANES, LANES), lambda b, i: (0, 0))]
    out = jax.ShapeDtypeStruct((B, R, W_BRANCH), F32)
    return pl.pallas_call(
        functools.partial(_rwkv_feat_kernel, n_ctx_tiles=n_ctx // ROW_TILE, n_tiles=n_tiles),
        grid=(B, n_tiles),
        in_specs=in_specs,
        out_specs=[pl.BlockSpec((1, ROW_TILE, W_BRANCH), lambda b, i: (b, i, 0))] * 10,
        out_shape=[out] * 10,
        compiler_params=_params("parallel", "parallel"),
        name="rwkv_features",
    )(*([p_main] * 9), p_lora, par, w2, a2, ones)


def _scan_kernel(w_ref, k_ref, b_ref, a_ref, r_ref, v_ref, y_ref, s_ref):
    n = HEAD_DIM

    @pl.when(pl.program_id(0) == 0)
    def _():
        s_ref[...] = jnp.zeros_like(s_ref)

    def row(ref, j, t):
        return ref[0, pl.ds(j * SCAN_STEPS + t, 1), :]

    zero = jnp.zeros((n, s_ref.shape[2]), F32)

    def first_sa(jb, sa):
        for jj in range(SCAN_J_UNROLL):
            j = jb * SCAN_J_UNROLL + jj
            sa = sa + s_ref[j] * row(a_ref, j, 0)
        return sa

    def step(t, sa):
        vt = v_ref[0, pl.ds(t, n, stride=SCAN_STEPS), :]
        t_next = jnp.minimum(t + 1, SCAN_STEPS - 1)

        def columns(jb, carry):
            y, sa_next = carry
            for jj in range(SCAN_J_UNROLL):
                j = jb * SCAN_J_UNROLL + jj
                sj = s_ref[j] * row(w_ref, j, t) + sa * row(b_ref, j, t) + vt * row(k_ref, j, t)
                s_ref[j] = sj
                y = y + sj * row(r_ref, j, t)
                sa_next = sa_next + sj * row(a_ref, j, t_next)
            return y, sa_next

        y, sa_next = lax.fori_loop(0, n // SCAN_J_UNROLL, columns, (zero, zero))
        y_ref[0, pl.ds(t, n, stride=SCAN_STEPS), :] = y
        return sa_next

    sa0 = lax.fori_loop(0, n // SCAN_J_UNROLL, first_sa, zero)
    lax.fori_loop(0, SCAN_STEPS, step, sa0)


def _wkv_scan(w, k, b, a, r, v):
    n_chunks, rows, chains = w.shape
    spec = pl.BlockSpec((1, rows, chains), lambda s: (s, 0, 0))
    return pl.pallas_call(
        _scan_kernel,
        grid=(n_chunks,),
        in_specs=[spec] * 6,
        out_specs=spec,
        out_shape=jax.ShapeDtypeStruct(w.shape, F32),
        scratch_shapes=[pltpu.VMEM((HEAD_DIM, HEAD_DIM, chains), F32)],
        compiler_params=_params("arbitrary"),
        name="wkv_scan",
    )(w, k, b, a, r, v)


def _flip_rows(x, flip):
    hi, mid, lo = _split3(x)
    return (jnp.dot(flip, hi, preferred_element_type=F32) + jnp.dot(flip, mid, preferred_element_type=F32)
            + jnp.dot(flip, lo, preferred_element_type=F32))


def _mirror_chunk(c, n_ctx_chunks, n_chunks):
    return jnp.where(c < n_ctx_chunks, n_ctx_chunks - 1 - c, n_ctx_chunks + n_chunks - 1 - c)


def _to_scan_kernel(zf_ref, zb_ref, flip_ref, o_ref, t_ref):
    nb = zf_ref.shape[0]
    for b in range(nb):
        t_ref[b] = zf_ref[b].T
        t_ref[nb + b] = _flip_rows(zb_ref[b], flip_ref[...]).T
    sub = RELAYOUT_ROWS // SCAN_STEPS

    def body(n, carry):
        slabs = [t_ref[g, pl.ds(n, N_HEADS, stride=HEAD_DIM), :] for g in range(2 * nb)]
        tile = jnp.concatenate(slabs, axis=0).T
        for q in range(sub):
            o_ref[q, pl.ds(pl.multiple_of(n * SCAN_STEPS, SCAN_STEPS), SCAN_STEPS), :] = (
                tile[q * SCAN_STEPS:(q + 1) * SCAN_STEPS])
        return carry

    lax.fori_loop(0, HEAD_DIM, body, 0, unroll=RELAYOUT_UNROLL)


def _to_scan(z_fwd, z_bwd, flip, n_ctx):
    B, R, _ = z_fwd.shape
    n_chunks = R // RELAYOUT_ROWS
    n_ctx_chunks = n_ctx // RELAYOUT_ROWS
    sub = RELAYOUT_ROWS // SCAN_STEPS
    chains = 2 * B * N_HEADS
    return pl.pallas_call(
        _to_scan_kernel,
        grid=(n_chunks,),
        in_specs=[pl.BlockSpec((B, RELAYOUT_ROWS, W_BRANCH), lambda c: (0, c, 0)),
                  pl.BlockSpec((B, RELAYOUT_ROWS, W_BRANCH),
                               lambda c: (0, _mirror_chunk(c, n_ctx_chunks, n_chunks), 0)),
                  pl.BlockSpec((RELAYOUT_ROWS, RELAYOUT_ROWS), lambda c: (0, 0))],
        out_specs=pl.BlockSpec((sub, HEAD_DIM * SCAN_STEPS, chains), lambda c: (c, 0, 0)),
        out_shape=jax.ShapeDtypeStruct((R // SCAN_STEPS, HEAD_DIM * SCAN_STEPS, chains), F32),
        scratch_shapes=[pltpu.VMEM((2 * B, W_BRANCH, RELAYOUT_ROWS), F32)],
        compiler_params=_params("parallel"),
        name="to_scan_layout",
    )(z_fwd, z_bwd, flip)


def _from_scan_kernel(y_ref, flip_ref, yf_ref, yb_ref, t_ref):
    nb = yf_ref.shape[0]
    sub = RELAYOUT_ROWS // SCAN_STEPS

    def body(n, carry):
        base = pl.multiple_of(n * SCAN_STEPS, SCAN_STEPS)
        tile = jnp.concatenate([y_ref[q, pl.ds(base, SCAN_STEPS), :] for q in range(sub)], axis=0)
        tile = tile.T
        for g in range(2 * nb):
            t_ref[g, pl.ds(n, N_HEADS, stride=HEAD_DIM), :] = tile[g * N_HEADS:(g + 1) * N_HEADS]
        return carry

    lax.fori_loop(0, HEAD_DIM, body, 0, unroll=RELAYOUT_UNROLL)
    for b in range(nb):
        yf_ref[b] = t_ref[b].T
        yb_ref[b] = _flip_rows(t_ref[nb + b].T, flip_ref[...])


def _from_scan(y, flip, n_batch, n_ctx):
    R = y.shape[0] * SCAN_STEPS
    n_chunks = R // RELAYOUT_ROWS
    n_ctx_chunks = n_ctx // RELAYOUT_ROWS
    sub = RELAYOUT_ROWS // SCAN_STEPS
    out = jax.ShapeDtypeStruct((n_batch, R, W_BRANCH), F32)
    return pl.pallas_call(
        _from_scan_kernel,
        grid=(n_chunks,),
        in_specs=[pl.BlockSpec((sub, HEAD_DIM * SCAN_STEPS, y.shape[2]), lambda c: (c, 0, 0)),
                  pl.BlockSpec((RELAYOUT_ROWS, RELAYOUT_ROWS), lambda c: (0, 0))],
        out_specs=[pl.BlockSpec((n_batch, RELAYOUT_ROWS, W_BRANCH), lambda c: (0, c, 0)),
                   pl.BlockSpec((n_batch, RELAYOUT_ROWS, W_BRANCH),
                                lambda c: (0, _mirror_chunk(c, n_ctx_chunks, n_chunks), 0))],
        out_shape=[out, out],
        scratch_shapes=[pltpu.VMEM((2 * n_batch, W_BRANCH, RELAYOUT_ROWS), F32)],
        compiler_params=_params("parallel"),
        name="from_scan_layout",
    )(y, flip)


def _rwkv_readout_kernel(yf_ref, yb_ref, bonus_ref, gate_ref, gb_ref, ones_ref, o_ref):
    ones = ones_ref[...]
    y = yf_ref[0] + yb_ref[0]
    mu = _head_sum(y, ones) * (1.0 / HEAD_DIM)
    yc = y - mu
    var = _head_sum(yc * yc, ones) * (1.0 / HEAD_DIM)
    gb = gb_ref[...]
    out = yc * lax.rsqrt(var + LNX_EPS) * gb[0:1, :] + gb[1:2, :] + bonus_ref[0]
    o_ref[0] = (out * _silu(gate_ref[0])).astype(o_ref.dtype)


def _rwkv_readout(y_fwd, y_bwd, bonus, p_main, lnx_gb, ones):
    B, R, _ = y_fwd.shape
    tile = pl.BlockSpec((1, ROW_TILE, W_BRANCH), lambda b, i: (b, i, 0))
    return pl.pallas_call(
        _rwkv_readout_kernel,
        grid=(B, R // ROW_TILE),
        in_specs=[tile, tile, tile,
                  pl.BlockSpec((1, ROW_TILE, W_BRANCH), lambda b, i: (b, i, COL_RW_GATE)),
                  pl.BlockSpec((8, W_BRANCH), lambda b, i: (0, 0)),
                  pl.BlockSpec((LANES, LANES), lambda b, i: (0, 0))],
        out_specs=tile,
        out_shape=jax.ShapeDtypeStruct((B, R, W_BRANCH), BF16),
        compiler_params=_params("parallel", "parallel"),
        name="rwkv_readout",
    )(y_fwd, y_bwd, bonus, p_main, lnx_gb, ones)


def _merge_kernel(na_ref, pool_ref, rw_ref, lna_ref, lpool_ref, lrw_ref, w_ref, o_ref):
    acc = None
    for br, (x_ref, l_ref) in enumerate(((na_ref, lna_ref), (pool_ref, lpool_ref), (rw_ref, lrw_ref))):
        t = _sigmoid(l_ref[...]) * jnp.dot(x_ref[...], w_ref[br], preferred_element_type=F32)
        acc = t if acc is None else acc + t
    o_ref[...] = acc.astype(o_ref.dtype)


def _merge(b_na, b_pool, b_rw, p_merge, w_branch):
    M = b_na.shape[0]
    tm, tn = min(_row_tile(M), 512), 1024
    nb = D_MODEL // tn
    x_spec = pl.BlockSpec((tm, W_BRANCH), lambda j, i: (i, 0))

    def logit(br):
        return pl.BlockSpec((tm, tn), lambda j, i: (i, br * nb + j))

    return pl.pallas_call(
        _merge_kernel,
        grid=(nb, M // tm),
        in_specs=[x_spec, x_spec, x_spec, logit(0), logit(1), logit(2),
                  pl.BlockSpec((N_BRANCH, W_BRANCH, tn), lambda j, i: (0, 0, j))],
        out_specs=pl.BlockSpec((tm, tn), lambda j, i: (i, j)),
        out_shape=jax.ShapeDtypeStruct((M, D_MODEL), BF16),
        compiler_params=_params("parallel", "parallel"),
        name="branch_merge",
    )(b_na, b_pool, b_rw, p_merge, p_merge, p_merge, w_branch)


def _out_kernel(m_ref, w_ref, x_ref, mod_ref, fg_ref, o_ref, *, n_ctx_tiles, tile_offset, final):
    gate = _mod_row(mod_ref, pl.program_id(1) + tile_offset, n_ctx_tiles)[:, 2 * D_MODEL:]
    x = x_ref[0] + gate * jnp.dot(m_ref[0], w_ref[...], preferred_element_type=F32)
    o_ref[0] = _rms(x, fg_ref[...]) if final else x


def _out_proj(merged, w_out, x_all, mod, final_g, n_ctx, final):
    B, R, _ = x_all.shape
    off = n_ctx // ROW_TILE if final else 0
    tile = pl.BlockSpec((1, ROW_TILE, D_MODEL), lambda b, i: (b, i + off, 0))
    return pl.pallas_call(
        functools.partial(_out_kernel, n_ctx_tiles=n_ctx // ROW_TILE, tile_offset=off, final=final),
        grid=(B, R // ROW_TILE - off),
        in_specs=[tile,
                  pl.BlockSpec((D_MODEL, D_MODEL), lambda b, i: (0, 0)),
                  tile,
                  pl.BlockSpec((MOD_ROWS, 3 * D_MODEL), lambda b, i: (0, 0)),
                  pl.BlockSpec((1, D_MODEL), lambda b, i: (0, 0))],
        out_specs=pl.BlockSpec((1, ROW_TILE, D_MODEL), lambda b, i: (b, i, 0)),
        out_shape=jax.ShapeDtypeStruct((B, R - off * ROW_TILE, D_MODEL), F32),
        compiler_params=_params("parallel", "parallel"),
        name="out_proj_final" if final else "out_proj",
    )(merged, w_out, x_all, mod, final_g)


def _head_ones():
    lane = np.arange(LANES)
    return jnp.asarray(lane[:, None] // HEAD_DIM == lane[None, :] // HEAD_DIM, BF16)


def _layer(x_all, mod, n_ctx, final, final_g, norm_g, w_in, na_rpb, pool_w, pool_scale, rw_mu, rw_w0, rw_w2,
           rw_a0, rw_a2, rw_k_k, rw_k_a, rw_r_k, rw_lnx_g, rw_lnx_b, w_branch, w_out):
    B, R, _ = x_all.shape
    rows = (R - n_ctx) // GRID_W
    ones = _head_ones()

    h = _norm_mod(x_all, norm_g[None], mod, n_ctx).reshape(B * R, D_MODEL)
    lo = N_MAIN + 2 * RWKV_LORA
    p_main = _matmul(h, w_in[:, :N_MAIN].astype(BF16), F32, "in_proj_main").reshape(B, R, N_MAIN)
    p_lora = _matmul(h, w_in[:, N_MAIN:lo].astype(BF16), F32, "in_proj_lora").reshape(B, R, 2 * RWKV_LORA)
    p_merge = _matmul(h, w_in[:, lo:].astype(BF16), F32, "in_proj_merge")

    b_na = _na_attention(p_main, _na_bias_tables(na_rpb, rows), n_ctx)
    b_pool = _pool(p_main, pool_w.astype(BF16), pool_scale[None], n_ctx)

    par = jnp.zeros((P_ROWS, W_BRANCH), F32)
    par = par.at[P_MU_R:P_MU_V + 1].set(rw_mu).at[P_W0_F:P_W0_B + 1].set(rw_w0).at[P_A0_F:P_A0_B + 1].set(rw_a0)
    par = par.at[P_K_K].set(rw_k_k).at[P_K_A].set(rw_k_a).at[P_R_K].set(rw_r_k.reshape(-1))
    zeros = jnp.zeros_like(rw_w2)
    w2 = jnp.concatenate([rw_w2, zeros], axis=1).astype(BF16)
    a2 = jnp.concatenate([zeros, rw_a2], axis=1).astype(BF16)
    r, v, a_in, w_f, k_f, b_f, w_b, k_b, b_b, bonus = _rwkv_features(p_main, p_lora, par, w2, a2, ones, n_ctx)
    flip = jnp.asarray(np.eye(RELAYOUT_ROWS)[::-1], BF16)
    y = _wkv_scan(*[_to_scan(zf, zb, flip, n_ctx)
                    for zf, zb in ((w_f, w_b), (k_f, k_b), (b_f, b_b), (a_in, a_in), (r, r), (v, v))])
    y_fwd, y_bwd = _from_scan(y, flip, B, n_ctx)
    lnx_gb = jnp.zeros((8, W_BRANCH), F32).at[0].set(rw_lnx_g).at[1].set(rw_lnx_b)
    b_rw = _rwkv_readout(y_fwd, y_bwd, bonus, p_main, lnx_gb, ones)

    def flat(z):
        return z.reshape(B * R, W_BRANCH)

    merged = _merge(flat(b_na), flat(b_pool), flat(b_rw), p_merge, w_branch.astype(BF16))
    return _out_proj(merged.reshape(B, R, D_MODEL), w_out.astype(BF16), x_all, mod, final_g[None], n_ctx, final)


def kernel(x, c, ctx, c_ctx, norm_g, w_mod, b_mod, w_in, na_rpb, pool_w, pool_scale, rw_mu, rw_w0, rw_w2, rw_a0,
           rw_a2, rw_k_k, rw_k_a, rw_r_k, rw_lnx_g, rw_lnx_b, w_branch, w_out, final_g):
    B, T, _ = x.shape
    n_ctx = ctx.shape[1]
    depth = w_in.shape[0]
    assert B <= CTX_MOD_ROW and n_ctx % ROW_TILE == 0 and T % ROW_TILE == 0
    assert ROW_TILE % RELAYOUT_ROWS == 0 and RELAYOUT_ROWS % SCAN_STEPS == 0

    cond = jnp.zeros((MOD_ROWS, D_MODEL), F32).at[:B].set(c).at[CTX_MOD_ROW].set(c_ctx)
    mods = _modulation(cond, w_mod.astype(BF16), b_mod[:, None, :])
    x_all = jnp.concatenate([ctx, x], axis=1)
    for layer in range(depth):
        x_all = _layer(x_all, mods[layer], n_ctx, layer == depth - 1, final_g, norm_g[layer], w_in[layer],
                       na_rpb[layer], pool_w[layer], pool_scale[layer], rw_mu[layer], rw_w0[layer], rw_w2[layer],
                       rw_a0[layer], rw_a2[layer], rw_k_k[layer], rw_k_a[layer], rw_r_k[layer], rw_lnx_g[layer],
                       rw_lnx_b[layer], w_branch[layer], w_out[layer])
    return x_all
```

```python
import functools

import numpy as np
import jax
import jax.numpy as jnp
from jax import lax
from jax.experimental import pallas as pl
from jax.experimental.pallas import tpu as pltpu

F32 = jnp.float32
BF16 = jnp.bfloat16

D_MODEL = 2048
W_BRANCH = D_MODEL // 2
N_BRANCH = 3
N_HEADS = 16
HEAD_DIM = 64
GRID_W = 64
NA_WIN_H = 8
NA_WIN_W = 16
POOL_WINDOWS = (2, 4, 8, 16)
POOL_GROUP_DIM = W_BRANCH // len(POOL_WINDOWS)
POOL_HALO = 8
RWKV_LORA = 64
RMS_EPS = 1e-6
LNX_EPS = 64e-5
NEG_INF = -1e30

LANES = 128
ROW_TILE = 256
NA_Q_ROWS = ROW_TILE // GRID_W
NA_K_ROWS = NA_Q_ROWS + NA_WIN_H
NA_K_TOK = NA_K_ROWS * GRID_W
CTX_MOD_ROW = 4
MOD_ROWS = 8
SCAN_STEPS = 32
SCAN_J_UNROLL = 32
RELAYOUT_UNROLL = 8
RELAYOUT_ROWS = 128
VMEM_LIMIT = 56 << 20

COL_Q, COL_K, COL_V, COL_NA_GATE, COL_POOL_U, COL_POOL_GATE, COL_RW_R, COL_RW_K, COL_RW_V, COL_RW_GATE = range(10)
N_MAIN = 10 * W_BRANCH

P_MU_R, P_MU_K, P_MU_V, P_W0_F, P_W0_B, P_A0_F, P_A0_B, P_K_K, P_K_A, P_R_K = range(10)
P_ROWS = 16


def _params(*sem):
    return pltpu.CompilerParams(dimension_semantics=sem, vmem_limit_bytes=VMEM_LIMIT)


def _sigmoid(x):
    return 1.0 / (1.0 + jnp.exp(-x))


def _silu(x):
    return x * _sigmoid(x)


def _split3(x):
    hi = x.astype(BF16)
    r1 = x - hi.astype(F32)
    mid = r1.astype(BF16)
    lo = (r1 - mid.astype(F32)).astype(BF16)
    return hi, mid, lo


def _head_major(z):
    lead = z.shape[:-1]
    return z.reshape(lead + (N_HEADS, HEAD_DIM)).swapaxes(-1, -2).reshape(lead + (W_BRANCH,))


def _head_sum(x):
    n_tiles = W_BRANCH // LANES
    part = x[:, :LANES]
    for c in range(1, n_tiles):
        part = part + x[:, c * LANES:(c + 1) * LANES]
    shift = N_HEADS
    while shift < LANES:
        part = part + pltpu.roll(part, shift, 1)
        shift *= 2
    return jnp.concatenate([part] * n_tiles, axis=1)


def _mod_kernel(cond_ref, w_ref, b_ref, o_ref):
    s = _silu(cond_ref[...])
    o_ref[0] = jnp.dot(s.astype(BF16), w_ref[0], preferred_element_type=F32) + b_ref[0]


def _modulation(cond, w_mod, b_mod):
    n_layers = w_mod.shape[0]
    tn = 3 * D_MODEL // 4
    return pl.pallas_call(
        _mod_kernel,
        grid=(n_layers, 4),
        in_specs=[pl.BlockSpec((MOD_ROWS, D_MODEL), lambda l, j: (0, 0)),
                  pl.BlockSpec((1, D_MODEL, tn), lambda l, j: (l, 0, j)),
                  pl.BlockSpec((1, 1, tn), lambda l, j: (l, 0, j))],
        out_specs=pl.BlockSpec((1, MOD_ROWS, tn), lambda l, j: (l, 0, j)),
        out_shape=jax.ShapeDtypeStruct((n_layers, MOD_ROWS, 3 * D_MODEL), F32),
        compiler_params=_params("arbitrary", "arbitrary"),
        name="adaln_modulation",
    )(cond, w_mod, b_mod)


def _mod_row(mod_ref, tile, n_ctx_tiles):
    row = jnp.where(tile < n_ctx_tiles, CTX_MOD_ROW, pl.program_id(0))
    return mod_ref[pl.ds(row, 1), :]


def _rms(x, g):
    return x * lax.rsqrt(jnp.mean(x * x, axis=-1, keepdims=True) + RMS_EPS) * g


def _norm_mod_kernel(x_ref, g_ref, mod_ref, h_ref, *, n_ctx_tiles):
    m = _mod_row(mod_ref, pl.program_id(1), n_ctx_tiles)
    shift = m[:, :D_MODEL]
    scale = m[:, D_MODEL:2 * D_MODEL]
    h_ref[0] = (_rms(x_ref[0], g_ref[...]) * (1.0 + scale) + shift).astype(BF16)


def _norm_mod(x_all, norm_g, mod, n_ctx):
    B, R, _ = x_all.shape
    return pl.pallas_call(
        functools.partial(_norm_mod_kernel, n_ctx_tiles=n_ctx // ROW_TILE),
        grid=(B, R // ROW_TILE),
        in_specs=[pl.BlockSpec((1, ROW_TILE, D_MODEL), lambda b, i: (b, i, 0)),
                  pl.BlockSpec((1, D_MODEL), lambda b, i: (0, 0)),
                  pl.BlockSpec((MOD_ROWS, 3 * D_MODEL), lambda b, i: (0, 0))],
        out_specs=pl.BlockSpec((1, ROW_TILE, D_MODEL), lambda b, i: (b, i, 0)),
        out_shape=jax.ShapeDtypeStruct((B, R, D_MODEL), BF16),
        compiler_params=_params("parallel", "parallel"),
        name="norm_modulate",
    )(x_all, norm_g, mod)


def _mm_kernel(a_ref, w_ref, o_ref):
    o_ref[...] = jnp.dot(a_ref[...], w_ref[...], preferred_element_type=F32).astype(o_ref.dtype)


def _row_tile(m):
    for t in (1024, 512, 256):
        if m % t == 0:
            return t
    raise ValueError(f"row count {m} is not a multiple of {ROW_TILE}")


def _matmul(a, w, out_dtype, name):
    M, K = a.shape
    N = w.shape[1]
    tm = _row_tile(M)
    tn = min(N, 1024)
    return pl.pallas_call(
        _mm_kernel,
        grid=(N // tn, M // tm),
        in_specs=[pl.BlockSpec((tm, K), lambda j, i: (i, 0)),
                  pl.BlockSpec((K, tn), lambda j, i: (0, j))],
        out_specs=pl.BlockSpec((tm, tn), lambda j, i: (i, j)),
        out_shape=jax.ShapeDtypeStruct((M, N), out_dtype),
        compiler_params=_params("parallel", "parallel"),
        name=name,
    )(a, w)


def _na_bias_tables(rpb, rows):
    n_blocks = rows // NA_Q_ROWS
    col = np.arange(GRID_W)
    c0 = np.clip(col - NA_WIN_W // 2, 0, GRID_W - NA_WIN_W)
    valid_c = (col[None, :] >= c0[:, None]) & (col[None, :] < c0[:, None] + NA_WIN_W)
    col_off = np.clip(col[None, :] - col[:, None] + NA_WIN_W - 1, 0, 2 * NA_WIN_W - 2)
    pick_c = jnp.asarray(np.eye(2 * NA_WIN_W - 1)[col_off], F32)
    tables = []
    for m in (0, 1, n_blocks - 1):
        q_row = NA_Q_ROWS * m + np.arange(NA_Q_ROWS)
        k_row = int(np.clip(NA_Q_ROWS * m - NA_Q_ROWS, 0, rows - NA_K_ROWS)) + np.arange(NA_K_ROWS)
        r0 = np.clip(q_row - NA_WIN_H // 2, 0, rows - NA_WIN_H)
        valid_r = (k_row[None, :] >= r0[:, None]) & (k_row[None, :] < r0[:, None] + NA_WIN_H)
        row_off = np.clip(k_row[None, :] - q_row[:, None] + NA_WIN_H - 1, 0, 2 * NA_WIN_H - 2)
        pick_r = jnp.asarray(np.eye(2 * NA_WIN_H - 1)[row_off], F32)
        bias = jnp.einsum("akr,hrc,qpc->haqkp", pick_r, rpb.astype(F32), pick_c, precision=lax.Precision.HIGHEST)
        valid = valid_r[:, None, :, None] & valid_c[None, :, None, :]
        tables.append(jnp.where(valid[None], bias, NEG_INF).reshape(N_HEADS, ROW_TILE, NA_K_TOK))
    return jnp.stack(tables)


def _attend(qe, keys, vals, biases):
    dn = (((1,), (1,)), ((), ()))
    scores = []
    for kk, bias in zip(keys, biases):
        s = lax.dot_general(qe, kk, dn, preferred_element_type=F32)
        scores.append(s if bias is None else s + bias)
    m = scores[0].max(axis=-1, keepdims=True)
    for s in scores[1:]:
        m = jnp.maximum(m, s.max(axis=-1, keepdims=True))
    num, den = None, None
    for s, vv in zip(scores, vals):
        p = jnp.exp(s - m)
        l = p.sum(axis=-1, keepdims=True)
        o = jnp.dot(p.astype(BF16), vv, preferred_element_type=F32)
        num = o if num is None else num + o
        den = l if den is None else den + l
    return num / den


def _na_kernel(q_ref, k_ref, v_ref, g_ref, bias_ref, o_ref, *, n_ctx, rows):
    j = pl.program_id(2)
    lane = lax.broadcasted_iota(jnp.int32, (1, LANES), 1)
    in_head = (lane < HEAD_DIM, lane >= HEAD_DIM)
    q = q_ref[0] * (HEAD_DIM ** -0.5)
    kc = k_ref[0, 0:n_ctx, :].astype(BF16)
    vc = v_ref[0, 0:n_ctx, :].astype(BF16)

    def heads(q):
        return [jnp.where(in_head[e], q, 0.0).astype(BF16) for e in range(2)]

    def finish(o0, o1):
        o = jnp.where(in_head[0], o0, o1)
        o_ref[0] = (o * _silu(g_ref[0])).astype(o_ref.dtype)

    @pl.when(j == 0)
    def _():
        finish(*[_attend(qe, [kc], [vc], [None]) for qe in heads(q)])

    @pl.when(j > 0)
    def _():
        k_row = jnp.clip(NA_Q_ROWS * (j - 1) - NA_Q_ROWS, 0, rows - NA_K_ROWS)
        start = pl.multiple_of(n_ctx + k_row * GRID_W, GRID_W)
        kw = k_ref[0, pl.ds(start, NA_K_TOK), :].astype(BF16)
        vw = v_ref[0, pl.ds(start, NA_K_TOK), :].astype(BF16)
        finish(*[_attend(qe, [kw, kc], [vw, vc], [bias_ref[0, e], None])
                 for e, qe in enumerate(heads(q))])


def _na_attention(p_main, bias_tables, n_ctx):
    B, R, _ = p_main.shape
    rows = (R - n_ctx) // GRID_W
    n_blocks = rows // NA_Q_ROWS
    pairs = W_BRANCH // LANES
    assert n_ctx == ROW_TILE and rows >= NA_K_ROWS and rows % NA_Q_ROWS == 0

    def col(c):
        return lambda b, hp, j: (b, 0, c * pairs + hp)

    def bias_idx(b, hp, j):
        return (jnp.where(j <= 1, 0, jnp.where(j == n_blocks, 2, 1)), hp, 0, 0)

    return pl.pallas_call(
        functools.partial(_na_kernel, n_ctx=n_ctx, rows=rows),
        grid=(B, pairs, n_blocks + 1),
        in_specs=[pl.BlockSpec((1, ROW_TILE, LANES), lambda b, hp, j: (b, j, COL_Q * pairs + hp)),
                  pl.BlockSpec((1, R, LANES), col(COL_K)),
                  pl.BlockSpec((1, R, LANES), col(COL_V)),
                  pl.BlockSpec((1, ROW_TILE, LANES), lambda b, hp, j: (b, j, COL_NA_GATE * pairs + hp)),
                  pl.BlockSpec((1, 2, ROW_TILE, NA_K_TOK), bias_idx)],
        out_specs=pl.BlockSpec((1, ROW_TILE, LANES), lambda b, hp, j: (b, j, hp)),
        out_shape=jax.ShapeDtypeStruct((B, R, W_BRANCH), BF16),
        compiler_params=_params("parallel", "parallel", "arbitrary"),
        name="neighbourhood_attention",
    )(p_main, p_main, p_main, p_main, bias_tables)


def _pool_kernel(u_ref, g_ref, w_ref, sc_ref, o_ref, pad_ref, *, n_ctx, n_lat):
    grp = pl.program_id(1)
    w = w_ref[0]
    scale = sc_ref[...]

    def run(win):
        half = win // 2
        for seq_start, seq_len in ((0, n_ctx), (n_ctx, n_lat)):
            zeros = jnp.zeros((POOL_HALO, POOL_GROUP_DIM), F32)
            pad_ref[0:POOL_HALO, :] = zeros
            pad_ref[POOL_HALO:POOL_HALO + seq_len, :] = u_ref[0, seq_start:seq_start + seq_len, :]
            pad_ref[POOL_HALO + seq_len:2 * POOL_HALO + seq_len, :] = zeros

            def chunk(c, carry):
                base = pl.multiple_of(c * ROW_TILE, ROW_TILE)
                x = pad_ref[pl.ds(base, ROW_TILE + 2 * POOL_HALO), :]
                acc = x[POOL_HALO - half:POOL_HALO - half + ROW_TILE]
                for o in range(-half + 1, half):
                    acc = acc + x[POOL_HALO + o:POOL_HALO + o + ROW_TILE]
                t = base + lax.broadcasted_iota(jnp.int32, (ROW_TILE, 1), 0)
                cnt = jnp.minimum(t + half, seq_len) - jnp.maximum(t - half, 0)
                diff = acc / cnt.astype(F32) - x[POOL_HALO:POOL_HALO + ROW_TILE]
                y = jnp.dot(diff.astype(BF16), w, preferred_element_type=F32) * scale
                rows = pl.ds(seq_start + base, ROW_TILE)
                o_ref[0, rows, :] = (y * _silu(g_ref[0, rows, :])).astype(o_ref.dtype)
                return carry

            lax.fori_loop(0, seq_len // ROW_TILE, chunk, 0)

    for gi, win in enumerate(POOL_WINDOWS):
        pl.when(grp == gi)(functools.partial(run, win))


def _pool(p_main, pool_w, pool_scale, n_ctx):
    B, R, _ = p_main.shape
    groups = len(POOL_WINDOWS)
    return pl.pallas_call(
        functools.partial(_pool_kernel, n_ctx=n_ctx, n_lat=R - n_ctx),
        grid=(B, groups),
        in_specs=[pl.BlockSpec((1, R, POOL_GROUP_DIM), lambda b, g: (b, 0, COL_POOL_U * groups + g)),
                  pl.BlockSpec((1, R, POOL_GROUP_DIM), lambda b, g: (b, 0, COL_POOL_GATE * groups + g)),
                  pl.BlockSpec((1, POOL_GROUP_DIM, POOL_GROUP_DIM), lambda b, g: (g, 0, 0)),
                  pl.BlockSpec((1, POOL_GROUP_DIM), lambda b, g: (0, g))],
        out_specs=pl.BlockSpec((1, R, POOL_GROUP_DIM), lambda b, g: (b, 0, g)),
        out_shape=jax.ShapeDtypeStruct((B, R, W_BRANCH), BF16),
        scratch_shapes=[pltpu.VMEM((R - n_ctx + 2 * POOL_HALO, POOL_GROUP_DIM), F32)],
        compiler_params=_params("parallel", "arbitrary"),
        name="multiscale_pool",
    )(p_main, p_main, pool_w, pool_scale)


def _rwkv_feat_kernel(r_ref, rp_ref, rn_ref, k_ref, kp_ref, kn_ref, v_ref, vp_ref, vn_ref, lora_ref,
                      par_ref, w2_ref, a2_ref,
                      ro_ref, vo_ref, ao_ref, wf_ref, kf_ref, bf_ref, wb_ref, kb_ref, bb_ref, bonus_ref,
                      *, n_ctx_tiles, n_tiles):
    i = pl.program_id(1)
    first = (i == 0) | (i == n_ctx_tiles)
    last = (i == n_ctx_tiles - 1) | (i == n_tiles - 1)
    row = lax.broadcasted_iota(jnp.int32, (ROW_TILE, 1), 0)
    par = par_ref[...]

    def prm(p):
        return par[p:p + 1, :]

    def mix(z_ref, prev_ref, next_ref, mu):
        z = z_ref[0]
        prev = jnp.where(first, 0.0, prev_ref[0, 7:8, :])
        nxt = jnp.where(last, 0.0, next_ref[0, 0:1, :])
        z_prev = jnp.where(row == 0, prev, pltpu.roll(z, 1, 0))
        z_next = jnp.where(row == ROW_TILE - 1, nxt, pltpu.roll(z, ROW_TILE - 1, 0))
        return z + mu * (0.5 * (z_prev + z_next) - z)

    r = mix(r_ref, rp_ref, rn_ref, prm(P_MU_R))
    k = mix(k_ref, kp_ref, kn_ref, prm(P_MU_K))
    v = mix(v_ref, vp_ref, vn_ref, prm(P_MU_V))
    ro_ref[0] = r
    vo_ref[0] = v

    kk = k * prm(P_K_K)
    kk = kk / jnp.maximum(jnp.sqrt(_head_sum(kk * kk)), 1e-12)
    ao_ref[0] = -kk

    lora = lora_ref[0]
    lane = lax.broadcasted_iota(jnp.int32, (1, LANES), 1)
    lora = jnp.where(lane < RWKV_LORA, jnp.tanh(lora), lora).astype(BF16)
    k_sum = None
    for d, (w_out, k_out, b_out) in enumerate(((wf_ref, kf_ref, bf_ref), (wb_ref, kb_ref, bb_ref))):
        x = prm(P_W0_F + d) + jnp.dot(lora, w2_ref[d], preferred_element_type=F32)
        w_log = -(jnp.maximum(-x, 0.0) + jnp.log1p(jnp.exp(-jnp.abs(x)))) - 0.5
        w_out[0] = jnp.exp(-jnp.exp(w_log))
        a = _sigmoid(prm(P_A0_F + d) + jnp.dot(lora, a2_ref[d], preferred_element_type=F32))
        k_d = k * (1.0 + (a - 1.0) * prm(P_K_A))
        k_out[0] = k_d
        b_out[0] = kk * a
        k_sum = k_d if k_sum is None else k_sum + k_d
    bonus_ref[0] = _head_sum(r * k_sum * prm(P_R_K)) * v


def _rwkv_features(p_main, p_lora, par, w2, a2, n_ctx):
    B, R, _ = p_main.shape
    n_tiles = R // ROW_TILE
    sub = ROW_TILE // 8

    def main(c):
        return pl.BlockSpec((1, ROW_TILE, W_BRANCH), lambda b, i: (b, i, c))

    def prev(c):
        return pl.BlockSpec((1, 8, W_BRANCH), lambda b, i: (b, jnp.maximum(i * sub - 1, 0), c))

    def nxt(c):
        return pl.BlockSpec((1, 8, W_BRANCH), lambda b, i: (b, jnp.minimum((i + 1) * sub, n_tiles * sub - 1), c))

    in_specs = []
    for c in (COL_RW_R, COL_RW_K, COL_RW_V):
        in_specs += [main(c), prev(c), nxt(c)]
    in_specs += [pl.BlockSpec((1, ROW_TILE, LANES), lambda b, i: (b, i, 0)),
                 pl.BlockSpec((P_ROWS, W_BRANCH), lambda b, i: (0, 0)),
                 pl.BlockSpec((2, LANES, W_BRANCH), lambda b, i: (0, 0, 0)),
                 pl.BlockSpec((2, LANES, W_BRANCH), lambda b, i: (0, 0, 0))]
    out = jax.ShapeDtypeStruct((B, R, W_BRANCH), F32)
    return pl.pallas_call(
        functools.partial(_rwkv_feat_kernel, n_ctx_tiles=n_ctx // ROW_TILE, n_tiles=n_tiles),
        grid=(B, n_tiles),
        in_specs=in_specs,
        out_specs=[pl.BlockSpec((1, ROW_TILE, W_BRANCH), lambda b, i: (b, i, 0))] * 10,
        out_shape=[out] * 10,
        compiler_params=_params("parallel", "parallel"),
        name="rwkv_features",
    )(*([p_main] * 9), p_lora, par, w2, a2)


def _scan_kernel(w_ref, k_ref, b_ref, a_ref, r_ref, v_ref, y_ref, s_ref):
    n = HEAD_DIM

    @pl.when(pl.program_id(0) == 0)
    def _():
        s_ref[...] = jnp.zeros_like(s_ref)

    def row(ref, j, t):
        return ref[0, pl.ds(j * SCAN_STEPS + t, 1), :]

    zero = jnp.zeros((n, s_ref.shape[2]), F32)

    def first_sa(jb, sa):
        for jj in range(SCAN_J_UNROLL):
            j = jb * SCAN_J_UNROLL + jj
            sa = sa + s_ref[j] * row(a_ref, j, 0)
        return sa

    def step(t, sa):
        vt = v_ref[0, pl.ds(t, n, stride=SCAN_STEPS), :]
        t_next = jnp.minimum(t + 1, SCAN_STEPS - 1)

        def columns(jb, carry):
            y, sa_next = carry
            for jj in range(SCAN_J_UNROLL):
                j = jb * SCAN_J_UNROLL + jj
                sj = s_ref[j] * row(w_ref, j, t) + sa * row(b_ref, j, t) + vt * row(k_ref, j, t)
                s_ref[j] = sj
                y = y + sj * row(r_ref, j, t)
                sa_next = sa_next + sj * row(a_ref, j, t_next)
            return y, sa_next

        y, sa_next = lax.fori_loop(0, n // SCAN_J_UNROLL, columns, (zero, zero))
        y_ref[0, pl.ds(t, n, stride=SCAN_STEPS), :] = y
        return sa_next

    sa0 = lax.fori_loop(0, n // SCAN_J_UNROLL, first_sa, zero)
    lax.fori_loop(0, SCAN_STEPS, step, sa0)


def _wkv_scan(w, k, b, a, r, v):
    n_chunks, rows, chains = w.shape
    spec = pl.BlockSpec((1, rows, chains), lambda s: (s, 0, 0))
    return pl.pallas_call(
        _scan_kernel,
        grid=(n_chunks,),
        in_specs=[spec] * 6,
        out_specs=spec,
        out_shape=jax.ShapeDtypeStruct(w.shape, F32),
        scratch_shapes=[pltpu.VMEM((HEAD_DIM, HEAD_DIM, chains), F32)],
        compiler_params=_params("arbitrary"),
        name="wkv_scan",
    )(w, k, b, a, r, v)


def _flip_rows(x, flip):
    hi, mid, lo = _split3(x)
    return (jnp.dot(flip, hi, preferred_element_type=F32) + jnp.dot(flip, mid, preferred_element_type=F32)
            + jnp.dot(flip, lo, preferred_element_type=F32))


def _mirror_chunk(c, n_ctx_chunks, n_chunks):
    return jnp.where(c < n_ctx_chunks, n_ctx_chunks - 1 - c, n_ctx_chunks + n_chunks - 1 - c)


def _to_scan_kernel(zf_ref, zb_ref, flip_ref, o_ref, t_ref):
    nb = zf_ref.shape[0]
    for b in range(nb):
        t_ref[b] = zf_ref[b].T
        t_ref[nb + b] = _flip_rows(zb_ref[b], flip_ref[...]).T
    sub = RELAYOUT_ROWS // SCAN_STEPS

    def body(n, carry):
        rows = pl.ds(pl.multiple_of(n * N_HEADS, N_HEADS), N_HEADS)
        slabs = [t_ref[g, rows, :] for g in range(2 * nb)]
        tile = jnp.concatenate(slabs, axis=0).T
        for q in range(sub):
            o_ref[q, pl.ds(pl.multiple_of(n * SCAN_STEPS, SCAN_STEPS), SCAN_STEPS), :] = (
                tile[q * SCAN_STEPS:(q + 1) * SCAN_STEPS])
        return carry

    lax.fori_loop(0, HEAD_DIM, body, 0, unroll=RELAYOUT_UNROLL)


def _to_scan(z_fwd, z_bwd, flip, n_ctx):
    B, R, _ = z_fwd.shape
    n_chunks = R // RELAYOUT_ROWS
    n_ctx_chunks = n_ctx // RELAYOUT_ROWS
    sub = RELAYOUT_ROWS // SCAN_STEPS
    chains = 2 * B * N_HEADS
    return pl.pallas_call(
        _to_scan_kernel,
        grid=(n_chunks,),
        in_specs=[pl.BlockSpec((B, RELAYOUT_ROWS, W_BRANCH), lambda c: (0, c, 0)),
                  pl.BlockSpec((B, RELAYOUT_ROWS, W_BRANCH),
                               lambda c: (0, _mirror_chunk(c, n_ctx_chunks, n_chunks), 0)),
                  pl.BlockSpec((RELAYOUT_ROWS, RELAYOUT_ROWS), lambda c: (0, 0))],
        out_specs=pl.BlockSpec((sub, HEAD_DIM * SCAN_STEPS, chains), lambda c: (c, 0, 0)),
        out_shape=jax.ShapeDtypeStruct((R // SCAN_STEPS, HEAD_DIM * SCAN_STEPS, chains), F32),
        scratch_shapes=[pltpu.VMEM((2 * B, W_BRANCH, RELAYOUT_ROWS), F32)],
        compiler_params=_params("parallel"),
        name="to_scan_layout",
    )(z_fwd, z_bwd, flip)


def _from_scan_kernel(y_ref, flip_ref, yf_ref, yb_ref, t_ref):
    nb = yf_ref.shape[0]
    sub = RELAYOUT_ROWS // SCAN_STEPS

    def body(n, carry):
        base = pl.multiple_of(n * SCAN_STEPS, SCAN_STEPS)
        tile = jnp.concatenate([y_ref[q, pl.ds(base, SCAN_STEPS), :] for q in range(sub)], axis=0)
        tile = tile.T
        rows = pl.ds(pl.multiple_of(n * N_HEADS, N_HEADS), N_HEADS)
        for g in range(2 * nb):
            t_ref[g, rows, :] = tile[g * N_HEADS:(g + 1) * N_HEADS]
        return carry

    lax.fori_loop(0, HEAD_DIM, body, 0, unroll=RELAYOUT_UNROLL)
    for b in range(nb):
        yf_ref[b] = t_ref[b].T
        yb_ref[b] = _flip_rows(t_ref[nb + b].T, flip_ref[...])


def _from_scan(y, flip, n_batch, n_ctx):
    R = y.shape[0] * SCAN_STEPS
    n_chunks = R // RELAYOUT_ROWS
    n_ctx_chunks = n_ctx // RELAYOUT_ROWS
    sub = RELAYOUT_ROWS // SCAN_STEPS
    out = jax.ShapeDtypeStruct((n_batch, R, W_BRANCH), F32)
    return pl.pallas_call(
        _from_scan_kernel,
        grid=(n_chunks,),
        in_specs=[pl.BlockSpec((sub, HEAD_DIM * SCAN_STEPS, y.shape[2]), lambda c: (c, 0, 0)),
                  pl.BlockSpec((RELAYOUT_ROWS, RELAYOUT_ROWS), lambda c: (0, 0))],
        out_specs=[pl.BlockSpec((n_batch, RELAYOUT_ROWS, W_BRANCH), lambda c: (0, c, 0)),
                   pl.BlockSpec((n_batch, RELAYOUT_ROWS, W_BRANCH),
                                lambda c: (0, _mirror_chunk(c, n_ctx_chunks, n_chunks), 0))],
        out_shape=[out, out],
        scratch_shapes=[pltpu.VMEM((2 * n_batch, W_BRANCH, RELAYOUT_ROWS), F32)],
        compiler_params=_params("parallel"),
        name="from_scan_layout",
    )(y, flip)


def _rwkv_readout_kernel(yf_ref, yb_ref, bonus_ref, gate_ref, gb_ref, o_ref):
    y = yf_ref[0] + yb_ref[0]
    mu = _head_sum(y) * (1.0 / HEAD_DIM)
    yc = y - mu
    var = _head_sum(yc * yc) * (1.0 / HEAD_DIM)
    gb = gb_ref[...]
    out = yc * lax.rsqrt(var + LNX_EPS) * gb[0:1, :] + gb[1:2, :] + bonus_ref[0]
    o_ref[0] = (out * _silu(gate_ref[0])).astype(o_ref.dtype)


def _rwkv_readout(y_fwd, y_bwd, bonus, p_main, lnx_gb):
    B, R, _ = y_fwd.shape
    tile = pl.BlockSpec((1, ROW_TILE, W_BRANCH), lambda b, i: (b, i, 0))
    return pl.pallas_call(
        _rwkv_readout_kernel,
        grid=(B, R // ROW_TILE),
        in_specs=[tile, tile, tile,
                  pl.BlockSpec((1, ROW_TILE, W_BRANCH), lambda b, i: (b, i, COL_RW_GATE)),
                  pl.BlockSpec((8, W_BRANCH), lambda b, i: (0, 0))],
        out_specs=tile,
        out_shape=jax.ShapeDtypeStruct((B, R, W_BRANCH), BF16),
        compiler_params=_params("parallel", "parallel"),
        name="rwkv_readout",
    )(y_fwd, y_bwd, bonus, p_main, lnx_gb)


def _merge_kernel(na_ref, pool_ref, rw_ref, lna_ref, lpool_ref, lrw_ref, w_ref, o_ref):
    acc = None
    for br, (x_ref, l_ref) in enumerate(((na_ref, lna_ref), (pool_ref, lpool_ref), (rw_ref, lrw_ref))):
        t = _sigmoid(l_ref[...]) * jnp.dot(x_ref[...], w_ref[br], preferred_element_type=F32)
        acc = t if acc is None else acc + t
    o_ref[...] = acc.astype(o_ref.dtype)


def _merge(b_na, b_pool, b_rw, p_merge, w_branch):
    M = b_na.shape[0]
    tm, tn = min(_row_tile(M), 512), 1024
    nb = D_MODEL // tn
    x_spec = pl.BlockSpec((tm, W_BRANCH), lambda j, i: (i, 0))

    def logit(br):
        return pl.BlockSpec((tm, tn), lambda j, i: (i, br * nb + j))

    return pl.pallas_call(
        _merge_kernel,
        grid=(nb, M // tm),
        in_specs=[x_spec, x_spec, x_spec, logit(0), logit(1), logit(2),
                  pl.BlockSpec((N_BRANCH, W_BRANCH, tn), lambda j, i: (0, 0, j))],
        out_specs=pl.BlockSpec((tm, tn), lambda j, i: (i, j)),
        out_shape=jax.ShapeDtypeStruct((M, D_MODEL), BF16),
        compiler_params=_params("parallel", "parallel"),
        name="branch_merge",
    )(b_na, b_pool, b_rw, p_merge, p_merge, p_merge, w_branch)


def _out_kernel(m_ref, w_ref, x_ref, mod_ref, fg_ref, o_ref, *, n_ctx_tiles, tile_offset, final):
    gate = _mod_row(mod_ref, pl.program_id(1) + tile_offset, n_ctx_tiles)[:, 2 * D_MODEL:]
    x = x_ref[0] + gate * jnp.dot(m_ref[0], w_ref[...], preferred_element_type=F32)
    o_ref[0] = _rms(x, fg_ref[...]) if final else x


def _out_proj(merged, w_out, x_all, mod, final_g, n_ctx, final):
    B, R, _ = x_all.shape
    off = n_ctx // ROW_TILE if final else 0
    tile = pl.BlockSpec((1, ROW_TILE, D_MODEL), lambda b, i: (b, i + off, 0))
    return pl.pallas_call(
        functools.partial(_out_kernel, n_ctx_tiles=n_ctx // ROW_TILE, tile_offset=off, final=final),
        grid=(B, R // ROW_TILE - off),
        in_specs=[tile,
                  pl.BlockSpec((D_MODEL, D_MODEL), lambda b, i: (0, 0)),
                  tile,
                  pl.BlockSpec((MOD_ROWS, 3 * D_MODEL), lambda b, i: (0, 0)),
                  pl.BlockSpec((1, D_MODEL), lambda b, i: (0, 0))],
        out_specs=pl.BlockSpec((1, ROW_TILE, D_MODEL), lambda b, i: (b, i, 0)),
        out_shape=jax.ShapeDtypeStruct((B, R - off * ROW_TILE, D_MODEL), F32),
        compiler_params=_params("parallel", "parallel"),
        name="out_proj_final" if final else "out_proj",
    )(merged, w_out, x_all, mod, final_g)


def _layer(x_all, mod, n_ctx, final, final_g, norm_g, w_in, na_rpb, pool_w, pool_scale, rw_mu, rw_w0, rw_w2,
           rw_a0, rw_a2, rw_k_k, rw_k_a, rw_r_k, rw_lnx_g, rw_lnx_b, w_branch, w_out):
    B, R, _ = x_all.shape
    rows = (R - n_ctx) // GRID_W

    h = _norm_mod(x_all, norm_g[None], mod, n_ctx).reshape(B * R, D_MODEL)
    lo = N_MAIN + 2 * RWKV_LORA
    n_rw = COL_RW_R * W_BRANCH
    w_rw = _head_major(w_in[:, n_rw:N_MAIN].reshape(D_MODEL, N_MAIN // W_BRANCH - COL_RW_R, W_BRANCH))
    w_main = jnp.concatenate([w_in[:, :n_rw], w_rw.reshape(D_MODEL, N_MAIN - n_rw)], axis=1)
    p_main = _matmul(h, w_main.astype(BF16), F32, "in_proj_main").reshape(B, R, N_MAIN)
    p_lora = _matmul(h, w_in[:, N_MAIN:lo].astype(BF16), F32, "in_proj_lora").reshape(B, R, 2 * RWKV_LORA)
    p_merge = _matmul(h, w_in[:, lo:].astype(BF16), F32, "in_proj_merge")

    b_na = _na_attention(p_main, _na_bias_tables(na_rpb, rows), n_ctx)
    b_pool = _pool(p_main, pool_w.astype(BF16), pool_scale[None], n_ctx)

    par = jnp.zeros((P_ROWS, W_BRANCH), F32)
    par = par.at[P_MU_R:P_MU_V + 1].set(rw_mu).at[P_W0_F:P_W0_B + 1].set(rw_w0).at[P_A0_F:P_A0_B + 1].set(rw_a0)
    par = _head_major(par.at[P_K_K].set(rw_k_k).at[P_K_A].set(rw_k_a).at[P_R_K].set(rw_r_k.reshape(-1)))
    zeros = jnp.zeros_like(rw_w2)
    w2 = _head_major(jnp.concatenate([rw_w2, zeros], axis=1)).astype(BF16)
    a2 = _head_major(jnp.concatenate([zeros, rw_a2], axis=1)).astype(BF16)
    r, v, a_in, w_f, k_f, b_f, w_b, k_b, b_b, bonus = _rwkv_features(p_main, p_lora, par, w2, a2, n_ctx)
    flip = jnp.asarray(np.eye(RELAYOUT_ROWS)[::-1], BF16)
    y = _wkv_scan(*[_to_scan(zf, zb, flip, n_ctx)
                    for zf, zb in ((w_f, w_b), (k_f, k_b), (b_f, b_b), (a_in, a_in), (r, r), (v, v))])
    y_fwd, y_bwd = _from_scan(y, flip, B, n_ctx)
    lnx_gb = _head_major(jnp.zeros((8, W_BRANCH), F32).at[0].set(rw_lnx_g).at[1].set(rw_lnx_b))
    b_rw = _rwkv_readout(y_fwd, y_bwd, bonus, p_main, lnx_gb)

    def flat(z):
        return z.reshape(B * R, W_BRANCH)

    w_rw_out = _head_major(w_branch[2].T).T
    w_br = jnp.stack([w_branch[0], w_branch[1], w_rw_out]).astype(BF16)
    merged = _merge(flat(b_na), flat(b_pool), flat(b_rw), p_merge, w_br)
    return _out_proj(merged.reshape(B, R, D_MODEL), w_out.astype(BF16), x_all, mod, final_g[None], n_ctx, final)


def kernel(x, c, ctx, c_ctx, norm_g, w_mod, b_mod, w_in, na_rpb, pool_w, pool_scale, rw_mu, rw_w0, rw_w2, rw_a0,
           rw_a2, rw_k_k, rw_k_a, rw_r_k, rw_lnx_g, rw_lnx_b, w_branch, w_out, final_g):
    B, T, _ = x.shape
    n_ctx = ctx.shape[1]
    depth = w_in.shape[0]
    assert B <= CTX_MOD_ROW and n_ctx % ROW_TILE == 0 and T % ROW_TILE == 0
    assert ROW_TILE % RELAYOUT_ROWS == 0 and RELAYOUT_ROWS % SCAN_STEPS == 0

    cond = jnp.zeros((MOD_ROWS, D_MODEL), F32).at[:B].set(c).at[CTX_MOD_ROW].set(c_ctx)
    mods = _modulation(cond, w_mod.astype(BF16), b_mod[:, None, :])
    x_all = jnp.concatenate([ctx, x], axis=1)
    for layer in range(depth):
        x_all = _layer(x_all, mods[layer], n_ctx, layer == depth - 1, final_g, norm_g[layer], w_in[layer],
                       na_rpb[layer], pool_w[layer], pool_scale[layer], rw_mu[layer], rw_w0[layer], rw_w2[layer],
                       rw_a0[layer], rw_a2[layer], rw_k_k[layer], rw_k_a[layer], rw_r_k[layer], rw_lnx_g[layer],
                       rw_lnx_b[layer], w_branch[layer], w_out[layer])
    return x_all
```

```python
import functools

import numpy as np
import jax
import jax.numpy as jnp
from jax import lax
from jax.experimental import pallas as pl
from jax.experimental.pallas import tpu as pltpu

F32 = jnp.float32
BF16 = jnp.bfloat16

D_MODEL = 2048
W_BRANCH = D_MODEL // 2
N_BRANCH = 3
N_HEADS = 16
HEAD_DIM = 64
GRID_W = 64
NA_WIN_H = 8
NA_WIN_W = 16
POOL_WINDOWS = (2, 4, 8, 16)
POOL_GROUP_DIM = W_BRANCH // len(POOL_WINDOWS)
POOL_HALO = 8
RWKV_LORA = 64
RMS_EPS = 1e-6
LNX_EPS = 64e-5
NEG_INF = -1e30

LANES = 128
ROW_TILE = 256
NA_Q_ROWS = ROW_TILE // GRID_W
NA_K_ROWS = NA_Q_ROWS + NA_WIN_H
NA_K_TOK = NA_K_ROWS * GRID_W
CTX_MOD_ROW = 4
MOD_ROWS = 8
SCAN_STEPS = 32
SCAN_J_UNROLL = 32
RELAYOUT_UNROLL = 8
RELAYOUT_ROWS = 128
VMEM_LIMIT = 56 << 20

COL_Q, COL_K, COL_V, COL_NA_GATE, COL_POOL_U, COL_POOL_GATE, COL_RW_R, COL_RW_K, COL_RW_V, COL_RW_GATE = range(10)
N_MAIN = 10 * W_BRANCH

P_MU_R, P_MU_K, P_MU_V, P_W0_F, P_W0_B, P_A0_F, P_A0_B, P_K_K, P_K_A, P_R_K = range(10)
P_ROWS = 16


def _params(*sem):
    return pltpu.CompilerParams(dimension_semantics=sem, vmem_limit_bytes=VMEM_LIMIT)


def _sigmoid(x):
    return 1.0 / (1.0 + jnp.exp(-x))


def _silu(x):
    return x * _sigmoid(x)


def _split3(x):
    hi = x.astype(BF16)
    r1 = x - hi.astype(F32)
    mid = r1.astype(BF16)
    lo = (r1 - mid.astype(F32)).astype(BF16)
    return hi, mid, lo


def _head_major(z):
    lead = z.shape[:-1]
    return z.reshape(lead + (N_HEADS, HEAD_DIM)).swapaxes(-1, -2).reshape(lead + (W_BRANCH,))


def _head_sum(x):
    n_tiles = W_BRANCH // LANES
    part = x[:, :LANES]
    for c in range(1, n_tiles):
        part = part + x[:, c * LANES:(c + 1) * LANES]
    shift = N_HEADS
    while shift < LANES:
        part = part + pltpu.roll(part, shift, 1)
        shift *= 2
    return jnp.concatenate([part] * n_tiles, axis=1)


def _mod_kernel(cond_ref, w_ref, b_ref, o_ref):
    s = _silu(cond_ref[...])
    o_ref[0] = jnp.dot(s.astype(BF16), w_ref[0], preferred_element_type=F32) + b_ref[0]


def _modulation(cond, w_mod, b_mod):
    n_layers = w_mod.shape[0]
    tn = 3 * D_MODEL // 4
    return pl.pallas_call(
        _mod_kernel,
        grid=(n_layers, 4),
        in_specs=[pl.BlockSpec((MOD_ROWS, D_MODEL), lambda l, j: (0, 0)),
                  pl.BlockSpec((1, D_MODEL, tn), lambda l, j: (l, 0, j)),
                  pl.BlockSpec((1, 1, tn), lambda l, j: (l, 0, j))],
        out_specs=pl.BlockSpec((1, MOD_ROWS, tn), lambda l, j: (l, 0, j)),
        out_shape=jax.ShapeDtypeStruct((n_layers, MOD_ROWS, 3 * D_MODEL), F32),
        compiler_params=_params("arbitrary", "arbitrary"),
        name="adaln_modulation",
    )(cond, w_mod, b_mod)


def _mod_row(mod_ref, tile, n_ctx_tiles):
    row = jnp.where(tile < n_ctx_tiles, CTX_MOD_ROW, pl.program_id(0))
    return mod_ref[pl.ds(row, 1), :]


def _rms(x, g):
    return x * lax.rsqrt(jnp.mean(x * x, axis=-1, keepdims=True) + RMS_EPS) * g


def _norm_mod_kernel(x_ref, g_ref, mod_ref, h_ref, *, n_ctx_tiles):
    m = _mod_row(mod_ref, pl.program_id(1), n_ctx_tiles)
    shift = m[:, :D_MODEL]
    scale = m[:, D_MODEL:2 * D_MODEL]
    h_ref[0] = (_rms(x_ref[0], g_ref[...]) * (1.0 + scale) + shift).astype(BF16)


def _norm_mod(x_all, norm_g, mod, n_ctx):
    B, R, _ = x_all.shape
    return pl.pallas_call(
        functools.partial(_norm_mod_kernel, n_ctx_tiles=n_ctx // ROW_TILE),
        grid=(B, R // ROW_TILE),
        in_specs=[pl.BlockSpec((1, ROW_TILE, D_MODEL), lambda b, i: (b, i, 0)),
                  pl.BlockSpec((1, D_MODEL), lambda b, i: (0, 0)),
                  pl.BlockSpec((MOD_ROWS, 3 * D_MODEL), lambda b, i: (0, 0))],
        out_specs=pl.BlockSpec((1, ROW_TILE, D_MODEL), lambda b, i: (b, i, 0)),
        out_shape=jax.ShapeDtypeStruct((B, R, D_MODEL), BF16),
        compiler_params=_params("parallel", "parallel"),
        name="norm_modulate",
    )(x_all, norm_g, mod)


def _mm_kernel(a_ref, w_ref, o_ref):
    o_ref[...] = jnp.dot(a_ref[...], w_ref[...], preferred_element_type=F32).astype(o_ref.dtype)


def _row_tile(m):
    for t in (1024, 512, 256):
        if m % t == 0:
            return t
    raise ValueError(f"row count {m} is not a multiple of {ROW_TILE}")


def _matmul(a, w, out_dtype, name):
    M, K = a.shape
    N = w.shape[1]
    tm = _row_tile(M)
    tn = min(N, 1024)
    return pl.pallas_call(
        _mm_kernel,
        grid=(N // tn, M // tm),
        in_specs=[pl.BlockSpec((tm, K), lambda j, i: (i, 0)),
                  pl.BlockSpec((K, tn), lambda j, i: (0, j))],
        out_specs=pl.BlockSpec((tm, tn), lambda j, i: (i, j)),
        out_shape=jax.ShapeDtypeStruct((M, N), out_dtype),
        compiler_params=_params("parallel", "parallel"),
        name=name,
    )(a, w)


def _na_bias_tables(rpb, rows):
    n_blocks = rows // NA_Q_ROWS
    col = np.arange(GRID_W)
    c0 = np.clip(col - NA_WIN_W // 2, 0, GRID_W - NA_WIN_W)
    valid_c = (col[None, :] >= c0[:, None]) & (col[None, :] < c0[:, None] + NA_WIN_W)
    col_off = np.clip(col[None, :] - col[:, None] + NA_WIN_W - 1, 0, 2 * NA_WIN_W - 2)
    pick_c = jnp.asarray(np.eye(2 * NA_WIN_W - 1)[col_off], F32)
    tables = []
    for m in (0, 1, n_blocks - 1):
        q_row = NA_Q_ROWS * m + np.arange(NA_Q_ROWS)
        k_row = int(np.clip(NA_Q_ROWS * m - NA_Q_ROWS, 0, rows - NA_K_ROWS)) + np.arange(NA_K_ROWS)
        r0 = np.clip(q_row - NA_WIN_H // 2, 0, rows - NA_WIN_H)
        valid_r = (k_row[None, :] >= r0[:, None]) & (k_row[None, :] < r0[:, None] + NA_WIN_H)
        row_off = np.clip(k_row[None, :] - q_row[:, None] + NA_WIN_H - 1, 0, 2 * NA_WIN_H - 2)
        pick_r = jnp.asarray(np.eye(2 * NA_WIN_H - 1)[row_off], F32)
        bias = jnp.einsum("akr,hrc,qpc->haqkp", pick_r, rpb.astype(F32), pick_c, precision=lax.Precision.HIGHEST)
        valid = valid_r[:, None, :, None] & valid_c[None, :, None, :]
        tables.append(jnp.where(valid[None], bias, NEG_INF).reshape(N_HEADS, ROW_TILE, NA_K_TOK))
    return jnp.stack(tables)


def _attend(qe, keys, vals, biases):
    dn = (((1,), (1,)), ((), ()))
    scores = []
    for kk, bias in zip(keys, biases):
        s = lax.dot_general(qe, kk, dn, preferred_element_type=F32)
        scores.append(s if bias is None else s + bias)
    m = scores[0].max(axis=-1, keepdims=True)
    for s in scores[1:]:
        m = jnp.maximum(m, s.max(axis=-1, keepdims=True))
    num, den = None, None
    for s, vv in zip(scores, vals):
        p = jnp.exp(s - m)
        l = p.sum(axis=-1, keepdims=True)
        o = jnp.dot(p.astype(BF16), vv, preferred_element_type=F32)
        num = o if num is None else num + o
        den = l if den is None else den + l
    return num / den


def _na_kernel(q_ref, k_ref, v_ref, g_ref, bias_ref, o_ref, *, n_ctx, rows):
    j = pl.program_id(2)
    lane = lax.broadcasted_iota(jnp.int32, (1, LANES), 1)
    in_head = (lane < HEAD_DIM, lane >= HEAD_DIM)
    q = q_ref[0] * (HEAD_DIM ** -0.5)
    kc = k_ref[0, 0:n_ctx, :].astype(BF16)
    vc = v_ref[0, 0:n_ctx, :].astype(BF16)

    def heads(q):
        return [jnp.where(in_head[e], q, 0.0).astype(BF16) for e in range(2)]

    def finish(o0, o1):
        o = jnp.where(in_head[0], o0, o1)
        o_ref[0] = (o * _silu(g_ref[0])).astype(o_ref.dtype)

    @pl.when(j == 0)
    def _():
        finish(*[_attend(qe, [kc], [vc], [None]) for qe in heads(q)])

    @pl.when(j > 0)
    def _():
        k_row = jnp.clip(NA_Q_ROWS * (j - 1) - NA_Q_ROWS, 0, rows - NA_K_ROWS)
        start = pl.multiple_of(n_ctx + k_row * GRID_W, GRID_W)
        kw = k_ref[0, pl.ds(start, NA_K_TOK), :].astype(BF16)
        vw = v_ref[0, pl.ds(start, NA_K_TOK), :].astype(BF16)
        finish(*[_attend(qe, [kw, kc], [vw, vc], [bias_ref[0, e], None])
                 for e, qe in enumerate(heads(q))])


def _na_attention(p_main, bias_tables, n_ctx):
    B, R, _ = p_main.shape
    rows = (R - n_ctx) // GRID_W
    n_blocks = rows // NA_Q_ROWS
    pairs = W_BRANCH // LANES
    assert n_ctx == ROW_TILE and rows >= NA_K_ROWS and rows % NA_Q_ROWS == 0

    def col(c):
        return lambda b, hp, j: (b, 0, c * pairs + hp)

    def bias_idx(b, hp, j):
        return (jnp.where(j <= 1, 0, jnp.where(j == n_blocks, 2, 1)), hp, 0, 0)

    return pl.pallas_call(
        functools.partial(_na_kernel, n_ctx=n_ctx, rows=rows),
        grid=(B, pairs, n_blocks + 1),
        in_specs=[pl.BlockSpec((1, ROW_TILE, LANES), lambda b, hp, j: (b, j, COL_Q * pairs + hp)),
                  pl.BlockSpec((1, R, LANES), col(COL_K)),
                  pl.BlockSpec((1, R, LANES), col(COL_V)),
                  pl.BlockSpec((1, ROW_TILE, LANES), lambda b, hp, j: (b, j, COL_NA_GATE * pairs + hp)),
                  pl.BlockSpec((1, 2, ROW_TILE, NA_K_TOK), bias_idx)],
        out_specs=pl.BlockSpec((1, ROW_TILE, LANES), lambda b, hp, j: (b, j, hp)),
        out_shape=jax.ShapeDtypeStruct((B, R, W_BRANCH), BF16),
        compiler_params=_params("parallel", "parallel", "arbitrary"),
        name="neighbourhood_attention",
    )(p_main, p_main, p_main, p_main, bias_tables)


def _pool_kernel(u_ref, g_ref, w_ref, sc_ref, o_ref, pad_ref, *, n_ctx, n_lat):
    grp = pl.program_id(1)
    w = w_ref[0]
    scale = sc_ref[...]

    def run(win):
        half = win // 2
        for seq_start, seq_len in ((0, n_ctx), (n_ctx, n_lat)):
            zeros = jnp.zeros((POOL_HALO, POOL_GROUP_DIM), F32)
            pad_ref[0:POOL_HALO, :] = zeros
            pad_ref[POOL_HALO:POOL_HALO + seq_len, :] = u_ref[0, seq_start:seq_start + seq_len, :]
            pad_ref[POOL_HALO + seq_len:2 * POOL_HALO + seq_len, :] = zeros

            def chunk(c, carry):
                base = pl.multiple_of(c * ROW_TILE, ROW_TILE)
                x = pad_ref[pl.ds(base, ROW_TILE + 2 * POOL_HALO), :]
                acc = x[POOL_HALO - half:POOL_HALO - half + ROW_TILE]
                for o in range(-half + 1, half):
                    acc = acc + x[POOL_HALO + o:POOL_HALO + o + ROW_TILE]
                t = base + lax.broadcasted_iota(jnp.int32, (ROW_TILE, 1), 0)
                cnt = jnp.minimum(t + half, seq_len) - jnp.maximum(t - half, 0)
                diff = acc / cnt.astype(F32) - x[POOL_HALO:POOL_HALO + ROW_TILE]
                y = jnp.dot(diff.astype(BF16), w, preferred_element_type=F32) * scale
                rows = pl.ds(seq_start + base, ROW_TILE)
                o_ref[0, rows, :] = (y * _silu(g_ref[0, rows, :])).astype(o_ref.dtype)
                return carry

            lax.fori_loop(0, seq_len // ROW_TILE, chunk, 0)

    for gi, win in enumerate(POOL_WINDOWS):
        pl.when(grp == gi)(functools.partial(run, win))


def _pool(p_main, pool_w, pool_scale, n_ctx):
    B, R, _ = p_main.shape
    groups = len(POOL_WINDOWS)
    return pl.pallas_call(
        functools.partial(_pool_kernel, n_ctx=n_ctx, n_lat=R - n_ctx),
        grid=(B, groups),
        in_specs=[pl.BlockSpec((1, R, POOL_GROUP_DIM), lambda b, g: (b, 0, COL_POOL_U * groups + g)),
                  pl.BlockSpec((1, R, POOL_GROUP_DIM), lambda b, g: (b, 0, COL_POOL_GATE * groups + g)),
                  pl.BlockSpec((1, POOL_GROUP_DIM, POOL_GROUP_DIM), lambda b, g: (g, 0, 0)),
                  pl.BlockSpec((1, POOL_GROUP_DIM), lambda b, g: (0, g))],
        out_specs=pl.BlockSpec((1, R, POOL_GROUP_DIM), lambda b, g: (b, 0, g)),
        out_shape=jax.ShapeDtypeStruct((B, R, W_BRANCH), BF16),
        scratch_shapes=[pltpu.VMEM((R - n_ctx + 2 * POOL_HALO, POOL_GROUP_DIM), F32)],
        compiler_params=_params("parallel", "arbitrary"),
        name="multiscale_pool",
    )(p_main, p_main, pool_w, pool_scale)


def _rwkv_feat_kernel(r_ref, rp_ref, rn_ref, k_ref, kp_ref, kn_ref, v_ref, vp_ref, vn_ref, lora_ref,
                      par_ref, w2_ref, a2_ref, tri_ref,
                      vo_ref, pf_ref, kf_ref, bf_ref, af_ref, rf_ref, pb_ref, kb_ref, bb_ref, ab_ref, rb_ref, bonus_ref,
                      *, n_ctx_tiles, n_tiles):
    i = pl.program_id(1)
    first = (i == 0) | (i == n_ctx_tiles)
    last = (i == n_ctx_tiles - 1) | (i == n_tiles - 1)
    row = lax.broadcasted_iota(jnp.int32, (ROW_TILE, 1), 0)
    par = par_ref[...]

    def prm(p):
        return par[p:p + 1, :]

    def mix(z_ref, prev_ref, next_ref, mu):
        z = z_ref[0]
        prev = jnp.where(first, 0.0, prev_ref[0, 7:8, :])
        nxt = jnp.where(last, 0.0, next_ref[0, 0:1, :])
        z_prev = jnp.where(row == 0, prev, pltpu.roll(z, 1, 0))
        z_next = jnp.where(row == ROW_TILE - 1, nxt, pltpu.roll(z, ROW_TILE - 1, 0))
        return z + mu * (0.5 * (z_prev + z_next) - z)

    r = mix(r_ref, rp_ref, rn_ref, prm(P_MU_R))
    k = mix(k_ref, kp_ref, kn_ref, prm(P_MU_K))
    v = mix(v_ref, vp_ref, vn_ref, prm(P_MU_V))
    vo_ref[0] = v

    kk = k * prm(P_K_K)
    kk = kk * jnp.minimum(lax.rsqrt(_head_sum(kk * kk)), 1e12)

    lora = lora_ref[0]
    lane = lax.broadcasted_iota(jnp.int32, (1, LANES), 1)
    lora = jnp.where(lane < RWKV_LORA, jnp.tanh(lora), lora).astype(BF16)
    k_sum = None
    outs = ((pf_ref, kf_ref, bf_ref, af_ref, rf_ref), (pb_ref, kb_ref, bb_ref, ab_ref, rb_ref))
    for d, (p_out, k_out, b_out, a_out, r_out) in enumerate(outs):
        x = prm(P_W0_F + d) + jnp.dot(lora, w2_ref[d], preferred_element_type=F32)
        w_log = -(jnp.maximum(-x, 0.0) + jnp.log(1.0 + jnp.exp(-jnp.abs(x)))) - 0.5
        neg_log_w = jnp.exp(w_log)
        hi, mid, lo = _split3(neg_log_w)
        tri = tri_ref[d]
        cs = (jnp.dot(tri, hi, preferred_element_type=F32) + jnp.dot(tri, mid, preferred_element_type=F32)
              + jnp.dot(tri, lo, preferred_element_type=F32))
        grow = jnp.exp(cs)
        shrink = jnp.exp(-cs)
        a = 0.5 + 0.5 * jnp.tanh(0.5 * (prm(P_A0_F + d) + jnp.dot(lora, a2_ref[d], preferred_element_type=F32)))
        k_d = k * (1.0 + (a - 1.0) * prm(P_K_A))
        p_out[0] = shrink
        k_out[0] = k_d * grow
        b_out[0] = kk * a * grow
        a_out[0] = -kk * jnp.exp(neg_log_w - cs)
        r_out[0] = r * shrink
        k_sum = k_d if k_sum is None else k_sum + k_d
    bonus_ref[0] = _head_sum(r * k_sum * prm(P_R_K)) * v


def _rwkv_features(p_main, p_lora, par, w2, a2, n_ctx):
    B, R, _ = p_main.shape
    n_tiles = R // ROW_TILE
    sub = ROW_TILE // 8

    def main(c):
        return pl.BlockSpec((1, ROW_TILE, W_BRANCH), lambda b, i: (b, i, c))

    def prev(c):
        return pl.BlockSpec((1, 8, W_BRANCH), lambda b, i: (b, jnp.maximum(i * sub - 1, 0), c))

    def nxt(c):
        return pl.BlockSpec((1, 8, W_BRANCH), lambda b, i: (b, jnp.minimum((i + 1) * sub, n_tiles * sub - 1), c))

    in_specs = []
    for c in (COL_RW_R, COL_RW_K, COL_RW_V):
        in_specs += [main(c), prev(c), nxt(c)]
    in_specs += [pl.BlockSpec((1, ROW_TILE, LANES), lambda b, i: (b, i, 0)),
                 pl.BlockSpec((P_ROWS, W_BRANCH), lambda b, i: (0, 0)),
                 pl.BlockSpec((2, LANES, W_BRANCH), lambda b, i: (0, 0, 0)),
                 pl.BlockSpec((2, LANES, W_BRANCH), lambda b, i: (0, 0, 0)),
                 pl.BlockSpec((2, ROW_TILE, ROW_TILE), lambda b, i: (0, 0, 0))]
    t_idx = np.arange(ROW_TILE)
    same_chunk = t_idx[:, None] // SCAN_STEPS == t_idx[None, :] // SCAN_STEPS
    tri = jnp.asarray(np.stack([same_chunk & (t_idx[None, :] <= t_idx[:, None]),
                                same_chunk & (t_idx[None, :] >= t_idx[:, None])]), BF16)
    out = jax.ShapeDtypeStruct((B, R, W_BRANCH), F32)
    return pl.pallas_call(
        functools.partial(_rwkv_feat_kernel, n_ctx_tiles=n_ctx // ROW_TILE, n_tiles=n_tiles),
        grid=(B, n_tiles),
        in_specs=in_specs,
        out_specs=[pl.BlockSpec((1, ROW_TILE, W_BRANCH), lambda b, i: (b, i, 0))] * 12,
        out_shape=[out] * 12,
        compiler_params=_params("parallel", "parallel"),
        name="rwkv_features",
    )(*([p_main] * 9), p_lora, par, w2, a2, tri)


def _scan_kernel(p_ref, k_ref, b_ref, a_ref, r_ref, v_ref, y_ref, s_ref):
    n = HEAD_DIM

    @pl.when(pl.program_id(0) == 0)
    def _():
        s_ref[...] = jnp.zeros_like(s_ref)

    def row(ref, j, t):
        return ref[0, pl.ds(j * SCAN_STEPS + t, 1), :]

    zero = jnp.zeros((n, s_ref.shape[2]), F32)

    def first_sa(jb, sa):
        for jj in range(SCAN_J_UNROLL):
            j = jb * SCAN_J_UNROLL + jj
            sa = sa + s_ref[j] * row(a_ref, j, 0)
        return sa

    def step(t, sa):
        vt = v_ref[0, pl.ds(t, n, stride=SCAN_STEPS), :]
        t_next = jnp.minimum(t + 1, SCAN_STEPS - 1)

        def columns(jb, carry):
            y, sa_next = carry
            for jj in range(SCAN_J_UNROLL):
                j = jb * SCAN_J_UNROLL + jj
                sj = s_ref[j] + sa * row(b_ref, j, t) + vt * row(k_ref, j, t)
                s_ref[j] = sj
                y = y + sj * row(r_ref, j, t)
                sa_next = sa_next + sj * row(a_ref, j, t_next)
            return y, sa_next

        y, sa_next = lax.fori_loop(0, n // SCAN_J_UNROLL, columns, (zero, zero))
        y_ref[0, pl.ds(t, n, stride=SCAN_STEPS), :] = y
        return sa_next

    def rescale(jb, carry):
        for jj in range(SCAN_J_UNROLL):
            j = jb * SCAN_J_UNROLL + jj
            s_ref[j] = s_ref[j] * row(p_ref, j, SCAN_STEPS - 1)
        return carry

    sa0 = lax.fori_loop(0, n // SCAN_J_UNROLL, first_sa, zero)
    lax.fori_loop(0, SCAN_STEPS, step, sa0)
    lax.fori_loop(0, n // SCAN_J_UNROLL, rescale, 0)


def _wkv_scan(p, k, b, a, r, v):
    n_chunks, rows, chains = p.shape
    spec = pl.BlockSpec((1, rows, chains), lambda s: (s, 0, 0))
    return pl.pallas_call(
        _scan_kernel,
        grid=(n_chunks,),
        in_specs=[spec] * 6,
        out_specs=spec,
        out_shape=jax.ShapeDtypeStruct(p.shape, F32),
        scratch_shapes=[pltpu.VMEM((HEAD_DIM, HEAD_DIM, chains), F32)],
        compiler_params=_params("arbitrary"),
        name="wkv_scan",
    )(p, k, b, a, r, v)


def _flip_rows(x, flip):
    hi, mid, lo = _split3(x)
    return (jnp.dot(flip, hi, preferred_element_type=F32) + jnp.dot(flip, mid, preferred_element_type=F32)
            + jnp.dot(flip, lo, preferred_element_type=F32))


def _mirror_chunk(c, n_ctx_chunks, n_chunks):
    return jnp.where(c < n_ctx_chunks, n_ctx_chunks - 1 - c, n_ctx_chunks + n_chunks - 1 - c)


def _to_scan_kernel(zf_ref, zb_ref, flip_ref, o_ref, t_ref):
    nb = zf_ref.shape[0]
    for b in range(nb):
        t_ref[b] = zf_ref[b].T
        t_ref[nb + b] = _flip_rows(zb_ref[b], flip_ref[...]).T
    sub = RELAYOUT_ROWS // SCAN_STEPS

    def body(n, carry):
        rows = pl.ds(pl.multiple_of(n * N_HEADS, N_HEADS), N_HEADS)
        slabs = [t_ref[g, rows, :] for g in range(2 * nb)]
        tile = jnp.concatenate(slabs, axis=0).T
        for q in range(sub):
            o_ref[q, pl.ds(pl.multiple_of(n * SCAN_STEPS, SCAN_STEPS), SCAN_STEPS), :] = (
                tile[q * SCAN_STEPS:(q + 1) * SCAN_STEPS])
        return carry

    lax.fori_loop(0, HEAD_DIM, body, 0, unroll=RELAYOUT_UNROLL)


def _to_scan(z_fwd, z_bwd, flip, n_ctx):
    B, R, _ = z_fwd.shape
    n_chunks = R // RELAYOUT_ROWS
    n_ctx_chunks = n_ctx // RELAYOUT_ROWS
    sub = RELAYOUT_ROWS // SCAN_STEPS
    chains = 2 * B * N_HEADS
    return pl.pallas_call(
        _to_scan_kernel,
        grid=(n_chunks,),
        in_specs=[pl.BlockSpec((B, RELAYOUT_ROWS, W_BRANCH), lambda c: (0, c, 0)),
                  pl.BlockSpec((B, RELAYOUT_ROWS, W_BRANCH),
                               lambda c: (0, _mirror_chunk(c, n_ctx_chunks, n_chunks), 0)),
                  pl.BlockSpec((RELAYOUT_ROWS, RELAYOUT_ROWS), lambda c: (0, 0))],
        out_specs=pl.BlockSpec((sub, HEAD_DIM * SCAN_STEPS, chains), lambda c: (c, 0, 0)),
        out_shape=jax.ShapeDtypeStruct((R // SCAN_STEPS, HEAD_DIM * SCAN_STEPS, chains), F32),
        scratch_shapes=[pltpu.VMEM((2 * B, W_BRANCH, RELAYOUT_ROWS), F32)],
        compiler_params=_params("parallel"),
        name="to_scan_layout",
    )(z_fwd, z_bwd, flip)


def _from_scan_kernel(y_ref, flip_ref, yf_ref, yb_ref, t_ref):
    nb = yf_ref.shape[0]
    sub = RELAYOUT_ROWS // SCAN_STEPS

    def body(n, carry):
        base = pl.multiple_of(n * SCAN_STEPS, SCAN_STEPS)
        tile = jnp.concatenate([y_ref[q, pl.ds(base, SCAN_STEPS), :] for q in range(sub)], axis=0)
        tile = tile.T
        rows = pl.ds(pl.multiple_of(n * N_HEADS, N_HEADS), N_HEADS)
        for g in range(2 * nb):
            t_ref[g, rows, :] = tile[g * N_HEADS:(g + 1) * N_HEADS]
        return carry

    lax.fori_loop(0, HEAD_DIM, body, 0, unroll=RELAYOUT_UNROLL)
    for b in range(nb):
        yf_ref[b] = t_ref[b].T
        yb_ref[b] = _flip_rows(t_ref[nb + b].T, flip_ref[...])


def _from_scan(y, flip, n_batch, n_ctx):
    R = y.shape[0] * SCAN_STEPS
    n_chunks = R // RELAYOUT_ROWS
    n_ctx_chunks = n_ctx // RELAYOUT_ROWS
    sub = RELAYOUT_ROWS // SCAN_STEPS
    out = jax.ShapeDtypeStruct((n_batch, R, W_BRANCH), F32)
    return pl.pallas_call(
        _from_scan_kernel,
        grid=(n_chunks,),
        in_specs=[pl.BlockSpec((sub, HEAD_DIM * SCAN_STEPS, y.shape[2]), lambda c: (c, 0, 0)),
                  pl.BlockSpec((RELAYOUT_ROWS, RELAYOUT_ROWS), lambda c: (0, 0))],
        out_specs=[pl.BlockSpec((n_batch, RELAYOUT_ROWS, W_BRANCH), lambda c: (0, c, 0)),
                   pl.BlockSpec((n_batch, RELAYOUT_ROWS, W_BRANCH),
                                lambda c: (0, _mirror_chunk(c, n_ctx_chunks, n_chunks), 0))],
        out_shape=[out, out],
        scratch_shapes=[pltpu.VMEM((2 * n_batch, W_BRANCH, RELAYOUT_ROWS), F32)],
        compiler_params=_params("parallel"),
        name="from_scan_layout",
    )(y, flip)


def _rwkv_readout_kernel(yf_ref, yb_ref, bonus_ref, gate_ref, gb_ref, o_ref):
    y = yf_ref[0] + yb_ref[0]
    mu = _head_sum(y) * (1.0 / HEAD_DIM)
    yc = y - mu
    var = _head_sum(yc * yc) * (1.0 / HEAD_DIM)
    gb = gb_ref[...]
    out = yc * lax.rsqrt(var + LNX_EPS) * gb[0:1, :] + gb[1:2, :] + bonus_ref[0]
    o_ref[0] = (out * _silu(gate_ref[0])).astype(o_ref.dtype)


def _rwkv_readout(y_fwd, y_bwd, bonus, p_main, lnx_gb):
    B, R, _ = y_fwd.shape
    tile = pl.BlockSpec((1, ROW_TILE, W_BRANCH), lambda b, i: (b, i, 0))
    return pl.pallas_call(
        _rwkv_readout_kernel,
        grid=(B, R // ROW_TILE),
        in_specs=[tile, tile, tile,
                  pl.BlockSpec((1, ROW_TILE, W_BRANCH), lambda b, i: (b, i, COL_RW_GATE)),
                  pl.BlockSpec((8, W_BRANCH), lambda b, i: (0, 0))],
        out_specs=tile,
        out_shape=jax.ShapeDtypeStruct((B, R, W_BRANCH), BF16),
        compiler_params=_params("parallel", "parallel"),
        name="rwkv_readout",
    )(y_fwd, y_bwd, bonus, p_main, lnx_gb)


def _merge_kernel(na_ref, pool_ref, rw_ref, lna_ref, lpool_ref, lrw_ref, w_ref, o_ref):
    acc = None
    for br, (x_ref, l_ref) in enumerate(((na_ref, lna_ref), (pool_ref, lpool_ref), (rw_ref, lrw_ref))):
        t = _sigmoid(l_ref[...]) * jnp.dot(x_ref[...], w_ref[br], preferred_element_type=F32)
        acc = t if acc is None else acc + t
    o_ref[...] = acc.astype(o_ref.dtype)


def _merge(b_na, b_pool, b_rw, p_merge, w_branch):
    M = b_na.shape[0]
    tm, tn = min(_row_tile(M), 512), 1024
    nb = D_MODEL // tn
    x_spec = pl.BlockSpec((tm, W_BRANCH), lambda j, i: (i, 0))

    def logit(br):
        return pl.BlockSpec((tm, tn), lambda j, i: (i, br * nb + j))

    return pl.pallas_call(
        _merge_kernel,
        grid=(nb, M // tm),
        in_specs=[x_spec, x_spec, x_spec, logit(0), logit(1), logit(2),
                  pl.BlockSpec((N_BRANCH, W_BRANCH, tn), lambda j, i: (0, 0, j))],
        out_specs=pl.BlockSpec((tm, tn), lambda j, i: (i, j)),
        out_shape=jax.ShapeDtypeStruct((M, D_MODEL), BF16),
        compiler_params=_params("parallel", "parallel"),
        name="branch_merge",
    )(b_na, b_pool, b_rw, p_merge, p_merge, p_merge, w_branch)


def _out_kernel(m_ref, w_ref, x_ref, mod_ref, fg_ref, o_ref, *, n_ctx_tiles, tile_offset, final):
    gate = _mod_row(mod_ref, pl.program_id(1) + tile_offset, n_ctx_tiles)[:, 2 * D_MODEL:]
    x = x_ref[0] + gate * jnp.dot(m_ref[0], w_ref[...], preferred_element_type=F32)
    o_ref[0] = _rms(x, fg_ref[...]) if final else x


def _out_proj(merged, w_out, x_all, mod, final_g, n_ctx, final):
    B, R, _ = x_all.shape
    off = n_ctx // ROW_TILE if final else 0
    tile = pl.BlockSpec((1, ROW_TILE, D_MODEL), lambda b, i: (b, i + off, 0))
    return pl.pallas_call(
        functools.partial(_out_kernel, n_ctx_tiles=n_ctx // ROW_TILE, tile_offset=off, final=final),
        grid=(B, R // ROW_TILE - off),
        in_specs=[tile,
                  pl.BlockSpec((D_MODEL, D_MODEL), lambda b, i: (0, 0)),
                  tile,
                  pl.BlockSpec((MOD_ROWS, 3 * D_MODEL), lambda b, i: (0, 0)),
                  pl.BlockSpec((1, D_MODEL), lambda b, i: (0, 0))],
        out_specs=pl.BlockSpec((1, ROW_TILE, D_MODEL), lambda b, i: (b, i, 0)),
        out_shape=jax.ShapeDtypeStruct((B, R - off * ROW_TILE, D_MODEL), F32),
        compiler_params=_params("parallel", "parallel"),
        name="out_proj_final" if final else "out_proj",
    )(merged, w_out, x_all, mod, final_g)


def _layer(x_all, mod, n_ctx, final, final_g, norm_g, w_in, na_rpb, pool_w, pool_scale, rw_mu, rw_w0, rw_w2,
           rw_a0, rw_a2, rw_k_k, rw_k_a, rw_r_k, rw_lnx_g, rw_lnx_b, w_branch, w_out):
    B, R, _ = x_all.shape
    rows = (R - n_ctx) // GRID_W

    h = _norm_mod(x_all, norm_g[None], mod, n_ctx).reshape(B * R, D_MODEL)
    lo = N_MAIN + 2 * RWKV_LORA
    n_rw = COL_RW_R * W_BRANCH
    w_rw = _head_major(w_in[:, n_rw:N_MAIN].reshape(D_MODEL, N_MAIN // W_BRANCH - COL_RW_R, W_BRANCH))
    w_main = jnp.concatenate([w_in[:, :n_rw], w_rw.reshape(D_MODEL, N_MAIN - n_rw)], axis=1)
    p_main = _matmul(h, w_main.astype(BF16), F32, "in_proj_main").reshape(B, R, N_MAIN)
    p_lora = _matmul(h, w_in[:, N_MAIN:lo].astype(BF16), F32, "in_proj_lora").reshape(B, R, 2 * RWKV_LORA)
    p_merge = _matmul(h, w_in[:, lo:].astype(BF16), F32, "in_proj_merge")

    b_na = _na_attention(p_main, _na_bias_tables(na_rpb, rows), n_ctx)
    b_pool = _pool(p_main, pool_w.astype(BF16), pool_scale[None], n_ctx)

    par = jnp.zeros((P_ROWS, W_BRANCH), F32)
    par = par.at[P_MU_R:P_MU_V + 1].set(rw_mu).at[P_W0_F:P_W0_B + 1].set(rw_w0).at[P_A0_F:P_A0_B + 1].set(rw_a0)
    par = _head_major(par.at[P_K_K].set(rw_k_k).at[P_K_A].set(rw_k_a).at[P_R_K].set(rw_r_k.reshape(-1)))
    zeros = jnp.zeros_like(rw_w2)
    w2 = _head_major(jnp.concatenate([rw_w2, zeros], axis=1)).astype(BF16)
    a2 = _head_major(jnp.concatenate([zeros, rw_a2], axis=1)).astype(BF16)
    v, p_f, k_f, b_f, a_f, r_f, p_b, k_b, b_b, a_b, r_b, bonus = _rwkv_features(p_main, p_lora, par, w2, a2, n_ctx)
    flip = jnp.asarray(np.eye(RELAYOUT_ROWS)[::-1], BF16)
    y = _wkv_scan(*[_to_scan(zf, zb, flip, n_ctx)
                    for zf, zb in ((p_f, p_b), (k_f, k_b), (b_f, b_b), (a_f, a_b), (r_f, r_b), (v, v))])
    y_fwd, y_bwd = _from_scan(y, flip, B, n_ctx)
    lnx_gb = _head_major(jnp.zeros((8, W_BRANCH), F32).at[0].set(rw_lnx_g).at[1].set(rw_lnx_b))
    b_rw = _rwkv_readout(y_fwd, y_bwd, bonus, p_main, lnx_gb)

    def flat(z):
        return z.reshape(B * R, W_BRANCH)

    w_rw_out = _head_major(w_branch[2].T).T
    w_br = jnp.stack([w_branch[0], w_branch[1], w_rw_out]).astype(BF16)
    merged = _merge(flat(b_na), flat(b_pool), flat(b_rw), p_merge, w_br)
    return _out_proj(merged.reshape(B, R, D_MODEL), w_out.astype(BF16), x_all, mod, final_g[None], n_ctx, final)


def kernel(x, c, ctx, c_ctx, norm_g, w_mod, b_mod, w_in, na_rpb, pool_w, pool_scale, rw_mu, rw_w0, rw_w2, rw_a0,
           rw_a2, rw_k_k, rw_k_a, rw_r_k, rw_lnx_g, rw_lnx_b, w_branch, w_out, final_g):
    B, T, _ = x.shape
    n_ctx = ctx.shape[1]
    depth = w_in.shape[0]
    assert B <= CTX_MOD_ROW and n_ctx % ROW_TILE == 0 and T % ROW_TILE == 0
    assert ROW_TILE % RELAYOUT_ROWS == 0 and RELAYOUT_ROWS % SCAN_STEPS == 0

    cond = jnp.zeros((MOD_ROWS, D_MODEL), F32).at[:B].set(c).at[CTX_MOD_ROW].set(c_ctx)
    mods = _modulation(cond, w_mod.astype(BF16), b_mod[:, None, :])
    x_all = jnp.concatenate([ctx, x], axis=1)
    for layer in range(depth):
        x_all = _layer(x_all, mods[layer], n_ctx, layer == depth - 1, final_g, norm_g[layer], w_in[layer],
                       na_rpb[layer], pool_w[layer], pool_scale[layer], rw_mu[layer], rw_w0[layer], rw_w2[layer],
                       rw_a0[layer], rw_a2[layer], rw_k_k[layer], rw_k_a[layer], rw_r_k[layer], rw_lnx_g[layer],
                       rw_lnx_b[layer], w_branch[layer], w_out[layer])
    return x_all
```

```python
import functools

import numpy as np
import jax
import jax.numpy as jnp
from jax import lax
from jax.experimental import pallas as pl
from jax.experimental.pallas import tpu as pltpu

F32 = jnp.float32
BF16 = jnp.bfloat16

D_MODEL = 2048
W_BRANCH = D_MODEL // 2
N_BRANCH = 3
N_HEADS = 16
HEAD_DIM = 64
GRID_W = 64
NA_WIN_H = 8
NA_WIN_W = 16
POOL_WINDOWS = (2, 4, 8, 16)
POOL_GROUP_DIM = W_BRANCH // len(POOL_WINDOWS)
POOL_HALO = 8
RWKV_LORA = 64
RMS_EPS = 1e-6
LNX_EPS = 64e-5
NEG_INF = -1e30

LANES = 128
ROW_TILE = 256
NA_Q_ROWS = ROW_TILE // GRID_W
NA_K_ROWS = NA_Q_ROWS + NA_WIN_H
NA_K_TOK = NA_K_ROWS * GRID_W
CTX_MOD_ROW = 4
MOD_ROWS = 8
SCAN_STEPS = 32
SCAN_J_UNROLL = 32
RELAYOUT_UNROLL = 8
RELAYOUT_ROWS = 128
VMEM_LIMIT = 56 << 20

COL_Q, COL_K, COL_V, COL_NA_GATE, COL_POOL_U, COL_POOL_GATE, COL_RW_R, COL_RW_K, COL_RW_V, COL_RW_GATE = range(10)
N_MAIN = 10 * W_BRANCH

P_MU_R, P_MU_K, P_MU_V, P_W0_F, P_W0_B, P_A0_F, P_A0_B, P_K_K, P_K_A, P_R_K = range(10)
P_ROWS = 16


def _params(*sem):
    return pltpu.CompilerParams(dimension_semantics=sem, vmem_limit_bytes=VMEM_LIMIT)


def _sigmoid(x):
    return 1.0 / (1.0 + jnp.exp(-x))


def _silu(x):
    return x * _sigmoid(x)


def _split3(x):
    hi = x.astype(BF16)
    r1 = x - hi.astype(F32)
    mid = r1.astype(BF16)
    lo = (r1 - mid.astype(F32)).astype(BF16)
    return hi, mid, lo


def _head_major(z):
    lead = z.shape[:-1]
    return z.reshape(lead + (N_HEADS, HEAD_DIM)).swapaxes(-1, -2).reshape(lead + (W_BRANCH,))


def _head_sum(x):
    n_tiles = W_BRANCH // LANES
    part = x[:, :LANES]
    for c in range(1, n_tiles):
        part = part + x[:, c * LANES:(c + 1) * LANES]
    shift = N_HEADS
    while shift < LANES:
        part = part + pltpu.roll(part, shift, 1)
        shift *= 2
    return jnp.concatenate([part] * n_tiles, axis=1)


def _mod_kernel(cond_ref, w_ref, b_ref, o_ref):
    s = _silu(cond_ref[...])
    o_ref[0] = jnp.dot(s.astype(BF16), w_ref[0], preferred_element_type=F32) + b_ref[0]


def _modulation(cond, w_mod, b_mod):
    n_layers = w_mod.shape[0]
    tn = 3 * D_MODEL // 4
    return pl.pallas_call(
        _mod_kernel,
        grid=(n_layers, 4),
        in_specs=[pl.BlockSpec((MOD_ROWS, D_MODEL), lambda l, j: (0, 0)),
                  pl.BlockSpec((1, D_MODEL, tn), lambda l, j: (l, 0, j)),
                  pl.BlockSpec((1, 1, tn), lambda l, j: (l, 0, j))],
        out_specs=pl.BlockSpec((1, MOD_ROWS, tn), lambda l, j: (l, 0, j)),
        out_shape=jax.ShapeDtypeStruct((n_layers, MOD_ROWS, 3 * D_MODEL), F32),
        compiler_params=_params("arbitrary", "arbitrary"),
        name="adaln_modulation",
    )(cond, w_mod, b_mod)


def _mod_row(mod_ref, tile, n_ctx_tiles):
    row = jnp.where(tile < n_ctx_tiles, CTX_MOD_ROW, pl.program_id(0))
    return mod_ref[pl.ds(row, 1), :]


def _rms(x, g):
    return x * lax.rsqrt(jnp.mean(x * x, axis=-1, keepdims=True) + RMS_EPS) * g


def _stream_specs(stream, n_ctx_tiles, tile_offset=0):
    lat_shift = stream[2]
    return [pl.BlockSpec((1, ROW_TILE, D_MODEL), lambda b, i: (b, jnp.minimum(i + tile_offset, n_ctx_tiles - 1), 0)),
            pl.BlockSpec((1, ROW_TILE, D_MODEL), lambda b, i: (b, jnp.maximum(i + tile_offset - lat_shift, 0), 0))]


def _stream_tile(c_ref, l_ref, tile, n_ctx_tiles):
    return jnp.where(tile < n_ctx_tiles, c_ref[0], l_ref[0])


def _norm_mod_kernel(c_ref, l_ref, g_ref, mod_ref, h_ref, *, n_ctx_tiles):
    tile = pl.program_id(1)
    m = _mod_row(mod_ref, tile, n_ctx_tiles)
    shift = m[:, :D_MODEL]
    scale = m[:, D_MODEL:2 * D_MODEL]
    x = _stream_tile(c_ref, l_ref, tile, n_ctx_tiles)
    h_ref[0] = (_rms(x, g_ref[...]) * (1.0 + scale) + shift).astype(BF16)


def _norm_mod(stream, n_rows, norm_g, mod, n_ctx):
    B = stream[0].shape[0]
    n_ctx_tiles = n_ctx // ROW_TILE
    return pl.pallas_call(
        functools.partial(_norm_mod_kernel, n_ctx_tiles=n_ctx_tiles),
        grid=(B, n_rows // ROW_TILE),
        in_specs=_stream_specs(stream, n_ctx_tiles) + [
            pl.BlockSpec((1, D_MODEL), lambda b, i: (0, 0)),
            pl.BlockSpec((MOD_ROWS, 3 * D_MODEL), lambda b, i: (0, 0))],
        out_specs=pl.BlockSpec((1, ROW_TILE, D_MODEL), lambda b, i: (b, i, 0)),
        out_shape=jax.ShapeDtypeStruct((B, n_rows, D_MODEL), BF16),
        compiler_params=_params("parallel", "parallel"),
        name="norm_modulate",
    )(stream[0], stream[1], norm_g, mod)


def _mm_kernel(a_ref, w_ref, o_ref):
    o_ref[...] = jnp.dot(a_ref[...], w_ref[...], preferred_element_type=F32).astype(o_ref.dtype)


def _row_tile(m):
    for t in (1024, 512, 256):
        if m % t == 0:
            return t
    raise ValueError(f"row count {m} is not a multiple of {ROW_TILE}")


def _matmul(a, w, out_dtype, name):
    M, K = a.shape
    N = w.shape[1]
    tm = _row_tile(M)
    tn = min(N, 1024)
    return pl.pallas_call(
        _mm_kernel,
        grid=(N // tn, M // tm),
        in_specs=[pl.BlockSpec((tm, K), lambda j, i: (i, 0)),
                  pl.BlockSpec((K, tn), lambda j, i: (0, j))],
        out_specs=pl.BlockSpec((tm, tn), lambda j, i: (i, j)),
        out_shape=jax.ShapeDtypeStruct((M, N), out_dtype),
        compiler_params=_params("parallel", "parallel"),
        name=name,
    )(a, w)


def _na_bias_tables(rpb, rows):
    n_blocks = rows // NA_Q_ROWS
    n_off = 2 * NA_WIN_H - 1
    col = np.arange(GRID_W)
    c0 = np.clip(col - NA_WIN_W // 2, 0, GRID_W - NA_WIN_W)
    valid_c = (col[None, :] >= c0[:, None]) & (col[None, :] < c0[:, None] + NA_WIN_W)
    col_off = np.clip(col[None, :] - col[:, None] + NA_WIN_W - 1, 0, 2 * NA_WIN_W - 2)
    pick_c = jnp.asarray(np.eye(2 * NA_WIN_W - 1)[col_off], F32)
    tile = jnp.einsum("hrc,qpc->hrqp", rpb.astype(F32), pick_c, precision=lax.Precision.HIGHEST)
    tile = jnp.where(valid_c, tile, NEG_INF)
    tile = jnp.concatenate([tile, jnp.full((N_HEADS, 1, GRID_W, GRID_W), NEG_INF, F32)], axis=1)
    tile = jnp.concatenate([tile, tile], axis=-1)
    picks = []
    for m in (0, 1, n_blocks - 1):
        q_row = NA_Q_ROWS * m + np.arange(NA_Q_ROWS)
        k_row = int(np.clip(NA_Q_ROWS * m - NA_Q_ROWS, 0, rows - NA_K_ROWS)) + np.arange(NA_K_ROWS)
        r0 = np.clip(q_row - NA_WIN_H // 2, 0, rows - NA_WIN_H)
        valid_r = (k_row[None, :] >= r0[:, None]) & (k_row[None, :] < r0[:, None] + NA_WIN_H)
        row_off = np.clip(k_row[None, :] - q_row[:, None] + NA_WIN_H - 1, 0, n_off - 1)
        picks.append(np.where(valid_r, row_off, n_off))
    picks = np.stack(picks)

    def build(t_ref, o_ref):
        left = lax.broadcasted_iota(jnp.int32, (1, LANES), 1) < GRID_W
        for ty in range(picks.shape[0]):
            for a in range(NA_Q_ROWS):
                for kp in range(NA_K_ROWS // 2):
                    pair = jnp.where(left, t_ref[0, int(picks[ty, a, 2 * kp])], t_ref[0, int(picks[ty, a, 2 * kp + 1])])
                    o_ref[ty, 0, a * GRID_W:(a + 1) * GRID_W, kp * LANES:(kp + 1) * LANES] = pair

    return pl.pallas_call(
        build,
        grid=(N_HEADS,),
        in_specs=[pl.BlockSpec((1, n_off + 1, GRID_W, LANES), lambda h: (h, 0, 0, 0))],
        out_specs=pl.BlockSpec((picks.shape[0], 1, ROW_TILE, NA_K_TOK), lambda h: (0, h, 0, 0)),
        out_shape=jax.ShapeDtypeStruct((picks.shape[0], N_HEADS, ROW_TILE, NA_K_TOK), F32),
        compiler_params=_params("parallel"),
        name="na_bias_tables",
    )(tile)


def _attend(qe, keys, vals, biases):
    dn = (((1,), (1,)), ((), ()))
    scores = []
    for kk, bias in zip(keys, biases):
        s = lax.dot_general(qe, kk, dn, preferred_element_type=F32)
        scores.append(s if bias is None else s + bias)
    m = scores[0].max(axis=-1, keepdims=True)
    for s in scores[1:]:
        m = jnp.maximum(m, s.max(axis=-1, keepdims=True))
    num, den = None, None
    for s, vv in zip(scores, vals):
        p = jnp.exp(s - m)
        l = p.sum(axis=-1, keepdims=True)
        o = jnp.dot(p.astype(BF16), vv, preferred_element_type=F32)
        num = o if num is None else num + o
        den = l if den is None else den + l
    return num / den


def _na_kernel(q_ref, k_ref, v_ref, g_ref, bias_ref, o_ref, *, n_ctx, rows):
    j = pl.program_id(2)
    lane = lax.broadcasted_iota(jnp.int32, (1, LANES), 1)
    in_head = (lane < HEAD_DIM, lane >= HEAD_DIM)
    q = q_ref[0] * (HEAD_DIM ** -0.5)
    kc = k_ref[0, 0:n_ctx, :].astype(BF16)
    vc = v_ref[0, 0:n_ctx, :].astype(BF16)

    def heads(q):
        return [jnp.where(in_head[e], q, 0.0).astype(BF16) for e in range(2)]

    def finish(o0, o1):
        o = jnp.where(in_head[0], o0, o1)
        o_ref[0] = (o * _silu(g_ref[0])).astype(o_ref.dtype)

    @pl.when(j == 0)
    def _():
        finish(*[_attend(qe, [kc], [vc], [None]) for qe in heads(q)])

    @pl.when(j > 0)
    def _():
        k_row = jnp.clip(NA_Q_ROWS * (j - 1) - NA_Q_ROWS, 0, rows - NA_K_ROWS)
        start = pl.multiple_of(n_ctx + k_row * GRID_W, GRID_W)
        kw = k_ref[0, pl.ds(start, NA_K_TOK), :].astype(BF16)
        vw = v_ref[0, pl.ds(start, NA_K_TOK), :].astype(BF16)
        finish(*[_attend(qe, [kw, kc], [vw, vc], [bias_ref[0, e], None])
                 for e, qe in enumerate(heads(q))])


def _na_attention(p_main, bias_tables, n_ctx):
    B, R, _ = p_main.shape
    rows = (R - n_ctx) // GRID_W
    n_blocks = rows // NA_Q_ROWS
    pairs = W_BRANCH // LANES
    assert n_ctx == ROW_TILE and rows >= NA_K_ROWS and rows % NA_Q_ROWS == 0

    def col(c):
        return lambda b, hp, j: (b, 0, c * pairs + hp)

    def bias_idx(b, hp, j):
        return (jnp.where(j <= 1, 0, jnp.where(j == n_blocks, 2, 1)), hp, 0, 0)

    return pl.pallas_call(
        functools.partial(_na_kernel, n_ctx=n_ctx, rows=rows),
        grid=(B, pairs, n_blocks + 1),
        in_specs=[pl.BlockSpec((1, ROW_TILE, LANES), lambda b, hp, j: (b, j, COL_Q * pairs + hp)),
                  pl.BlockSpec((1, R, LANES), col(COL_K)),
                  pl.BlockSpec((1, R, LANES), col(COL_V)),
                  pl.BlockSpec((1, ROW_TILE, LANES), lambda b, hp, j: (b, j, COL_NA_GATE * pairs + hp)),
                  pl.BlockSpec((1, 2, ROW_TILE, NA_K_TOK), bias_idx)],
        out_specs=pl.BlockSpec((1, ROW_TILE, LANES), lambda b, hp, j: (b, j, hp)),
        out_shape=jax.ShapeDtypeStruct((B, R, W_BRANCH), BF16),
        compiler_params=_params("parallel", "parallel", "arbitrary"),
        name="neighbourhood_attention",
    )(p_main, p_main, p_main, p_main, bias_tables)


def _pool_kernel(u_ref, g_ref, w_ref, sc_ref, o_ref, pad_ref, *, n_ctx, n_lat):
    grp = pl.program_id(1)
    w = w_ref[0]
    scale = sc_ref[...]

    def run(win):
        half = win // 2
        for seq_start, seq_len in ((0, n_ctx), (n_ctx, n_lat)):
            zeros = jnp.zeros((POOL_HALO, POOL_GROUP_DIM), F32)
            pad_ref[0:POOL_HALO, :] = zeros
            pad_ref[POOL_HALO:POOL_HALO + seq_len, :] = u_ref[0, seq_start:seq_start + seq_len, :]
            pad_ref[POOL_HALO + seq_len:2 * POOL_HALO + seq_len, :] = zeros

            def chunk(c, carry):
                base = pl.multiple_of(c * ROW_TILE, ROW_TILE)
                x = pad_ref[pl.ds(base, ROW_TILE + 2 * POOL_HALO), :]
                acc = x[POOL_HALO - half:POOL_HALO - half + ROW_TILE]
                for o in range(-half + 1, half):
                    acc = acc + x[POOL_HALO + o:POOL_HALO + o + ROW_TILE]
                t = base + lax.broadcasted_iota(jnp.int32, (ROW_TILE, 1), 0)
                cnt = jnp.minimum(t + half, seq_len) - jnp.maximum(t - half, 0)
                diff = acc / cnt.astype(F32) - x[POOL_HALO:POOL_HALO + ROW_TILE]
                y = jnp.dot(diff.astype(BF16), w, preferred_element_type=F32) * scale
                rows = pl.ds(seq_start + base, ROW_TILE)
                o_ref[0, rows, :] = (y * _silu(g_ref[0, rows, :])).astype(o_ref.dtype)
                return carry

            lax.fori_loop(0, seq_len // ROW_TILE, chunk, 0)

    for gi, win in enumerate(POOL_WINDOWS):
        pl.when(grp == gi)(functools.partial(run, win))


def _pool(p_main, pool_w, pool_scale, n_ctx):
    B, R, _ = p_main.shape
    groups = len(POOL_WINDOWS)
    return pl.pallas_call(
        functools.partial(_pool_kernel, n_ctx=n_ctx, n_lat=R - n_ctx),
        grid=(B, groups),
        in_specs=[pl.BlockSpec((1, R, POOL_GROUP_DIM), lambda b, g: (b, 0, COL_POOL_U * groups + g)),
                  pl.BlockSpec((1, R, POOL_GROUP_DIM), lambda b, g: (b, 0, COL_POOL_GATE * groups + g)),
                  pl.BlockSpec((1, POOL_GROUP_DIM, POOL_GROUP_DIM), lambda b, g: (g, 0, 0)),
                  pl.BlockSpec((1, POOL_GROUP_DIM), lambda b, g: (0, g))],
        out_specs=pl.BlockSpec((1, R, POOL_GROUP_DIM), lambda b, g: (b, 0, g)),
        out_shape=jax.ShapeDtypeStruct((B, R, W_BRANCH), BF16),
        scratch_shapes=[pltpu.VMEM((R - n_ctx + 2 * POOL_HALO, POOL_GROUP_DIM), F32)],
        compiler_params=_params("parallel", "arbitrary"),
        name="multiscale_pool",
    )(p_main, p_main, pool_w, pool_scale)


def _rwkv_feat_kernel(r_ref, rp_ref, rn_ref, k_ref, kp_ref, kn_ref, v_ref, vp_ref, vn_ref, lora_ref,
                      par_ref, w2_ref, a2_ref, tri_ref,
                      vo_ref, pf_ref, kf_ref, bf_ref, af_ref, rf_ref, pb_ref, kb_ref, bb_ref, ab_ref, rb_ref, bonus_ref,
                      *, n_ctx_tiles, n_tiles):
    i = pl.program_id(1)
    first = (i == 0) | (i == n_ctx_tiles)
    last = (i == n_ctx_tiles - 1) | (i == n_tiles - 1)
    row = lax.broadcasted_iota(jnp.int32, (ROW_TILE, 1), 0)
    par = par_ref[...]

    def prm(p):
        return par[p:p + 1, :]

    def mix(z_ref, prev_ref, next_ref, mu):
        z = z_ref[0]
        prev = jnp.where(first, 0.0, prev_ref[0, 7:8, :])
        nxt = jnp.where(last, 0.0, next_ref[0, 0:1, :])
        z_prev = jnp.where(row == 0, prev, pltpu.roll(z, 1, 0))
        z_next = jnp.where(row == ROW_TILE - 1, nxt, pltpu.roll(z, ROW_TILE - 1, 0))
        return z + mu * (0.5 * (z_prev + z_next) - z)

    r = mix(r_ref, rp_ref, rn_ref, prm(P_MU_R))
    k = mix(k_ref, kp_ref, kn_ref, prm(P_MU_K))
    v = mix(v_ref, vp_ref, vn_ref, prm(P_MU_V))
    vo_ref[0] = v

    kk = k * prm(P_K_K)
    kk = kk * jnp.minimum(lax.rsqrt(_head_sum(kk * kk)), 1e12)

    lora = lora_ref[0]
    lane = lax.broadcasted_iota(jnp.int32, (1, LANES), 1)
    lora = jnp.where(lane < RWKV_LORA, jnp.tanh(lora), lora).astype(BF16)
    k_sum = None
    outs = ((pf_ref, kf_ref, bf_ref, af_ref, rf_ref), (pb_ref, kb_ref, bb_ref, ab_ref, rb_ref))
    for d, (p_out, k_out, b_out, a_out, r_out) in enumerate(outs):
        x = prm(P_W0_F + d) + jnp.dot(lora, w2_ref[d], preferred_element_type=F32)
        w_log = -(jnp.maximum(-x, 0.0) + jnp.log(1.0 + jnp.exp(-jnp.abs(x)))) - 0.5
        neg_log_w = jnp.exp(w_log)
        hi, mid, lo = _split3(neg_log_w)
        tri = tri_ref[d]
        cs = (jnp.dot(tri, hi, preferred_element_type=F32) + jnp.dot(tri, mid, preferred_element_type=F32)
              + jnp.dot(tri, lo, preferred_element_type=F32))
        grow = jnp.exp(cs)
        shrink = jnp.exp(-cs)
        a = 0.5 + 0.5 * jnp.tanh(0.5 * (prm(P_A0_F + d) + jnp.dot(lora, a2_ref[d], preferred_element_type=F32)))
        k_d = k * (1.0 + (a - 1.0) * prm(P_K_A))
        p_out[0] = shrink
        k_out[0] = k_d * grow
        b_out[0] = kk * a * grow
        a_out[0] = -kk * jnp.exp(neg_log_w - cs)
        r_out[0] = r * shrink
        k_sum = k_d if k_sum is None else k_sum + k_d
    bonus_ref[0] = _head_sum(r * k_sum * prm(P_R_K)) * v


def _rwkv_features(p_main, p_lora, par, w2, a2, n_ctx):
    B, R, _ = p_main.shape
    n_tiles = R // ROW_TILE
    sub = ROW_TILE // 8

    def main(c):
        return pl.BlockSpec((1, ROW_TILE, W_BRANCH), lambda b, i: (b, i, c))

    def prev(c):
        return pl.BlockSpec((1, 8, W_BRANCH), lambda b, i: (b, jnp.maximum(i * sub - 1, 0), c))

    def nxt(c):
        return pl.BlockSpec((1, 8, W_BRANCH), lambda b, i: (b, jnp.minimum((i + 1) * sub, n_tiles * sub - 1), c))

    in_specs = []
    for c in (COL_RW_R, COL_RW_K, COL_RW_V):
        in_specs += [main(c), prev(c), nxt(c)]
    in_specs += [pl.BlockSpec((1, ROW_TILE, LANES), lambda b, i: (b, i, 0)),
                 pl.BlockSpec((P_ROWS, W_BRANCH), lambda b, i: (0, 0)),
                 pl.BlockSpec((2, LANES, W_BRANCH), lambda b, i: (0, 0, 0)),
                 pl.BlockSpec((2, LANES, W_BRANCH), lambda b, i: (0, 0, 0)),
                 pl.BlockSpec((2, ROW_TILE, ROW_TILE), lambda b, i: (0, 0, 0))]
    t_idx = np.arange(ROW_TILE)
    same_chunk = t_idx[:, None] // SCAN_STEPS == t_idx[None, :] // SCAN_STEPS
    tri = jnp.asarray(np.stack([same_chunk & (t_idx[None, :] <= t_idx[:, None]),
                                same_chunk & (t_idx[None, :] >= t_idx[:, None])]), BF16)
    out = jax.ShapeDtypeStruct((B, R, W_BRANCH), F32)
    return pl.pallas_call(
        functools.partial(_rwkv_feat_kernel, n_ctx_tiles=n_ctx // ROW_TILE, n_tiles=n_tiles),
        grid=(B, n_tiles),
        in_specs=in_specs,
        out_specs=[pl.BlockSpec((1, ROW_TILE, W_BRANCH), lambda b, i: (b, i, 0))] * 12,
        out_shape=[out] * 12,
        compiler_params=_params("parallel", "parallel"),
        name="rwkv_features",
    )(*([p_main] * 9), p_lora, par, w2, a2, tri)


def _scan_kernel(p_ref, k_ref, b_ref, a_ref, r_ref, v_ref, y_ref, s_ref):
    n = HEAD_DIM

    @pl.when(pl.program_id(0) == 0)
    def _():
        s_ref[...] = jnp.zeros_like(s_ref)

    def row(ref, j, t):
        return ref[0, pl.ds(j * SCAN_STEPS + t, 1), :]

    zero = jnp.zeros((n, s_ref.shape[2]), F32)

    def first_sa(jb, sa):
        for jj in range(SCAN_J_UNROLL):
            j = jb * SCAN_J_UNROLL + jj
            sa = sa + s_ref[j] * row(a_ref, j, 0)
        return sa

    def step(t, sa):
        vt = v_ref[0, pl.ds(t, n, stride=SCAN_STEPS), :]
        t_next = jnp.minimum(t + 1, SCAN_STEPS - 1)

        def columns(jb, carry):
            y, sa_next = carry
            for jj in range(SCAN_J_UNROLL):
                j = jb * SCAN_J_UNROLL + jj
                sj = s_ref[j] + sa * row(b_ref, j, t) + vt * row(k_ref, j, t)
                s_ref[j] = sj
                y = y + sj * row(r_ref, j, t)
                sa_next = sa_next + sj * row(a_ref, j, t_next)
            return y, sa_next

        y, sa_next = lax.fori_loop(0, n // SCAN_J_UNROLL, columns, (zero, zero))
        y_ref[0, pl.ds(t, n, stride=SCAN_STEPS), :] = y
        return sa_next

    def rescale(jb, carry):
        for jj in range(SCAN_J_UNROLL):
            j = jb * SCAN_J_UNROLL + jj
            s_ref[j] = s_ref[j] * row(p_ref, j, SCAN_STEPS - 1)
        return carry

    sa0 = lax.fori_loop(0, n // SCAN_J_UNROLL, first_sa, zero)
    lax.fori_loop(0, SCAN_STEPS, step, sa0)
    lax.fori_loop(0, n // SCAN_J_UNROLL, rescale, 0)


def _wkv_scan(p, k, b, a, r, v):
    n_chunks, rows, chains = p.shape
    spec = pl.BlockSpec((1, rows, chains), lambda s: (s, 0, 0))
    return pl.pallas_call(
        _scan_kernel,
        grid=(n_chunks,),
        in_specs=[spec] * 6,
        out_specs=spec,
        out_shape=jax.ShapeDtypeStruct(p.shape, F32),
        scratch_shapes=[pltpu.VMEM((HEAD_DIM, HEAD_DIM, chains), F32)],
        compiler_params=_params("arbitrary"),
        name="wkv_scan",
    )(p, k, b, a, r, v)


def _flip_rows(x, flip):
    hi, mid, lo = _split3(x)
    return (jnp.dot(flip, hi, preferred_element_type=F32) + jnp.dot(flip, mid, preferred_element_type=F32)
            + jnp.dot(flip, lo, preferred_element_type=F32))


def _mirror_chunk(c, n_ctx_chunks, n_chunks):
    return jnp.where(c < n_ctx_chunks, n_ctx_chunks - 1 - c, n_ctx_chunks + n_chunks - 1 - c)


def _to_scan_kernel(zf_ref, zb_ref, flip_ref, o_ref, t_ref):
    nb = zf_ref.shape[0]
    for b in range(nb):
        t_ref[b] = zf_ref[b].T
        t_ref[nb + b] = _flip_rows(zb_ref[b], flip_ref[...]).T
    sub = RELAYOUT_ROWS // SCAN_STEPS

    def body(n, carry):
        rows = pl.ds(pl.multiple_of(n * N_HEADS, N_HEADS), N_HEADS)
        slabs = [t_ref[g, rows, :] for g in range(2 * nb)]
        tile = jnp.concatenate(slabs, axis=0).T
        for q in range(sub):
            o_ref[q, pl.ds(pl.multiple_of(n * SCAN_STEPS, SCAN_STEPS), SCAN_STEPS), :] = (
                tile[q * SCAN_STEPS:(q + 1) * SCAN_STEPS])
        return carry

    lax.fori_loop(0, HEAD_DIM, body, 0, unroll=RELAYOUT_UNROLL)


def _to_scan(z_fwd, z_bwd, flip, n_ctx):
    B, R, _ = z_fwd.shape
    n_chunks = R // RELAYOUT_ROWS
    n_ctx_chunks = n_ctx // RELAYOUT_ROWS
    sub = RELAYOUT_ROWS // SCAN_STEPS
    chains = 2 * B * N_HEADS
    return pl.pallas_call(
        _to_scan_kernel,
        grid=(n_chunks,),
        in_specs=[pl.BlockSpec((B, RELAYOUT_ROWS, W_BRANCH), lambda c: (0, c, 0)),
                  pl.BlockSpec((B, RELAYOUT_ROWS, W_BRANCH),
                               lambda c: (0, _mirror_chunk(c, n_ctx_chunks, n_chunks), 0)),
                  pl.BlockSpec((RELAYOUT_ROWS, RELAYOUT_ROWS), lambda c: (0, 0))],
        out_specs=pl.BlockSpec((sub, HEAD_DIM * SCAN_STEPS, chains), lambda c: (c, 0, 0)),
        out_shape=jax.ShapeDtypeStruct((R // SCAN_STEPS, HEAD_DIM * SCAN_STEPS, chains), F32),
        scratch_shapes=[pltpu.VMEM((2 * B, W_BRANCH, RELAYOUT_ROWS), F32)],
        compiler_params=_params("parallel"),
        name="to_scan_layout",
    )(z_fwd, z_bwd, flip)


def _from_scan_kernel(y_ref, flip_ref, yf_ref, yb_ref, t_ref):
    nb = yf_ref.shape[0]
    sub = RELAYOUT_ROWS // SCAN_STEPS

    def body(n, carry):
        base = pl.multiple_of(n * SCAN_STEPS, SCAN_STEPS)
        tile = jnp.concatenate([y_ref[q, pl.ds(base, SCAN_STEPS), :] for q in range(sub)], axis=0)
        tile = tile.T
        rows = pl.ds(pl.multiple_of(n * N_HEADS, N_HEADS), N_HEADS)
        for g in range(2 * nb):
            t_ref[g, rows, :] = tile[g * N_HEADS:(g + 1) * N_HEADS]
        return carry

    lax.fori_loop(0, HEAD_DIM, body, 0, unroll=RELAYOUT_UNROLL)
    for b in range(nb):
        yf_ref[b] = t_ref[b].T
        yb_ref[b] = _flip_rows(t_ref[nb + b].T, flip_ref[...])


def _from_scan(y, flip, n_batch, n_ctx):
    R = y.shape[0] * SCAN_STEPS
    n_chunks = R // RELAYOUT_ROWS
    n_ctx_chunks = n_ctx // RELAYOUT_ROWS
    sub = RELAYOUT_ROWS // SCAN_STEPS
    out = jax.ShapeDtypeStruct((n_batch, R, W_BRANCH), F32)
    return pl.pallas_call(
        _from_scan_kernel,
        grid=(n_chunks,),
        in_specs=[pl.BlockSpec((sub, HEAD_DIM * SCAN_STEPS, y.shape[2]), lambda c: (c, 0, 0)),
                  pl.BlockSpec((RELAYOUT_ROWS, RELAYOUT_ROWS), lambda c: (0, 0))],
        out_specs=[pl.BlockSpec((n_batch, RELAYOUT_ROWS, W_BRANCH), lambda c: (0, c, 0)),
                   pl.BlockSpec((n_batch, RELAYOUT_ROWS, W_BRANCH),
                                lambda c: (0, _mirror_chunk(c, n_ctx_chunks, n_chunks), 0))],
        out_shape=[out, out],
        scratch_shapes=[pltpu.VMEM((2 * n_batch, W_BRANCH, RELAYOUT_ROWS), F32)],
        compiler_params=_params("parallel"),
        name="from_scan_layout",
    )(y, flip)


def _rwkv_readout_kernel(yf_ref, yb_ref, bonus_ref, gate_ref, gb_ref, o_ref):
    y = yf_ref[0] + yb_ref[0]
    mu = _head_sum(y) * (1.0 / HEAD_DIM)
    yc = y - mu
    var = _head_sum(yc * yc) * (1.0 / HEAD_DIM)
    gb = gb_ref[...]
    out = yc * lax.rsqrt(var + LNX_EPS) * gb[0:1, :] + gb[1:2, :] + bonus_ref[0]
    o_ref[0] = (out * _silu(gate_ref[0])).astype(o_ref.dtype)


def _rwkv_readout(y_fwd, y_bwd, bonus, p_main, lnx_gb):
    B, R, _ = y_fwd.shape
    tile = pl.BlockSpec((1, ROW_TILE, W_BRANCH), lambda b, i: (b, i, 0))
    return pl.pallas_call(
        _rwkv_readout_kernel,
        grid=(B, R // ROW_TILE),
        in_specs=[tile, tile, tile,
                  pl.BlockSpec((1, ROW_TILE, W_BRANCH), lambda b, i: (b, i, COL_RW_GATE)),
                  pl.BlockSpec((8, W_BRANCH), lambda b, i: (0, 0))],
        out_specs=tile,
        out_shape=jax.ShapeDtypeStruct((B, R, W_BRANCH), BF16),
        compiler_params=_params("parallel", "parallel"),
        name="rwkv_readout",
    )(y_fwd, y_bwd, bonus, p_main, lnx_gb)


def _merge_kernel(na_ref, pool_ref, rw_ref, lna_ref, lpool_ref, lrw_ref, w_ref, o_ref):
    acc = None
    for br, (x_ref, l_ref) in enumerate(((na_ref, lna_ref), (pool_ref, lpool_ref), (rw_ref, lrw_ref))):
        t = _sigmoid(l_ref[...]) * jnp.dot(x_ref[...], w_ref[br], preferred_element_type=F32)
        acc = t if acc is None else acc + t
    o_ref[...] = acc.astype(o_ref.dtype)


def _merge(b_na, b_pool, b_rw, p_merge, w_branch):
    M = b_na.shape[0]
    tm, tn = min(_row_tile(M), 512), 1024
    nb = D_MODEL // tn
    x_spec = pl.BlockSpec((tm, W_BRANCH), lambda j, i: (i, 0))

    def logit(br):
        return pl.BlockSpec((tm, tn), lambda j, i: (i, br * nb + j))

    return pl.pallas_call(
        _merge_kernel,
        grid=(nb, M // tm),
        in_specs=[x_spec, x_spec, x_spec, logit(0), logit(1), logit(2),
                  pl.BlockSpec((N_BRANCH, W_BRANCH, tn), lambda j, i: (0, 0, j))],
        out_specs=pl.BlockSpec((tm, tn), lambda j, i: (i, j)),
        out_shape=jax.ShapeDtypeStruct((M, D_MODEL), BF16),
        compiler_params=_params("parallel", "parallel"),
        name="branch_merge",
    )(b_na, b_pool, b_rw, p_merge, p_merge, p_merge, w_branch)


def _out_kernel(m_ref, w_ref, c_ref, l_ref, mod_ref, fg_ref, o_ref, *, n_ctx_tiles, tile_offset, final):
    tile = pl.program_id(1) + tile_offset
    gate = _mod_row(mod_ref, tile, n_ctx_tiles)[:, 2 * D_MODEL:]
    x = _stream_tile(c_ref, l_ref, tile, n_ctx_tiles)
    x = x + gate * jnp.dot(m_ref[0], w_ref[...], preferred_element_type=F32)
    o_ref[0] = _rms(x, fg_ref[...]) if final else x


def _out_proj(merged, w_out, stream, mod, final_g, n_ctx, final):
    B, R, _ = merged.shape
    n_ctx_tiles = n_ctx // ROW_TILE
    off = n_ctx_tiles if final else 0
    return pl.pallas_call(
        functools.partial(_out_kernel, n_ctx_tiles=n_ctx_tiles, tile_offset=off, final=final),
        grid=(B, R // ROW_TILE - off),
        in_specs=[pl.BlockSpec((1, ROW_TILE, D_MODEL), lambda b, i: (b, i + off, 0)),
                  pl.BlockSpec((D_MODEL, D_MODEL), lambda b, i: (0, 0))]
        + _stream_specs(stream, n_ctx_tiles, off) + [
            pl.BlockSpec((MOD_ROWS, 3 * D_MODEL), lambda b, i: (0, 0)),
            pl.BlockSpec((1, D_MODEL), lambda b, i: (0, 0))],
        out_specs=pl.BlockSpec((1, ROW_TILE, D_MODEL), lambda b, i: (b, i, 0)),
        out_shape=jax.ShapeDtypeStruct((B, R - off * ROW_TILE, D_MODEL), F32),
        compiler_params=_params("parallel", "parallel"),
        name="out_proj_final" if final else "out_proj",
    )(merged, w_out, stream[0], stream[1], mod, final_g)


def _layer(stream, R, mod, n_ctx, final, final_g, norm_g, w_in, na_rpb, pool_w, pool_scale, rw_mu, rw_w0, rw_w2,
           rw_a0, rw_a2, rw_k_k, rw_k_a, rw_r_k, rw_lnx_g, rw_lnx_b, w_branch, w_out):
    B = stream[0].shape[0]
    rows = (R - n_ctx) // GRID_W

    h = _norm_mod(stream, R, norm_g[None], mod, n_ctx).reshape(B * R, D_MODEL)
    lo = N_MAIN + 2 * RWKV_LORA
    n_rw = COL_RW_R * W_BRANCH
    w_rw = _head_major(w_in[:, n_rw:N_MAIN].reshape(D_MODEL, N_MAIN // W_BRANCH - COL_RW_R, W_BRANCH))
    w_main = jnp.concatenate([w_in[:, :n_rw], w_rw.reshape(D_MODEL, N_MAIN - n_rw)], axis=1)
    p_main = _matmul(h, w_main.astype(BF16), F32, "in_proj_main").reshape(B, R, N_MAIN)
    p_lora = _matmul(h, w_in[:, N_MAIN:lo].astype(BF16), F32, "in_proj_lora").reshape(B, R, 2 * RWKV_LORA)
    p_merge = _matmul(h, w_in[:, lo:].astype(BF16), F32, "in_proj_merge")

    b_na = _na_attention(p_main, _na_bias_tables(na_rpb, rows), n_ctx)
    b_pool = _pool(p_main, pool_w.astype(BF16), pool_scale[None], n_ctx)

    par = jnp.zeros((P_ROWS, W_BRANCH), F32)
    par = par.at[P_MU_R:P_MU_V + 1].set(rw_mu).at[P_W0_F:P_W0_B + 1].set(rw_w0).at[P_A0_F:P_A0_B + 1].set(rw_a0)
    par = _head_major(par.at[P_K_K].set(rw_k_k).at[P_K_A].set(rw_k_a).at[P_R_K].set(rw_r_k.reshape(-1)))
    zeros = jnp.zeros_like(rw_w2)
    w2 = _head_major(jnp.concatenate([rw_w2, zeros], axis=1)).astype(BF16)
    a2 = _head_major(jnp.concatenate([zeros, rw_a2], axis=1)).astype(BF16)
    v, p_f, k_f, b_f, a_f, r_f, p_b, k_b, b_b, a_b, r_b, bonus = _rwkv_features(p_main, p_lora, par, w2, a2, n_ctx)
    flip = jnp.asarray(np.eye(RELAYOUT_ROWS)[::-1], BF16)
    y = _wkv_scan(*[_to_scan(zf, zb, flip, n_ctx)
                    for zf, zb in ((p_f, p_b), (k_f, k_b), (b_f, b_b), (a_f, a_b), (r_f, r_b), (v, v))])
    y_fwd, y_bwd = _from_scan(y, flip, B, n_ctx)
    lnx_gb = _head_major(jnp.zeros((8, W_BRANCH), F32).at[0].set(rw_lnx_g).at[1].set(rw_lnx_b))
    b_rw = _rwkv_readout(y_fwd, y_bwd, bonus, p_main, lnx_gb)

    def flat(z):
        return z.reshape(B * R, W_BRANCH)

    w_rw_out = _head_major(w_branch[2].T).T
    w_br = jnp.stack([w_branch[0], w_branch[1], w_rw_out]).astype(BF16)
    merged = _merge(flat(b_na), flat(b_pool), flat(b_rw), p_merge, w_br)
    return _out_proj(merged.reshape(B, R, D_MODEL), w_out.astype(BF16), stream, mod, final_g[None], n_ctx, final)


def kernel(x, c, ctx, c_ctx, norm_g, w_mod, b_mod, w_in, na_rpb, pool_w, pool_scale, rw_mu, rw_w0, rw_w2, rw_a0,
           rw_a2, rw_k_k, rw_k_a, rw_r_k, rw_lnx_g, rw_lnx_b, w_branch, w_out, final_g):
    B, T, _ = x.shape
    n_ctx = ctx.shape[1]
    depth = w_in.shape[0]
    assert B <= CTX_MOD_ROW and n_ctx % ROW_TILE == 0 and T % ROW_TILE == 0
    assert ROW_TILE % RELAYOUT_ROWS == 0 and RELAYOUT_ROWS % SCAN_STEPS == 0

    cond = jnp.zeros((MOD_ROWS, D_MODEL), F32).at[:B].set(c).at[CTX_MOD_ROW].set(c_ctx)
    mods = _modulation(cond, w_mod.astype(BF16), b_mod[:, None, :])
    stream = (ctx, x, n_ctx // ROW_TILE)
    for layer in range(depth):
        out = _layer(stream, n_ctx + T, mods[layer], n_ctx, layer == depth - 1, final_g, norm_g[layer], w_in[layer],
                     na_rpb[layer], pool_w[layer], pool_scale[layer], rw_mu[layer], rw_w0[layer], rw_w2[layer],
                     rw_a0[layer], rw_a2[layer], rw_k_k[layer], rw_k_a[layer], rw_r_k[layer], rw_lnx_g[layer],
                     rw_lnx_b[layer], w_branch[layer], w_out[layer])
        stream = (out, out, 0)
    return out
```

```python
import functools

import numpy as np
import jax
import jax.numpy as jnp
from jax import lax
from jax.experimental import pallas as pl
from jax.experimental.pallas import tpu as pltpu

F32 = jnp.float32
BF16 = jnp.bfloat16

D_MODEL = 2048
W_BRANCH = D_MODEL // 2
N_BRANCH = 3
N_HEADS = 16
HEAD_DIM = 64
GRID_W = 64
NA_WIN_H = 8
NA_WIN_W = 16
POOL_WINDOWS = (2, 4, 8, 16)
POOL_GROUP_DIM = W_BRANCH // len(POOL_WINDOWS)
POOL_HALO = 8
RWKV_LORA = 64
RMS_EPS = 1e-6
LNX_EPS = 64e-5
NEG_INF = -1e30

LANES = 128
ROW_TILE = 256
NA_Q_ROWS = ROW_TILE // GRID_W
NA_K_ROWS = NA_Q_ROWS + NA_WIN_H
NA_K_TOK = NA_K_ROWS * GRID_W
CTX_MOD_ROW = 4
MOD_ROWS = 8
SCAN_STEPS = 64
SCAN_J_UNROLL = 32
RELAYOUT_UNROLL = 8
RELAYOUT_ROWS = 128
VMEM_LIMIT = 56 << 20

COL_Q, COL_K, COL_V, COL_NA_GATE, COL_POOL_U, COL_POOL_GATE, COL_RW_R, COL_RW_K, COL_RW_V, COL_RW_GATE = range(10)
N_MAIN = 10 * W_BRANCH

P_MU_R, P_MU_K, P_MU_V, P_W0_F, P_W0_B, P_A0_F, P_A0_B, P_K_K, P_K_A, P_R_K = range(10)
P_ROWS = 16


def _params(*sem):
    return pltpu.CompilerParams(dimension_semantics=sem, vmem_limit_bytes=VMEM_LIMIT)


def _sigmoid(x):
    return 1.0 / (1.0 + jnp.exp(-x))


def _silu(x):
    return x * _sigmoid(x)


def _split3(x):
    hi = x.astype(BF16)
    r1 = x - hi.astype(F32)
    mid = r1.astype(BF16)
    lo = (r1 - mid.astype(F32)).astype(BF16)
    return hi, mid, lo


def _head_major(z):
    lead = z.shape[:-1]
    return z.reshape(lead + (N_HEADS, HEAD_DIM)).swapaxes(-1, -2).reshape(lead + (W_BRANCH,))


def _head_sum(x):
    n_tiles = W_BRANCH // LANES
    part = x[:, :LANES]
    for c in range(1, n_tiles):
        part = part + x[:, c * LANES:(c + 1) * LANES]
    shift = N_HEADS
    while shift < LANES:
        part = part + pltpu.roll(part, shift, 1)
        shift *= 2
    return jnp.concatenate([part] * n_tiles, axis=1)


def _mod_kernel(cond_ref, w_ref, b_ref, o_ref):
    s = _silu(cond_ref[...])
    o_ref[0] = jnp.dot(s.astype(BF16), w_ref[0], preferred_element_type=F32) + b_ref[0]


def _modulation(cond, w_mod, b_mod):
    n_layers = w_mod.shape[0]
    tn = 3 * D_MODEL // 4
    return pl.pallas_call(
        _mod_kernel,
        grid=(n_layers, 4),
        in_specs=[pl.BlockSpec((MOD_ROWS, D_MODEL), lambda l, j: (0, 0)),
                  pl.BlockSpec((1, D_MODEL, tn), lambda l, j: (l, 0, j)),
                  pl.BlockSpec((1, 1, tn), lambda l, j: (l, 0, j))],
        out_specs=pl.BlockSpec((1, MOD_ROWS, tn), lambda l, j: (l, 0, j)),
        out_shape=jax.ShapeDtypeStruct((n_layers, MOD_ROWS, 3 * D_MODEL), F32),
        compiler_params=_params("arbitrary", "arbitrary"),
        name="adaln_modulation",
    )(cond, w_mod, b_mod)


def _mod_row(mod_ref, tile, n_ctx_tiles):
    row = jnp.where(tile < n_ctx_tiles, CTX_MOD_ROW, pl.program_id(0))
    return mod_ref[pl.ds(row, 1), :]


def _rms(x, g):
    return x * lax.rsqrt(jnp.mean(x * x, axis=-1, keepdims=True) + RMS_EPS) * g


def _stream_specs(stream, n_ctx_tiles, tile_offset=0):
    lat_shift = stream[2]
    return [pl.BlockSpec((1, ROW_TILE, D_MODEL), lambda b, i: (b, jnp.minimum(i + tile_offset, n_ctx_tiles - 1), 0)),
            pl.BlockSpec((1, ROW_TILE, D_MODEL), lambda b, i: (b, jnp.maximum(i + tile_offset - lat_shift, 0), 0))]


def _stream_tile(c_ref, l_ref, tile, n_ctx_tiles):
    return jnp.where(tile < n_ctx_tiles, c_ref[0], l_ref[0])


def _norm_mod_kernel(c_ref, l_ref, g_ref, mod_ref, h_ref, *, n_ctx_tiles):
    tile = pl.program_id(1)
    m = _mod_row(mod_ref, tile, n_ctx_tiles)
    shift = m[:, :D_MODEL]
    scale = m[:, D_MODEL:2 * D_MODEL]
    x = _stream_tile(c_ref, l_ref, tile, n_ctx_tiles)
    h_ref[0] = (_rms(x, g_ref[...]) * (1.0 + scale) + shift).astype(BF16)


def _norm_mod(stream, n_rows, norm_g, mod, n_ctx):
    B = stream[0].shape[0]
    n_ctx_tiles = n_ctx // ROW_TILE
    return pl.pallas_call(
        functools.partial(_norm_mod_kernel, n_ctx_tiles=n_ctx_tiles),
        grid=(B, n_rows // ROW_TILE),
        in_specs=_stream_specs(stream, n_ctx_tiles) + [
            pl.BlockSpec((1, D_MODEL), lambda b, i: (0, 0)),
            pl.BlockSpec((MOD_ROWS, 3 * D_MODEL), lambda b, i: (0, 0))],
        out_specs=pl.BlockSpec((1, ROW_TILE, D_MODEL), lambda b, i: (b, i, 0)),
        out_shape=jax.ShapeDtypeStruct((B, n_rows, D_MODEL), BF16),
        compiler_params=_params("parallel", "parallel"),
        name="norm_modulate",
    )(stream[0], stream[1], norm_g, mod)


def _mm_kernel(a_ref, w_ref, o_ref):
    o_ref[...] = jnp.dot(a_ref[...], w_ref[...], preferred_element_type=F32).astype(o_ref.dtype)


def _row_tile(m):
    for t in (1024, 512, 256):
        if m % t == 0:
            return t
    raise ValueError(f"row count {m} is not a multiple of {ROW_TILE}")


def _matmul(a, w, out_dtype, name):
    M, K = a.shape
    N = w.shape[1]
    tm = _row_tile(M)
    tn = min(N, 1024)
    return pl.pallas_call(
        _mm_kernel,
        grid=(N // tn, M // tm),
        in_specs=[pl.BlockSpec((tm, K), lambda j, i: (i, 0)),
                  pl.BlockSpec((K, tn), lambda j, i: (0, j))],
        out_specs=pl.BlockSpec((tm, tn), lambda j, i: (i, j)),
        out_shape=jax.ShapeDtypeStruct((M, N), out_dtype),
        compiler_params=_params("parallel", "parallel"),
        name=name,
    )(a, w)


def _na_bias_tables(rpb, rows):
    n_blocks = rows // NA_Q_ROWS
    n_off = 2 * NA_WIN_H - 1
    col = np.arange(GRID_W)
    c0 = np.clip(col - NA_WIN_W // 2, 0, GRID_W - NA_WIN_W)
    valid_c = (col[None, :] >= c0[:, None]) & (col[None, :] < c0[:, None] + NA_WIN_W)
    col_off = np.clip(col[None, :] - col[:, None] + NA_WIN_W - 1, 0, 2 * NA_WIN_W - 2)
    pick_c = jnp.asarray(np.eye(2 * NA_WIN_W - 1)[col_off], F32)
    tile = jnp.einsum("hrc,qpc->hrqp", rpb.astype(F32), pick_c, precision=lax.Precision.HIGHEST)
    tile = jnp.where(valid_c, tile, NEG_INF)
    tile = jnp.concatenate([tile, jnp.full((N_HEADS, 1, GRID_W, GRID_W), NEG_INF, F32)], axis=1)
    tile = jnp.concatenate([tile, tile], axis=-1)
    picks = []
    for m in (0, 1, n_blocks - 1):
        q_row = NA_Q_ROWS * m + np.arange(NA_Q_ROWS)
        k_row = int(np.clip(NA_Q_ROWS * m - NA_Q_ROWS, 0, rows - NA_K_ROWS)) + np.arange(NA_K_ROWS)
        r0 = np.clip(q_row - NA_WIN_H // 2, 0, rows - NA_WIN_H)
        valid_r = (k_row[None, :] >= r0[:, None]) & (k_row[None, :] < r0[:, None] + NA_WIN_H)
        row_off = np.clip(k_row[None, :] - q_row[:, None] + NA_WIN_H - 1, 0, n_off - 1)
        picks.append(np.where(valid_r, row_off, n_off))
    picks = np.stack(picks)

    def build(t_ref, o_ref):
        left = lax.broadcasted_iota(jnp.int32, (1, LANES), 1) < GRID_W
        for ty in range(picks.shape[0]):
            for a in range(NA_Q_ROWS):
                for kp in range(NA_K_ROWS // 2):
                    pair = jnp.where(left, t_ref[0, int(picks[ty, a, 2 * kp])], t_ref[0, int(picks[ty, a, 2 * kp + 1])])
                    o_ref[ty, 0, a * GRID_W:(a + 1) * GRID_W, kp * LANES:(kp + 1) * LANES] = pair

    return pl.pallas_call(
        build,
        grid=(N_HEADS,),
        in_specs=[pl.BlockSpec((1, n_off + 1, GRID_W, LANES), lambda h: (h, 0, 0, 0))],
        out_specs=pl.BlockSpec((picks.shape[0], 1, ROW_TILE, NA_K_TOK), lambda h: (0, h, 0, 0)),
        out_shape=jax.ShapeDtypeStruct((picks.shape[0], N_HEADS, ROW_TILE, NA_K_TOK), F32),
        compiler_params=_params("parallel"),
        name="na_bias_tables",
    )(tile)


def _attend(qe, keys, vals, biases):
    dn = (((1,), (1,)), ((), ()))
    scores = []
    for kk, bias in zip(keys, biases):
        s = lax.dot_general(qe, kk, dn, preferred_element_type=F32)
        scores.append(s if bias is None else s + bias)
    m = scores[0].max(axis=-1, keepdims=True)
    for s in scores[1:]:
        m = jnp.maximum(m, s.max(axis=-1, keepdims=True))
    num, den = None, None
    for s, vv in zip(scores, vals):
        p = jnp.exp(s - m)
        l = p.sum(axis=-1, keepdims=True)
        o = jnp.dot(p.astype(BF16), vv, preferred_element_type=F32)
        num = o if num is None else num + o
        den = l if den is None else den + l
    return num / den


def _na_kernel(q_ref, k_ref, v_ref, g_ref, bias_ref, o_ref, *, n_ctx, rows):
    j = pl.program_id(2)
    lane = lax.broadcasted_iota(jnp.int32, (1, LANES), 1)
    in_head = (lane < HEAD_DIM, lane >= HEAD_DIM)
    q = q_ref[0] * (HEAD_DIM ** -0.5)
    kc = k_ref[0, 0:n_ctx, :].astype(BF16)
    vc = v_ref[0, 0:n_ctx, :].astype(BF16)

    def heads(q):
        return [jnp.where(in_head[e], q, 0.0).astype(BF16) for e in range(2)]

    def finish(o0, o1):
        o = jnp.where(in_head[0], o0, o1)
        o_ref[0] = (o * _silu(g_ref[0])).astype(o_ref.dtype)

    @pl.when(j == 0)
    def _():
        finish(*[_attend(qe, [kc], [vc], [None]) for qe in heads(q)])

    @pl.when(j > 0)
    def _():
        k_row = jnp.clip(NA_Q_ROWS * (j - 1) - NA_Q_ROWS, 0, rows - NA_K_ROWS)
        start = pl.multiple_of(n_ctx + k_row * GRID_W, GRID_W)
        kw = k_ref[0, pl.ds(start, NA_K_TOK), :].astype(BF16)
        vw = v_ref[0, pl.ds(start, NA_K_TOK), :].astype(BF16)
        finish(*[_attend(qe, [kw, kc], [vw, vc], [bias_ref[0, e], None])
                 for e, qe in enumerate(heads(q))])


def _na_attention(p_main, bias_tables, n_ctx):
    B, R, _ = p_main.shape
    rows = (R - n_ctx) // GRID_W
    n_blocks = rows // NA_Q_ROWS
    pairs = W_BRANCH // LANES
    assert n_ctx == ROW_TILE and rows >= NA_K_ROWS and rows % NA_Q_ROWS == 0

    def col(c):
        return lambda b, hp, j: (b, 0, c * pairs + hp)

    def bias_idx(b, hp, j):
        return (jnp.where(j <= 1, 0, jnp.where(j == n_blocks, 2, 1)), hp, 0, 0)

    return pl.pallas_call(
        functools.partial(_na_kernel, n_ctx=n_ctx, rows=rows),
        grid=(B, pairs, n_blocks + 1),
        in_specs=[pl.BlockSpec((1, ROW_TILE, LANES), lambda b, hp, j: (b, j, COL_Q * pairs + hp)),
                  pl.BlockSpec((1, R, LANES), col(COL_K)),
                  pl.BlockSpec((1, R, LANES), col(COL_V)),
                  pl.BlockSpec((1, ROW_TILE, LANES), lambda b, hp, j: (b, j, COL_NA_GATE * pairs + hp)),
                  pl.BlockSpec((1, 2, ROW_TILE, NA_K_TOK), bias_idx)],
        out_specs=pl.BlockSpec((1, ROW_TILE, LANES), lambda b, hp, j: (b, j, hp)),
        out_shape=jax.ShapeDtypeStruct((B, R, W_BRANCH), BF16),
        compiler_params=_params("parallel", "parallel", "arbitrary"),
        name="neighbourhood_attention",
    )(p_main, p_main, p_main, p_main, bias_tables)


def _pool_kernel(u_ref, g_ref, w_ref, sc_ref, o_ref, pad_ref, *, n_ctx, n_lat):
    grp = pl.program_id(1)
    w = w_ref[0]
    scale = sc_ref[...]

    def run(win):
        half = win // 2
        for seq_start, seq_len in ((0, n_ctx), (n_ctx, n_lat)):
            zeros = jnp.zeros((POOL_HALO, POOL_GROUP_DIM), F32)
            pad_ref[0:POOL_HALO, :] = zeros
            pad_ref[POOL_HALO:POOL_HALO + seq_len, :] = u_ref[0, seq_start:seq_start + seq_len, :]
            pad_ref[POOL_HALO + seq_len:2 * POOL_HALO + seq_len, :] = zeros

            def chunk(c, carry):
                base = pl.multiple_of(c * ROW_TILE, ROW_TILE)
                x = pad_ref[pl.ds(base, ROW_TILE + 2 * POOL_HALO), :]
                acc = x[POOL_HALO - half:POOL_HALO - half + ROW_TILE]
                for o in range(-half + 1, half):
                    acc = acc + x[POOL_HALO + o:POOL_HALO + o + ROW_TILE]
                t = base + lax.broadcasted_iota(jnp.int32, (ROW_TILE, 1), 0)
                cnt = jnp.minimum(t + half, seq_len) - jnp.maximum(t - half, 0)
                diff = acc / cnt.astype(F32) - x[POOL_HALO:POOL_HALO + ROW_TILE]
                y = jnp.dot(diff.astype(BF16), w, preferred_element_type=F32) * scale
                rows = pl.ds(seq_start + base, ROW_TILE)
                o_ref[0, rows, :] = (y * _silu(g_ref[0, rows, :])).astype(o_ref.dtype)
                return carry

            lax.fori_loop(0, seq_len // ROW_TILE, chunk, 0)

    for gi, win in enumerate(POOL_WINDOWS):
        pl.when(grp == gi)(functools.partial(run, win))


def _pool(p_main, pool_w, pool_scale, n_ctx):
    B, R, _ = p_main.shape
    groups = len(POOL_WINDOWS)
    return pl.pallas_call(
        functools.partial(_pool_kernel, n_ctx=n_ctx, n_lat=R - n_ctx),
        grid=(B, groups),
        in_specs=[pl.BlockSpec((1, R, POOL_GROUP_DIM), lambda b, g: (b, 0, COL_POOL_U * groups + g)),
                  pl.BlockSpec((1, R, POOL_GROUP_DIM), lambda b, g: (b, 0, COL_POOL_GATE * groups + g)),
                  pl.BlockSpec((1, POOL_GROUP_DIM, POOL_GROUP_DIM), lambda b, g: (g, 0, 0)),
                  pl.BlockSpec((1, POOL_GROUP_DIM), lambda b, g: (0, g))],
        out_specs=pl.BlockSpec((1, R, POOL_GROUP_DIM), lambda b, g: (b, 0, g)),
        out_shape=jax.ShapeDtypeStruct((B, R, W_BRANCH), BF16),
        scratch_shapes=[pltpu.VMEM((R - n_ctx + 2 * POOL_HALO, POOL_GROUP_DIM), F32)],
        compiler_params=_params("parallel", "arbitrary"),
        name="multiscale_pool",
    )(p_main, p_main, pool_w, pool_scale)


def _rwkv_feat_kernel(r_ref, rp_ref, rn_ref, k_ref, kp_ref, kn_ref, v_ref, vp_ref, vn_ref, lora_ref,
                      par_ref, w2_ref, a2_ref, tri_ref,
                      vo_ref, pf_ref, kf_ref, bf_ref, af_ref, rf_ref, pb_ref, kb_ref, bb_ref, ab_ref, rb_ref, bonus_ref,
                      *, n_ctx_tiles, n_tiles):
    i = pl.program_id(1)
    first = (i == 0) | (i == n_ctx_tiles)
    last = (i == n_ctx_tiles - 1) | (i == n_tiles - 1)
    row = lax.broadcasted_iota(jnp.int32, (ROW_TILE, 1), 0)
    par = par_ref[...]

    def prm(p):
        return par[p:p + 1, :]

    def mix(z_ref, prev_ref, next_ref, mu):
        z = z_ref[0]
        prev = jnp.where(first, 0.0, prev_ref[0, 7:8, :])
        nxt = jnp.where(last, 0.0, next_ref[0, 0:1, :])
        z_prev = jnp.where(row == 0, prev, pltpu.roll(z, 1, 0))
        z_next = jnp.where(row == ROW_TILE - 1, nxt, pltpu.roll(z, ROW_TILE - 1, 0))
        return z + mu * (0.5 * (z_prev + z_next) - z)

    r = mix(r_ref, rp_ref, rn_ref, prm(P_MU_R))
    k = mix(k_ref, kp_ref, kn_ref, prm(P_MU_K))
    v = mix(v_ref, vp_ref, vn_ref, prm(P_MU_V))
    vo_ref[0] = v

    kk = k * prm(P_K_K)
    kk = kk * jnp.minimum(lax.rsqrt(_head_sum(kk * kk)), 1e12)

    lora = lora_ref[0]
    lane = lax.broadcasted_iota(jnp.int32, (1, LANES), 1)
    lora = jnp.where(lane < RWKV_LORA, jnp.tanh(lora), lora).astype(BF16)
    k_sum = None
    outs = ((pf_ref, kf_ref, bf_ref, af_ref, rf_ref), (pb_ref, kb_ref, bb_ref, ab_ref, rb_ref))
    for d, (p_out, k_out, b_out, a_out, r_out) in enumerate(outs):
        x = prm(P_W0_F + d) + jnp.dot(lora, w2_ref[d], preferred_element_type=F32)
        w_log = -(jnp.maximum(-x, 0.0) + jnp.log(1.0 + jnp.exp(-jnp.abs(x)))) - 0.5
        neg_log_w = jnp.exp(w_log)
        hi, mid, lo = _split3(neg_log_w)
        tri = tri_ref[d]
        cs = (jnp.dot(tri, hi, preferred_element_type=F32) + jnp.dot(tri, mid, preferred_element_type=F32)
              + jnp.dot(tri, lo, preferred_element_type=F32))
        grow = jnp.exp(cs)
        shrink = jnp.exp(-cs)
        a = 0.5 + 0.5 * jnp.tanh(0.5 * (prm(P_A0_F + d) + jnp.dot(lora, a2_ref[d], preferred_element_type=F32)))
        k_d = k * (1.0 + (a - 1.0) * prm(P_K_A))
        p_out[0] = shrink
        k_out[0] = k_d * grow
        b_out[0] = kk * a * grow
        a_out[0] = -kk * jnp.exp(neg_log_w - cs)
        r_out[0] = r * shrink
        k_sum = k_d if k_sum is None else k_sum + k_d
    bonus_ref[0] = _head_sum(r * k_sum * prm(P_R_K)) * v


def _rwkv_features(p_main, p_lora, par, w2, a2, n_ctx):
    B, R, _ = p_main.shape
    n_tiles = R // ROW_TILE
    sub = ROW_TILE // 8

    def main(c):
        return pl.BlockSpec((1, ROW_TILE, W_BRANCH), lambda b, i: (b, i, c))

    def prev(c):
        return pl.BlockSpec((1, 8, W_BRANCH), lambda b, i: (b, jnp.maximum(i * sub - 1, 0), c))

    def nxt(c):
        return pl.BlockSpec((1, 8, W_BRANCH), lambda b, i: (b, jnp.minimum((i + 1) * sub, n_tiles * sub - 1), c))

    in_specs = []
    for c in (COL_RW_R, COL_RW_K, COL_RW_V):
        in_specs += [main(c), prev(c), nxt(c)]
    in_specs += [pl.BlockSpec((1, ROW_TILE, LANES), lambda b, i: (b, i, 0)),
                 pl.BlockSpec((P_ROWS, W_BRANCH), lambda b, i: (0, 0)),
                 pl.BlockSpec((2, LANES, W_BRANCH), lambda b, i: (0, 0, 0)),
                 pl.BlockSpec((2, LANES, W_BRANCH), lambda b, i: (0, 0, 0)),
                 pl.BlockSpec((2, ROW_TILE, ROW_TILE), lambda b, i: (0, 0, 0))]
    t_idx = np.arange(ROW_TILE)
    same_chunk = t_idx[:, None] // SCAN_STEPS == t_idx[None, :] // SCAN_STEPS
    tri = jnp.asarray(np.stack([same_chunk & (t_idx[None, :] <= t_idx[:, None]),
                                same_chunk & (t_idx[None, :] >= t_idx[:, None])]), BF16)
    out = jax.ShapeDtypeStruct((B, R, W_BRANCH), F32)
    return pl.pallas_call(
        functools.partial(_rwkv_feat_kernel, n_ctx_tiles=n_ctx // ROW_TILE, n_tiles=n_tiles),
        grid=(B, n_tiles),
        in_specs=in_specs,
        out_specs=[pl.BlockSpec((1, ROW_TILE, W_BRANCH), lambda b, i: (b, i, 0))] * 12,
        out_shape=[out] * 12,
        compiler_params=_params("parallel", "parallel"),
        name="rwkv_features",
    )(*([p_main] * 9), p_lora, par, w2, a2, tri)


def _scan_kernel(p_ref, k_ref, b_ref, a_ref, r_ref, v_ref, y_ref, s_ref):
    n = HEAD_DIM

    @pl.when(pl.program_id(0) == 0)
    def _():
        s_ref[...] = jnp.zeros_like(s_ref)

    def row(ref, j, t):
        return ref[0, pl.ds(j * SCAN_STEPS + t, 1), :]

    zero = jnp.zeros((n, s_ref.shape[2]), F32)

    def first_sa(jb, sa):
        for jj in range(SCAN_J_UNROLL):
            j = jb * SCAN_J_UNROLL + jj
            sa = sa + s_ref[j] * row(a_ref, j, 0)
        return sa

    def step(t, sa):
        tile_rows = pl.ds(pl.multiple_of(t * n, n), n)
        vt = v_ref[0, tile_rows, :]
        t_next = jnp.minimum(t + 1, SCAN_STEPS - 1)

        def columns(jb, carry):
            y, sa_next = carry
            for jj in range(SCAN_J_UNROLL):
                j = jb * SCAN_J_UNROLL + jj
                sj = s_ref[j] + sa * row(b_ref, j, t) + vt * row(k_ref, j, t)
                s_ref[j] = sj
                y = y + sj * row(r_ref, j, t)
                sa_next = sa_next + sj * row(a_ref, j, t_next)
            return y, sa_next

        y, sa_next = lax.fori_loop(0, n // SCAN_J_UNROLL, columns, (zero, zero))
        y_ref[0, tile_rows, :] = y
        return sa_next

    def rescale(jb, carry):
        for jj in range(SCAN_J_UNROLL):
            j = jb * SCAN_J_UNROLL + jj
            s_ref[j] = s_ref[j] * row(p_ref, j, SCAN_STEPS - 1)
        return carry

    sa0 = lax.fori_loop(0, n // SCAN_J_UNROLL, first_sa, zero)
    lax.fori_loop(0, SCAN_STEPS, step, sa0)
    lax.fori_loop(0, n // SCAN_J_UNROLL, rescale, 0)


def _wkv_scan(p, k, b, a, r, v):
    n_chunks, rows, chains = p.shape
    spec = pl.BlockSpec((1, rows, chains), lambda s: (s, 0, 0))
    return pl.pallas_call(
        _scan_kernel,
        grid=(n_chunks,),
        in_specs=[spec] * 6,
        out_specs=spec,
        out_shape=jax.ShapeDtypeStruct(p.shape, F32),
        scratch_shapes=[pltpu.VMEM((HEAD_DIM, HEAD_DIM, chains), F32)],
        compiler_params=_params("arbitrary"),
        name="wkv_scan",
    )(p, k, b, a, r, v)


def _flip_rows(x, flip):
    hi, mid, lo = _split3(x)
    return (jnp.dot(flip, hi, preferred_element_type=F32) + jnp.dot(flip, mid, preferred_element_type=F32)
            + jnp.dot(flip, lo, preferred_element_type=F32))


def _mirror_chunk(c, n_ctx_chunks, n_chunks):
    return jnp.where(c < n_ctx_chunks, n_ctx_chunks - 1 - c, n_ctx_chunks + n_chunks - 1 - c)


def _to_scan_kernel(zf_ref, zb_ref, flip_ref, o_ref, t_ref, *, step_major):
    nb = zf_ref.shape[0]
    for b in range(nb):
        t_ref[b] = zf_ref[b].T
        t_ref[nb + b] = _flip_rows(zb_ref[b], flip_ref[...]).T
    sub = RELAYOUT_ROWS // SCAN_STEPS

    def body(n, carry):
        rows = pl.ds(pl.multiple_of(n * N_HEADS, N_HEADS), N_HEADS)
        slabs = [t_ref[g, rows, :] for g in range(2 * nb)]
        tile = jnp.concatenate(slabs, axis=0).T
        for q in range(sub):
            if step_major:
                dst = pl.ds(n, SCAN_STEPS, stride=HEAD_DIM)
            else:
                dst = pl.ds(pl.multiple_of(n * SCAN_STEPS, SCAN_STEPS), SCAN_STEPS)
            o_ref[q, dst, :] = tile[q * SCAN_STEPS:(q + 1) * SCAN_STEPS]
        return carry

    lax.fori_loop(0, HEAD_DIM, body, 0, unroll=RELAYOUT_UNROLL)


def _to_scan(z_fwd, z_bwd, flip, n_ctx, step_major=False):
    B, R, _ = z_fwd.shape
    n_chunks = R // RELAYOUT_ROWS
    n_ctx_chunks = n_ctx // RELAYOUT_ROWS
    sub = RELAYOUT_ROWS // SCAN_STEPS
    chains = 2 * B * N_HEADS
    return pl.pallas_call(
        functools.partial(_to_scan_kernel, step_major=step_major),
        grid=(n_chunks,),
        in_specs=[pl.BlockSpec((B, RELAYOUT_ROWS, W_BRANCH), lambda c: (0, c, 0)),
                  pl.BlockSpec((B, RELAYOUT_ROWS, W_BRANCH),
                               lambda c: (0, _mirror_chunk(c, n_ctx_chunks, n_chunks), 0)),
                  pl.BlockSpec((RELAYOUT_ROWS, RELAYOUT_ROWS), lambda c: (0, 0))],
        out_specs=pl.BlockSpec((sub, HEAD_DIM * SCAN_STEPS, chains), lambda c: (c, 0, 0)),
        out_shape=jax.ShapeDtypeStruct((R // SCAN_STEPS, HEAD_DIM * SCAN_STEPS, chains), F32),
        scratch_shapes=[pltpu.VMEM((2 * B, W_BRANCH, RELAYOUT_ROWS), F32)],
        compiler_params=_params("parallel"),
        name="to_scan_layout",
    )(z_fwd, z_bwd, flip)


def _from_scan_kernel(y_ref, flip_ref, yf_ref, yb_ref, t_ref):
    nb = yf_ref.shape[0]
    sub = RELAYOUT_ROWS // SCAN_STEPS

    def body(n, carry):
        tile = jnp.concatenate([y_ref[q, pl.ds(n, SCAN_STEPS, stride=HEAD_DIM), :] for q in range(sub)], axis=0)
        tile = tile.T
        rows = pl.ds(pl.multiple_of(n * N_HEADS, N_HEADS), N_HEADS)
        for g in range(2 * nb):
            t_ref[g, rows, :] = tile[g * N_HEADS:(g + 1) * N_HEADS]
        return carry

    lax.fori_loop(0, HEAD_DIM, body, 0, unroll=RELAYOUT_UNROLL)
    for b in range(nb):
        yf_ref[b] = t_ref[b].T
        yb_ref[b] = _flip_rows(t_ref[nb + b].T, flip_ref[...])


def _from_scan(y, flip, n_batch, n_ctx):
    R = y.shape[0] * SCAN_STEPS
    n_chunks = R // RELAYOUT_ROWS
    n_ctx_chunks = n_ctx // RELAYOUT_ROWS
    sub = RELAYOUT_ROWS // SCAN_STEPS
    out = jax.ShapeDtypeStruct((n_batch, R, W_BRANCH), F32)
    return pl.pallas_call(
        _from_scan_kernel,
        grid=(n_chunks,),
        in_specs=[pl.BlockSpec((sub, HEAD_DIM * SCAN_STEPS, y.shape[2]), lambda c: (c, 0, 0)),
                  pl.BlockSpec((RELAYOUT_ROWS, RELAYOUT_ROWS), lambda c: (0, 0))],
        out_specs=[pl.BlockSpec((n_batch, RELAYOUT_ROWS, W_BRANCH), lambda c: (0, c, 0)),
                   pl.BlockSpec((n_batch, RELAYOUT_ROWS, W_BRANCH),
                                lambda c: (0, _mirror_chunk(c, n_ctx_chunks, n_chunks), 0))],
        out_shape=[out, out],
        scratch_shapes=[pltpu.VMEM((2 * n_batch, W_BRANCH, RELAYOUT_ROWS), F32)],
        compiler_params=_params("parallel"),
        name="from_scan_layout",
    )(y, flip)


def _rwkv_readout_kernel(yf_ref, yb_ref, bonus_ref, gate_ref, gb_ref, o_ref):
    y = yf_ref[0] + yb_ref[0]
    mu = _head_sum(y) * (1.0 / HEAD_DIM)
    yc = y - mu
    var = _head_sum(yc * yc) * (1.0 / HEAD_DIM)
    gb = gb_ref[...]
    out = yc * lax.rsqrt(var + LNX_EPS) * gb[0:1, :] + gb[1:2, :] + bonus_ref[0]
    o_ref[0] = (out * _silu(gate_ref[0])).astype(o_ref.dtype)


def _rwkv_readout(y_fwd, y_bwd, bonus, p_main, lnx_gb):
    B, R, _ = y_fwd.shape
    tile = pl.BlockSpec((1, ROW_TILE, W_BRANCH), lambda b, i: (b, i, 0))
    return pl.pallas_call(
        _rwkv_readout_kernel,
        grid=(B, R // ROW_TILE),
        in_specs=[tile, tile, tile,
                  pl.BlockSpec((1, ROW_TILE, W_BRANCH), lambda b, i: (b, i, COL_RW_GATE)),
                  pl.BlockSpec((8, W_BRANCH), lambda b, i: (0, 0))],
        out_specs=tile,
        out_shape=jax.ShapeDtypeStruct((B, R, W_BRANCH), BF16),
        compiler_params=_params("parallel", "parallel"),
        name="rwkv_readout",
    )(y_fwd, y_bwd, bonus, p_main, lnx_gb)


def _merge_kernel(na_ref, pool_ref, rw_ref, lna_ref, lpool_ref, lrw_ref, w_ref, o_ref):
    acc = None
    for br, (x_ref, l_ref) in enumerate(((na_ref, lna_ref), (pool_ref, lpool_ref), (rw_ref, lrw_ref))):
        t = _sigmoid(l_ref[...]) * jnp.dot(x_ref[...], w_ref[br], preferred_element_type=F32)
        acc = t if acc is None else acc + t
    o_ref[...] = acc.astype(o_ref.dtype)


def _merge(b_na, b_pool, b_rw, p_merge, w_branch):
    M = b_na.shape[0]
    tm, tn = min(_row_tile(M), 512), 1024
    nb = D_MODEL // tn
    x_spec = pl.BlockSpec((tm, W_BRANCH), lambda j, i: (i, 0))

    def logit(br):
        return pl.BlockSpec((tm, tn), lambda j, i: (i, br * nb + j))

    return pl.pallas_call(
        _merge_kernel,
        grid=(nb, M // tm),
        in_specs=[x_spec, x_spec, x_spec, logit(0), logit(1), logit(2),
                  pl.BlockSpec((N_BRANCH, W_BRANCH, tn), lambda j, i: (0, 0, j))],
        out_specs=pl.BlockSpec((tm, tn), lambda j, i: (i, j)),
        out_shape=jax.ShapeDtypeStruct((M, D_MODEL), BF16),
        compiler_params=_params("parallel", "parallel"),
        name="branch_merge",
    )(b_na, b_pool, b_rw, p_merge, p_merge, p_merge, w_branch)


def _out_kernel(m_ref, w_ref, c_ref, l_ref, mod_ref, fg_ref, o_ref, *, n_ctx_tiles, tile_offset, final):
    tile = pl.program_id(1) + tile_offset
    gate = _mod_row(mod_ref, tile, n_ctx_tiles)[:, 2 * D_MODEL:]
    x = _stream_tile(c_ref, l_ref, tile, n_ctx_tiles)
    x = x + gate * jnp.dot(m_ref[0], w_ref[...], preferred_element_type=F32)
    o_ref[0] = _rms(x, fg_ref[...]) if final else x


def _out_proj(merged, w_out, stream, mod, final_g, n_ctx, final):
    B, R, _ = merged.shape
    n_ctx_tiles = n_ctx // ROW_TILE
    off = n_ctx_tiles if final else 0
    return pl.pallas_call(
        functools.partial(_out_kernel, n_ctx_tiles=n_ctx_tiles, tile_offset=off, final=final),
        grid=(B, R // ROW_TILE - off),
        in_specs=[pl.BlockSpec((1, ROW_TILE, D_MODEL), lambda b, i: (b, i + off, 0)),
                  pl.BlockSpec((D_MODEL, D_MODEL), lambda b, i: (0, 0))]
        + _stream_specs(stream, n_ctx_tiles, off) + [
            pl.BlockSpec((MOD_ROWS, 3 * D_MODEL), lambda b, i: (0, 0)),
            pl.BlockSpec((1, D_MODEL), lambda b, i: (0, 0))],
        out_specs=pl.BlockSpec((1, ROW_TILE, D_MODEL), lambda b, i: (b, i, 0)),
        out_shape=jax.ShapeDtypeStruct((B, R - off * ROW_TILE, D_MODEL), F32),
        compiler_params=_params("parallel", "parallel"),
        name="out_proj_final" if final else "out_proj",
    )(merged, w_out, stream[0], stream[1], mod, final_g)


def _layer(stream, R, mod, n_ctx, final, final_g, norm_g, w_in, na_rpb, pool_w, pool_scale, rw_mu, rw_w0, rw_w2,
           rw_a0, rw_a2, rw_k_k, rw_k_a, rw_r_k, rw_lnx_g, rw_lnx_b, w_branch, w_out):
    B = stream[0].shape[0]
    rows = (R - n_ctx) // GRID_W

    h = _norm_mod(stream, R, norm_g[None], mod, n_ctx).reshape(B * R, D_MODEL)
    lo = N_MAIN + 2 * RWKV_LORA
    n_rw = COL_RW_R * W_BRANCH
    w_rw = _head_major(w_in[:, n_rw:N_MAIN].reshape(D_MODEL, N_MAIN // W_BRANCH - COL_RW_R, W_BRANCH))
    w_main = jnp.concatenate([w_in[:, :n_rw], w_rw.reshape(D_MODEL, N_MAIN - n_rw)], axis=1)
    p_main = _matmul(h, w_main.astype(BF16), F32, "in_proj_main").reshape(B, R, N_MAIN)
    p_lora = _matmul(h, w_in[:, N_MAIN:lo].astype(BF16), F32, "in_proj_lora").reshape(B, R, 2 * RWKV_LORA)
    p_merge = _matmul(h, w_in[:, lo:].astype(BF16), F32, "in_proj_merge")

    b_na = _na_attention(p_main, _na_bias_tables(na_rpb, rows), n_ctx)
    b_pool = _pool(p_main, pool_w.astype(BF16), pool_scale[None], n_ctx)

    par = jnp.zeros((P_ROWS, W_BRANCH), F32)
    par = par.at[P_MU_R:P_MU_V + 1].set(rw_mu).at[P_W0_F:P_W0_B + 1].set(rw_w0).at[P_A0_F:P_A0_B + 1].set(rw_a0)
    par = _head_major(par.at[P_K_K].set(rw_k_k).at[P_K_A].set(rw_k_a).at[P_R_K].set(rw_r_k.reshape(-1)))
    zeros = jnp.zeros_like(rw_w2)
    w2 = _head_major(jnp.concatenate([rw_w2, zeros], axis=1)).astype(BF16)
    a2 = _head_major(jnp.concatenate([zeros, rw_a2], axis=1)).astype(BF16)
    v, p_f, k_f, b_f, a_f, r_f, p_b, k_b, b_b, a_b, r_b, bonus = _rwkv_features(p_main, p_lora, par, w2, a2, n_ctx)
    flip = jnp.asarray(np.eye(RELAYOUT_ROWS)[::-1], BF16)
    y = _wkv_scan(*[_to_scan(zf, zb, flip, n_ctx)
                    for zf, zb in ((p_f, p_b), (k_f, k_b), (b_f, b_b), (a_f, a_b), (r_f, r_b))],
                  _to_scan(v, v, flip, n_ctx, step_major=True))
    y_fwd, y_bwd = _from_scan(y, flip, B, n_ctx)
    lnx_gb = _head_major(jnp.zeros((8, W_BRANCH), F32).at[0].set(rw_lnx_g).at[1].set(rw_lnx_b))
    b_rw = _rwkv_readout(y_fwd, y_bwd, bonus, p_main, lnx_gb)

    def flat(z):
        return z.reshape(B * R, W_BRANCH)

    w_rw_out = _head_major(w_branch[2].T).T
    w_br = jnp.stack([w_branch[0], w_branch[1], w_rw_out]).astype(BF16)
    merged = _merge(flat(b_na), flat(b_pool), flat(b_rw), p_merge, w_br)
    return _out_proj(merged.reshape(B, R, D_MODEL), w_out.astype(BF16), stream, mod, final_g[None], n_ctx, final)


def kernel(x, c, ctx, c_ctx, norm_g, w_mod, b_mod, w_in, na_rpb, pool_w, pool_scale, rw_mu, rw_w0, rw_w2, rw_a0,
           rw_a2, rw_k_k, rw_k_a, rw_r_k, rw_lnx_g, rw_lnx_b, w_branch, w_out, final_g):
    B, T, _ = x.shape
    n_ctx = ctx.shape[1]
    depth = w_in.shape[0]
    assert B <= CTX_MOD_ROW and n_ctx % ROW_TILE == 0 and T % ROW_TILE == 0
    assert ROW_TILE % RELAYOUT_ROWS == 0 and RELAYOUT_ROWS % SCAN_STEPS == 0

    cond = jnp.zeros((MOD_ROWS, D_MODEL), F32).at[:B].set(c).at[CTX_MOD_ROW].set(c_ctx)
    mods = _modulation(cond, w_mod.astype(BF16), b_mod[:, None, :])
    stream = (ctx, x, n_ctx // ROW_TILE)
    for layer in range(depth):
        out = _layer(stream, n_ctx + T, mods[layer], n_ctx, layer == depth - 1, final_g, norm_g[layer], w_in[layer],
                     na_rpb[layer], pool_w[layer], pool_scale[layer], rw_mu[layer], rw_w0[layer], rw_w2[layer],
                     rw_a0[layer], rw_a2[layer], rw_k_k[layer], rw_k_a[layer], rw_r_k[layer], rw_lnx_g[layer],
                     rw_lnx_b[layer], w_branch[layer], w_out[layer])
        stream = (out, out, 0)
    return out
```

```python
import functools

import numpy as np
import jax
import jax.numpy as jnp
from jax import lax
from jax.experimental import pallas as pl
from jax.experimental.pallas import tpu as pltpu

F32 = jnp.float32
BF16 = jnp.bfloat16

D_MODEL = 2048
W_BRANCH = D_MODEL // 2
N_BRANCH = 3
N_HEADS = 16
HEAD_DIM = 64
GRID_W = 64
NA_WIN_H = 8
NA_WIN_W = 16
POOL_WINDOWS = (2, 4, 8, 16)
POOL_GROUP_DIM = W_BRANCH // len(POOL_WINDOWS)
POOL_HALO = 8
RWKV_LORA = 64
RMS_EPS = 1e-6
LNX_EPS = 64e-5
NEG_INF = -1e30

LANES = 128
ROW_TILE = 256
NA_Q_ROWS = ROW_TILE // GRID_W
NA_K_ROWS = NA_Q_ROWS + NA_WIN_H
NA_K_TOK = NA_K_ROWS * GRID_W
CTX_MOD_ROW = 4
MOD_ROWS = 8
SCAN_STEPS = 64
SCAN_J_UNROLL = 32
RELAYOUT_UNROLL = 8
RELAYOUT_ROWS = 128
VMEM_LIMIT = 56 << 20

COL_Q, COL_K, COL_V, COL_NA_GATE, COL_POOL_U, COL_POOL_GATE, COL_RW_R, COL_RW_K, COL_RW_V, COL_RW_GATE = range(10)
N_MAIN = 10 * W_BRANCH

P_MU_R, P_MU_K, P_MU_V, P_W0_F, P_W0_B, P_A0_F, P_A0_B, P_K_K, P_K_A, P_R_K = range(10)
P_ROWS = 16


def _params(*sem):
    return pltpu.CompilerParams(dimension_semantics=sem, vmem_limit_bytes=VMEM_LIMIT)


def _sigmoid(x):
    return 1.0 / (1.0 + jnp.exp(-x))


def _silu(x):
    return x * _sigmoid(x)


def _split3(x):
    hi = x.astype(BF16)
    r1 = x - hi.astype(F32)
    mid = r1.astype(BF16)
    lo = (r1 - mid.astype(F32)).astype(BF16)
    return hi, mid, lo


def _head_major(z):
    lead = z.shape[:-1]
    return z.reshape(lead + (N_HEADS, HEAD_DIM)).swapaxes(-1, -2).reshape(lead + (W_BRANCH,))


def _head_sum(x):
    n_tiles = W_BRANCH // LANES
    part = x[:, :LANES]
    for c in range(1, n_tiles):
        part = part + x[:, c * LANES:(c + 1) * LANES]
    shift = N_HEADS
    while shift < LANES:
        part = part + pltpu.roll(part, shift, 1)
        shift *= 2
    return jnp.concatenate([part] * n_tiles, axis=1)


def _mod_kernel(cond_ref, w_ref, b_ref, o_ref):
    s = _silu(cond_ref[...])
    o_ref[0] = jnp.dot(s.astype(BF16), w_ref[0], preferred_element_type=F32) + b_ref[0]


def _modulation(cond, w_mod, b_mod):
    n_layers = w_mod.shape[0]
    tn = 3 * D_MODEL // 4
    return pl.pallas_call(
        _mod_kernel,
        grid=(n_layers, 4),
        in_specs=[pl.BlockSpec((MOD_ROWS, D_MODEL), lambda l, j: (0, 0)),
                  pl.BlockSpec((1, D_MODEL, tn), lambda l, j: (l, 0, j)),
                  pl.BlockSpec((1, 1, tn), lambda l, j: (l, 0, j))],
        out_specs=pl.BlockSpec((1, MOD_ROWS, tn), lambda l, j: (l, 0, j)),
        out_shape=jax.ShapeDtypeStruct((n_layers, MOD_ROWS, 3 * D_MODEL), F32),
        compiler_params=_params("arbitrary", "arbitrary"),
        name="adaln_modulation",
    )(cond, w_mod, b_mod)


def _mod_row(mod_ref, tile, n_ctx_tiles):
    row = jnp.where(tile < n_ctx_tiles, CTX_MOD_ROW, pl.program_id(0))
    return mod_ref[pl.ds(row, 1), :]


def _rms(x, g):
    return x * lax.rsqrt(jnp.mean(x * x, axis=-1, keepdims=True) + RMS_EPS) * g


def _stream_specs(stream, n_ctx_tiles, tile_offset=0):
    lat_shift = stream[2]
    return [pl.BlockSpec((1, ROW_TILE, D_MODEL), lambda b, i: (b, jnp.minimum(i + tile_offset, n_ctx_tiles - 1), 0)),
            pl.BlockSpec((1, ROW_TILE, D_MODEL), lambda b, i: (b, jnp.maximum(i + tile_offset - lat_shift, 0), 0))]


def _stream_tile(c_ref, l_ref, tile, n_ctx_tiles):
    return jnp.where(tile < n_ctx_tiles, c_ref[0], l_ref[0])


def _norm_mod_kernel(c_ref, l_ref, g_ref, mod_ref, h_ref, *, n_ctx_tiles):
    tile = pl.program_id(1)
    m = _mod_row(mod_ref, tile, n_ctx_tiles)
    shift = m[:, :D_MODEL]
    scale = m[:, D_MODEL:2 * D_MODEL]
    x = _stream_tile(c_ref, l_ref, tile, n_ctx_tiles)
    h_ref[0] = (_rms(x, g_ref[...]) * (1.0 + scale) + shift).astype(BF16)


def _norm_mod(stream, n_rows, norm_g, mod, n_ctx):
    B = stream[0].shape[0]
    n_ctx_tiles = n_ctx // ROW_TILE
    return pl.pallas_call(
        functools.partial(_norm_mod_kernel, n_ctx_tiles=n_ctx_tiles),
        grid=(B, n_rows // ROW_TILE),
        in_specs=_stream_specs(stream, n_ctx_tiles) + [
            pl.BlockSpec((1, D_MODEL), lambda b, i: (0, 0)),
            pl.BlockSpec((MOD_ROWS, 3 * D_MODEL), lambda b, i: (0, 0))],
        out_specs=pl.BlockSpec((1, ROW_TILE, D_MODEL), lambda b, i: (b, i, 0)),
        out_shape=jax.ShapeDtypeStruct((B, n_rows, D_MODEL), BF16),
        compiler_params=_params("parallel", "parallel"),
        name="norm_modulate",
    )(stream[0], stream[1], norm_g, mod)


def _mm_kernel(a_ref, w_ref, o_ref):
    o_ref[...] = jnp.dot(a_ref[...], w_ref[...], preferred_element_type=F32).astype(o_ref.dtype)


def _row_tile(m):
    for t in (1024, 512, 256):
        if m % t == 0:
            return t
    raise ValueError(f"row count {m} is not a multiple of {ROW_TILE}")


def _matmul(a, w, out_dtype, name):
    M, K = a.shape
    N = w.shape[1]
    tm = _row_tile(M)
    tn = min(N, 1024)
    return pl.pallas_call(
        _mm_kernel,
        grid=(N // tn, M // tm),
        in_specs=[pl.BlockSpec((tm, K), lambda j, i: (i, 0)),
                  pl.BlockSpec((K, tn), lambda j, i: (0, j))],
        out_specs=pl.BlockSpec((tm, tn), lambda j, i: (i, j)),
        out_shape=jax.ShapeDtypeStruct((M, N), out_dtype),
        compiler_params=_params("parallel", "parallel"),
        name=name,
    )(a, w)


def _na_bias_tables(rpb, rows):
    n_blocks = rows // NA_Q_ROWS
    n_off = 2 * NA_WIN_H - 1
    col = np.arange(GRID_W)
    c0 = np.clip(col - NA_WIN_W // 2, 0, GRID_W - NA_WIN_W)
    valid_c = (col[None, :] >= c0[:, None]) & (col[None, :] < c0[:, None] + NA_WIN_W)
    col_off = np.clip(col[None, :] - col[:, None] + NA_WIN_W - 1, 0, 2 * NA_WIN_W - 2)
    pick_c = jnp.asarray(np.eye(2 * NA_WIN_W - 1)[col_off], F32)
    tile = jnp.einsum("hrc,qpc->hrqp", rpb.astype(F32), pick_c, precision=lax.Precision.HIGHEST)
    tile = jnp.where(valid_c, tile, NEG_INF)
    tile = jnp.concatenate([tile, jnp.full((N_HEADS, 1, GRID_W, GRID_W), NEG_INF, F32)], axis=1)
    tile = jnp.concatenate([tile, tile], axis=-1)
    picks = []
    for m in (0, 1, n_blocks - 1):
        q_row = NA_Q_ROWS * m + np.arange(NA_Q_ROWS)
        k_row = int(np.clip(NA_Q_ROWS * m - NA_Q_ROWS, 0, rows - NA_K_ROWS)) + np.arange(NA_K_ROWS)
        r0 = np.clip(q_row - NA_WIN_H // 2, 0, rows - NA_WIN_H)
        valid_r = (k_row[None, :] >= r0[:, None]) & (k_row[None, :] < r0[:, None] + NA_WIN_H)
        row_off = np.clip(k_row[None, :] - q_row[:, None] + NA_WIN_H - 1, 0, n_off - 1)
        picks.append(np.where(valid_r, row_off, n_off))
    picks = np.stack(picks)

    def build(t_ref, o_ref):
        left = lax.broadcasted_iota(jnp.int32, (1, LANES), 1) < GRID_W
        for ty in range(picks.shape[0]):
            for a in range(NA_Q_ROWS):
                for kp in range(NA_K_ROWS // 2):
                    pair = jnp.where(left, t_ref[0, int(picks[ty, a, 2 * kp])], t_ref[0, int(picks[ty, a, 2 * kp + 1])])
                    o_ref[ty, 0, a * GRID_W:(a + 1) * GRID_W, kp * LANES:(kp + 1) * LANES] = pair

    return pl.pallas_call(
        build,
        grid=(N_HEADS,),
        in_specs=[pl.BlockSpec((1, n_off + 1, GRID_W, LANES), lambda h: (h, 0, 0, 0))],
        out_specs=pl.BlockSpec((picks.shape[0], 1, ROW_TILE, NA_K_TOK), lambda h: (0, h, 0, 0)),
        out_shape=jax.ShapeDtypeStruct((picks.shape[0], N_HEADS, ROW_TILE, NA_K_TOK), F32),
        compiler_params=_params("parallel"),
        name="na_bias_tables",
    )(tile)


def _attend(qe, keys, vals, biases):
    dn = (((1,), (1,)), ((), ()))
    scores = []
    for kk, bias in zip(keys, biases):
        s = lax.dot_general(qe, kk, dn, preferred_element_type=F32)
        scores.append(s if bias is None else s + bias)
    m = scores[0].max(axis=-1, keepdims=True)
    for s in scores[1:]:
        m = jnp.maximum(m, s.max(axis=-1, keepdims=True))
    num, den = None, None
    for s, vv in zip(scores, vals):
        p = jnp.exp(s - m)
        l = p.sum(axis=-1, keepdims=True)
        o = jnp.dot(p.astype(BF16), vv, preferred_element_type=F32)
        num = o if num is None else num + o
        den = l if den is None else den + l
    return num / den


def _na_kernel(q_ref, k_ref, v_ref, g_ref, bias_ref, o_ref, *, n_ctx, rows):
    j = pl.program_id(2)
    lane = lax.broadcasted_iota(jnp.int32, (1, LANES), 1)
    in_head = (lane < HEAD_DIM, lane >= HEAD_DIM)
    q = q_ref[0] * (HEAD_DIM ** -0.5)
    kc = k_ref[0, 0:n_ctx, :].astype(BF16)
    vc = v_ref[0, 0:n_ctx, :].astype(BF16)

    def heads(q):
        return [jnp.where(in_head[e], q, 0.0).astype(BF16) for e in range(2)]

    def finish(o0, o1):
        o = jnp.where(in_head[0], o0, o1)
        o_ref[0] = (o * _silu(g_ref[0])).astype(o_ref.dtype)

    @pl.when(j == 0)
    def _():
        finish(*[_attend(qe, [kc], [vc], [None]) for qe in heads(q)])

    @pl.when(j > 0)
    def _():
        k_row = jnp.clip(NA_Q_ROWS * (j - 1) - NA_Q_ROWS, 0, rows - NA_K_ROWS)
        start = pl.multiple_of(n_ctx + k_row * GRID_W, GRID_W)
        kw = k_ref[0, pl.ds(start, NA_K_TOK), :].astype(BF16)
        vw = v_ref[0, pl.ds(start, NA_K_TOK), :].astype(BF16)
        finish(*[_attend(qe, [kw, kc], [vw, vc], [bias_ref[0, e], None])
                 for e, qe in enumerate(heads(q))])


def _na_attention(p_main, bias_tables, n_ctx):
    B, R, _ = p_main.shape
    rows = (R - n_ctx) // GRID_W
    n_blocks = rows // NA_Q_ROWS
    pairs = W_BRANCH // LANES
    assert n_ctx == ROW_TILE and rows >= NA_K_ROWS and rows % NA_Q_ROWS == 0

    def col(c):
        return lambda b, hp, j: (b, 0, c * pairs + hp)

    def bias_idx(b, hp, j):
        return (jnp.where(j <= 1, 0, jnp.where(j == n_blocks, 2, 1)), hp, 0, 0)

    return pl.pallas_call(
        functools.partial(_na_kernel, n_ctx=n_ctx, rows=rows),
        grid=(B, pairs, n_blocks + 1),
        in_specs=[pl.BlockSpec((1, ROW_TILE, LANES), lambda b, hp, j: (b, j, COL_Q * pairs + hp)),
                  pl.BlockSpec((1, R, LANES), col(COL_K)),
                  pl.BlockSpec((1, R, LANES), col(COL_V)),
                  pl.BlockSpec((1, ROW_TILE, LANES), lambda b, hp, j: (b, j, COL_NA_GATE * pairs + hp)),
                  pl.BlockSpec((1, 2, ROW_TILE, NA_K_TOK), bias_idx)],
        out_specs=pl.BlockSpec((1, ROW_TILE, LANES), lambda b, hp, j: (b, j, hp)),
        out_shape=jax.ShapeDtypeStruct((B, R, W_BRANCH), BF16),
        compiler_params=_params("parallel", "parallel", "arbitrary"),
        name="neighbourhood_attention",
    )(p_main, p_main, p_main, p_main, bias_tables)


def _pool_kernel(u_ref, g_ref, w_ref, sc_ref, o_ref, pad_ref, *, n_ctx, n_lat):
    grp = pl.program_id(1)
    w = w_ref[0]
    scale = sc_ref[...]

    def run(win):
        half = win // 2
        for seq_start, seq_len in ((0, n_ctx), (n_ctx, n_lat)):
            zeros = jnp.zeros((POOL_HALO, POOL_GROUP_DIM), F32)
            pad_ref[0:POOL_HALO, :] = zeros
            pad_ref[POOL_HALO:POOL_HALO + seq_len, :] = u_ref[0, seq_start:seq_start + seq_len, :]
            pad_ref[POOL_HALO + seq_len:2 * POOL_HALO + seq_len, :] = zeros

            def chunk(c, carry):
                base = pl.multiple_of(c * ROW_TILE, ROW_TILE)
                x = pad_ref[pl.ds(base, ROW_TILE + 2 * POOL_HALO), :]
                acc = x[POOL_HALO - half:POOL_HALO - half + ROW_TILE]
                for o in range(-half + 1, half):
                    acc = acc + x[POOL_HALO + o:POOL_HALO + o + ROW_TILE]
                t = base + lax.broadcasted_iota(jnp.int32, (ROW_TILE, 1), 0)
                cnt = jnp.minimum(t + half, seq_len) - jnp.maximum(t - half, 0)
                diff = acc / cnt.astype(F32) - x[POOL_HALO:POOL_HALO + ROW_TILE]
                y = jnp.dot(diff.astype(BF16), w, preferred_element_type=F32) * scale
                rows = pl.ds(seq_start + base, ROW_TILE)
                o_ref[0, rows, :] = (y * _silu(g_ref[0, rows, :])).astype(o_ref.dtype)
                return carry

            lax.fori_loop(0, seq_len // ROW_TILE, chunk, 0)

    for gi, win in enumerate(POOL_WINDOWS):
        pl.when(grp == gi)(functools.partial(run, win))


def _pool(p_main, pool_w, pool_scale, n_ctx):
    B, R, _ = p_main.shape
    groups = len(POOL_WINDOWS)
    return pl.pallas_call(
        functools.partial(_pool_kernel, n_ctx=n_ctx, n_lat=R - n_ctx),
        grid=(B, groups),
        in_specs=[pl.BlockSpec((1, R, POOL_GROUP_DIM), lambda b, g: (b, 0, COL_POOL_U * groups + g)),
                  pl.BlockSpec((1, R, POOL_GROUP_DIM), lambda b, g: (b, 0, COL_POOL_GATE * groups + g)),
                  pl.BlockSpec((1, POOL_GROUP_DIM, POOL_GROUP_DIM), lambda b, g: (g, 0, 0)),
                  pl.BlockSpec((1, POOL_GROUP_DIM), lambda b, g: (0, g))],
        out_specs=pl.BlockSpec((1, R, POOL_GROUP_DIM), lambda b, g: (b, 0, g)),
        out_shape=jax.ShapeDtypeStruct((B, R, W_BRANCH), BF16),
        scratch_shapes=[pltpu.VMEM((R - n_ctx + 2 * POOL_HALO, POOL_GROUP_DIM), F32)],
        compiler_params=_params("parallel", "arbitrary"),
        name="multiscale_pool",
    )(p_main, p_main, pool_w, pool_scale)


def _rwkv_feat_kernel(r_ref, rp_ref, rn_ref, k_ref, kp_ref, kn_ref, v_ref, vp_ref, vn_ref, lora_ref,
                      par_ref, w2_ref, a2_ref, tri_ref,
                      vo_ref, kf_ref, bf_ref, af_ref, rf_ref, kb_ref, bb_ref, ab_ref, rb_ref, ptot_ref, bonus_ref,
                      *, n_ctx_tiles, n_tiles):
    i = pl.program_id(1)
    first = (i == 0) | (i == n_ctx_tiles)
    last = (i == n_ctx_tiles - 1) | (i == n_tiles - 1)
    row = lax.broadcasted_iota(jnp.int32, (ROW_TILE, 1), 0)
    par = par_ref[...]

    def prm(p):
        return par[p:p + 1, :]

    def mix(z_ref, prev_ref, next_ref, mu):
        z = z_ref[0]
        prev = jnp.where(first, 0.0, prev_ref[0, 7:8, :])
        nxt = jnp.where(last, 0.0, next_ref[0, 0:1, :])
        z_prev = jnp.where(row == 0, prev, pltpu.roll(z, 1, 0))
        z_next = jnp.where(row == ROW_TILE - 1, nxt, pltpu.roll(z, ROW_TILE - 1, 0))
        return z + mu * (0.5 * (z_prev + z_next) - z)

    r = mix(r_ref, rp_ref, rn_ref, prm(P_MU_R))
    k = mix(k_ref, kp_ref, kn_ref, prm(P_MU_K))
    v = mix(v_ref, vp_ref, vn_ref, prm(P_MU_V))
    vo_ref[0] = v

    kk = k * prm(P_K_K)
    kk = kk * jnp.minimum(lax.rsqrt(_head_sum(kk * kk)), 1e12)

    lora = lora_ref[0]
    lane = lax.broadcasted_iota(jnp.int32, (1, LANES), 1)
    lora = jnp.where(lane < RWKV_LORA, jnp.tanh(lora), lora).astype(BF16)
    k_sum = None
    outs = ((kf_ref, bf_ref, af_ref, rf_ref), (kb_ref, bb_ref, ab_ref, rb_ref))
    chunk_decay = []
    for d, (k_out, b_out, a_out, r_out) in enumerate(outs):
        x = prm(P_W0_F + d) + jnp.dot(lora, w2_ref[d], preferred_element_type=F32)
        w_log = -(jnp.maximum(-x, 0.0) + jnp.log(1.0 + jnp.exp(-jnp.abs(x)))) - 0.5
        neg_log_w = jnp.exp(w_log)
        hi, mid, lo = _split3(neg_log_w)
        tri = tri_ref[d]
        cs = (jnp.dot(tri, hi, preferred_element_type=F32) + jnp.dot(tri, mid, preferred_element_type=F32)
              + jnp.dot(tri, lo, preferred_element_type=F32))
        grow = jnp.exp(cs)
        shrink = jnp.exp(-cs)
        a = 0.5 + 0.5 * jnp.tanh(0.5 * (prm(P_A0_F + d) + jnp.dot(lora, a2_ref[d], preferred_element_type=F32)))
        k_d = k * (1.0 + (a - 1.0) * prm(P_K_A))
        for q in range(ROW_TILE // SCAN_STEPS):
            last = q * SCAN_STEPS + (SCAN_STEPS - 1 if d == 0 else 0)
            chunk_decay.append(shrink[last:last + 1])
        k_out[0] = k_d * grow
        b_out[0] = kk * a * grow
        a_out[0] = -kk * jnp.exp(neg_log_w - cs)
        r_out[0] = r * shrink
        k_sum = k_d if k_sum is None else k_sum + k_d
    ptot_ref[0, 0] = jnp.concatenate(chunk_decay, axis=0)
    bonus_ref[0] = _head_sum(r * k_sum * prm(P_R_K)) * v


def _rwkv_features(p_main, p_lora, par, w2, a2, n_ctx):
    B, R, _ = p_main.shape
    n_tiles = R // ROW_TILE
    sub = ROW_TILE // 8
    chunks_per_tile = ROW_TILE // SCAN_STEPS
    assert 2 * chunks_per_tile == 8

    def main(c):
        return pl.BlockSpec((1, ROW_TILE, W_BRANCH), lambda b, i: (b, i, c))

    def prev(c):
        return pl.BlockSpec((1, 8, W_BRANCH), lambda b, i: (b, jnp.maximum(i * sub - 1, 0), c))

    def nxt(c):
        return pl.BlockSpec((1, 8, W_BRANCH), lambda b, i: (b, jnp.minimum((i + 1) * sub, n_tiles * sub - 1), c))

    in_specs = []
    for c in (COL_RW_R, COL_RW_K, COL_RW_V):
        in_specs += [main(c), prev(c), nxt(c)]
    in_specs += [pl.BlockSpec((1, ROW_TILE, LANES), lambda b, i: (b, i, 0)),
                 pl.BlockSpec((P_ROWS, W_BRANCH), lambda b, i: (0, 0)),
                 pl.BlockSpec((2, LANES, W_BRANCH), lambda b, i: (0, 0, 0)),
                 pl.BlockSpec((2, LANES, W_BRANCH), lambda b, i: (0, 0, 0)),
                 pl.BlockSpec((2, ROW_TILE, ROW_TILE), lambda b, i: (0, 0, 0))]
    t_idx = np.arange(ROW_TILE)
    same_chunk = t_idx[:, None] // SCAN_STEPS == t_idx[None, :] // SCAN_STEPS
    tri = jnp.asarray(np.stack([same_chunk & (t_idx[None, :] <= t_idx[:, None]),
                                same_chunk & (t_idx[None, :] >= t_idx[:, None])]), BF16)
    out = jax.ShapeDtypeStruct((B, R, W_BRANCH), F32)
    return pl.pallas_call(
        functools.partial(_rwkv_feat_kernel, n_ctx_tiles=n_ctx // ROW_TILE, n_tiles=n_tiles),
        grid=(B, n_tiles),
        in_specs=in_specs,
        out_specs=[pl.BlockSpec((1, ROW_TILE, W_BRANCH), lambda b, i: (b, i, 0))] * 9
        + [pl.BlockSpec((1, 1, 2 * chunks_per_tile, W_BRANCH), lambda b, i: (b, i, 0, 0)),
           pl.BlockSpec((1, ROW_TILE, W_BRANCH), lambda b, i: (b, i, 0))],
        out_shape=[out] * 9 + [jax.ShapeDtypeStruct((B, n_tiles, 2 * chunks_per_tile, W_BRANCH), F32), out],
        compiler_params=_params("parallel", "parallel"),
        name="rwkv_features",
    )(*([p_main] * 9), p_lora, par, w2, a2, tri)


def _scan_kernel(p_ref, k_ref, b_ref, a_ref, r_ref, v_ref, y_ref, s_ref):
    n = HEAD_DIM

    @pl.when(pl.program_id(0) == 0)
    def _():
        s_ref[...] = jnp.zeros_like(s_ref)

    def row(ref, j, t):
        return ref[0, pl.ds(j * SCAN_STEPS + t, 1), :]

    zero = jnp.zeros((n, s_ref.shape[2]), F32)

    def first_sa(jb, sa):
        for jj in range(SCAN_J_UNROLL):
            j = jb * SCAN_J_UNROLL + jj
            sa = sa + s_ref[j] * row(a_ref, j, 0)
        return sa

    def step(t, sa):
        tile_rows = pl.ds(pl.multiple_of(t * n, n), n)
        vt = v_ref[0, tile_rows, :]
        t_next = jnp.minimum(t + 1, SCAN_STEPS - 1)

        def columns(jb, carry):
            y, sa_next = carry
            for jj in range(SCAN_J_UNROLL):
                j = jb * SCAN_J_UNROLL + jj
                sj = s_ref[j] + sa * row(b_ref, j, t) + vt * row(k_ref, j, t)
                s_ref[j] = sj
                y = y + sj * row(r_ref, j, t)
                sa_next = sa_next + sj * row(a_ref, j, t_next)
            return y, sa_next

        y, sa_next = lax.fori_loop(0, n // SCAN_J_UNROLL, columns, (zero, zero))
        y_ref[0, tile_rows, :] = y
        return sa_next

    def rescale(jb, carry):
        for jj in range(SCAN_J_UNROLL):
            j = jb * SCAN_J_UNROLL + jj
            s_ref[j] = s_ref[j] * p_ref[0, pl.ds(j, 1), :]
        return carry

    sa0 = lax.fori_loop(0, n // SCAN_J_UNROLL, first_sa, zero)
    lax.fori_loop(0, SCAN_STEPS, step, sa0)
    lax.fori_loop(0, n // SCAN_J_UNROLL, rescale, 0)


def _chunk_decay_to_scan(ptot, n_ctx):
    B, n_tiles = ptot.shape[:2]
    pt = ptot.reshape(B, n_tiles, 2, ROW_TILE // SCAN_STEPS, HEAD_DIM, N_HEADS)
    n_ctx_chunks = n_ctx // SCAN_STEPS

    def chains(z):
        return z.reshape(B, -1, HEAD_DIM, N_HEADS).transpose(1, 2, 0, 3).reshape(-1, HEAD_DIM, B * N_HEADS)

    fwd, bwd = chains(pt[:, :, 0]), chains(pt[:, :, 1])
    bwd = jnp.concatenate([bwd[:n_ctx_chunks][::-1], bwd[n_ctx_chunks:][::-1]], axis=0)
    return jnp.concatenate([fwd, bwd], axis=-1)


def _wkv_scan(p, k, b, a, r, v):
    n_chunks, rows, chains = k.shape
    spec = pl.BlockSpec((1, rows, chains), lambda s: (s, 0, 0))
    return pl.pallas_call(
        _scan_kernel,
        grid=(n_chunks,),
        in_specs=[pl.BlockSpec((1, HEAD_DIM, chains), lambda s: (s, 0, 0))] + [spec] * 5,
        out_specs=spec,
        out_shape=jax.ShapeDtypeStruct(k.shape, F32),
        scratch_shapes=[pltpu.VMEM((HEAD_DIM, HEAD_DIM, chains), F32)],
        compiler_params=_params("arbitrary"),
        name="wkv_scan",
    )(p, k, b, a, r, v)


def _flip_rows(x, flip):
    hi, mid, lo = _split3(x)
    return (jnp.dot(flip, hi, preferred_element_type=F32) + jnp.dot(flip, mid, preferred_element_type=F32)
            + jnp.dot(flip, lo, preferred_element_type=F32))


def _mirror_chunk(c, n_ctx_chunks, n_chunks):
    return jnp.where(c < n_ctx_chunks, n_ctx_chunks - 1 - c, n_ctx_chunks + n_chunks - 1 - c)


def _to_scan_kernel(zf_ref, zb_ref, flip_ref, o_ref, t_ref, *, step_major):
    nb = zf_ref.shape[0]
    for b in range(nb):
        t_ref[b] = zf_ref[b].T
        t_ref[nb + b] = _flip_rows(zb_ref[b], flip_ref[...]).T
    sub = RELAYOUT_ROWS // SCAN_STEPS

    def body(n, carry):
        rows = pl.ds(pl.multiple_of(n * N_HEADS, N_HEADS), N_HEADS)
        slabs = [t_ref[g, rows, :] for g in range(2 * nb)]
        tile = jnp.concatenate(slabs, axis=0).T
        for q in range(sub):
            if step_major:
                dst = pl.ds(n, SCAN_STEPS, stride=HEAD_DIM)
            else:
                dst = pl.ds(pl.multiple_of(n * SCAN_STEPS, SCAN_STEPS), SCAN_STEPS)
            o_ref[q, dst, :] = tile[q * SCAN_STEPS:(q + 1) * SCAN_STEPS]
        return carry

    lax.fori_loop(0, HEAD_DIM, body, 0, unroll=RELAYOUT_UNROLL)


def _to_scan(z_fwd, z_bwd, flip, n_ctx, step_major=False):
    B, R, _ = z_fwd.shape
    n_chunks = R // RELAYOUT_ROWS
    n_ctx_chunks = n_ctx // RELAYOUT_ROWS
    sub = RELAYOUT_ROWS // SCAN_STEPS
    chains = 2 * B * N_HEADS
    return pl.pallas_call(
        functools.partial(_to_scan_kernel, step_major=step_major),
        grid=(n_chunks,),
        in_specs=[pl.BlockSpec((B, RELAYOUT_ROWS, W_BRANCH), lambda c: (0, c, 0)),
                  pl.BlockSpec((B, RELAYOUT_ROWS, W_BRANCH),
                               lambda c: (0, _mirror_chunk(c, n_ctx_chunks, n_chunks), 0)),
                  pl.BlockSpec((RELAYOUT_ROWS, RELAYOUT_ROWS), lambda c: (0, 0))],
        out_specs=pl.BlockSpec((sub, HEAD_DIM * SCAN_STEPS, chains), lambda c: (c, 0, 0)),
        out_shape=jax.ShapeDtypeStruct((R // SCAN_STEPS, HEAD_DIM * SCAN_STEPS, chains), F32),
        scratch_shapes=[pltpu.VMEM((2 * B, W_BRANCH, RELAYOUT_ROWS), F32)],
        compiler_params=_params("parallel"),
        name="to_scan_layout",
    )(z_fwd, z_bwd, flip)


def _from_scan_kernel(y_ref, flip_ref, yf_ref, yb_ref, t_ref):
    nb = yf_ref.shape[0]
    sub = RELAYOUT_ROWS // SCAN_STEPS

    def body(n, carry):
        tile = jnp.concatenate([y_ref[q, pl.ds(n, SCAN_STEPS, stride=HEAD_DIM), :] for q in range(sub)], axis=0)
        tile = tile.T
        rows = pl.ds(pl.multiple_of(n * N_HEADS, N_HEADS), N_HEADS)
        for g in range(2 * nb):
            t_ref[g, rows, :] = tile[g * N_HEADS:(g + 1) * N_HEADS]
        return carry

    lax.fori_loop(0, HEAD_DIM, body, 0, unroll=RELAYOUT_UNROLL)
    for b in range(nb):
        yf_ref[b] = t_ref[b].T
        yb_ref[b] = _flip_rows(t_ref[nb + b].T, flip_ref[...])


def _from_scan(y, flip, n_batch, n_ctx):
    R = y.shape[0] * SCAN_STEPS
    n_chunks = R // RELAYOUT_ROWS
    n_ctx_chunks = n_ctx // RELAYOUT_ROWS
    sub = RELAYOUT_ROWS // SCAN_STEPS
    out = jax.ShapeDtypeStruct((n_batch, R, W_BRANCH), F32)
    return pl.pallas_call(
        _from_scan_kernel,
        grid=(n_chunks,),
        in_specs=[pl.BlockSpec((sub, HEAD_DIM * SCAN_STEPS, y.shape[2]), lambda c: (c, 0, 0)),
                  pl.BlockSpec((RELAYOUT_ROWS, RELAYOUT_ROWS), lambda c: (0, 0))],
        out_specs=[pl.BlockSpec((n_batch, RELAYOUT_ROWS, W_BRANCH), lambda c: (0, c, 0)),
                   pl.BlockSpec((n_batch, RELAYOUT_ROWS, W_BRANCH),
                                lambda c: (0, _mirror_chunk(c, n_ctx_chunks, n_chunks), 0))],
        out_shape=[out, out],
        scratch_shapes=[pltpu.VMEM((2 * n_batch, W_BRANCH, RELAYOUT_ROWS), F32)],
        compiler_params=_params("parallel"),
        name="from_scan_layout",
    )(y, flip)


def _rwkv_readout_kernel(yf_ref, yb_ref, bonus_ref, gate_ref, gb_ref, o_ref):
    y = yf_ref[0] + yb_ref[0]
    mu = _head_sum(y) * (1.0 / HEAD_DIM)
    yc = y - mu
    var = _head_sum(yc * yc) * (1.0 / HEAD_DIM)
    gb = gb_ref[...]
    out = yc * lax.rsqrt(var + LNX_EPS) * gb[0:1, :] + gb[1:2, :] + bonus_ref[0]
    o_ref[0] = (out * _silu(gate_ref[0])).astype(o_ref.dtype)


def _rwkv_readout(y_fwd, y_bwd, bonus, p_main, lnx_gb):
    B, R, _ = y_fwd.shape
    tile = pl.BlockSpec((1, ROW_TILE, W_BRANCH), lambda b, i: (b, i, 0))
    return pl.pallas_call(
        _rwkv_readout_kernel,
        grid=(B, R // ROW_TILE),
        in_specs=[tile, tile, tile,
                  pl.BlockSpec((1, ROW_TILE, W_BRANCH), lambda b, i: (b, i, COL_RW_GATE)),
                  pl.BlockSpec((8, W_BRANCH), lambda b, i: (0, 0))],
        out_specs=tile,
        out_shape=jax.ShapeDtypeStruct((B, R, W_BRANCH), BF16),
        compiler_params=_params("parallel", "parallel"),
        name="rwkv_readout",
    )(y_fwd, y_bwd, bonus, p_main, lnx_gb)


def _merge_kernel(na_ref, pool_ref, rw_ref, lna_ref, lpool_ref, lrw_ref, w_ref, o_ref):
    acc = None
    for br, (x_ref, l_ref) in enumerate(((na_ref, lna_ref), (pool_ref, lpool_ref), (rw_ref, lrw_ref))):
        t = _sigmoid(l_ref[...]) * jnp.dot(x_ref[...], w_ref[br], preferred_element_type=F32)
        acc = t if acc is None else acc + t
    o_ref[...] = acc.astype(o_ref.dtype)


def _merge(b_na, b_pool, b_rw, p_merge, w_branch):
    M = b_na.shape[0]
    tm, tn = min(_row_tile(M), 512), 1024
    nb = D_MODEL // tn
    x_spec = pl.BlockSpec((tm, W_BRANCH), lambda j, i: (i, 0))

    def logit(br):
        return pl.BlockSpec((tm, tn), lambda j, i: (i, br * nb + j))

    return pl.pallas_call(
        _merge_kernel,
        grid=(nb, M // tm),
        in_specs=[x_spec, x_spec, x_spec, logit(0), logit(1), logit(2),
                  pl.BlockSpec((N_BRANCH, W_BRANCH, tn), lambda j, i: (0, 0, j))],
        out_specs=pl.BlockSpec((tm, tn), lambda j, i: (i, j)),
        out_shape=jax.ShapeDtypeStruct((M, D_MODEL), BF16),
        compiler_params=_params("parallel", "parallel"),
        name="branch_merge",
    )(b_na, b_pool, b_rw, p_merge, p_merge, p_merge, w_branch)


def _out_kernel(m_ref, w_ref, c_ref, l_ref, mod_ref, fg_ref, o_ref, *, n_ctx_tiles, tile_offset, final):
    tile = pl.program_id(1) + tile_offset
    gate = _mod_row(mod_ref, tile, n_ctx_tiles)[:, 2 * D_MODEL:]
    x = _stream_tile(c_ref, l_ref, tile, n_ctx_tiles)
    x = x + gate * jnp.dot(m_ref[0], w_ref[...], preferred_element_type=F32)
    o_ref[0] = _rms(x, fg_ref[...]) if final else x


def _out_proj(merged, w_out, stream, mod, final_g, n_ctx, final):
    B, R, _ = merged.shape
    n_ctx_tiles = n_ctx // ROW_TILE
    off = n_ctx_tiles if final else 0
    return pl.pallas_call(
        functools.partial(_out_kernel, n_ctx_tiles=n_ctx_tiles, tile_offset=off, final=final),
        grid=(B, R // ROW_TILE - off),
        in_specs=[pl.BlockSpec((1, ROW_TILE, D_MODEL), lambda b, i: (b, i + off, 0)),
                  pl.BlockSpec((D_MODEL, D_MODEL), lambda b, i: (0, 0))]
        + _stream_specs(stream, n_ctx_tiles, off) + [
            pl.BlockSpec((MOD_ROWS, 3 * D_MODEL), lambda b, i: (0, 0)),
            pl.BlockSpec((1, D_MODEL), lambda b, i: (0, 0))],
        out_specs=pl.BlockSpec((1, ROW_TILE, D_MODEL), lambda b, i: (b, i, 0)),
        out_shape=jax.ShapeDtypeStruct((B, R - off * ROW_TILE, D_MODEL), F32),
        compiler_params=_params("parallel", "parallel"),
        name="out_proj_final" if final else "out_proj",
    )(merged, w_out, stream[0], stream[1], mod, final_g)


def _layer(stream, R, mod, n_ctx, final, final_g, norm_g, w_in, na_rpb, pool_w, pool_scale, rw_mu, rw_w0, rw_w2,
           rw_a0, rw_a2, rw_k_k, rw_k_a, rw_r_k, rw_lnx_g, rw_lnx_b, w_branch, w_out):
    B = stream[0].shape[0]
    rows = (R - n_ctx) // GRID_W

    h = _norm_mod(stream, R, norm_g[None], mod, n_ctx).reshape(B * R, D_MODEL)
    lo = N_MAIN + 2 * RWKV_LORA
    n_rw = COL_RW_R * W_BRANCH
    w_rw = _head_major(w_in[:, n_rw:N_MAIN].reshape(D_MODEL, N_MAIN // W_BRANCH - COL_RW_R, W_BRANCH))
    w_main = jnp.concatenate([w_in[:, :n_rw], w_rw.reshape(D_MODEL, N_MAIN - n_rw)], axis=1)
    p_main = _matmul(h, w_main.astype(BF16), F32, "in_proj_main").reshape(B, R, N_MAIN)
    p_lora = _matmul(h, w_in[:, N_MAIN:lo].astype(BF16), F32, "in_proj_lora").reshape(B, R, 2 * RWKV_LORA)
    p_merge = _matmul(h, w_in[:, lo:].astype(BF16), F32, "in_proj_merge")

    b_na = _na_attention(p_main, _na_bias_tables(na_rpb, rows), n_ctx)
    b_pool = _pool(p_main, pool_w.astype(BF16), pool_scale[None], n_ctx)

    par = jnp.zeros((P_ROWS, W_BRANCH), F32)
    par = par.at[P_MU_R:P_MU_V + 1].set(rw_mu).at[P_W0_F:P_W0_B + 1].set(rw_w0).at[P_A0_F:P_A0_B + 1].set(rw_a0)
    par = _head_major(par.at[P_K_K].set(rw_k_k).at[P_K_A].set(rw_k_a).at[P_R_K].set(rw_r_k.reshape(-1)))
    zeros = jnp.zeros_like(rw_w2)
    w2 = _head_major(jnp.concatenate([rw_w2, zeros], axis=1)).astype(BF16)
    a2 = _head_major(jnp.concatenate([zeros, rw_a2], axis=1)).astype(BF16)
    v, k_f, b_f, a_f, r_f, k_b, b_b, a_b, r_b, ptot, bonus = _rwkv_features(p_main, p_lora, par, w2, a2, n_ctx)
    flip = jnp.asarray(np.eye(RELAYOUT_ROWS)[::-1], BF16)
    y = _wkv_scan(_chunk_decay_to_scan(ptot, n_ctx),
                  *[_to_scan(zf, zb, flip, n_ctx) for zf, zb in ((k_f, k_b), (b_f, b_b), (a_f, a_b), (r_f, r_b))],
                  _to_scan(v, v, flip, n_ctx, step_major=True))
    y_fwd, y_bwd = _from_scan(y, flip, B, n_ctx)
    lnx_gb = _head_major(jnp.zeros((8, W_BRANCH), F32).at[0].set(rw_lnx_g).at[1].set(rw_lnx_b))
    b_rw = _rwkv_readout(y_fwd, y_bwd, bonus, p_main, lnx_gb)

    def flat(z):
        return z.reshape(B * R, W_BRANCH)

    w_rw_out = _head_major(w_branch[2].T).T
    w_br = jnp.stack([w_branch[0], w_branch[1], w_rw_out]).astype(BF16)
    merged = _merge(flat(b_na), flat(b_pool), flat(b_rw), p_merge, w_br)
    return _out_proj(merged.reshape(B, R, D_MODEL), w_out.astype(BF16), stream, mod, final_g[None], n_ctx, final)


def kernel(x, c, ctx, c_ctx, norm_g, w_mod, b_mod, w_in, na_rpb, pool_w, pool_scale, rw_mu, rw_w0, rw_w2, rw_a0,
           rw_a2, rw_k_k, rw_k_a, rw_r_k, rw_lnx_g, rw_lnx_b, w_branch, w_out, final_g):
    B, T, _ = x.shape
    n_ctx = ctx.shape[1]
    depth = w_in.shape[0]
    assert B <= CTX_MOD_ROW and n_ctx % ROW_TILE == 0 and T % ROW_TILE == 0
    assert ROW_TILE % RELAYOUT_ROWS == 0 and RELAYOUT_ROWS % SCAN_STEPS == 0

    cond = jnp.zeros((MOD_ROWS, D_MODEL), F32).at[:B].set(c).at[CTX_MOD_ROW].set(c_ctx)
    mods = _modulation(cond, w_mod.astype(BF16), b_mod[:, None, :])
    stream = (ctx, x, n_ctx // ROW_TILE)
    for layer in range(depth):
        out = _layer(stream, n_ctx + T, mods[layer], n_ctx, layer == depth - 1, final_g, norm_g[layer], w_in[layer],
                     na_rpb[layer], pool_w[layer], pool_scale[layer], rw_mu[layer], rw_w0[layer], rw_w2[layer],
                     rw_a0[layer], rw_a2[layer], rw_k_k[layer], rw_k_a[layer], rw_r_k[layer], rw_lnx_g[layer],
                     rw_lnx_b[layer], w_branch[layer], w_out[layer])
        stream = (out, out, 0)
    return out
```

```python
import functools

import numpy as np
import jax
import jax.numpy as jnp
from jax import lax
from jax.experimental import pallas as pl
from jax.experimental.pallas import tpu as pltpu

F32 = jnp.float32
BF16 = jnp.bfloat16

D_MODEL = 2048
W_BRANCH = D_MODEL // 2
N_BRANCH = 3
N_HEADS = 16
HEAD_DIM = 64
GRID_W = 64
NA_WIN_H = 8
NA_WIN_W = 16
POOL_WINDOWS = (2, 4, 8, 16)
POOL_GROUP_DIM = W_BRANCH // len(POOL_WINDOWS)
POOL_HALO = max(POOL_WINDOWS) // 2
RWKV_LORA = 64
RMS_EPS = 1e-6
LNX_EPS = 64e-5
NEG_INF = -1e30

LANES = 128
ROW_TILE = 256
NA_Q_ROWS = ROW_TILE // GRID_W
NA_K_ROWS = NA_Q_ROWS + NA_WIN_H
NA_K_TOK = NA_K_ROWS * GRID_W
CTX_MOD_ROW = 4
MOD_ROWS = 8
SCAN_STEPS = 64
SCAN_J_UNROLL = 32
RELAYOUT_UNROLL = 8
RELAYOUT_ROWS = 128
VMEM_LIMIT = 56 << 20

COL_Q, COL_K, COL_V, COL_NA_GATE, COL_POOL_U, COL_POOL_GATE = range(6)
COL_RW_R, COL_RW_K, COL_RW_V, COL_RW_GATE = range(4)
N_ATTN_POOL = 6 * W_BRANCH
N_MAIN = 10 * W_BRANCH

P_MU_R, P_MU_K, P_MU_V, P_W0_F, P_W0_B, P_A0_F, P_A0_B, P_K_K, P_K_A, P_R_K = range(10)
P_ROWS = 16


def _params(*sem):
    return pltpu.CompilerParams(dimension_semantics=sem, vmem_limit_bytes=VMEM_LIMIT)


def _sigmoid(x):
    return 1.0 / (1.0 + jnp.exp(-x))


def _silu(x):
    return x * _sigmoid(x)


def _split3(x):
    hi = x.astype(BF16)
    r1 = x - hi.astype(F32)
    mid = r1.astype(BF16)
    lo = (r1 - mid.astype(F32)).astype(BF16)
    return hi, mid, lo


def _head_major(z):
    lead = z.shape[:-1]
    return z.reshape(lead + (N_HEADS, HEAD_DIM)).swapaxes(-1, -2).reshape(lead + (W_BRANCH,))


def _head_sum(x):
    n_tiles = W_BRANCH // LANES
    part = x[:, :LANES]
    for c in range(1, n_tiles):
        part = part + x[:, c * LANES:(c + 1) * LANES]
    shift = N_HEADS
    while shift < LANES:
        part = part + pltpu.roll(part, shift, 1)
        shift *= 2
    return jnp.concatenate([part] * n_tiles, axis=1)


def _mod_kernel(cond_ref, w_ref, b_ref, o_ref):
    s = _silu(cond_ref[...])
    o_ref[0] = jnp.dot(s.astype(BF16), w_ref[0].astype(BF16), preferred_element_type=F32) + b_ref[0]


def _modulation(cond, w_mod, b_mod):
    n_layers = w_mod.shape[0]
    tn = 3 * D_MODEL // 4
    return pl.pallas_call(
        _mod_kernel,
        grid=(n_layers, 4),
        in_specs=[pl.BlockSpec((MOD_ROWS, D_MODEL), lambda l, j: (0, 0)),
                  pl.BlockSpec((1, D_MODEL, tn), lambda l, j: (l, 0, j)),
                  pl.BlockSpec((1, 1, tn), lambda l, j: (l, 0, j))],
        out_specs=pl.BlockSpec((1, MOD_ROWS, tn), lambda l, j: (l, 0, j)),
        out_shape=jax.ShapeDtypeStruct((n_layers, MOD_ROWS, 3 * D_MODEL), F32),
        compiler_params=_params("arbitrary", "arbitrary"),
        name="adaln_modulation",
    )(cond, w_mod, b_mod)


def _mod_row(mod_ref, tile, n_ctx_tiles):
    row = jnp.where(tile < n_ctx_tiles, CTX_MOD_ROW, pl.program_id(0))
    return mod_ref[pl.ds(row, 1), :]


def _rms(x, g):
    return x * lax.rsqrt(jnp.mean(x * x, axis=-1, keepdims=True) + RMS_EPS) * g


def _stream_specs(stream, n_ctx_tiles, tile_offset=0):
    lat_shift = stream[2]
    return [pl.BlockSpec((1, ROW_TILE, D_MODEL), lambda b, i: (b, jnp.minimum(i + tile_offset, n_ctx_tiles - 1), 0)),
            pl.BlockSpec((1, ROW_TILE, D_MODEL), lambda b, i: (b, jnp.maximum(i + tile_offset - lat_shift, 0), 0))]


def _stream_tile(c_ref, l_ref, tile, n_ctx_tiles):
    return jnp.where(tile < n_ctx_tiles, c_ref[0], l_ref[0])


def _norm_mod_kernel(c_ref, l_ref, g_ref, mod_ref, h_ref, *, n_ctx_tiles):
    tile = pl.program_id(1)
    m = _mod_row(mod_ref, tile, n_ctx_tiles)
    shift = m[:, :D_MODEL]
    scale = m[:, D_MODEL:2 * D_MODEL]
    x = _stream_tile(c_ref, l_ref, tile, n_ctx_tiles)
    h_ref[0] = (_rms(x, g_ref[...]) * (1.0 + scale) + shift).astype(BF16)


def _norm_mod(stream, n_rows, norm_g, mod, n_ctx):
    B = stream[0].shape[0]
    n_ctx_tiles = n_ctx // ROW_TILE
    return pl.pallas_call(
        functools.partial(_norm_mod_kernel, n_ctx_tiles=n_ctx_tiles),
        grid=(B, n_rows // ROW_TILE),
        in_specs=_stream_specs(stream, n_ctx_tiles) + [
            pl.BlockSpec((1, D_MODEL), lambda b, i: (0, 0)),
            pl.BlockSpec((MOD_ROWS, 3 * D_MODEL), lambda b, i: (0, 0))],
        out_specs=pl.BlockSpec((1, ROW_TILE, D_MODEL), lambda b, i: (b, i, 0)),
        out_shape=jax.ShapeDtypeStruct((B, n_rows, D_MODEL), BF16),
        compiler_params=_params("parallel", "parallel"),
        name="norm_modulate",
    )(stream[0], stream[1], norm_g, mod)


def _mm_kernel(a_ref, w_ref, o_ref):
    o_ref[...] = jnp.dot(a_ref[...], w_ref[...], preferred_element_type=F32).astype(o_ref.dtype)


def _row_tile(m):
    for t in (1024, 512, 256):
        if m % t == 0:
            return t
    raise ValueError(f"row count {m} is not a multiple of {ROW_TILE}")


def _mm_cast_kernel(a_ref, w_ref, o_ref, wb_ref):
    @pl.when(pl.program_id(1) == 0)
    def _():
        wb_ref[...] = w_ref[...].astype(BF16)

    o_ref[...] = jnp.dot(a_ref[...], wb_ref[...], preferred_element_type=F32).astype(o_ref.dtype)


def _matmul(a, w, out_dtype, name, n_cols=None, layer=None):
    M, K = a.shape
    N = w.shape[-1] if n_cols is None else n_cols
    tm = _row_tile(M)
    tn = min(N, 1024)
    cast = w.dtype != BF16
    if layer is None:
        w_spec = pl.BlockSpec((K, tn), lambda j, i: (0, j))
    else:
        w_spec = pl.BlockSpec((None, K, tn), lambda j, i: (layer, 0, j))
    return pl.pallas_call(
        _mm_cast_kernel if cast else _mm_kernel,
        grid=(N // tn, M // tm),
        in_specs=[pl.BlockSpec((tm, K), lambda j, i: (i, 0)), w_spec],
        out_specs=pl.BlockSpec((tm, tn), lambda j, i: (i, j)),
        out_shape=jax.ShapeDtypeStruct((M, N), out_dtype),
        scratch_shapes=[pltpu.VMEM((K, tn), BF16)] if cast else [],
        compiler_params=_params("parallel", "arbitrary" if cast else "parallel"),
        name=name,
    )(a, w)


def _na_bias_tables(rpb, rows):
    n_blocks = rows // NA_Q_ROWS
    n_off = 2 * NA_WIN_H - 1
    col = np.arange(GRID_W)
    c0 = np.clip(col - NA_WIN_W // 2, 0, GRID_W - NA_WIN_W)
    valid_c = (col[None, :] >= c0[:, None]) & (col[None, :] < c0[:, None] + NA_WIN_W)
    col_off = np.clip(col[None, :] - col[:, None] + NA_WIN_W - 1, 0, 2 * NA_WIN_W - 2)
    pick_c = jnp.asarray(np.eye(2 * NA_WIN_W - 1)[col_off], F32)
    tile = jnp.einsum("hrc,qpc->hrqp", rpb.astype(F32), pick_c, precision=lax.Precision.HIGHEST)
    tile = jnp.where(valid_c, tile, NEG_INF)
    tile = jnp.concatenate([tile, jnp.full((N_HEADS, 1, GRID_W, GRID_W), NEG_INF, F32)], axis=1)
    tile = jnp.concatenate([tile, tile], axis=-1)
    picks = []
    for m in (0, 1, n_blocks - 1):
        q_row = NA_Q_ROWS * m + np.arange(NA_Q_ROWS)
        k_row = int(np.clip(NA_Q_ROWS * m - NA_Q_ROWS, 0, rows - NA_K_ROWS)) + np.arange(NA_K_ROWS)
        r0 = np.clip(q_row - NA_WIN_H // 2, 0, rows - NA_WIN_H)
        valid_r = (k_row[None, :] >= r0[:, None]) & (k_row[None, :] < r0[:, None] + NA_WIN_H)
        row_off = np.clip(k_row[None, :] - q_row[:, None] + NA_WIN_H - 1, 0, n_off - 1)
        picks.append(np.where(valid_r, row_off, n_off))
    picks = np.stack(picks)

    def build(t_ref, o_ref):
        left = lax.broadcasted_iota(jnp.int32, (1, LANES), 1) < GRID_W
        for ty in range(picks.shape[0]):
            for a in range(NA_Q_ROWS):
                for kp in range(NA_K_ROWS // 2):
                    pair = jnp.where(left, t_ref[0, int(picks[ty, a, 2 * kp])], t_ref[0, int(picks[ty, a, 2 * kp + 1])])
                    o_ref[ty, 0, a * GRID_W:(a + 1) * GRID_W, kp * LANES:(kp + 1) * LANES] = pair

    return pl.pallas_call(
        build,
        grid=(N_HEADS,),
        in_specs=[pl.BlockSpec((1, n_off + 1, GRID_W, LANES), lambda h: (h, 0, 0, 0))],
        out_specs=pl.BlockSpec((picks.shape[0], 1, ROW_TILE, NA_K_TOK), lambda h: (0, h, 0, 0)),
        out_shape=jax.ShapeDtypeStruct((picks.shape[0], N_HEADS, ROW_TILE, NA_K_TOK), F32),
        compiler_params=_params("parallel"),
        name="na_bias_tables",
    )(tile)


def _attend(qe, keys, vals, biases):
    dn = (((1,), (1,)), ((), ()))
    scores = []
    for kk, bias in zip(keys, biases):
        s = lax.dot_general(qe, kk, dn, preferred_element_type=F32)
        scores.append(s if bias is None else s + bias)
    m = scores[0].max(axis=-1, keepdims=True)
    for s in scores[1:]:
        m = jnp.maximum(m, s.max(axis=-1, keepdims=True))
    num, den = None, None
    for s, vv in zip(scores, vals):
        p = jnp.exp(s - m)
        l = p.sum(axis=-1, keepdims=True)
        o = jnp.dot(p.astype(BF16), vv, preferred_element_type=F32)
        num = o if num is None else num + o
        den = l if den is None else den + l
    return num / den


def _na_kernel(q_ref, k_ref, v_ref, g_ref, bias_ref, o_ref, *, n_ctx, rows):
    j = pl.program_id(2)
    lane = lax.broadcasted_iota(jnp.int32, (1, LANES), 1)
    in_head = (lane < HEAD_DIM, lane >= HEAD_DIM)
    q = q_ref[0] * (HEAD_DIM ** -0.5)
    kc = k_ref[0, 0:n_ctx, :].astype(BF16)
    vc = v_ref[0, 0:n_ctx, :].astype(BF16)

    def heads(q):
        return [jnp.where(in_head[e], q, 0.0).astype(BF16) for e in range(2)]

    def finish(o0, o1):
        o = jnp.where(in_head[0], o0, o1)
        o_ref[0] = (o * _silu(g_ref[0])).astype(o_ref.dtype)

    @pl.when(j == 0)
    def _():
        finish(*[_attend(qe, [kc], [vc], [None]) for qe in heads(q)])

    @pl.when(j > 0)
    def _():
        k_row = jnp.clip(NA_Q_ROWS * (j - 1) - NA_Q_ROWS, 0, rows - NA_K_ROWS)
        start = pl.multiple_of(n_ctx + k_row * GRID_W, GRID_W)
        kw = k_ref[0, pl.ds(start, NA_K_TOK), :].astype(BF16)
        vw = v_ref[0, pl.ds(start, NA_K_TOK), :].astype(BF16)
        finish(*[_attend(qe, [kw, kc], [vw, vc], [bias_ref[0, e], None])
                 for e, qe in enumerate(heads(q))])


def _na_attention(p_ap, bias_tables, n_ctx):
    B, R, _ = p_ap.shape
    rows = (R - n_ctx) // GRID_W
    n_blocks = rows // NA_Q_ROWS
    pairs = W_BRANCH // LANES
    assert n_ctx == ROW_TILE and rows >= NA_K_ROWS and rows % NA_Q_ROWS == 0

    def col(c):
        return lambda b, hp, j: (b, 0, c * pairs + hp)

    def bias_idx(b, hp, j):
        return (jnp.where(j <= 1, 0, jnp.where(j == n_blocks, 2, 1)), hp, 0, 0)

    return pl.pallas_call(
        functools.partial(_na_kernel, n_ctx=n_ctx, rows=rows),
        grid=(B, pairs, n_blocks + 1),
        in_specs=[pl.BlockSpec((1, ROW_TILE, LANES), lambda b, hp, j: (b, j, COL_Q * pairs + hp)),
                  pl.BlockSpec((1, R, LANES), col(COL_K)),
                  pl.BlockSpec((1, R, LANES), col(COL_V)),
                  pl.BlockSpec((1, ROW_TILE, LANES), lambda b, hp, j: (b, j, COL_NA_GATE * pairs + hp)),
                  pl.BlockSpec((1, 2, ROW_TILE, NA_K_TOK), bias_idx)],
        out_specs=pl.BlockSpec((1, ROW_TILE, LANES), lambda b, hp, j: (b, j, hp)),
        out_shape=jax.ShapeDtypeStruct((B, R, W_BRANCH), BF16),
        compiler_params=_params("parallel", "parallel", "arbitrary"),
        name="neighbourhood_attention",
    )(p_ap, p_ap, p_ap, p_ap, bias_tables)


def _pool_kernel(u_ref, g_ref, w_ref, sc_ref, o_ref, pad_ref, *, n_ctx, n_lat):
    grp = pl.program_id(1)
    w = w_ref[0]
    scale = sc_ref[...]

    def run(win):
        half = win // 2
        for seq_start, seq_len in ((0, n_ctx), (n_ctx, n_lat)):
            zeros = jnp.zeros((POOL_HALO, POOL_GROUP_DIM), F32)
            pad_ref[0:POOL_HALO, :] = zeros
            pad_ref[POOL_HALO:POOL_HALO + seq_len, :] = u_ref[0, seq_start:seq_start + seq_len, :]
            pad_ref[POOL_HALO + seq_len:2 * POOL_HALO + seq_len, :] = zeros

            def chunk(c, carry):
                base = pl.multiple_of(c * ROW_TILE, ROW_TILE)
                x = pad_ref[pl.ds(base, ROW_TILE + 2 * POOL_HALO), :]
                acc = x[POOL_HALO - half:POOL_HALO - half + ROW_TILE]
                for o in range(-half + 1, half):
                    acc = acc + x[POOL_HALO + o:POOL_HALO + o + ROW_TILE]
                t = base + lax.broadcasted_iota(jnp.int32, (ROW_TILE, 1), 0)
                cnt = jnp.minimum(t + half, seq_len) - jnp.maximum(t - half, 0)
                diff = acc / cnt.astype(F32) - x[POOL_HALO:POOL_HALO + ROW_TILE]
                y = jnp.dot(diff.astype(BF16), w, preferred_element_type=F32) * scale
                rows = pl.ds(seq_start + base, ROW_TILE)
                o_ref[0, rows, :] = (y * _silu(g_ref[0, rows, :])).astype(o_ref.dtype)
                return carry

            lax.fori_loop(0, seq_len // ROW_TILE, chunk, 0)

    for gi, win in enumerate(POOL_WINDOWS):
        pl.when(grp == gi)(functools.partial(run, win))


def _pool(p_ap, pool_w, pool_scale, n_ctx):
    B, R, _ = p_ap.shape
    groups = len(POOL_WINDOWS)
    return pl.pallas_call(
        functools.partial(_pool_kernel, n_ctx=n_ctx, n_lat=R - n_ctx),
        grid=(B, groups),
        in_specs=[pl.BlockSpec((1, R, POOL_GROUP_DIM), lambda b, g: (b, 0, COL_POOL_U * groups + g)),
                  pl.BlockSpec((1, R, POOL_GROUP_DIM), lambda b, g: (b, 0, COL_POOL_GATE * groups + g)),
                  pl.BlockSpec((1, POOL_GROUP_DIM, POOL_GROUP_DIM), lambda b, g: (g, 0, 0)),
                  pl.BlockSpec((1, POOL_GROUP_DIM), lambda b, g: (0, g))],
        out_specs=pl.BlockSpec((1, R, POOL_GROUP_DIM), lambda b, g: (b, 0, g)),
        out_shape=jax.ShapeDtypeStruct((B, R, W_BRANCH), BF16),
        scratch_shapes=[pltpu.VMEM((R - n_ctx + 2 * POOL_HALO, POOL_GROUP_DIM), F32)],
        compiler_params=_params("parallel", "arbitrary"),
        name="multiscale_pool",
    )(p_ap, p_ap, pool_w, pool_scale)


def _rwkv_feat_kernel(r_ref, rp_ref, rn_ref, k_ref, kp_ref, kn_ref, v_ref, vp_ref, vn_ref, lora_ref,
                      par_ref, w2_ref, a2_ref, tri_ref,
                      vo_ref, kf_ref, bf_ref, af_ref, rf_ref, kb_ref, bb_ref, ab_ref, rb_ref, ptot_ref, bonus_ref,
                      *, n_ctx_tiles, n_tiles):
    i = pl.program_id(1)
    first = (i == 0) | (i == n_ctx_tiles)
    last = (i == n_ctx_tiles - 1) | (i == n_tiles - 1)
    row = lax.broadcasted_iota(jnp.int32, (ROW_TILE, 1), 0)
    par = par_ref[...]

    def prm(p):
        return par[p:p + 1, :]

    def mix(z_ref, prev_ref, next_ref, mu):
        z = z_ref[0]
        prev = jnp.where(first, 0.0, prev_ref[0, 7:8, :])
        nxt = jnp.where(last, 0.0, next_ref[0, 0:1, :])
        z_prev = jnp.where(row == 0, prev, pltpu.roll(z, 1, 0))
        z_next = jnp.where(row == ROW_TILE - 1, nxt, pltpu.roll(z, ROW_TILE - 1, 0))
        return z + mu * (0.5 * (z_prev + z_next) - z)

    r = mix(r_ref, rp_ref, rn_ref, prm(P_MU_R))
    k = mix(k_ref, kp_ref, kn_ref, prm(P_MU_K))
    v = mix(v_ref, vp_ref, vn_ref, prm(P_MU_V))
    vo_ref[0] = v

    kk = k * prm(P_K_K)
    kk = kk * jnp.minimum(lax.rsqrt(_head_sum(kk * kk)), 1e12)

    lora = lora_ref[0]
    lane = lax.broadcasted_iota(jnp.int32, (1, LANES), 1)
    lora = jnp.where(lane < RWKV_LORA, jnp.tanh(lora), lora).astype(BF16)
    k_sum = None
    outs = ((kf_ref, bf_ref, af_ref, rf_ref), (kb_ref, bb_ref, ab_ref, rb_ref))
    chunk_decay = []
    for d, (k_out, b_out, a_out, r_out) in enumerate(outs):
        x = prm(P_W0_F + d) + jnp.dot(lora, w2_ref[d], preferred_element_type=F32)
        w_log = -(jnp.maximum(-x, 0.0) + jnp.log(1.0 + jnp.exp(-jnp.abs(x)))) - 0.5
        neg_log_w = jnp.exp(w_log)
        hi, mid, lo = _split3(neg_log_w)
        tri = tri_ref[d]
        cs = (jnp.dot(tri, hi, preferred_element_type=F32) + jnp.dot(tri, mid, preferred_element_type=F32)
              + jnp.dot(tri, lo, preferred_element_type=F32))
        grow = jnp.exp(cs)
        shrink = jnp.exp(-cs)
        a = 0.5 + 0.5 * jnp.tanh(0.5 * (prm(P_A0_F + d) + jnp.dot(lora, a2_ref[d], preferred_element_type=F32)))
        k_d = k * (1.0 + (a - 1.0) * prm(P_K_A))
        for q in range(ROW_TILE // SCAN_STEPS):
            last = q * SCAN_STEPS + (SCAN_STEPS - 1 if d == 0 else 0)
            chunk_decay.append(shrink[last:last + 1])
        k_out[0] = k_d * grow
        b_out[0] = kk * a * grow
        a_out[0] = -kk * jnp.exp(neg_log_w - cs)
        r_out[0] = r * shrink
        k_sum = k_d if k_sum is None else k_sum + k_d
    ptot_ref[0, 0] = jnp.concatenate(chunk_decay, axis=0)
    bonus_ref[0] = _head_sum(r * k_sum * prm(P_R_K)) * v


def _rwkv_features(p_rw, p_lora, par, w2, a2, n_ctx):
    B, R, _ = p_rw.shape
    n_tiles = R // ROW_TILE
    sub = ROW_TILE // 8
    chunks_per_tile = ROW_TILE // SCAN_STEPS
    assert 2 * chunks_per_tile == 8

    def main(c):
        return pl.BlockSpec((1, ROW_TILE, W_BRANCH), lambda b, i: (b, i, c))

    def prev(c):
        return pl.BlockSpec((1, 8, W_BRANCH), lambda b, i: (b, jnp.maximum(i * sub - 1, 0), c))

    def nxt(c):
        return pl.BlockSpec((1, 8, W_BRANCH), lambda b, i: (b, jnp.minimum((i + 1) * sub, n_tiles * sub - 1), c))

    in_specs = []
    for c in (COL_RW_R, COL_RW_K, COL_RW_V):
        in_specs += [main(c), prev(c), nxt(c)]
    in_specs += [pl.BlockSpec((1, ROW_TILE, LANES), lambda b, i: (b, i, 0)),
                 pl.BlockSpec((P_ROWS, W_BRANCH), lambda b, i: (0, 0)),
                 pl.BlockSpec((2, LANES, W_BRANCH), lambda b, i: (0, 0, 0)),
                 pl.BlockSpec((2, LANES, W_BRANCH), lambda b, i: (0, 0, 0)),
                 pl.BlockSpec((2, ROW_TILE, ROW_TILE), lambda b, i: (0, 0, 0))]
    t_idx = np.arange(ROW_TILE)
    same_chunk = t_idx[:, None] // SCAN_STEPS == t_idx[None, :] // SCAN_STEPS
    tri = jnp.asarray(np.stack([same_chunk & (t_idx[None, :] <= t_idx[:, None]),
                                same_chunk & (t_idx[None, :] >= t_idx[:, None])]), BF16)
    out = jax.ShapeDtypeStruct((B, R, W_BRANCH), F32)
    return pl.pallas_call(
        functools.partial(_rwkv_feat_kernel, n_ctx_tiles=n_ctx // ROW_TILE, n_tiles=n_tiles),
        grid=(B, n_tiles),
        in_specs=in_specs,
        out_specs=[pl.BlockSpec((1, ROW_TILE, W_BRANCH), lambda b, i: (b, i, 0))] * 9
        + [pl.BlockSpec((1, 1, 2 * chunks_per_tile, W_BRANCH), lambda b, i: (b, i, 0, 0)),
           pl.BlockSpec((1, ROW_TILE, W_BRANCH), lambda b, i: (b, i, 0))],
        out_shape=[out] * 9 + [jax.ShapeDtypeStruct((B, n_tiles, 2 * chunks_per_tile, W_BRANCH), F32), out],
        compiler_params=_params("parallel", "parallel"),
        name="rwkv_features",
    )(*([p_rw] * 9), p_lora, par, w2, a2, tri)


def _scan_kernel(p_ref, k_ref, b_ref, a_ref, r_ref, v_ref, y_ref, s_ref):
    n = HEAD_DIM

    @pl.when(pl.program_id(0) == 0)
    def _():
        s_ref[...] = jnp.zeros_like(s_ref)

    def row(ref, j, t):
        return ref[0, pl.ds(j * SCAN_STEPS + t, 1), :]

    zero = jnp.zeros((n, s_ref.shape[2]), F32)

    def first_sa(jb, sa):
        for jj in range(SCAN_J_UNROLL):
            j = jb * SCAN_J_UNROLL + jj
            sa = sa + s_ref[j] * row(a_ref, j, 0)
        return sa

    def step(t, sa):
        tile_rows = pl.ds(pl.multiple_of(t * n, n), n)
        vt = v_ref[0, tile_rows, :]
        t_next = jnp.minimum(t + 1, SCAN_STEPS - 1)

        def columns(jb, carry):
            y, sa_next = carry
            for jj in range(SCAN_J_UNROLL):
                j = jb * SCAN_J_UNROLL + jj
                sj = s_ref[j] + sa * row(b_ref, j, t) + vt * row(k_ref, j, t)
                s_ref[j] = sj
                y = y + sj * row(r_ref, j, t)
                sa_next = sa_next + sj * row(a_ref, j, t_next)
            return y, sa_next

        y, sa_next = lax.fori_loop(0, n // SCAN_J_UNROLL, columns, (zero, zero))
        y_ref[0, tile_rows, :] = y
        return sa_next

    def rescale(jb, carry):
        for jj in range(SCAN_J_UNROLL):
            j = jb * SCAN_J_UNROLL + jj
            s_ref[j] = s_ref[j] * p_ref[0, pl.ds(j, 1), :]
        return carry

    sa0 = lax.fori_loop(0, n // SCAN_J_UNROLL, first_sa, zero)
    lax.fori_loop(0, SCAN_STEPS, step, sa0)
    lax.fori_loop(0, n // SCAN_J_UNROLL, rescale, 0)


def _chunk_decay_to_scan(ptot, n_ctx):
    B, n_tiles = ptot.shape[:2]
    pt = ptot.reshape(B, n_tiles, 2, ROW_TILE // SCAN_STEPS, HEAD_DIM, N_HEADS)
    n_ctx_chunks = n_ctx // SCAN_STEPS

    def chains(z):
        return z.reshape(B, -1, HEAD_DIM, N_HEADS).transpose(1, 2, 0, 3).reshape(-1, HEAD_DIM, B * N_HEADS)

    fwd, bwd = chains(pt[:, :, 0]), chains(pt[:, :, 1])
    bwd = jnp.concatenate([bwd[:n_ctx_chunks][::-1], bwd[n_ctx_chunks:][::-1]], axis=0)
    return jnp.concatenate([fwd, bwd], axis=-1)


def _wkv_scan(p, k, b, a, r, v):
    n_chunks, rows, chains = k.shape
    spec = pl.BlockSpec((1, rows, chains), lambda s: (s, 0, 0))
    return pl.pallas_call(
        _scan_kernel,
        grid=(n_chunks,),
        in_specs=[pl.BlockSpec((1, HEAD_DIM, chains), lambda s: (s, 0, 0))] + [spec] * 5,
        out_specs=spec,
        out_shape=jax.ShapeDtypeStruct(k.shape, F32),
        scratch_shapes=[pltpu.VMEM((HEAD_DIM, HEAD_DIM, chains), F32)],
        compiler_params=_params("arbitrary"),
        name="wkv_scan",
    )(p, k, b, a, r, v)


def _flip_rows(x, flip):
    hi, mid, lo = _split3(x)
    return (jnp.dot(flip, hi, preferred_element_type=F32) + jnp.dot(flip, mid, preferred_element_type=F32)
            + jnp.dot(flip, lo, preferred_element_type=F32))


def _mirror_chunk(c, n_ctx_chunks, n_chunks):
    return jnp.where(c < n_ctx_chunks, n_ctx_chunks - 1 - c, n_ctx_chunks + n_chunks - 1 - c)


def _to_scan_kernel(zf_ref, zb_ref, flip_ref, o_ref, t_ref, *, step_major):
    nb = zf_ref.shape[0]
    for b in range(nb):
        t_ref[b] = zf_ref[b].T
        t_ref[nb + b] = _flip_rows(zb_ref[b], flip_ref[...]).T
    sub = RELAYOUT_ROWS // SCAN_STEPS

    def body(n, carry):
        rows = pl.ds(pl.multiple_of(n * N_HEADS, N_HEADS), N_HEADS)
        slabs = [t_ref[g, rows, :] for g in range(2 * nb)]
        tile = jnp.concatenate(slabs, axis=0).T
        for q in range(sub):
            if step_major:
                dst = pl.ds(n, SCAN_STEPS, stride=HEAD_DIM)
            else:
                dst = pl.ds(pl.multiple_of(n * SCAN_STEPS, SCAN_STEPS), SCAN_STEPS)
            o_ref[q, dst, :] = tile[q * SCAN_STEPS:(q + 1) * SCAN_STEPS]
        return carry

    lax.fori_loop(0, HEAD_DIM, body, 0, unroll=RELAYOUT_UNROLL)


def _to_scan(z_fwd, z_bwd, flip, n_ctx, step_major=False):
    B, R, _ = z_fwd.shape
    n_chunks = R // RELAYOUT_ROWS
    n_ctx_chunks = n_ctx // RELAYOUT_ROWS
    sub = RELAYOUT_ROWS // SCAN_STEPS
    chains = 2 * B * N_HEADS
    return pl.pallas_call(
        functools.partial(_to_scan_kernel, step_major=step_major),
        grid=(n_chunks,),
        in_specs=[pl.BlockSpec((B, RELAYOUT_ROWS, W_BRANCH), lambda c: (0, c, 0)),
                  pl.BlockSpec((B, RELAYOUT_ROWS, W_BRANCH),
                               lambda c: (0, _mirror_chunk(c, n_ctx_chunks, n_chunks), 0)),
                  pl.BlockSpec((RELAYOUT_ROWS, RELAYOUT_ROWS), lambda c: (0, 0))],
        out_specs=pl.BlockSpec((sub, HEAD_DIM * SCAN_STEPS, chains), lambda c: (c, 0, 0)),
        out_shape=jax.ShapeDtypeStruct((R // SCAN_STEPS, HEAD_DIM * SCAN_STEPS, chains), F32),
        scratch_shapes=[pltpu.VMEM((2 * B, W_BRANCH, RELAYOUT_ROWS), F32)],
        compiler_params=_params("parallel"),
        name="to_scan_layout",
    )(z_fwd, z_bwd, flip)


def _from_scan_kernel(y_ref, flip_ref, yf_ref, yb_ref, t_ref):
    nb = yf_ref.shape[0]
    sub = RELAYOUT_ROWS // SCAN_STEPS

    def body(n, carry):
        tile = jnp.concatenate([y_ref[q, pl.ds(n, SCAN_STEPS, stride=HEAD_DIM), :] for q in range(sub)], axis=0)
        tile = tile.T
        rows = pl.ds(pl.multiple_of(n * N_HEADS, N_HEADS), N_HEADS)
        for g in range(2 * nb):
            t_ref[g, rows, :] = tile[g * N_HEADS:(g + 1) * N_HEADS]
        return carry

    lax.fori_loop(0, HEAD_DIM, body, 0, unroll=RELAYOUT_UNROLL)
    for b in range(nb):
        yf_ref[b] = t_ref[b].T
        yb_ref[b] = _flip_rows(t_ref[nb + b].T, flip_ref[...])


def _from_scan(y, flip, n_batch, n_ctx):
    R = y.shape[0] * SCAN_STEPS
    n_chunks = R // RELAYOUT_ROWS
    n_ctx_chunks = n_ctx // RELAYOUT_ROWS
    sub = RELAYOUT_ROWS // SCAN_STEPS
    out = jax.ShapeDtypeStruct((n_batch, R, W_BRANCH), F32)
    return pl.pallas_call(
        _from_scan_kernel,
        grid=(n_chunks,),
        in_specs=[pl.BlockSpec((sub, HEAD_DIM * SCAN_STEPS, y.shape[2]), lambda c: (c, 0, 0)),
                  pl.BlockSpec((RELAYOUT_ROWS, RELAYOUT_ROWS), lambda c: (0, 0))],
        out_specs=[pl.BlockSpec((n_batch, RELAYOUT_ROWS, W_BRANCH), lambda c: (0, c, 0)),
                   pl.BlockSpec((n_batch, RELAYOUT_ROWS, W_BRANCH),
                                lambda c: (0, _mirror_chunk(c, n_ctx_chunks, n_chunks), 0))],
        out_shape=[out, out],
        scratch_shapes=[pltpu.VMEM((2 * n_batch, W_BRANCH, RELAYOUT_ROWS), F32)],
        compiler_params=_params("parallel"),
        name="from_scan_layout",
    )(y, flip)


def _rwkv_readout_kernel(yf_ref, yb_ref, bonus_ref, gate_ref, gb_ref, o_ref):
    y = yf_ref[0] + yb_ref[0]
    mu = _head_sum(y) * (1.0 / HEAD_DIM)
    yc = y - mu
    var = _head_sum(yc * yc) * (1.0 / HEAD_DIM)
    gb = gb_ref[...]
    out = yc * lax.rsqrt(var + LNX_EPS) * gb[0:1, :] + gb[1:2, :] + bonus_ref[0]
    o_ref[0] = (out * _silu(gate_ref[0])).astype(o_ref.dtype)


def _rwkv_readout(y_fwd, y_bwd, bonus, p_rw, lnx_gb):
    B, R, _ = y_fwd.shape
    tile = pl.BlockSpec((1, ROW_TILE, W_BRANCH), lambda b, i: (b, i, 0))
    return pl.pallas_call(
        _rwkv_readout_kernel,
        grid=(B, R // ROW_TILE),
        in_specs=[tile, tile, tile,
                  pl.BlockSpec((1, ROW_TILE, W_BRANCH), lambda b, i: (b, i, COL_RW_GATE)),
                  pl.BlockSpec((8, W_BRANCH), lambda b, i: (0, 0))],
        out_specs=tile,
        out_shape=jax.ShapeDtypeStruct((B, R, W_BRANCH), BF16),
        compiler_params=_params("parallel", "parallel"),
        name="rwkv_readout",
    )(y_fwd, y_bwd, bonus, p_rw, lnx_gb)


def _merge_kernel(na_ref, pool_ref, rw_ref, lna_ref, lpool_ref, lrw_ref, w_ref, o_ref):
    acc = None
    for br, (x_ref, l_ref) in enumerate(((na_ref, lna_ref), (pool_ref, lpool_ref), (rw_ref, lrw_ref))):
        t = _sigmoid(l_ref[...]) * jnp.dot(x_ref[...], w_ref[br], preferred_element_type=F32)
        acc = t if acc is None else acc + t
    o_ref[...] = acc.astype(o_ref.dtype)


def _merge(b_na, b_pool, b_rw, p_merge, w_branch):
    M = b_na.shape[0]
    tm, tn = min(_row_tile(M), 512), 1024
    nb = D_MODEL // tn
    x_spec = pl.BlockSpec((tm, W_BRANCH), lambda j, i: (i, 0))

    def logit(br):
        return pl.BlockSpec((tm, tn), lambda j, i: (i, br * nb + j))

    return pl.pallas_call(
        _merge_kernel,
        grid=(nb, M // tm),
        in_specs=[x_spec, x_spec, x_spec, logit(0), logit(1), logit(2),
                  pl.BlockSpec((N_BRANCH, W_BRANCH, tn), lambda j, i: (0, 0, j))],
        out_specs=pl.BlockSpec((tm, tn), lambda j, i: (i, j)),
        out_shape=jax.ShapeDtypeStruct((M, D_MODEL), BF16),
        compiler_params=_params("parallel", "parallel"),
        name="branch_merge",
    )(b_na, b_pool, b_rw, p_merge, p_merge, p_merge, w_branch)


def _out_kernel(m_ref, w_ref, c_ref, l_ref, mod_ref, fg_ref, o_ref, *, n_ctx_tiles, tile_offset, final):
    tile = pl.program_id(1) + tile_offset
    gate = _mod_row(mod_ref, tile, n_ctx_tiles)[:, 2 * D_MODEL:]
    x = _stream_tile(c_ref, l_ref, tile, n_ctx_tiles)
    x = x + gate * jnp.dot(m_ref[0], w_ref[...], preferred_element_type=F32)
    o_ref[0] = _rms(x, fg_ref[...]) if final else x


def _out_proj(merged, w_out, stream, mod, final_g, n_ctx, final):
    B, R, _ = merged.shape
    n_ctx_tiles = n_ctx // ROW_TILE
    off = n_ctx_tiles if final else 0
    return pl.pallas_call(
        functools.partial(_out_kernel, n_ctx_tiles=n_ctx_tiles, tile_offset=off, final=final),
        grid=(B, R // ROW_TILE - off),
        in_specs=[pl.BlockSpec((1, ROW_TILE, D_MODEL), lambda b, i: (b, i + off, 0)),
                  pl.BlockSpec((D_MODEL, D_MODEL), lambda b, i: (0, 0))]
        + _stream_specs(stream, n_ctx_tiles, off) + [
            pl.BlockSpec((MOD_ROWS, 3 * D_MODEL), lambda b, i: (0, 0)),
            pl.BlockSpec((1, D_MODEL), lambda b, i: (0, 0))],
        out_specs=pl.BlockSpec((1, ROW_TILE, D_MODEL), lambda b, i: (b, i, 0)),
        out_shape=jax.ShapeDtypeStruct((B, R - off * ROW_TILE, D_MODEL), F32),
        compiler_params=_params("parallel", "parallel"),
        name="out_proj_final" if final else "out_proj",
    )(merged, w_out, stream[0], stream[1], mod, final_g)


def _layer(stream, R, mod, n_ctx, final, final_g, norm_g, w_in_all, layer, na_rpb, pool_w, pool_scale, rw_mu, rw_w0, rw_w2,
           rw_a0, rw_a2, rw_k_k, rw_k_a, rw_r_k, rw_lnx_g, rw_lnx_b, w_branch, w_out):
    B = stream[0].shape[0]
    rows = (R - n_ctx) // GRID_W

    h = _norm_mod(stream, R, norm_g[None], mod, n_ctx).reshape(B * R, D_MODEL)
    lo = N_MAIN + 2 * RWKV_LORA
    w_in = w_in_all[layer]
    p_ap = _matmul(h, w_in_all, F32, "in_proj_attn_pool", n_cols=N_ATTN_POOL, layer=layer)
    p_ap = p_ap.reshape(B, R, N_ATTN_POOL)
    w_rw = _head_major(w_in[:, N_ATTN_POOL:N_MAIN].reshape(D_MODEL, -1, W_BRANCH)).reshape(D_MODEL, -1)
    p_rw = _matmul(h, w_rw.astype(BF16), F32, "in_proj_rwkv").reshape(B, R, N_MAIN - N_ATTN_POOL)
    p_lora = _matmul(h, w_in[:, N_MAIN:lo].astype(BF16), F32, "in_proj_lora").reshape(B, R, 2 * RWKV_LORA)
    p_merge = _matmul(h, w_in[:, lo:].astype(BF16), F32, "in_proj_merge")

    b_na = _na_attention(p_ap, _na_bias_tables(na_rpb, rows), n_ctx)
    b_pool = _pool(p_ap, pool_w.astype(BF16), pool_scale[None], n_ctx)

    par = jnp.zeros((P_ROWS, W_BRANCH), F32)
    par = par.at[P_MU_R:P_MU_V + 1].set(rw_mu).at[P_W0_F:P_W0_B + 1].set(rw_w0).at[P_A0_F:P_A0_B + 1].set(rw_a0)
    par = _head_major(par.at[P_K_K].set(rw_k_k).at[P_K_A].set(rw_k_a).at[P_R_K].set(rw_r_k.reshape(-1)))
    zeros = jnp.zeros_like(rw_w2)
    w2 = _head_major(jnp.concatenate([rw_w2, zeros], axis=1)).astype(BF16)
    a2 = _head_major(jnp.concatenate([zeros, rw_a2], axis=1)).astype(BF16)
    v, k_f, b_f, a_f, r_f, k_b, b_b, a_b, r_b, ptot, bonus = _rwkv_features(p_rw, p_lora, par, w2, a2, n_ctx)
    flip = jnp.asarray(np.eye(RELAYOUT_ROWS)[::-1], BF16)
    y = _wkv_scan(_chunk_decay_to_scan(ptot, n_ctx),
                  *[_to_scan(zf, zb, flip, n_ctx) for zf, zb in ((k_f, k_b), (b_f, b_b), (a_f, a_b), (r_f, r_b))],
                  _to_scan(v, v, flip, n_ctx, step_major=True))
    y_fwd, y_bwd = _from_scan(y, flip, B, n_ctx)
    lnx_gb = _head_major(jnp.zeros((8, W_BRANCH), F32).at[0].set(rw_lnx_g).at[1].set(rw_lnx_b))
    b_rw = _rwkv_readout(y_fwd, y_bwd, bonus, p_rw, lnx_gb)

    def flat(z):
        return z.reshape(B * R, W_BRANCH)

    w_rw_out = _head_major(w_branch[2].T).T
    w_br = jnp.stack([w_branch[0], w_branch[1], w_rw_out]).astype(BF16)
    merged = _merge(flat(b_na), flat(b_pool), flat(b_rw), p_merge, w_br)
    return _out_proj(merged.reshape(B, R, D_MODEL), w_out.astype(BF16), stream, mod, final_g[None], n_ctx, final)


def kernel(x, c, ctx, c_ctx, norm_g, w_mod, b_mod, w_in, na_rpb, pool_w, pool_scale, rw_mu, rw_w0, rw_w2, rw_a0,
           rw_a2, rw_k_k, rw_k_a, rw_r_k, rw_lnx_g, rw_lnx_b, w_branch, w_out, final_g):
    B, T, _ = x.shape
    n_ctx = ctx.shape[1]
    depth = w_in.shape[0]
    assert B <= CTX_MOD_ROW and n_ctx % ROW_TILE == 0 and T % ROW_TILE == 0
    assert ROW_TILE % RELAYOUT_ROWS == 0 and RELAYOUT_ROWS % SCAN_STEPS == 0

    cond = jnp.zeros((MOD_ROWS, D_MODEL), F32).at[:B].set(c).at[CTX_MOD_ROW].set(c_ctx)
    mods = _modulation(cond, w_mod, b_mod[:, None, :])
    stream = (ctx, x, n_ctx // ROW_TILE)
    for layer in range(depth):
        out = _layer(stream, n_ctx + T, mods[layer], n_ctx, layer == depth - 1, final_g, norm_g[layer], w_in, layer,
                     na_rpb[layer], pool_w[layer], pool_scale[layer], rw_mu[layer], rw_w0[layer], rw_w2[layer],
                     rw_a0[layer], rw_a2[layer], rw_k_k[layer], rw_k_a[layer], rw_r_k[layer], rw_lnx_g[layer],
                     rw_lnx_b[layer], w_branch[layer], w_out[layer])
        stream = (out, out, 0)
    return out
```

```python
import functools

import numpy as np
import jax
import jax.numpy as jnp
from jax import lax
from jax.experimental import pallas as pl
from jax.experimental.pallas import tpu as pltpu

F32 = jnp.float32
BF16 = jnp.bfloat16

D_MODEL = 2048
W_BRANCH = D_MODEL // 2
N_BRANCH = 3
N_HEADS = 16
HEAD_DIM = 64
GRID_W = 64
NA_WIN_H = 8
NA_WIN_W = 16
POOL_WINDOWS = (2, 4, 8, 16)
POOL_GROUP_DIM = W_BRANCH // len(POOL_WINDOWS)
POOL_HALO = max(POOL_WINDOWS) // 2
RWKV_LORA = 64
RMS_EPS = 1e-6
LNX_EPS = 64e-5
NEG_INF = -1e30

LANES = 128
ROW_TILE = 256
NA_Q_ROWS = ROW_TILE // GRID_W
NA_K_ROWS = NA_Q_ROWS + NA_WIN_H
NA_K_TOK = NA_K_ROWS * GRID_W
CTX_MOD_ROW = 4
MOD_ROWS = 8
SCAN_STEPS = 64
SCAN_J_UNROLL = 32
RELAYOUT_UNROLL = 32
RELAYOUT_ROWS = 128
VMEM_LIMIT = 56 << 20

COL_Q, COL_K, COL_V, COL_NA_GATE, COL_POOL_U, COL_POOL_GATE = range(6)
COL_RW_R, COL_RW_K, COL_RW_V, COL_RW_GATE = range(4)
N_ATTN_POOL = 6 * W_BRANCH
N_MAIN = 10 * W_BRANCH

P_MU_R, P_MU_K, P_MU_V, P_W0_F, P_W0_B, P_A0_F, P_A0_B, P_K_K, P_K_A, P_R_K = range(10)
P_ROWS = 16


def _params(*sem):
    return pltpu.CompilerParams(dimension_semantics=sem, vmem_limit_bytes=VMEM_LIMIT)


def _sigmoid(x):
    return 1.0 / (1.0 + jnp.exp(-x))


def _silu(x):
    return x * _sigmoid(x)


def _split3(x):
    hi = x.astype(BF16)
    r1 = x - hi.astype(F32)
    mid = r1.astype(BF16)
    lo = (r1 - mid.astype(F32)).astype(BF16)
    return hi, mid, lo


def _head_major(z):
    lead = z.shape[:-1]
    return z.reshape(lead + (N_HEADS, HEAD_DIM)).swapaxes(-1, -2).reshape(lead + (W_BRANCH,))


def _head_sum(x):
    n_tiles = W_BRANCH // LANES
    part = x[:, :LANES]
    for c in range(1, n_tiles):
        part = part + x[:, c * LANES:(c + 1) * LANES]
    shift = N_HEADS
    while shift < LANES:
        part = part + pltpu.roll(part, shift, 1)
        shift *= 2
    return jnp.concatenate([part] * n_tiles, axis=1)


def _mod_kernel(cond_ref, w_ref, b_ref, o_ref):
    s = _silu(cond_ref[...])
    o_ref[0] = jnp.dot(s.astype(BF16), w_ref[0].astype(BF16), preferred_element_type=F32) + b_ref[0]


def _modulation(cond, w_mod, b_mod):
    n_layers = w_mod.shape[0]
    tn = 3 * D_MODEL // 4
    return pl.pallas_call(
        _mod_kernel,
        grid=(n_layers, 4),
        in_specs=[pl.BlockSpec((MOD_ROWS, D_MODEL), lambda l, j: (0, 0)),
                  pl.BlockSpec((1, D_MODEL, tn), lambda l, j: (l, 0, j)),
                  pl.BlockSpec((1, 1, tn), lambda l, j: (l, 0, j))],
        out_specs=pl.BlockSpec((1, MOD_ROWS, tn), lambda l, j: (l, 0, j)),
        out_shape=jax.ShapeDtypeStruct((n_layers, MOD_ROWS, 3 * D_MODEL), F32),
        compiler_params=_params("arbitrary", "arbitrary"),
        name="adaln_modulation",
    )(cond, w_mod, b_mod)


def _mod_row(mod_ref, tile, n_ctx_tiles):
    row = jnp.where(tile < n_ctx_tiles, CTX_MOD_ROW, pl.program_id(0))
    return mod_ref[pl.ds(row, 1), :]


def _rms(x, g):
    return x * lax.rsqrt(jnp.mean(x * x, axis=-1, keepdims=True) + RMS_EPS) * g


def _stream_specs(stream, n_ctx_tiles, tile_offset=0):
    lat_shift = stream[2]
    return [pl.BlockSpec((1, ROW_TILE, D_MODEL), lambda b, i: (b, jnp.minimum(i + tile_offset, n_ctx_tiles - 1), 0)),
            pl.BlockSpec((1, ROW_TILE, D_MODEL), lambda b, i: (b, jnp.maximum(i + tile_offset - lat_shift, 0), 0))]


def _stream_tile(c_ref, l_ref, tile, n_ctx_tiles):
    return jnp.where(tile < n_ctx_tiles, c_ref[0], l_ref[0])


def _norm_mod_kernel(c_ref, l_ref, g_ref, mod_ref, h_ref, *, n_ctx_tiles):
    tile = pl.program_id(1)
    m = _mod_row(mod_ref, tile, n_ctx_tiles)
    shift = m[:, :D_MODEL]
    scale = m[:, D_MODEL:2 * D_MODEL]
    x = _stream_tile(c_ref, l_ref, tile, n_ctx_tiles)
    h_ref[0] = (_rms(x, g_ref[...]) * (1.0 + scale) + shift).astype(BF16)


def _norm_mod(stream, n_rows, norm_g, mod, n_ctx):
    B = stream[0].shape[0]
    n_ctx_tiles = n_ctx // ROW_TILE
    return pl.pallas_call(
        functools.partial(_norm_mod_kernel, n_ctx_tiles=n_ctx_tiles),
        grid=(B, n_rows // ROW_TILE),
        in_specs=_stream_specs(stream, n_ctx_tiles) + [
            pl.BlockSpec((1, D_MODEL), lambda b, i: (0, 0)),
            pl.BlockSpec((MOD_ROWS, 3 * D_MODEL), lambda b, i: (0, 0))],
        out_specs=pl.BlockSpec((1, ROW_TILE, D_MODEL), lambda b, i: (b, i, 0)),
        out_shape=jax.ShapeDtypeStruct((B, n_rows, D_MODEL), BF16),
        compiler_params=_params("parallel", "parallel"),
        name="norm_modulate",
    )(stream[0], stream[1], norm_g, mod)


def _mm_kernel(a_ref, w_ref, o_ref):
    o_ref[...] = jnp.dot(a_ref[...], w_ref[...], preferred_element_type=F32).astype(o_ref.dtype)


def _row_tile(m):
    for t in (1024, 512, 256):
        if m % t == 0:
            return t
    raise ValueError(f"row count {m} is not a multiple of {ROW_TILE}")


def _mm_cast_kernel(a_ref, w_ref, o_ref, wb_ref):
    @pl.when(pl.program_id(1) == 0)
    def _():
        wb_ref[...] = w_ref[...].astype(BF16)

    o_ref[...] = jnp.dot(a_ref[...], wb_ref[...], preferred_element_type=F32).astype(o_ref.dtype)


def _matmul(a, w, out_dtype, name, n_cols=None, layer=None):
    M, K = a.shape
    N = w.shape[-1] if n_cols is None else n_cols
    tm = _row_tile(M)
    tn = min(N, 1024)
    cast = w.dtype != BF16
    if layer is None:
        w_spec = pl.BlockSpec((K, tn), lambda j, i: (0, j))
    else:
        w_spec = pl.BlockSpec((None, K, tn), lambda j, i: (layer, 0, j))
    return pl.pallas_call(
        _mm_cast_kernel if cast else _mm_kernel,
        grid=(N // tn, M // tm),
        in_specs=[pl.BlockSpec((tm, K), lambda j, i: (i, 0)), w_spec],
        out_specs=pl.BlockSpec((tm, tn), lambda j, i: (i, j)),
        out_shape=jax.ShapeDtypeStruct((M, N), out_dtype),
        scratch_shapes=[pltpu.VMEM((K, tn), BF16)] if cast else [],
        compiler_params=_params("parallel", "arbitrary" if cast else "parallel"),
        name=name,
    )(a, w)


def _na_bias_tables(rpb, rows):
    n_blocks = rows // NA_Q_ROWS
    n_off = 2 * NA_WIN_H - 1
    col = np.arange(GRID_W)
    c0 = np.clip(col - NA_WIN_W // 2, 0, GRID_W - NA_WIN_W)
    valid_c = (col[None, :] >= c0[:, None]) & (col[None, :] < c0[:, None] + NA_WIN_W)
    col_off = np.clip(col[None, :] - col[:, None] + NA_WIN_W - 1, 0, 2 * NA_WIN_W - 2)
    pick_c = jnp.asarray(np.eye(2 * NA_WIN_W - 1)[col_off], F32)
    tile = jnp.einsum("hrc,qpc->hrqp", rpb.astype(F32), pick_c, precision=lax.Precision.HIGHEST)
    tile = jnp.where(valid_c, tile, NEG_INF)
    tile = jnp.concatenate([tile, jnp.full((N_HEADS, 1, GRID_W, GRID_W), NEG_INF, F32)], axis=1)
    tile = jnp.concatenate([tile, tile], axis=-1)
    picks = []
    for m in (0, 1, n_blocks - 1):
        q_row = NA_Q_ROWS * m + np.arange(NA_Q_ROWS)
        k_row = int(np.clip(NA_Q_ROWS * m - NA_Q_ROWS, 0, rows - NA_K_ROWS)) + np.arange(NA_K_ROWS)
        r0 = np.clip(q_row - NA_WIN_H // 2, 0, rows - NA_WIN_H)
        valid_r = (k_row[None, :] >= r0[:, None]) & (k_row[None, :] < r0[:, None] + NA_WIN_H)
        row_off = np.clip(k_row[None, :] - q_row[:, None] + NA_WIN_H - 1, 0, n_off - 1)
        picks.append(np.where(valid_r, row_off, n_off))
    picks = np.stack(picks)

    def build(t_ref, o_ref):
        left = lax.broadcasted_iota(jnp.int32, (1, LANES), 1) < GRID_W
        for ty in range(picks.shape[0]):
            for a in range(NA_Q_ROWS):
                for kp in range(NA_K_ROWS // 2):
                    pair = jnp.where(left, t_ref[0, int(picks[ty, a, 2 * kp])], t_ref[0, int(picks[ty, a, 2 * kp + 1])])
                    o_ref[ty, 0, a * GRID_W:(a + 1) * GRID_W, kp * LANES:(kp + 1) * LANES] = pair

    return pl.pallas_call(
        build,
        grid=(N_HEADS,),
        in_specs=[pl.BlockSpec((1, n_off + 1, GRID_W, LANES), lambda h: (h, 0, 0, 0))],
        out_specs=pl.BlockSpec((picks.shape[0], 1, ROW_TILE, NA_K_TOK), lambda h: (0, h, 0, 0)),
        out_shape=jax.ShapeDtypeStruct((picks.shape[0], N_HEADS, ROW_TILE, NA_K_TOK), F32),
        compiler_params=_params("parallel"),
        name="na_bias_tables",
    )(tile)


def _attend(qe, keys, vals, biases):
    dn = (((1,), (1,)), ((), ()))
    scores = []
    for kk, bias in zip(keys, biases):
        s = lax.dot_general(qe, kk, dn, preferred_element_type=F32)
        scores.append(s if bias is None else s + bias)
    m = scores[0].max(axis=-1, keepdims=True)
    for s in scores[1:]:
        m = jnp.maximum(m, s.max(axis=-1, keepdims=True))
    num, den = None, None
    for s, vv in zip(scores, vals):
        p = jnp.exp(s - m)
        l = p.sum(axis=-1, keepdims=True)
        o = jnp.dot(p.astype(BF16), vv, preferred_element_type=F32)
        num = o if num is None else num + o
        den = l if den is None else den + l
    return num / den


def _na_kernel(q_ref, k_ref, v_ref, g_ref, bias_ref, o_ref, *, n_ctx, rows):
    j = pl.program_id(2)
    lane = lax.broadcasted_iota(jnp.int32, (1, LANES), 1)
    in_head = (lane < HEAD_DIM, lane >= HEAD_DIM)
    q = q_ref[0] * (HEAD_DIM ** -0.5)
    kc = k_ref[0, 0:n_ctx, :].astype(BF16)
    vc = v_ref[0, 0:n_ctx, :].astype(BF16)

    def heads(q):
        return [jnp.where(in_head[e], q, 0.0).astype(BF16) for e in range(2)]

    def finish(o0, o1):
        o = jnp.where(in_head[0], o0, o1)
        o_ref[0] = (o * _silu(g_ref[0])).astype(o_ref.dtype)

    @pl.when(j == 0)
    def _():
        finish(*[_attend(qe, [kc], [vc], [None]) for qe in heads(q)])

    @pl.when(j > 0)
    def _():
        k_row = jnp.clip(NA_Q_ROWS * (j - 1) - NA_Q_ROWS, 0, rows - NA_K_ROWS)
        start = pl.multiple_of(n_ctx + k_row * GRID_W, GRID_W)
        kw = k_ref[0, pl.ds(start, NA_K_TOK), :].astype(BF16)
        vw = v_ref[0, pl.ds(start, NA_K_TOK), :].astype(BF16)
        finish(*[_attend(qe, [kw, kc], [vw, vc], [bias_ref[0, e], None])
                 for e, qe in enumerate(heads(q))])


def _na_attention(p_ap, bias_tables, n_ctx):
    B, R, _ = p_ap.shape
    rows = (R - n_ctx) // GRID_W
    n_blocks = rows // NA_Q_ROWS
    pairs = W_BRANCH // LANES
    assert n_ctx == ROW_TILE and rows >= NA_K_ROWS and rows % NA_Q_ROWS == 0

    def col(c):
        return lambda b, hp, j: (b, 0, c * pairs + hp)

    def bias_idx(b, hp, j):
        return (jnp.where(j <= 1, 0, jnp.where(j == n_blocks, 2, 1)), hp, 0, 0)

    return pl.pallas_call(
        functools.partial(_na_kernel, n_ctx=n_ctx, rows=rows),
        grid=(B, pairs, n_blocks + 1),
        in_specs=[pl.BlockSpec((1, ROW_TILE, LANES), lambda b, hp, j: (b, j, COL_Q * pairs + hp)),
                  pl.BlockSpec((1, R, LANES), col(COL_K)),
                  pl.BlockSpec((1, R, LANES), col(COL_V)),
                  pl.BlockSpec((1, ROW_TILE, LANES), lambda b, hp, j: (b, j, COL_NA_GATE * pairs + hp)),
                  pl.BlockSpec((1, 2, ROW_TILE, NA_K_TOK), bias_idx)],
        out_specs=pl.BlockSpec((1, ROW_TILE, LANES), lambda b, hp, j: (b, j, hp)),
        out_shape=jax.ShapeDtypeStruct((B, R, W_BRANCH), BF16),
        compiler_params=_params("parallel", "parallel", "arbitrary"),
        name="neighbourhood_attention",
    )(p_ap, p_ap, p_ap, p_ap, bias_tables)


def _pool_kernel(u_ref, g_ref, w_ref, sc_ref, o_ref, pad_ref, *, n_ctx, n_lat):
    grp = pl.program_id(1)
    w = w_ref[0]
    scale = sc_ref[...]

    def run(win):
        half = win // 2
        for seq_start, seq_len in ((0, n_ctx), (n_ctx, n_lat)):
            zeros = jnp.zeros((POOL_HALO, POOL_GROUP_DIM), F32)
            pad_ref[0:POOL_HALO, :] = zeros
            pad_ref[POOL_HALO:POOL_HALO + seq_len, :] = u_ref[0, seq_start:seq_start + seq_len, :]
            pad_ref[POOL_HALO + seq_len:2 * POOL_HALO + seq_len, :] = zeros

            def chunk(c, carry):
                base = pl.multiple_of(c * ROW_TILE, ROW_TILE)
                x = pad_ref[pl.ds(base, ROW_TILE + 2 * POOL_HALO), :]
                acc = x[POOL_HALO - half:POOL_HALO - half + ROW_TILE]
                for o in range(-half + 1, half):
                    acc = acc + x[POOL_HALO + o:POOL_HALO + o + ROW_TILE]
                t = base + lax.broadcasted_iota(jnp.int32, (ROW_TILE, 1), 0)
                cnt = jnp.minimum(t + half, seq_len) - jnp.maximum(t - half, 0)
                diff = acc / cnt.astype(F32) - x[POOL_HALO:POOL_HALO + ROW_TILE]
                y = jnp.dot(diff.astype(BF16), w, preferred_element_type=F32) * scale
                rows = pl.ds(seq_start + base, ROW_TILE)
                o_ref[0, rows, :] = (y * _silu(g_ref[0, rows, :])).astype(o_ref.dtype)
                return carry

            lax.fori_loop(0, seq_len // ROW_TILE, chunk, 0)

    for gi, win in enumerate(POOL_WINDOWS):
        pl.when(grp == gi)(functools.partial(run, win))


def _pool(p_ap, pool_w, pool_scale, n_ctx):
    B, R, _ = p_ap.shape
    groups = len(POOL_WINDOWS)
    return pl.pallas_call(
        functools.partial(_pool_kernel, n_ctx=n_ctx, n_lat=R - n_ctx),
        grid=(B, groups),
        in_specs=[pl.BlockSpec((1, R, POOL_GROUP_DIM), lambda b, g: (b, 0, COL_POOL_U * groups + g)),
                  pl.BlockSpec((1, R, POOL_GROUP_DIM), lambda b, g: (b, 0, COL_POOL_GATE * groups + g)),
                  pl.BlockSpec((1, POOL_GROUP_DIM, POOL_GROUP_DIM), lambda b, g: (g, 0, 0)),
                  pl.BlockSpec((1, POOL_GROUP_DIM), lambda b, g: (0, g))],
        out_specs=pl.BlockSpec((1, R, POOL_GROUP_DIM), lambda b, g: (b, 0, g)),
        out_shape=jax.ShapeDtypeStruct((B, R, W_BRANCH), BF16),
        scratch_shapes=[pltpu.VMEM((R - n_ctx + 2 * POOL_HALO, POOL_GROUP_DIM), F32)],
        compiler_params=_params("parallel", "arbitrary"),
        name="multiscale_pool",
    )(p_ap, p_ap, pool_w, pool_scale)


def _rwkv_feat_kernel(r_ref, rp_ref, rn_ref, k_ref, kp_ref, kn_ref, v_ref, vp_ref, vn_ref, lora_ref,
                      par_ref, w2_ref, a2_ref, tri_ref,
                      vo_ref, kf_ref, bf_ref, af_ref, rf_ref, kb_ref, bb_ref, ab_ref, rb_ref, ptot_ref, bonus_ref,
                      *, n_ctx_tiles, n_tiles):
    i = pl.program_id(1)
    first = (i == 0) | (i == n_ctx_tiles)
    last = (i == n_ctx_tiles - 1) | (i == n_tiles - 1)
    row = lax.broadcasted_iota(jnp.int32, (ROW_TILE, 1), 0)
    par = par_ref[...]

    def prm(p):
        return par[p:p + 1, :]

    def mix(z_ref, prev_ref, next_ref, mu):
        z = z_ref[0]
        prev = jnp.where(first, 0.0, prev_ref[0, 7:8, :])
        nxt = jnp.where(last, 0.0, next_ref[0, 0:1, :])
        z_prev = jnp.where(row == 0, prev, pltpu.roll(z, 1, 0))
        z_next = jnp.where(row == ROW_TILE - 1, nxt, pltpu.roll(z, ROW_TILE - 1, 0))
        return z + mu * (0.5 * (z_prev + z_next) - z)

    r = mix(r_ref, rp_ref, rn_ref, prm(P_MU_R))
    k = mix(k_ref, kp_ref, kn_ref, prm(P_MU_K))
    v = mix(v_ref, vp_ref, vn_ref, prm(P_MU_V))
    vo_ref[0] = v.T

    kk = k * prm(P_K_K)
    kk = kk * jnp.minimum(lax.rsqrt(_head_sum(kk * kk)), 1e12)

    lora = lora_ref[0]
    lane = lax.broadcasted_iota(jnp.int32, (1, LANES), 1)
    lora = jnp.where(lane < RWKV_LORA, jnp.tanh(lora), lora).astype(BF16)
    k_sum = None
    outs = ((kf_ref, bf_ref, af_ref, rf_ref), (kb_ref, bb_ref, ab_ref, rb_ref))
    chunk_decay = []
    for d, (k_out, b_out, a_out, r_out) in enumerate(outs):
        x = prm(P_W0_F + d) + jnp.dot(lora, w2_ref[d], preferred_element_type=F32)
        w_log = -(jnp.maximum(-x, 0.0) + jnp.log(1.0 + jnp.exp(-jnp.abs(x)))) - 0.5
        neg_log_w = jnp.exp(w_log)
        hi, mid, lo = _split3(neg_log_w)
        tri = tri_ref[d]
        cs = (jnp.dot(tri, hi, preferred_element_type=F32) + jnp.dot(tri, mid, preferred_element_type=F32)
              + jnp.dot(tri, lo, preferred_element_type=F32))
        grow = jnp.exp(cs)
        shrink = jnp.exp(-cs)
        a = 0.5 + 0.5 * jnp.tanh(0.5 * (prm(P_A0_F + d) + jnp.dot(lora, a2_ref[d], preferred_element_type=F32)))
        k_d = k * (1.0 + (a - 1.0) * prm(P_K_A))
        for q in range(ROW_TILE // SCAN_STEPS):
            last = q * SCAN_STEPS + (SCAN_STEPS - 1 if d == 0 else 0)
            chunk_decay.append(shrink[last:last + 1])
        k_out[0] = (k_d * grow).T
        b_out[0] = (kk * a * grow).T
        a_out[0] = (-kk * jnp.exp(neg_log_w - cs)).T
        r_out[0] = (r * shrink).T
        k_sum = k_d if k_sum is None else k_sum + k_d
    ptot_ref[0, 0] = jnp.concatenate(chunk_decay, axis=0)
    bonus_ref[0] = _head_sum(r * k_sum * prm(P_R_K)) * v


def _rwkv_features(p_rw, p_lora, par, w2, a2, n_ctx):
    B, R, _ = p_rw.shape
    n_tiles = R // ROW_TILE
    sub = ROW_TILE // 8
    chunks_per_tile = ROW_TILE // SCAN_STEPS
    assert 2 * chunks_per_tile == 8

    def main(c):
        return pl.BlockSpec((1, ROW_TILE, W_BRANCH), lambda b, i: (b, i, c))

    def prev(c):
        return pl.BlockSpec((1, 8, W_BRANCH), lambda b, i: (b, jnp.maximum(i * sub - 1, 0), c))

    def nxt(c):
        return pl.BlockSpec((1, 8, W_BRANCH), lambda b, i: (b, jnp.minimum((i + 1) * sub, n_tiles * sub - 1), c))

    in_specs = []
    for c in (COL_RW_R, COL_RW_K, COL_RW_V):
        in_specs += [main(c), prev(c), nxt(c)]
    in_specs += [pl.BlockSpec((1, ROW_TILE, LANES), lambda b, i: (b, i, 0)),
                 pl.BlockSpec((P_ROWS, W_BRANCH), lambda b, i: (0, 0)),
                 pl.BlockSpec((2, LANES, W_BRANCH), lambda b, i: (0, 0, 0)),
                 pl.BlockSpec((2, LANES, W_BRANCH), lambda b, i: (0, 0, 0)),
                 pl.BlockSpec((2, ROW_TILE, ROW_TILE), lambda b, i: (0, 0, 0))]
    t_idx = np.arange(ROW_TILE)
    same_chunk = t_idx[:, None] // SCAN_STEPS == t_idx[None, :] // SCAN_STEPS
    tri = jnp.asarray(np.stack([same_chunk & (t_idx[None, :] <= t_idx[:, None]),
                                same_chunk & (t_idx[None, :] >= t_idx[:, None])]), BF16)
    out = jax.ShapeDtypeStruct((B, R, W_BRANCH), F32)
    return pl.pallas_call(
        functools.partial(_rwkv_feat_kernel, n_ctx_tiles=n_ctx // ROW_TILE, n_tiles=n_tiles),
        grid=(B, n_tiles),
        in_specs=in_specs,
        out_specs=[pl.BlockSpec((1, W_BRANCH, ROW_TILE), lambda b, i: (b, 0, i))] * 9
        + [pl.BlockSpec((1, 1, 2 * chunks_per_tile, W_BRANCH), lambda b, i: (b, i, 0, 0)),
           pl.BlockSpec((1, ROW_TILE, W_BRANCH), lambda b, i: (b, i, 0))],
        out_shape=[jax.ShapeDtypeStruct((B, W_BRANCH, R), F32)] * 9 + [jax.ShapeDtypeStruct((B, n_tiles, 2 * chunks_per_tile, W_BRANCH), F32), out],
        compiler_params=_params("parallel", "parallel"),
        name="rwkv_features",
    )(*([p_rw] * 9), p_lora, par, w2, a2, tri)


def _scan_kernel(p_ref, k_ref, b_ref, a_ref, r_ref, v_ref, y_ref, s_ref):
    n = HEAD_DIM

    @pl.when(pl.program_id(0) == 0)
    def _():
        s_ref[...] = jnp.zeros_like(s_ref)

    def row(ref, j, t):
        return ref[0, pl.ds(j * SCAN_STEPS + t, 1), :]

    zero = jnp.zeros((n, s_ref.shape[2]), F32)

    def first_sa(jb, sa):
        for jj in range(SCAN_J_UNROLL):
            j = jb * SCAN_J_UNROLL + jj
            sa = sa + s_ref[j] * row(a_ref, j, 0)
        return sa

    def step(t, sa):
        tile_rows = pl.ds(pl.multiple_of(t * n, n), n)
        vt = v_ref[0, tile_rows, :]
        t_next = jnp.minimum(t + 1, SCAN_STEPS - 1)

        def columns(jb, carry):
            y, sa_next = carry
            for jj in range(SCAN_J_UNROLL):
                j = jb * SCAN_J_UNROLL + jj
                sj = s_ref[j] + sa * row(b_ref, j, t) + vt * row(k_ref, j, t)
                s_ref[j] = sj
                y = y + sj * row(r_ref, j, t)
                sa_next = sa_next + sj * row(a_ref, j, t_next)
            return y, sa_next

        y, sa_next = lax.fori_loop(0, n // SCAN_J_UNROLL, columns, (zero, zero))
        y_ref[0, tile_rows, :] = y
        return sa_next

    def rescale(jb, carry):
        for jj in range(SCAN_J_UNROLL):
            j = jb * SCAN_J_UNROLL + jj
            s_ref[j] = s_ref[j] * p_ref[0, pl.ds(j, 1), :]
        return carry

    sa0 = lax.fori_loop(0, n // SCAN_J_UNROLL, first_sa, zero)
    lax.fori_loop(0, SCAN_STEPS, step, sa0)
    lax.fori_loop(0, n // SCAN_J_UNROLL, rescale, 0)


def _chunk_decay_to_scan(ptot, n_ctx):
    B, n_tiles = ptot.shape[:2]
    pt = ptot.reshape(B, n_tiles, 2, ROW_TILE // SCAN_STEPS, HEAD_DIM, N_HEADS)
    n_ctx_chunks = n_ctx // SCAN_STEPS

    def chains(z):
        return z.reshape(B, -1, HEAD_DIM, N_HEADS).transpose(1, 2, 0, 3).reshape(-1, HEAD_DIM, B * N_HEADS)

    fwd, bwd = chains(pt[:, :, 0]), chains(pt[:, :, 1])
    bwd = jnp.concatenate([bwd[:n_ctx_chunks][::-1], bwd[n_ctx_chunks:][::-1]], axis=0)
    return jnp.concatenate([fwd, bwd], axis=-1)


def _wkv_scan(p, k, b, a, r, v):
    n_chunks, rows, chains = k.shape
    spec = pl.BlockSpec((1, rows, chains), lambda s: (s, 0, 0))
    return pl.pallas_call(
        _scan_kernel,
        grid=(n_chunks,),
        in_specs=[pl.BlockSpec((1, HEAD_DIM, chains), lambda s: (s, 0, 0))] + [spec] * 5,
        out_specs=spec,
        out_shape=jax.ShapeDtypeStruct(k.shape, F32),
        scratch_shapes=[pltpu.VMEM((HEAD_DIM, HEAD_DIM, chains), F32)],
        compiler_params=_params("arbitrary"),
        name="wkv_scan",
    )(p, k, b, a, r, v)


def _flip_rows(x, flip):
    hi, mid, lo = _split3(x)
    return (jnp.dot(flip, hi, preferred_element_type=F32) + jnp.dot(flip, mid, preferred_element_type=F32)
            + jnp.dot(flip, lo, preferred_element_type=F32))


def _mirror_chunk(c, n_ctx_chunks, n_chunks):
    return jnp.where(c < n_ctx_chunks, n_ctx_chunks - 1 - c, n_ctx_chunks + n_chunks - 1 - c)


def _reverse_backward(rows, flip, n_fwd):
    hi, mid, lo = _split3(rows[n_fwd:])
    back = (jnp.dot(hi, flip, preferred_element_type=F32) + jnp.dot(mid, flip, preferred_element_type=F32)
            + jnp.dot(lo, flip, preferred_element_type=F32))
    return jnp.concatenate([rows[:n_fwd], back], axis=0)


def _to_scan_kernel(zf_ref, zb_ref, flip_ref, o_ref, *, step_major):
    nb = zf_ref.shape[0]
    flip = flip_ref[...]
    sub = RELAYOUT_ROWS // SCAN_STEPS

    def body(n, carry):
        rows = pl.ds(pl.multiple_of(n * N_HEADS, N_HEADS), N_HEADS)
        slabs = [zf_ref[b, rows, :] for b in range(nb)] + [zb_ref[b, rows, :] for b in range(nb)]
        tile = _reverse_backward(jnp.concatenate(slabs, axis=0), flip, nb * N_HEADS).T
        for q in range(sub):
            if step_major:
                dst = pl.ds(n, SCAN_STEPS, stride=HEAD_DIM)
            else:
                dst = pl.ds(pl.multiple_of(n * SCAN_STEPS, SCAN_STEPS), SCAN_STEPS)
            o_ref[q, dst, :] = tile[q * SCAN_STEPS:(q + 1) * SCAN_STEPS]
        return carry

    lax.fori_loop(0, HEAD_DIM, body, 0, unroll=RELAYOUT_UNROLL)


def _to_scan(z_fwd, z_bwd, flip, n_ctx, step_major=False):
    B, _, R = z_fwd.shape
    n_chunks = R // RELAYOUT_ROWS
    n_ctx_chunks = n_ctx // RELAYOUT_ROWS
    sub = RELAYOUT_ROWS // SCAN_STEPS
    chains = 2 * B * N_HEADS
    return pl.pallas_call(
        functools.partial(_to_scan_kernel, step_major=step_major),
        grid=(n_chunks,),
        in_specs=[pl.BlockSpec((B, W_BRANCH, RELAYOUT_ROWS), lambda c: (0, 0, c)),
                  pl.BlockSpec((B, W_BRANCH, RELAYOUT_ROWS),
                               lambda c: (0, 0, _mirror_chunk(c, n_ctx_chunks, n_chunks))),
                  pl.BlockSpec((RELAYOUT_ROWS, RELAYOUT_ROWS), lambda c: (0, 0))],
        out_specs=pl.BlockSpec((sub, HEAD_DIM * SCAN_STEPS, chains), lambda c: (c, 0, 0)),
        out_shape=jax.ShapeDtypeStruct((R // SCAN_STEPS, HEAD_DIM * SCAN_STEPS, chains), F32),
        compiler_params=_params("parallel"),
        name="to_scan_layout",
    )(z_fwd, z_bwd, flip)


def _from_scan_kernel(y_ref, flip_ref, yf_ref, yb_ref):
    nb = yf_ref.shape[0]
    flip = flip_ref[...]
    sub = RELAYOUT_ROWS // SCAN_STEPS

    def body(n, carry):
        tile = jnp.concatenate([y_ref[q, pl.ds(n, SCAN_STEPS, stride=HEAD_DIM), :] for q in range(sub)], axis=0)
        tile = _reverse_backward(tile.T, flip, nb * N_HEADS)
        rows = pl.ds(pl.multiple_of(n * N_HEADS, N_HEADS), N_HEADS)
        for b in range(nb):
            yf_ref[b, rows, :] = tile[b * N_HEADS:(b + 1) * N_HEADS]
            yb_ref[b, rows, :] = tile[(nb + b) * N_HEADS:(nb + b + 1) * N_HEADS]
        return carry

    lax.fori_loop(0, HEAD_DIM, body, 0, unroll=RELAYOUT_UNROLL)


def _from_scan(y, flip, n_batch, n_ctx):
    R = y.shape[0] * SCAN_STEPS
    n_chunks = R // RELAYOUT_ROWS
    n_ctx_chunks = n_ctx // RELAYOUT_ROWS
    sub = RELAYOUT_ROWS // SCAN_STEPS
    out = jax.ShapeDtypeStruct((n_batch, W_BRANCH, R), F32)
    return pl.pallas_call(
        _from_scan_kernel,
        grid=(n_chunks,),
        in_specs=[pl.BlockSpec((sub, HEAD_DIM * SCAN_STEPS, y.shape[2]), lambda c: (c, 0, 0)),
                  pl.BlockSpec((RELAYOUT_ROWS, RELAYOUT_ROWS), lambda c: (0, 0))],
        out_specs=[pl.BlockSpec((n_batch, W_BRANCH, RELAYOUT_ROWS), lambda c: (0, 0, c)),
                   pl.BlockSpec((n_batch, W_BRANCH, RELAYOUT_ROWS),
                                lambda c: (0, 0, _mirror_chunk(c, n_ctx_chunks, n_chunks)))],
        out_shape=[out, out],
        compiler_params=_params("parallel"),
        name="from_scan_layout",
    )(y, flip)


def _rwkv_readout_kernel(yf_ref, yb_ref, bonus_ref, gate_ref, gb_ref, o_ref):
    y = (yf_ref[0] + yb_ref[0]).T
    mu = _head_sum(y) * (1.0 / HEAD_DIM)
    yc = y - mu
    var = _head_sum(yc * yc) * (1.0 / HEAD_DIM)
    gb = gb_ref[...]
    out = yc * lax.rsqrt(var + LNX_EPS) * gb[0:1, :] + gb[1:2, :] + bonus_ref[0]
    o_ref[0] = (out * _silu(gate_ref[0])).astype(o_ref.dtype)


def _rwkv_readout(y_fwd, y_bwd, bonus, p_rw, lnx_gb):
    B, R, _ = bonus.shape
    tile = pl.BlockSpec((1, ROW_TILE, W_BRANCH), lambda b, i: (b, i, 0))
    tile_t = pl.BlockSpec((1, W_BRANCH, ROW_TILE), lambda b, i: (b, 0, i))
    return pl.pallas_call(
        _rwkv_readout_kernel,
        grid=(B, R // ROW_TILE),
        in_specs=[tile_t, tile_t, tile,
                  pl.BlockSpec((1, ROW_TILE, W_BRANCH), lambda b, i: (b, i, COL_RW_GATE)),
                  pl.BlockSpec((8, W_BRANCH), lambda b, i: (0, 0))],
        out_specs=tile,
        out_shape=jax.ShapeDtypeStruct((B, R, W_BRANCH), BF16),
        compiler_params=_params("parallel", "parallel"),
        name="rwkv_readout",
    )(y_fwd, y_bwd, bonus, p_rw, lnx_gb)


def _merge_kernel(na_ref, pool_ref, rw_ref, lna_ref, lpool_ref, lrw_ref, w_ref, o_ref):
    acc = None
    for br, (x_ref, l_ref) in enumerate(((na_ref, lna_ref), (pool_ref, lpool_ref), (rw_ref, lrw_ref))):
        t = _sigmoid(l_ref[...]) * jnp.dot(x_ref[...], w_ref[br], preferred_element_type=F32)
        acc = t if acc is None else acc + t
    o_ref[...] = acc.astype(o_ref.dtype)


def _merge(b_na, b_pool, b_rw, p_merge, w_branch):
    M = b_na.shape[0]
    tm, tn = min(_row_tile(M), 512), 1024
    nb = D_MODEL // tn
    x_spec = pl.BlockSpec((tm, W_BRANCH), lambda j, i: (i, 0))

    def logit(br):
        return pl.BlockSpec((tm, tn), lambda j, i: (i, br * nb + j))

    return pl.pallas_call(
        _merge_kernel,
        grid=(nb, M // tm),
        in_specs=[x_spec, x_spec, x_spec, logit(0), logit(1), logit(2),
                  pl.BlockSpec((N_BRANCH, W_BRANCH, tn), lambda j, i: (0, 0, j))],
        out_specs=pl.BlockSpec((tm, tn), lambda j, i: (i, j)),
        out_shape=jax.ShapeDtypeStruct((M, D_MODEL), BF16),
        compiler_params=_params("parallel", "parallel"),
        name="branch_merge",
    )(b_na, b_pool, b_rw, p_merge, p_merge, p_merge, w_branch)


def _out_kernel(m_ref, w_ref, c_ref, l_ref, mod_ref, fg_ref, o_ref, *, n_ctx_tiles, tile_offset, final):
    tile = pl.program_id(1) + tile_offset
    gate = _mod_row(mod_ref, tile, n_ctx_tiles)[:, 2 * D_MODEL:]
    x = _stream_tile(c_ref, l_ref, tile, n_ctx_tiles)
    x = x + gate * jnp.dot(m_ref[0], w_ref[...], preferred_element_type=F32)
    o_ref[0] = _rms(x, fg_ref[...]) if final else x


def _out_proj(merged, w_out, stream, mod, final_g, n_ctx, final):
    B, R, _ = merged.shape
    n_ctx_tiles = n_ctx // ROW_TILE
    off = n_ctx_tiles if final else 0
    return pl.pallas_call(
        functools.partial(_out_kernel, n_ctx_tiles=n_ctx_tiles, tile_offset=off, final=final),
        grid=(B, R // ROW_TILE - off),
        in_specs=[pl.BlockSpec((1, ROW_TILE, D_MODEL), lambda b, i: (b, i + off, 0)),
                  pl.BlockSpec((D_MODEL, D_MODEL), lambda b, i: (0, 0))]
        + _stream_specs(stream, n_ctx_tiles, off) + [
            pl.BlockSpec((MOD_ROWS, 3 * D_MODEL), lambda b, i: (0, 0)),
            pl.BlockSpec((1, D_MODEL), lambda b, i: (0, 0))],
        out_specs=pl.BlockSpec((1, ROW_TILE, D_MODEL), lambda b, i: (b, i, 0)),
        out_shape=jax.ShapeDtypeStruct((B, R - off * ROW_TILE, D_MODEL), F32),
        compiler_params=_params("parallel", "parallel"),
        name="out_proj_final" if final else "out_proj",
    )(merged, w_out, stream[0], stream[1], mod, final_g)


def _layer(stream, R, mod, n_ctx, final, final_g, norm_g, w_in_all, layer, na_rpb, pool_w, pool_scale, rw_mu, rw_w0, rw_w2,
           rw_a0, rw_a2, rw_k_k, rw_k_a, rw_r_k, rw_lnx_g, rw_lnx_b, w_branch, w_out):
    B = stream[0].shape[0]
    rows = (R - n_ctx) // GRID_W

    h = _norm_mod(stream, R, norm_g[None], mod, n_ctx).reshape(B * R, D_MODEL)
    lo = N_MAIN + 2 * RWKV_LORA
    w_in = w_in_all[layer]
    p_ap = _matmul(h, w_in_all, F32, "in_proj_attn_pool", n_cols=N_ATTN_POOL, layer=layer)
    p_ap = p_ap.reshape(B, R, N_ATTN_POOL)
    w_rw = _head_major(w_in[:, N_ATTN_POOL:N_MAIN].reshape(D_MODEL, -1, W_BRANCH)).reshape(D_MODEL, -1)
    p_rw = _matmul(h, w_rw.astype(BF16), F32, "in_proj_rwkv").reshape(B, R, N_MAIN - N_ATTN_POOL)
    p_lora = _matmul(h, w_in[:, N_MAIN:lo].astype(BF16), F32, "in_proj_lora").reshape(B, R, 2 * RWKV_LORA)
    p_merge = _matmul(h, w_in[:, lo:].astype(BF16), F32, "in_proj_merge")

    b_na = _na_attention(p_ap, _na_bias_tables(na_rpb, rows), n_ctx)
    b_pool = _pool(p_ap, pool_w.astype(BF16), pool_scale[None], n_ctx)

    par = jnp.zeros((P_ROWS, W_BRANCH), F32)
    par = par.at[P_MU_R:P_MU_V + 1].set(rw_mu).at[P_W0_F:P_W0_B + 1].set(rw_w0).at[P_A0_F:P_A0_B + 1].set(rw_a0)
    par = _head_major(par.at[P_K_K].set(rw_k_k).at[P_K_A].set(rw_k_a).at[P_R_K].set(rw_r_k.reshape(-1)))
    zeros = jnp.zeros_like(rw_w2)
    w2 = _head_major(jnp.concatenate([rw_w2, zeros], axis=1)).astype(BF16)
    a2 = _head_major(jnp.concatenate([zeros, rw_a2], axis=1)).astype(BF16)
    v, k_f, b_f, a_f, r_f, k_b, b_b, a_b, r_b, ptot, bonus = _rwkv_features(p_rw, p_lora, par, w2, a2, n_ctx)
    flip = jnp.asarray(np.eye(RELAYOUT_ROWS)[::-1], BF16)
    y = _wkv_scan(_chunk_decay_to_scan(ptot, n_ctx),
                  *[_to_scan(zf, zb, flip, n_ctx) for zf, zb in ((k_f, k_b), (b_f, b_b), (a_f, a_b), (r_f, r_b))],
                  _to_scan(v, v, flip, n_ctx, step_major=True))
    y_fwd, y_bwd = _from_scan(y, flip, B, n_ctx)
    lnx_gb = _head_major(jnp.zeros((8, W_BRANCH), F32).at[0].set(rw_lnx_g).at[1].set(rw_lnx_b))
    b_rw = _rwkv_readout(y_fwd, y_bwd, bonus, p_rw, lnx_gb)

    def flat(z):
        return z.reshape(B * R, W_BRANCH)

    w_rw_out = _head_major(w_branch[2].T).T
    w_br = jnp.stack([w_branch[0], w_branch[1], w_rw_out]).astype(BF16)
    merged = _merge(flat(b_na), flat(b_pool), flat(b_rw), p_merge, w_br)
    return _out_proj(merged.reshape(B, R, D_MODEL), w_out.astype(BF16), stream, mod, final_g[None], n_ctx, final)


def kernel(x, c, ctx, c_ctx, norm_g, w_mod, b_mod, w_in, na_rpb, pool_w, pool_scale, rw_mu, rw_w0, rw_w2, rw_a0,
           rw_a2, rw_k_k, rw_k_a, rw_r_k, rw_lnx_g, rw_lnx_b, w_branch, w_out, final_g):
    B, T, _ = x.shape
    n_ctx = ctx.shape[1]
    depth = w_in.shape[0]
    assert B <= CTX_MOD_ROW and n_ctx % ROW_TILE == 0 and T % ROW_TILE == 0
    assert ROW_TILE % RELAYOUT_ROWS == 0 and RELAYOUT_ROWS % SCAN_STEPS == 0

    cond = jnp.zeros((MOD_ROWS, D_MODEL), F32).at[:B].set(c).at[CTX_MOD_ROW].set(c_ctx)
    mods = _modulation(cond, w_mod, b_mod[:, None, :])
    stream = (ctx, x, n_ctx // ROW_TILE)
    for layer in range(depth):
        out = _layer(stream, n_ctx + T, mods[layer], n_ctx, layer == depth - 1, final_g, norm_g[layer], w_in, layer,
                     na_rpb[layer], pool_w[layer], pool_scale[layer], rw_mu[layer], rw_w0[layer], rw_w2[layer],
                     rw_a0[layer], rw_a2[layer], rw_k_k[layer], rw_k_a[layer], rw_r_k[layer], rw_lnx_g[layer],
                     rw_lnx_b[layer], w_branch[layer], w_out[layer])
        stream = (out, out, 0)
    return out
```

```python
import functools

import numpy as np
import jax
import jax.numpy as jnp
from jax import lax
from jax.experimental import pallas as pl
from jax.experimental.pallas import tpu as pltpu

F32 = jnp.float32
BF16 = jnp.bfloat16

D_MODEL = 2048
W_BRANCH = D_MODEL // 2
N_BRANCH = 3
N_HEADS = 16
HEAD_DIM = 64
GRID_W = 64
NA_WIN_H = 8
NA_WIN_W = 16
POOL_WINDOWS = (2, 4, 8, 16)
POOL_GROUP_DIM = W_BRANCH // len(POOL_WINDOWS)
POOL_HALO = max(POOL_WINDOWS) // 2
RWKV_LORA = 64
RMS_EPS = 1e-6
LNX_EPS = 64e-5
NEG_INF = -1e30

LANES = 128
ROW_TILE = 256
NA_Q_ROWS = ROW_TILE // GRID_W
NA_K_ROWS = NA_Q_ROWS + NA_WIN_H
NA_K_TOK = NA_K_ROWS * GRID_W
CTX_MOD_ROW = 4
MOD_ROWS = 8
SCAN_STEPS = 64
SCAN_J_UNROLL = 64
RELAYOUT_UNROLL = 32
RELAYOUT_ROWS = 128
VMEM_LIMIT = 56 << 20

COL_Q, COL_K, COL_V, COL_NA_GATE, COL_POOL_U, COL_POOL_GATE = range(6)
COL_RW_R, COL_RW_K, COL_RW_V, COL_RW_GATE = range(4)
N_ATTN_POOL = 6 * W_BRANCH
N_MAIN = 10 * W_BRANCH

P_MU_R, P_MU_K, P_MU_V, P_W0_F, P_W0_B, P_A0_F, P_A0_B, P_K_K, P_K_A, P_R_K = range(10)
P_ROWS = 16


def _params(*sem):
    return pltpu.CompilerParams(dimension_semantics=sem, vmem_limit_bytes=VMEM_LIMIT)


def _sigmoid(x):
    return 1.0 / (1.0 + jnp.exp(-x))


def _silu(x):
    return x * _sigmoid(x)


def _split3(x):
    hi = x.astype(BF16)
    r1 = x - hi.astype(F32)
    mid = r1.astype(BF16)
    lo = (r1 - mid.astype(F32)).astype(BF16)
    return hi, mid, lo


def _head_major(z):
    lead = z.shape[:-1]
    return z.reshape(lead + (N_HEADS, HEAD_DIM)).swapaxes(-1, -2).reshape(lead + (W_BRANCH,))


def _head_sum(x):
    n_tiles = W_BRANCH // LANES
    part = x[:, :LANES]
    for c in range(1, n_tiles):
        part = part + x[:, c * LANES:(c + 1) * LANES]
    shift = N_HEADS
    while shift < LANES:
        part = part + pltpu.roll(part, shift, 1)
        shift *= 2
    return jnp.concatenate([part] * n_tiles, axis=1)


def _mod_kernel(cond_ref, w_ref, b_ref, o_ref):
    s = _silu(cond_ref[...])
    o_ref[0] = jnp.dot(s.astype(BF16), w_ref[0].astype(BF16), preferred_element_type=F32) + b_ref[0]


def _modulation(cond, w_mod, b_mod):
    n_layers = w_mod.shape[0]
    tn = 3 * D_MODEL // 4
    return pl.pallas_call(
        _mod_kernel,
        grid=(n_layers, 4),
        in_specs=[pl.BlockSpec((MOD_ROWS, D_MODEL), lambda l, j: (0, 0)),
                  pl.BlockSpec((1, D_MODEL, tn), lambda l, j: (l, 0, j)),
                  pl.BlockSpec((1, 1, tn), lambda l, j: (l, 0, j))],
        out_specs=pl.BlockSpec((1, MOD_ROWS, tn), lambda l, j: (l, 0, j)),
        out_shape=jax.ShapeDtypeStruct((n_layers, MOD_ROWS, 3 * D_MODEL), F32),
        compiler_params=_params("arbitrary", "arbitrary"),
        name="adaln_modulation",
    )(cond, w_mod, b_mod)


def _mod_row(mod_ref, tile, n_ctx_tiles):
    row = jnp.where(tile < n_ctx_tiles, CTX_MOD_ROW, pl.program_id(0))
    return mod_ref[pl.ds(row, 1), :]


def _rms(x, g):
    return x * lax.rsqrt(jnp.mean(x * x, axis=-1, keepdims=True) + RMS_EPS) * g


def _stream_specs(stream, n_ctx_tiles, tile_offset=0):
    lat_shift = stream[2]
    return [pl.BlockSpec((1, ROW_TILE, D_MODEL), lambda b, i: (b, jnp.minimum(i + tile_offset, n_ctx_tiles - 1), 0)),
            pl.BlockSpec((1, ROW_TILE, D_MODEL), lambda b, i: (b, jnp.maximum(i + tile_offset - lat_shift, 0), 0))]


def _stream_tile(c_ref, l_ref, tile, n_ctx_tiles):
    return jnp.where(tile < n_ctx_tiles, c_ref[0], l_ref[0])


def _norm_mod_kernel(c_ref, l_ref, g_ref, mod_ref, h_ref, *, n_ctx_tiles):
    tile = pl.program_id(1)
    m = _mod_row(mod_ref, tile, n_ctx_tiles)
    shift = m[:, :D_MODEL]
    scale = m[:, D_MODEL:2 * D_MODEL]
    x = _stream_tile(c_ref, l_ref, tile, n_ctx_tiles)
    h_ref[0] = (_rms(x, g_ref[...]) * (1.0 + scale) + shift).astype(BF16)


def _norm_mod(stream, n_rows, norm_g, mod, n_ctx):
    B = stream[0].shape[0]
    n_ctx_tiles = n_ctx // ROW_TILE
    return pl.pallas_call(
        functools.partial(_norm_mod_kernel, n_ctx_tiles=n_ctx_tiles),
        grid=(B, n_rows // ROW_TILE),
        in_specs=_stream_specs(stream, n_ctx_tiles) + [
            pl.BlockSpec((1, D_MODEL), lambda b, i: (0, 0)),
            pl.BlockSpec((MOD_ROWS, 3 * D_MODEL), lambda b, i: (0, 0))],
        out_specs=pl.BlockSpec((1, ROW_TILE, D_MODEL), lambda b, i: (b, i, 0)),
        out_shape=jax.ShapeDtypeStruct((B, n_rows, D_MODEL), BF16),
        compiler_params=_params("parallel", "parallel"),
        name="norm_modulate",
    )(stream[0], stream[1], norm_g, mod)


def _mm_kernel(a_ref, w_ref, o_ref):
    o_ref[...] = jnp.dot(a_ref[...], w_ref[...], preferred_element_type=F32).astype(o_ref.dtype)


def _row_tile(m):
    for t in (1024, 512, 256):
        if m % t == 0:
            return t
    raise ValueError(f"row count {m} is not a multiple of {ROW_TILE}")


def _mm_cast_kernel(a_ref, w_ref, *rest, permute):
    perm_ref, o_ref, wb_ref = rest if permute else (None,) + rest

    @pl.when(pl.program_id(1) == 0)
    def _():
        wb = w_ref[...].astype(BF16)
        if permute:
            wb = jnp.dot(wb, perm_ref[...], preferred_element_type=F32).astype(BF16)
        wb_ref[...] = wb

    o_ref[...] = jnp.dot(a_ref[...], wb_ref[...], preferred_element_type=F32).astype(o_ref.dtype)


def _matmul(a, w, out_dtype, name, n_cols=None, layer=None, first_col=0, perm=None):
    M, K = a.shape
    N = w.shape[-1] if n_cols is None else n_cols
    tm = _row_tile(M)
    tn = min(N, 1024)
    assert first_col % tn == 0
    j0 = first_col // tn
    cast = w.dtype != BF16
    assert cast or perm is None
    if layer is None:
        w_spec = pl.BlockSpec((K, tn), lambda j, i: (0, j0 + j))
    else:
        w_spec = pl.BlockSpec((None, K, tn), lambda j, i: (layer, 0, j0 + j))
    in_specs = [pl.BlockSpec((tm, K), lambda j, i: (i, 0)), w_spec]
    args = [a, w]
    if perm is not None:
        in_specs.append(pl.BlockSpec((tn, tn), lambda j, i: (0, 0)))
        args.append(perm)
    return pl.pallas_call(
        functools.partial(_mm_cast_kernel, permute=perm is not None) if cast else _mm_kernel,
        grid=(N // tn, M // tm),
        in_specs=in_specs,
        out_specs=pl.BlockSpec((tm, tn), lambda j, i: (i, j)),
        out_shape=jax.ShapeDtypeStruct((M, N), out_dtype),
        scratch_shapes=[pltpu.VMEM((K, tn), BF16)] if cast else [],
        compiler_params=_params("parallel", "arbitrary" if cast else "parallel"),
        name=name,
    )(*args)


def _na_bias_tables(rpb, rows):
    n_blocks = rows // NA_Q_ROWS
    n_off = 2 * NA_WIN_H - 1
    col = np.arange(GRID_W)
    c0 = np.clip(col - NA_WIN_W // 2, 0, GRID_W - NA_WIN_W)
    valid_c = (col[None, :] >= c0[:, None]) & (col[None, :] < c0[:, None] + NA_WIN_W)
    col_off = np.clip(col[None, :] - col[:, None] + NA_WIN_W - 1, 0, 2 * NA_WIN_W - 2)
    pick_c = jnp.asarray(np.eye(2 * NA_WIN_W - 1)[col_off], F32)
    tile = jnp.einsum("hrc,qpc->hrqp", rpb.astype(F32), pick_c, precision=lax.Precision.HIGHEST)
    tile = jnp.where(valid_c, tile, NEG_INF)
    tile = jnp.concatenate([tile, jnp.full((N_HEADS, 1, GRID_W, GRID_W), NEG_INF, F32)], axis=1)
    tile = jnp.concatenate([tile, tile], axis=-1)
    picks = []
    for m in (0, 1, n_blocks - 1):
        q_row = NA_Q_ROWS * m + np.arange(NA_Q_ROWS)
        k_row = int(np.clip(NA_Q_ROWS * m - NA_Q_ROWS, 0, rows - NA_K_ROWS)) + np.arange(NA_K_ROWS)
        r0 = np.clip(q_row - NA_WIN_H // 2, 0, rows - NA_WIN_H)
        valid_r = (k_row[None, :] >= r0[:, None]) & (k_row[None, :] < r0[:, None] + NA_WIN_H)
        row_off = np.clip(k_row[None, :] - q_row[:, None] + NA_WIN_H - 1, 0, n_off - 1)
        picks.append(np.where(valid_r, row_off, n_off))
    picks = np.stack(picks)

    def build(t_ref, o_ref):
        left = lax.broadcasted_iota(jnp.int32, (1, LANES), 1) < GRID_W
        for ty in range(picks.shape[0]):
            for a in range(NA_Q_ROWS):
                for kp in range(NA_K_ROWS // 2):
                    pair = jnp.where(left, t_ref[0, int(picks[ty, a, 2 * kp])], t_ref[0, int(picks[ty, a, 2 * kp + 1])])
                    o_ref[ty, 0, a * GRID_W:(a + 1) * GRID_W, kp * LANES:(kp + 1) * LANES] = pair

    return pl.pallas_call(
        build,
        grid=(N_HEADS,),
        in_specs=[pl.BlockSpec((1, n_off + 1, GRID_W, LANES), lambda h: (h, 0, 0, 0))],
        out_specs=pl.BlockSpec((picks.shape[0], 1, ROW_TILE, NA_K_TOK), lambda h: (0, h, 0, 0)),
        out_shape=jax.ShapeDtypeStruct((picks.shape[0], N_HEADS, ROW_TILE, NA_K_TOK), F32),
        compiler_params=_params("parallel"),
        name="na_bias_tables",
    )(tile)


def _attend(qe, keys, vals, biases):
    dn = (((1,), (1,)), ((), ()))
    scores = []
    for kk, bias in zip(keys, biases):
        s = lax.dot_general(qe, kk, dn, preferred_element_type=F32)
        scores.append(s if bias is None else s + bias)
    m = scores[0].max(axis=-1, keepdims=True)
    for s in scores[1:]:
        m = jnp.maximum(m, s.max(axis=-1, keepdims=True))
    num, den = None, None
    for s, vv in zip(scores, vals):
        p = jnp.exp(s - m)
        l = p.sum(axis=-1, keepdims=True)
        o = jnp.dot(p.astype(BF16), vv, preferred_element_type=F32)
        num = o if num is None else num + o
        den = l if den is None else den + l
    return num / den


def _na_kernel(q_ref, k_ref, v_ref, g_ref, bias_ref, o_ref, *, n_ctx, rows):
    j = pl.program_id(2)
    lane = lax.broadcasted_iota(jnp.int32, (1, LANES), 1)
    in_head = (lane < HEAD_DIM, lane >= HEAD_DIM)
    q = q_ref[0] * (HEAD_DIM ** -0.5)
    kc = k_ref[0, 0:n_ctx, :].astype(BF16)
    vc = v_ref[0, 0:n_ctx, :].astype(BF16)

    def heads(q):
        return [jnp.where(in_head[e], q, 0.0).astype(BF16) for e in range(2)]

    def finish(o0, o1):
        o = jnp.where(in_head[0], o0, o1)
        o_ref[0] = (o * _silu(g_ref[0])).astype(o_ref.dtype)

    @pl.when(j == 0)
    def _():
        finish(*[_attend(qe, [kc], [vc], [None]) for qe in heads(q)])

    @pl.when(j > 0)
    def _():
        k_row = jnp.clip(NA_Q_ROWS * (j - 1) - NA_Q_ROWS, 0, rows - NA_K_ROWS)
        start = pl.multiple_of(n_ctx + k_row * GRID_W, GRID_W)
        kw = k_ref[0, pl.ds(start, NA_K_TOK), :].astype(BF16)
        vw = v_ref[0, pl.ds(start, NA_K_TOK), :].astype(BF16)
        finish(*[_attend(qe, [kw, kc], [vw, vc], [bias_ref[0, e], None])
                 for e, qe in enumerate(heads(q))])


def _na_attention(p_ap, bias_tables, n_ctx):
    B, R, _ = p_ap.shape
    rows = (R - n_ctx) // GRID_W
    n_blocks = rows // NA_Q_ROWS
    pairs = W_BRANCH // LANES
    assert n_ctx == ROW_TILE and rows >= NA_K_ROWS and rows % NA_Q_ROWS == 0

    def col(c):
        return lambda b, hp, j: (b, 0, c * pairs + hp)

    def bias_idx(b, hp, j):
        return (jnp.where(j <= 1, 0, jnp.where(j == n_blocks, 2, 1)), hp, 0, 0)

    return pl.pallas_call(
        functools.partial(_na_kernel, n_ctx=n_ctx, rows=rows),
        grid=(B, pairs, n_blocks + 1),
        in_specs=[pl.BlockSpec((1, ROW_TILE, LANES), lambda b, hp, j: (b, j, COL_Q * pairs + hp)),
                  pl.BlockSpec((1, R, LANES), col(COL_K)),
                  pl.BlockSpec((1, R, LANES), col(COL_V)),
                  pl.BlockSpec((1, ROW_TILE, LANES), lambda b, hp, j: (b, j, COL_NA_GATE * pairs + hp)),
                  pl.BlockSpec((1, 2, ROW_TILE, NA_K_TOK), bias_idx)],
        out_specs=pl.BlockSpec((1, ROW_TILE, LANES), lambda b, hp, j: (b, j, hp)),
        out_shape=jax.ShapeDtypeStruct((B, R, W_BRANCH), BF16),
        compiler_params=_params("parallel", "parallel", "arbitrary"),
        name="neighbourhood_attention",
    )(p_ap, p_ap, p_ap, p_ap, bias_tables)


def _pool_kernel(u_ref, g_ref, w_ref, sc_ref, o_ref, pad_ref, *, n_ctx, n_lat):
    grp = pl.program_id(1)
    w = w_ref[0]
    scale = sc_ref[...]

    def run(win):
        half = win // 2
        for seq_start, seq_len in ((0, n_ctx), (n_ctx, n_lat)):
            zeros = jnp.zeros((POOL_HALO, POOL_GROUP_DIM), F32)
            pad_ref[0:POOL_HALO, :] = zeros
            pad_ref[POOL_HALO:POOL_HALO + seq_len, :] = u_ref[0, seq_start:seq_start + seq_len, :]
            pad_ref[POOL_HALO + seq_len:2 * POOL_HALO + seq_len, :] = zeros

            def chunk(c, carry):
                base = pl.multiple_of(c * ROW_TILE, ROW_TILE)
                x = pad_ref[pl.ds(base, ROW_TILE + 2 * POOL_HALO), :]
                acc = x[POOL_HALO - half:POOL_HALO - half + ROW_TILE]
                for o in range(-half + 1, half):
                    acc = acc + x[POOL_HALO + o:POOL_HALO + o + ROW_TILE]
                t = base + lax.broadcasted_iota(jnp.int32, (ROW_TILE, 1), 0)
                cnt = jnp.minimum(t + half, seq_len) - jnp.maximum(t - half, 0)
                diff = acc / cnt.astype(F32) - x[POOL_HALO:POOL_HALO + ROW_TILE]
                y = jnp.dot(diff.astype(BF16), w, preferred_element_type=F32) * scale
                rows = pl.ds(seq_start + base, ROW_TILE)
                o_ref[0, rows, :] = (y * _silu(g_ref[0, rows, :])).astype(o_ref.dtype)
                return carry

            lax.fori_loop(0, seq_len // ROW_TILE, chunk, 0)

    for gi, win in enumerate(POOL_WINDOWS):
        pl.when(grp == gi)(functools.partial(run, win))


def _pool(p_ap, pool_w, pool_scale, n_ctx):
    B, R, _ = p_ap.shape
    groups = len(POOL_WINDOWS)
    return pl.pallas_call(
        functools.partial(_pool_kernel, n_ctx=n_ctx, n_lat=R - n_ctx),
        grid=(B, groups),
        in_specs=[pl.BlockSpec((1, R, POOL_GROUP_DIM), lambda b, g: (b, 0, COL_POOL_U * groups + g)),
                  pl.BlockSpec((1, R, POOL_GROUP_DIM), lambda b, g: (b, 0, COL_POOL_GATE * groups + g)),
                  pl.BlockSpec((1, POOL_GROUP_DIM, POOL_GROUP_DIM), lambda b, g: (g, 0, 0)),
                  pl.BlockSpec((1, POOL_GROUP_DIM), lambda b, g: (0, g))],
        out_specs=pl.BlockSpec((1, R, POOL_GROUP_DIM), lambda b, g: (b, 0, g)),
        out_shape=jax.ShapeDtypeStruct((B, R, W_BRANCH), BF16),
        scratch_shapes=[pltpu.VMEM((R - n_ctx + 2 * POOL_HALO, POOL_GROUP_DIM), F32)],
        compiler_params=_params("parallel", "arbitrary"),
        name="multiscale_pool",
    )(p_ap, p_ap, pool_w, pool_scale)


def _rwkv_feat_kernel(r_ref, rp_ref, rn_ref, k_ref, kp_ref, kn_ref, v_ref, vp_ref, vn_ref, lora_ref,
                      par_ref, w2_ref, a2_ref, tri_ref,
                      vo_ref, kf_ref, bf_ref, af_ref, rf_ref, kb_ref, bb_ref, ab_ref, rb_ref, ptot_ref, bonus_ref,
                      *, n_ctx_tiles, n_tiles):
    i = pl.program_id(1)
    first = (i == 0) | (i == n_ctx_tiles)
    last = (i == n_ctx_tiles - 1) | (i == n_tiles - 1)
    row = lax.broadcasted_iota(jnp.int32, (ROW_TILE, 1), 0)
    par = par_ref[...]

    def prm(p):
        return par[p:p + 1, :]

    def mix(z_ref, prev_ref, next_ref, mu):
        z = z_ref[0]
        prev = jnp.where(first, 0.0, prev_ref[0, 7:8, :])
        nxt = jnp.where(last, 0.0, next_ref[0, 0:1, :])
        z_prev = jnp.where(row == 0, prev, pltpu.roll(z, 1, 0))
        z_next = jnp.where(row == ROW_TILE - 1, nxt, pltpu.roll(z, ROW_TILE - 1, 0))
        return z + mu * (0.5 * (z_prev + z_next) - z)

    r = mix(r_ref, rp_ref, rn_ref, prm(P_MU_R))
    k = mix(k_ref, kp_ref, kn_ref, prm(P_MU_K))
    v = mix(v_ref, vp_ref, vn_ref, prm(P_MU_V))
    vo_ref[0] = v.T

    kk = k * prm(P_K_K)
    kk = kk * jnp.minimum(lax.rsqrt(_head_sum(kk * kk)), 1e12)

    lora = lora_ref[0]
    lane = lax.broadcasted_iota(jnp.int32, (1, LANES), 1)
    lora = jnp.where(lane < RWKV_LORA, jnp.tanh(lora), lora).astype(BF16)
    k_sum = None
    outs = ((kf_ref, bf_ref, af_ref, rf_ref), (kb_ref, bb_ref, ab_ref, rb_ref))
    chunk_decay = []
    for d, (k_out, b_out, a_out, r_out) in enumerate(outs):
        x = prm(P_W0_F + d) + jnp.dot(lora, w2_ref[d], preferred_element_type=F32)
        w_log = -(jnp.maximum(-x, 0.0) + jnp.log(1.0 + jnp.exp(-jnp.abs(x)))) - 0.5
        neg_log_w = jnp.exp(w_log)
        hi, mid, lo = _split3(neg_log_w)
        tri = tri_ref[d]
        cs = (jnp.dot(tri, hi, preferred_element_type=F32) + jnp.dot(tri, mid, preferred_element_type=F32)
              + jnp.dot(tri, lo, preferred_element_type=F32))
        grow = jnp.exp(cs)
        shrink = jnp.exp(-cs)
        a = 0.5 + 0.5 * jnp.tanh(0.5 * (prm(P_A0_F + d) + jnp.dot(lora, a2_ref[d], preferred_element_type=F32)))
        k_d = k * (1.0 + (a - 1.0) * prm(P_K_A))
        for q in range(ROW_TILE // SCAN_STEPS):
            last = q * SCAN_STEPS + (SCAN_STEPS - 1 if d == 0 else 0)
            chunk_decay.append(shrink[last:last + 1])
        k_out[0] = (k_d * grow).T
        b_out[0] = (kk * a * grow).T
        a_out[0] = (-kk * jnp.exp(neg_log_w - cs)).T
        r_out[0] = (r * shrink).T
        k_sum = k_d if k_sum is None else k_sum + k_d
    ptot_ref[0, 0] = jnp.concatenate(chunk_decay, axis=0)
    bonus_ref[0] = _head_sum(r * k_sum * prm(P_R_K)) * v


def _rwkv_features(p_rw, p_lora, par, w2, a2, n_ctx):
    B, R, _ = p_rw.shape
    n_tiles = R // ROW_TILE
    sub = ROW_TILE // 8
    chunks_per_tile = ROW_TILE // SCAN_STEPS
    assert 2 * chunks_per_tile == 8

    def main(c):
        return pl.BlockSpec((1, ROW_TILE, W_BRANCH), lambda b, i: (b, i, c))

    def prev(c):
        return pl.BlockSpec((1, 8, W_BRANCH), lambda b, i: (b, jnp.maximum(i * sub - 1, 0), c))

    def nxt(c):
        return pl.BlockSpec((1, 8, W_BRANCH), lambda b, i: (b, jnp.minimum((i + 1) * sub, n_tiles * sub - 1), c))

    in_specs = []
    for c in (COL_RW_R, COL_RW_K, COL_RW_V):
        in_specs += [main(c), prev(c), nxt(c)]
    in_specs += [pl.BlockSpec((1, ROW_TILE, LANES), lambda b, i: (b, i, 0)),
                 pl.BlockSpec((P_ROWS, W_BRANCH), lambda b, i: (0, 0)),
                 pl.BlockSpec((2, LANES, W_BRANCH), lambda b, i: (0, 0, 0)),
                 pl.BlockSpec((2, LANES, W_BRANCH), lambda b, i: (0, 0, 0)),
                 pl.BlockSpec((2, ROW_TILE, ROW_TILE), lambda b, i: (0, 0, 0))]
    t_idx = np.arange(ROW_TILE)
    same_chunk = t_idx[:, None] // SCAN_STEPS == t_idx[None, :] // SCAN_STEPS
    tri = jnp.asarray(np.stack([same_chunk & (t_idx[None, :] <= t_idx[:, None]),
                                same_chunk & (t_idx[None, :] >= t_idx[:, None])]), BF16)
    out = jax.ShapeDtypeStruct((B, R, W_BRANCH), F32)
    return pl.pallas_call(
        functools.partial(_rwkv_feat_kernel, n_ctx_tiles=n_ctx // ROW_TILE, n_tiles=n_tiles),
        grid=(B, n_tiles),
        in_specs=in_specs,
        out_specs=[pl.BlockSpec((1, W_BRANCH, ROW_TILE), lambda b, i: (b, 0, i))] * 9
        + [pl.BlockSpec((1, 1, 2 * chunks_per_tile, W_BRANCH), lambda b, i: (b, i, 0, 0)),
           pl.BlockSpec((1, ROW_TILE, W_BRANCH), lambda b, i: (b, i, 0))],
        out_shape=[jax.ShapeDtypeStruct((B, W_BRANCH, R), F32)] * 9 + [jax.ShapeDtypeStruct((B, n_tiles, 2 * chunks_per_tile, W_BRANCH), F32), out],
        compiler_params=_params("parallel", "parallel"),
        name="rwkv_features",
    )(*([p_rw] * 9), p_lora, par, w2, a2, tri)


def _scan_kernel(p_ref, k_ref, b_ref, a_ref, r_ref, v_ref, y_ref, s_ref):
    n = HEAD_DIM

    @pl.when(pl.program_id(0) == 0)
    def _():
        s_ref[...] = jnp.zeros_like(s_ref)

    def row(ref, j, t):
        return ref[0, pl.ds(j * SCAN_STEPS + t, 1), :]

    zero = jnp.zeros((n, s_ref.shape[2]), F32)

    def first_sa(jb, sa):
        for jj in range(SCAN_J_UNROLL):
            j = jb * SCAN_J_UNROLL + jj
            sa = sa + s_ref[j] * row(a_ref, j, 0)
        return sa

    def step(t, sa):
        tile_rows = pl.ds(pl.multiple_of(t * n, n), n)
        vt = v_ref[0, tile_rows, :]
        t_next = jnp.minimum(t + 1, SCAN_STEPS - 1)

        def columns(jb, carry):
            y, sa_next = carry
            for jj in range(SCAN_J_UNROLL):
                j = jb * SCAN_J_UNROLL + jj
                sj = s_ref[j] + sa * row(b_ref, j, t) + vt * row(k_ref, j, t)
                s_ref[j] = sj
                y = y + sj * row(r_ref, j, t)
                sa_next = sa_next + sj * row(a_ref, j, t_next)
            return y, sa_next

        y, sa_next = lax.fori_loop(0, n // SCAN_J_UNROLL, columns, (zero, zero))
        y_ref[0, tile_rows, :] = y
        return sa_next

    def rescale(jb, carry):
        for jj in range(SCAN_J_UNROLL):
            j = jb * SCAN_J_UNROLL + jj
            s_ref[j] = s_ref[j] * p_ref[0, pl.ds(j, 1), :]
        return carry

    sa0 = lax.fori_loop(0, n // SCAN_J_UNROLL, first_sa, zero)
    lax.fori_loop(0, SCAN_STEPS, step, sa0)
    lax.fori_loop(0, n // SCAN_J_UNROLL, rescale, 0)


def _chunk_decay_to_scan(ptot, n_ctx):
    B, n_tiles = ptot.shape[:2]
    pt = ptot.reshape(B, n_tiles, 2, ROW_TILE // SCAN_STEPS, HEAD_DIM, N_HEADS)
    n_ctx_chunks = n_ctx // SCAN_STEPS

    def chains(z):
        return z.reshape(B, -1, HEAD_DIM, N_HEADS).transpose(1, 2, 0, 3).reshape(-1, HEAD_DIM, B * N_HEADS)

    fwd, bwd = chains(pt[:, :, 0]), chains(pt[:, :, 1])
    bwd = jnp.concatenate([bwd[:n_ctx_chunks][::-1], bwd[n_ctx_chunks:][::-1]], axis=0)
    return jnp.concatenate([fwd, bwd], axis=-1)


def _wkv_scan(p, k, b, a, r, v):
    n_chunks, rows, chains = k.shape
    spec = pl.BlockSpec((1, rows, chains), lambda s: (s, 0, 0))
    return pl.pallas_call(
        _scan_kernel,
        grid=(n_chunks,),
        in_specs=[pl.BlockSpec((1, HEAD_DIM, chains), lambda s: (s, 0, 0))] + [spec] * 5,
        out_specs=spec,
        out_shape=jax.ShapeDtypeStruct(k.shape, F32),
        scratch_shapes=[pltpu.VMEM((HEAD_DIM, HEAD_DIM, chains), F32)],
        compiler_params=_params("arbitrary"),
        name="wkv_scan",
    )(p, k, b, a, r, v)


def _flip_rows(x, flip):
    hi, mid, lo = _split3(x)
    return (jnp.dot(flip, hi, preferred_element_type=F32) + jnp.dot(flip, mid, preferred_element_type=F32)
            + jnp.dot(flip, lo, preferred_element_type=F32))


def _mirror_chunk(c, n_ctx_chunks, n_chunks):
    return jnp.where(c < n_ctx_chunks, n_ctx_chunks - 1 - c, n_ctx_chunks + n_chunks - 1 - c)


def _reverse_backward(rows, flip, n_fwd):
    hi, mid, lo = _split3(rows[n_fwd:])
    back = (jnp.dot(hi, flip, preferred_element_type=F32) + jnp.dot(mid, flip, preferred_element_type=F32)
            + jnp.dot(lo, flip, preferred_element_type=F32))
    return jnp.concatenate([rows[:n_fwd], back], axis=0)


def _to_scan_kernel(zf_ref, zb_ref, flip_ref, o_ref, *, step_major):
    nb = zf_ref.shape[0]
    flip = flip_ref[...]
    sub = RELAYOUT_ROWS // SCAN_STEPS

    def body(n, carry):
        rows = pl.ds(pl.multiple_of(n * N_HEADS, N_HEADS), N_HEADS)
        slabs = [zf_ref[b, rows, :] for b in range(nb)] + [zb_ref[b, rows, :] for b in range(nb)]
        tile = _reverse_backward(jnp.concatenate(slabs, axis=0), flip, nb * N_HEADS).T
        for q in range(sub):
            if step_major:
                dst = pl.ds(n, SCAN_STEPS, stride=HEAD_DIM)
            else:
                dst = pl.ds(pl.multiple_of(n * SCAN_STEPS, SCAN_STEPS), SCAN_STEPS)
            o_ref[q, dst, :] = tile[q * SCAN_STEPS:(q + 1) * SCAN_STEPS]
        return carry

    lax.fori_loop(0, HEAD_DIM, body, 0, unroll=RELAYOUT_UNROLL)


def _to_scan(z_fwd, z_bwd, flip, n_ctx, step_major=False):
    B, _, R = z_fwd.shape
    n_chunks = R // RELAYOUT_ROWS
    n_ctx_chunks = n_ctx // RELAYOUT_ROWS
    sub = RELAYOUT_ROWS // SCAN_STEPS
    chains = 2 * B * N_HEADS
    return pl.pallas_call(
        functools.partial(_to_scan_kernel, step_major=step_major),
        grid=(n_chunks,),
        in_specs=[pl.BlockSpec((B, W_BRANCH, RELAYOUT_ROWS), lambda c: (0, 0, c)),
                  pl.BlockSpec((B, W_BRANCH, RELAYOUT_ROWS),
                               lambda c: (0, 0, _mirror_chunk(c, n_ctx_chunks, n_chunks))),
                  pl.BlockSpec((RELAYOUT_ROWS, RELAYOUT_ROWS), lambda c: (0, 0))],
        out_specs=pl.BlockSpec((sub, HEAD_DIM * SCAN_STEPS, chains), lambda c: (c, 0, 0)),
        out_shape=jax.ShapeDtypeStruct((R // SCAN_STEPS, HEAD_DIM * SCAN_STEPS, chains), F32),
        compiler_params=_params("parallel"),
        name="to_scan_layout",
    )(z_fwd, z_bwd, flip)


def _from_scan_kernel(y_ref, flip_ref, yf_ref, yb_ref):
    nb = yf_ref.shape[0]
    flip = flip_ref[...]
    sub = RELAYOUT_ROWS // SCAN_STEPS

    def body(n, carry):
        tile = jnp.concatenate([y_ref[q, pl.ds(n, SCAN_STEPS, stride=HEAD_DIM), :] for q in range(sub)], axis=0)
        tile = _reverse_backward(tile.T, flip, nb * N_HEADS)
        rows = pl.ds(pl.multiple_of(n * N_HEADS, N_HEADS), N_HEADS)
        for b in range(nb):
            yf_ref[b, rows, :] = tile[b * N_HEADS:(b + 1) * N_HEADS]
            yb_ref[b, rows, :] = tile[(nb + b) * N_HEADS:(nb + b + 1) * N_HEADS]
        return carry

    lax.fori_loop(0, HEAD_DIM, body, 0, unroll=RELAYOUT_UNROLL)


def _from_scan(y, flip, n_batch, n_ctx):
    R = y.shape[0] * SCAN_STEPS
    n_chunks = R // RELAYOUT_ROWS
    n_ctx_chunks = n_ctx // RELAYOUT_ROWS
    sub = RELAYOUT_ROWS // SCAN_STEPS
    out = jax.ShapeDtypeStruct((n_batch, W_BRANCH, R), F32)
    return pl.pallas_call(
        _from_scan_kernel,
        grid=(n_chunks,),
        in_specs=[pl.BlockSpec((sub, HEAD_DIM * SCAN_STEPS, y.shape[2]), lambda c: (c, 0, 0)),
                  pl.BlockSpec((RELAYOUT_ROWS, RELAYOUT_ROWS), lambda c: (0, 0))],
        out_specs=[pl.BlockSpec((n_batch, W_BRANCH, RELAYOUT_ROWS), lambda c: (0, 0, c)),
                   pl.BlockSpec((n_batch, W_BRANCH, RELAYOUT_ROWS),
                                lambda c: (0, 0, _mirror_chunk(c, n_ctx_chunks, n_chunks)))],
        out_shape=[out, out],
        compiler_params=_params("parallel"),
        name="from_scan_layout",
    )(y, flip)


def _rwkv_readout_kernel(yf_ref, yb_ref, bonus_ref, gate_ref, gb_ref, o_ref):
    y = (yf_ref[0] + yb_ref[0]).T
    mu = _head_sum(y) * (1.0 / HEAD_DIM)
    yc = y - mu
    var = _head_sum(yc * yc) * (1.0 / HEAD_DIM)
    gb = gb_ref[...]
    out = yc * lax.rsqrt(var + LNX_EPS) * gb[0:1, :] + gb[1:2, :] + bonus_ref[0]
    o_ref[0] = (out * _silu(gate_ref[0])).astype(o_ref.dtype)


def _rwkv_readout(y_fwd, y_bwd, bonus, p_rw, lnx_gb):
    B, R, _ = bonus.shape
    tile = pl.BlockSpec((1, ROW_TILE, W_BRANCH), lambda b, i: (b, i, 0))
    tile_t = pl.BlockSpec((1, W_BRANCH, ROW_TILE), lambda b, i: (b, 0, i))
    return pl.pallas_call(
        _rwkv_readout_kernel,
        grid=(B, R // ROW_TILE),
        in_specs=[tile_t, tile_t, tile,
                  pl.BlockSpec((1, ROW_TILE, W_BRANCH), lambda b, i: (b, i, COL_RW_GATE)),
                  pl.BlockSpec((8, W_BRANCH), lambda b, i: (0, 0))],
        out_specs=tile,
        out_shape=jax.ShapeDtypeStruct((B, R, W_BRANCH), BF16),
        compiler_params=_params("parallel", "parallel"),
        name="rwkv_readout",
    )(y_fwd, y_bwd, bonus, p_rw, lnx_gb)


def _merge_kernel(na_ref, pool_ref, rw_ref, lna_ref, lpool_ref, lrw_ref, w_ref, o_ref):
    acc = None
    for br, (x_ref, l_ref) in enumerate(((na_ref, lna_ref), (pool_ref, lpool_ref), (rw_ref, lrw_ref))):
        t = _sigmoid(l_ref[...]) * jnp.dot(x_ref[...], w_ref[br], preferred_element_type=F32)
        acc = t if acc is None else acc + t
    o_ref[...] = acc.astype(o_ref.dtype)


def _merge(b_na, b_pool, b_rw, p_merge, w_branch):
    M = b_na.shape[0]
    tm, tn = min(_row_tile(M), 512), 1024
    nb = D_MODEL // tn
    x_spec = pl.BlockSpec((tm, W_BRANCH), lambda j, i: (i, 0))

    def logit(br):
        return pl.BlockSpec((tm, tn), lambda j, i: (i, br * nb + j))

    return pl.pallas_call(
        _merge_kernel,
        grid=(nb, M // tm),
        in_specs=[x_spec, x_spec, x_spec, logit(0), logit(1), logit(2),
                  pl.BlockSpec((N_BRANCH, W_BRANCH, tn), lambda j, i: (0, 0, j))],
        out_specs=pl.BlockSpec((tm, tn), lambda j, i: (i, j)),
        out_shape=jax.ShapeDtypeStruct((M, D_MODEL), BF16),
        compiler_params=_params("parallel", "parallel"),
        name="branch_merge",
    )(b_na, b_pool, b_rw, p_merge, p_merge, p_merge, w_branch)


def _out_kernel(m_ref, w_ref, c_ref, l_ref, mod_ref, fg_ref, o_ref, *, n_ctx_tiles, tile_offset, final):
    tile = pl.program_id(1) + tile_offset
    gate = _mod_row(mod_ref, tile, n_ctx_tiles)[:, 2 * D_MODEL:]
    x = _stream_tile(c_ref, l_ref, tile, n_ctx_tiles)
    x = x + gate * jnp.dot(m_ref[0], w_ref[...], preferred_element_type=F32)
    o_ref[0] = _rms(x, fg_ref[...]) if final else x


def _out_proj(merged, w_out, stream, mod, final_g, n_ctx, final):
    B, R, _ = merged.shape
    n_ctx_tiles = n_ctx // ROW_TILE
    off = n_ctx_tiles if final else 0
    return pl.pallas_call(
        functools.partial(_out_kernel, n_ctx_tiles=n_ctx_tiles, tile_offset=off, final=final),
        grid=(B, R // ROW_TILE - off),
        in_specs=[pl.BlockSpec((1, ROW_TILE, D_MODEL), lambda b, i: (b, i + off, 0)),
                  pl.BlockSpec((D_MODEL, D_MODEL), lambda b, i: (0, 0))]
        + _stream_specs(stream, n_ctx_tiles, off) + [
            pl.BlockSpec((MOD_ROWS, 3 * D_MODEL), lambda b, i: (0, 0)),
            pl.BlockSpec((1, D_MODEL), lambda b, i: (0, 0))],
        out_specs=pl.BlockSpec((1, ROW_TILE, D_MODEL), lambda b, i: (b, i, 0)),
        out_shape=jax.ShapeDtypeStruct((B, R - off * ROW_TILE, D_MODEL), F32),
        compiler_params=_params("parallel", "parallel"),
        name="out_proj_final" if final else "out_proj",
    )(merged, w_out, stream[0], stream[1], mod, final_g)


def _layer(stream, R, mod, n_ctx, final, final_g, norm_g, w_in_all, layer, na_rpb, pool_w, pool_scale, rw_mu, rw_w0, rw_w2,
           rw_a0, rw_a2, rw_k_k, rw_k_a, rw_r_k, rw_lnx_g, rw_lnx_b, w_branch, w_out):
    B = stream[0].shape[0]
    rows = (R - n_ctx) // GRID_W

    h = _norm_mod(stream, R, norm_g[None], mod, n_ctx).reshape(B * R, D_MODEL)
    lo = N_MAIN + 2 * RWKV_LORA
    w_in = w_in_all[layer]
    p_ap = _matmul(h, w_in_all, F32, "in_proj_attn_pool", n_cols=N_ATTN_POOL, layer=layer)
    p_ap = p_ap.reshape(B, R, N_ATTN_POOL)
    head_major = jnp.asarray(_head_major(np.eye(W_BRANCH, dtype=np.float32)), BF16)
    p_rw = _matmul(h, w_in_all, F32, "in_proj_rwkv", n_cols=N_MAIN - N_ATTN_POOL, layer=layer,
                   first_col=N_ATTN_POOL, perm=head_major).reshape(B, R, N_MAIN - N_ATTN_POOL)
    p_lora = _matmul(h, w_in[:, N_MAIN:lo].astype(BF16), F32, "in_proj_lora").reshape(B, R, 2 * RWKV_LORA)
    p_merge = _matmul(h, w_in[:, lo:].astype(BF16), F32, "in_proj_merge")

    b_na = _na_attention(p_ap, _na_bias_tables(na_rpb, rows), n_ctx)
    b_pool = _pool(p_ap, pool_w.astype(BF16), pool_scale[None], n_ctx)

    par = jnp.zeros((P_ROWS, W_BRANCH), F32)
    par = par.at[P_MU_R:P_MU_V + 1].set(rw_mu).at[P_W0_F:P_W0_B + 1].set(rw_w0).at[P_A0_F:P_A0_B + 1].set(rw_a0)
    par = _head_major(par.at[P_K_K].set(rw_k_k).at[P_K_A].set(rw_k_a).at[P_R_K].set(rw_r_k.reshape(-1)))
    zeros = jnp.zeros_like(rw_w2)
    w2 = _head_major(jnp.concatenate([rw_w2, zeros], axis=1)).astype(BF16)
    a2 = _head_major(jnp.concatenate([zeros, rw_a2], axis=1)).astype(BF16)
    v, k_f, b_f, a_f, r_f, k_b, b_b, a_b, r_b, ptot, bonus = _rwkv_features(p_rw, p_lora, par, w2, a2, n_ctx)
    flip = jnp.asarray(np.eye(RELAYOUT_ROWS)[::-1], BF16)
    y = _wkv_scan(_chunk_decay_to_scan(ptot, n_ctx),
                  *[_to_scan(zf, zb, flip, n_ctx) for zf, zb in ((k_f, k_b), (b_f, b_b), (a_f, a_b), (r_f, r_b))],
                  _to_scan(v, v, flip, n_ctx, step_major=True))
    y_fwd, y_bwd = _from_scan(y, flip, B, n_ctx)
    lnx_gb = _head_major(jnp.zeros((8, W_BRANCH), F32).at[0].set(rw_lnx_g).at[1].set(rw_lnx_b))
    b_rw = _rwkv_readout(y_fwd, y_bwd, bonus, p_rw, lnx_gb)

    def flat(z):
        return z.reshape(B * R, W_BRANCH)

    w_rw_out = _head_major(w_branch[2].T).T
    w_br = jnp.stack([w_branch[0], w_branch[1], w_rw_out]).astype(BF16)
    merged = _merge(flat(b_na), flat(b_pool), flat(b_rw), p_merge, w_br)
    return _out_proj(merged.reshape(B, R, D_MODEL), w_out.astype(BF16), stream, mod, final_g[None], n_ctx, final)


def kernel(x, c, ctx, c_ctx, norm_g, w_mod, b_mod, w_in, na_rpb, pool_w, pool_scale, rw_mu, rw_w0, rw_w2, rw_a0,
           rw_a2, rw_k_k, rw_k_a, rw_r_k, rw_lnx_g, rw_lnx_b, w_branch, w_out, final_g):
    B, T, _ = x.shape
    n_ctx = ctx.shape[1]
    depth = w_in.shape[0]
    assert B <= CTX_MOD_ROW and n_ctx % ROW_TILE == 0 and T % ROW_TILE == 0
    assert ROW_TILE % RELAYOUT_ROWS == 0 and RELAYOUT_ROWS % SCAN_STEPS == 0

    cond = jnp.zeros((MOD_ROWS, D_MODEL), F32).at[:B].set(c).at[CTX_MOD_ROW].set(c_ctx)
    mods = _modulation(cond, w_mod, b_mod[:, None, :])
    stream = (ctx, x, n_ctx // ROW_TILE)
    for layer in range(depth):
        out = _layer(stream, n_ctx + T, mods[layer], n_ctx, layer == depth - 1, final_g, norm_g[layer], w_in, layer,
                     na_rpb[layer], pool_w[layer], pool_scale[layer], rw_mu[layer], rw_w0[layer], rw_w2[layer],
                     rw_a0[layer], rw_a2[layer], rw_k_k[layer], rw_k_a[layer], rw_r_k[layer], rw_lnx_g[layer],
                     rw_lnx_b[layer], w_branch[layer], w_out[layer])
        stream = (out, out, 0)
    return out
```

```python
import functools

import numpy as np
import jax
import jax.numpy as jnp
from jax import lax
from jax.experimental import pallas as pl
from jax.experimental.pallas import tpu as pltpu

F32 = jnp.float32
BF16 = jnp.bfloat16

D_MODEL = 2048
W_BRANCH = D_MODEL // 2
N_BRANCH = 3
N_HEADS = 16
HEAD_DIM = 64
GRID_W = 64
NA_WIN_H = 8
NA_WIN_W = 16
POOL_WINDOWS = (2, 4, 8, 16)
POOL_GROUP_DIM = W_BRANCH // len(POOL_WINDOWS)
POOL_HALO = max(POOL_WINDOWS) // 2
RWKV_LORA = 64
RMS_EPS = 1e-6
LNX_EPS = 64e-5
NEG_INF = -1e30

LANES = 128
ROW_TILE = 256
NA_Q_ROWS = ROW_TILE // GRID_W
NA_K_ROWS = NA_Q_ROWS + NA_WIN_H
NA_K_TOK = NA_K_ROWS * GRID_W
CTX_MOD_ROW = 4
MOD_ROWS = 8
SCAN_STEPS = 64
SCAN_J_UNROLL = 64
RELAYOUT_UNROLL = 32
RELAYOUT_ROWS = 128
VMEM_LIMIT = 56 << 20

COL_Q, COL_K, COL_V, COL_NA_GATE, COL_POOL_U, COL_POOL_GATE = range(6)
COL_RW_R, COL_RW_K, COL_RW_V, COL_RW_GATE = range(4)
N_ATTN_POOL = 6 * W_BRANCH
N_MAIN = 10 * W_BRANCH

P_MU_R, P_MU_K, P_MU_V, P_W0_F, P_W0_B, P_A0_F, P_A0_B, P_K_K, P_K_A, P_R_K = range(10)
P_ROWS = 16


def _params(*sem):
    return pltpu.CompilerParams(dimension_semantics=sem, vmem_limit_bytes=VMEM_LIMIT)


def _sigmoid(x):
    return 1.0 / (1.0 + jnp.exp(-x))


def _silu(x):
    return x * _sigmoid(x)


def _split3(x):
    hi = x.astype(BF16)
    r1 = x - hi.astype(F32)
    mid = r1.astype(BF16)
    lo = (r1 - mid.astype(F32)).astype(BF16)
    return hi, mid, lo


def _head_major(z):
    lead = z.shape[:-1]
    return z.reshape(lead + (N_HEADS, HEAD_DIM)).swapaxes(-1, -2).reshape(lead + (W_BRANCH,))


def _head_sum(x):
    n_tiles = W_BRANCH // LANES
    part = x[:, :LANES]
    for c in range(1, n_tiles):
        part = part + x[:, c * LANES:(c + 1) * LANES]
    shift = N_HEADS
    while shift < LANES:
        part = part + pltpu.roll(part, shift, 1)
        shift *= 2
    return jnp.concatenate([part] * n_tiles, axis=1)


def _mod_kernel(cond_ref, w_ref, b_ref, o_ref):
    s = _silu(cond_ref[...])
    o_ref[0] = jnp.dot(s.astype(BF16), w_ref[0].astype(BF16), preferred_element_type=F32) + b_ref[0]


def _modulation(cond, w_mod, b_mod):
    n_layers = w_mod.shape[0]
    tn = 3 * D_MODEL // 4
    return pl.pallas_call(
        _mod_kernel,
        grid=(n_layers, 4),
        in_specs=[pl.BlockSpec((MOD_ROWS, D_MODEL), lambda l, j: (0, 0)),
                  pl.BlockSpec((1, D_MODEL, tn), lambda l, j: (l, 0, j)),
                  pl.BlockSpec((1, 1, tn), lambda l, j: (l, 0, j))],
        out_specs=pl.BlockSpec((1, MOD_ROWS, tn), lambda l, j: (l, 0, j)),
        out_shape=jax.ShapeDtypeStruct((n_layers, MOD_ROWS, 3 * D_MODEL), F32),
        compiler_params=_params("arbitrary", "arbitrary"),
        name="adaln_modulation",
    )(cond, w_mod, b_mod)


def _mod_row(mod_ref, tile, n_ctx_tiles):
    row = jnp.where(tile < n_ctx_tiles, CTX_MOD_ROW, pl.program_id(0))
    return mod_ref[pl.ds(row, 1), :]


def _rms(x, g):
    return x * lax.rsqrt(jnp.mean(x * x, axis=-1, keepdims=True) + RMS_EPS) * g


def _stream_specs(stream, n_ctx_tiles, tile_offset=0):
    lat_shift = stream[2]
    return [pl.BlockSpec((1, ROW_TILE, D_MODEL), lambda b, i: (b, jnp.minimum(i + tile_offset, n_ctx_tiles - 1), 0)),
            pl.BlockSpec((1, ROW_TILE, D_MODEL), lambda b, i: (b, jnp.maximum(i + tile_offset - lat_shift, 0), 0))]


def _stream_tile(c_ref, l_ref, tile, n_ctx_tiles):
    return jnp.where(tile < n_ctx_tiles, c_ref[0], l_ref[0])


def _norm_mod_kernel(c_ref, l_ref, g_ref, mod_ref, h_ref, *, n_ctx_tiles):
    tile = pl.program_id(1)
    m = _mod_row(mod_ref, tile, n_ctx_tiles)
    shift = m[:, :D_MODEL]
    scale = m[:, D_MODEL:2 * D_MODEL]
    x = _stream_tile(c_ref, l_ref, tile, n_ctx_tiles)
    h_ref[0] = (_rms(x, g_ref[...]) * (1.0 + scale) + shift).astype(BF16)


def _norm_mod(stream, n_rows, norm_g, mod, n_ctx):
    B = stream[0].shape[0]
    n_ctx_tiles = n_ctx // ROW_TILE
    return pl.pallas_call(
        functools.partial(_norm_mod_kernel, n_ctx_tiles=n_ctx_tiles),
        grid=(B, n_rows // ROW_TILE),
        in_specs=_stream_specs(stream, n_ctx_tiles) + [
            pl.BlockSpec((1, D_MODEL), lambda b, i: (0, 0)),
            pl.BlockSpec((MOD_ROWS, 3 * D_MODEL), lambda b, i: (0, 0))],
        out_specs=pl.BlockSpec((1, ROW_TILE, D_MODEL), lambda b, i: (b, i, 0)),
        out_shape=jax.ShapeDtypeStruct((B, n_rows, D_MODEL), BF16),
        compiler_params=_params("parallel", "parallel"),
        name="norm_modulate",
    )(stream[0], stream[1], norm_g, mod)


def _mm_kernel(a_ref, w_ref, o_ref):
    o_ref[...] = jnp.dot(a_ref[...], w_ref[...], preferred_element_type=F32).astype(o_ref.dtype)


def _row_tile(m):
    for t in (1024, 512, 256):
        if m % t == 0:
            return t
    raise ValueError(f"row count {m} is not a multiple of {ROW_TILE}")


def _mm_cast_kernel(a_ref, w_ref, o_ref, wb_ref):
    @pl.when(pl.program_id(1) == 0)
    def _():
        w = w_ref[...] if len(w_ref.shape) == 2 else w_ref[0]
        wb_ref[...] = w.astype(BF16)

    o_ref[...] = jnp.dot(a_ref[...], wb_ref[...], preferred_element_type=F32).astype(o_ref.dtype)


def _matmul(a, w, out_dtype, name, n_cols=None, layer=None, first_col=0):
    M, K = a.shape
    N = w.shape[-1] if n_cols is None else n_cols
    tm = _row_tile(M)
    tn = min(N, 1024)
    assert first_col % LANES == 0
    cast = w.dtype != BF16
    def col(j):
        return pl.multiple_of(first_col + j * tn, LANES)

    if layer is None:
        w_spec = pl.BlockSpec((pl.Element(K), pl.Element(tn)), lambda j, i: (0, col(j)))
    else:
        w_spec = pl.BlockSpec((pl.Element(1), pl.Element(K), pl.Element(tn)), lambda j, i: (layer, 0, col(j)))
    return pl.pallas_call(
        _mm_cast_kernel if cast else _mm_kernel,
        grid=(N // tn, M // tm),
        in_specs=[pl.BlockSpec((tm, K), lambda j, i: (i, 0)), w_spec],
        out_specs=pl.BlockSpec((tm, tn), lambda j, i: (i, j)),
        out_shape=jax.ShapeDtypeStruct((M, N), out_dtype),
        scratch_shapes=[pltpu.VMEM((K, tn), BF16)] if cast else [],
        compiler_params=_params("parallel", "arbitrary" if cast else "parallel"),
        name=name,
    )(a, w)


def _na_bias_tables(rpb, rows):
    n_blocks = rows // NA_Q_ROWS
    n_off = 2 * NA_WIN_H - 1
    col = np.arange(GRID_W)
    c0 = np.clip(col - NA_WIN_W // 2, 0, GRID_W - NA_WIN_W)
    valid_c = (col[None, :] >= c0[:, None]) & (col[None, :] < c0[:, None] + NA_WIN_W)
    col_off = np.clip(col[None, :] - col[:, None] + NA_WIN_W - 1, 0, 2 * NA_WIN_W - 2)
    pick_c = jnp.asarray(np.eye(2 * NA_WIN_W - 1)[col_off], F32)
    tile = jnp.einsum("hrc,qpc->hrqp", rpb.astype(F32), pick_c, precision=lax.Precision.HIGHEST)
    tile = jnp.where(valid_c, tile, NEG_INF)
    tile = jnp.concatenate([tile, jnp.full((N_HEADS, 1, GRID_W, GRID_W), NEG_INF, F32)], axis=1)
    tile = jnp.concatenate([tile, tile], axis=-1)
    picks = []
    for m in (0, 1, n_blocks - 1):
        q_row = NA_Q_ROWS * m + np.arange(NA_Q_ROWS)
        k_row = int(np.clip(NA_Q_ROWS * m - NA_Q_ROWS, 0, rows - NA_K_ROWS)) + np.arange(NA_K_ROWS)
        r0 = np.clip(q_row - NA_WIN_H // 2, 0, rows - NA_WIN_H)
        valid_r = (k_row[None, :] >= r0[:, None]) & (k_row[None, :] < r0[:, None] + NA_WIN_H)
        row_off = np.clip(k_row[None, :] - q_row[:, None] + NA_WIN_H - 1, 0, n_off - 1)
        picks.append(np.where(valid_r, row_off, n_off))
    picks = np.stack(picks)

    def build(t_ref, o_ref):
        left = lax.broadcasted_iota(jnp.int32, (1, LANES), 1) < GRID_W
        for ty in range(picks.shape[0]):
            for a in range(NA_Q_ROWS):
                for kp in range(NA_K_ROWS // 2):
                    pair = jnp.where(left, t_ref[0, int(picks[ty, a, 2 * kp])], t_ref[0, int(picks[ty, a, 2 * kp + 1])])
                    o_ref[ty, 0, a * GRID_W:(a + 1) * GRID_W, kp * LANES:(kp + 1) * LANES] = pair

    return pl.pallas_call(
        build,
        grid=(N_HEADS,),
        in_specs=[pl.BlockSpec((1, n_off + 1, GRID_W, LANES), lambda h: (h, 0, 0, 0))],
        out_specs=pl.BlockSpec((picks.shape[0], 1, ROW_TILE, NA_K_TOK), lambda h: (0, h, 0, 0)),
        out_shape=jax.ShapeDtypeStruct((picks.shape[0], N_HEADS, ROW_TILE, NA_K_TOK), F32),
        compiler_params=_params("parallel"),
        name="na_bias_tables",
    )(tile)


def _attend(qe, keys, vals, biases):
    dn = (((1,), (1,)), ((), ()))
    scores = []
    for kk, bias in zip(keys, biases):
        s = lax.dot_general(qe, kk, dn, preferred_element_type=F32)
        scores.append(s if bias is None else s + bias)
    m = scores[0].max(axis=-1, keepdims=True)
    for s in scores[1:]:
        m = jnp.maximum(m, s.max(axis=-1, keepdims=True))
    num, den = None, None
    for s, vv in zip(scores, vals):
        p = jnp.exp(s - m)
        l = p.sum(axis=-1, keepdims=True)
        o = jnp.dot(p.astype(BF16), vv, preferred_element_type=F32)
        num = o if num is None else num + o
        den = l if den is None else den + l
    return num / den


def _na_kernel(q_ref, k_ref, v_ref, g_ref, bias_ref, o_ref, *, n_ctx, rows):
    j = pl.program_id(2)
    lane = lax.broadcasted_iota(jnp.int32, (1, LANES), 1)
    in_head = (lane < HEAD_DIM, lane >= HEAD_DIM)
    q = q_ref[0] * (HEAD_DIM ** -0.5)
    kc = k_ref[0, 0:n_ctx, :].astype(BF16)
    vc = v_ref[0, 0:n_ctx, :].astype(BF16)

    def heads(q):
        return [jnp.where(in_head[e], q, 0.0).astype(BF16) for e in range(2)]

    def finish(o0, o1):
        o = jnp.where(in_head[0], o0, o1)
        o_ref[0] = (o * _silu(g_ref[0])).astype(o_ref.dtype)

    @pl.when(j == 0)
    def _():
        finish(*[_attend(qe, [kc], [vc], [None]) for qe in heads(q)])

    @pl.when(j > 0)
    def _():
        k_row = jnp.clip(NA_Q_ROWS * (j - 1) - NA_Q_ROWS, 0, rows - NA_K_ROWS)
        start = pl.multiple_of(n_ctx + k_row * GRID_W, GRID_W)
        kw = k_ref[0, pl.ds(start, NA_K_TOK), :].astype(BF16)
        vw = v_ref[0, pl.ds(start, NA_K_TOK), :].astype(BF16)
        finish(*[_attend(qe, [kw, kc], [vw, vc], [bias_ref[0, e], None])
                 for e, qe in enumerate(heads(q))])


def _na_attention(p_ap, bias_tables, n_ctx):
    B, R, _ = p_ap.shape
    rows = (R - n_ctx) // GRID_W
    n_blocks = rows // NA_Q_ROWS
    pairs = W_BRANCH // LANES
    assert n_ctx == ROW_TILE and rows >= NA_K_ROWS and rows % NA_Q_ROWS == 0

    def col(c):
        return lambda b, hp, j: (b, 0, c * pairs + hp)

    def bias_idx(b, hp, j):
        return (jnp.where(j <= 1, 0, jnp.where(j == n_blocks, 2, 1)), hp, 0, 0)

    return pl.pallas_call(
        functools.partial(_na_kernel, n_ctx=n_ctx, rows=rows),
        grid=(B, pairs, n_blocks + 1),
        in_specs=[pl.BlockSpec((1, ROW_TILE, LANES), lambda b, hp, j: (b, j, COL_Q * pairs + hp)),
                  pl.BlockSpec((1, R, LANES), col(COL_K)),
                  pl.BlockSpec((1, R, LANES), col(COL_V)),
                  pl.BlockSpec((1, ROW_TILE, LANES), lambda b, hp, j: (b, j, COL_NA_GATE * pairs + hp)),
                  pl.BlockSpec((1, 2, ROW_TILE, NA_K_TOK), bias_idx)],
        out_specs=pl.BlockSpec((1, ROW_TILE, LANES), lambda b, hp, j: (b, j, hp)),
        out_shape=jax.ShapeDtypeStruct((B, R, W_BRANCH), BF16),
        compiler_params=_params("parallel", "parallel", "arbitrary"),
        name="neighbourhood_attention",
    )(p_ap, p_ap, p_ap, p_ap, bias_tables)


def _pool_kernel(u_ref, g_ref, w_ref, sc_ref, o_ref, pad_ref, *, n_ctx, n_lat):
    grp = pl.program_id(1)
    w = w_ref[0]
    scale = sc_ref[...]

    def run(win):
        half = win // 2
        for seq_start, seq_len in ((0, n_ctx), (n_ctx, n_lat)):
            zeros = jnp.zeros((POOL_HALO, POOL_GROUP_DIM), F32)
            pad_ref[0:POOL_HALO, :] = zeros
            pad_ref[POOL_HALO:POOL_HALO + seq_len, :] = u_ref[0, seq_start:seq_start + seq_len, :]
            pad_ref[POOL_HALO + seq_len:2 * POOL_HALO + seq_len, :] = zeros

            def chunk(c, carry):
                base = pl.multiple_of(c * ROW_TILE, ROW_TILE)
                x = pad_ref[pl.ds(base, ROW_TILE + 2 * POOL_HALO), :]
                acc = x[POOL_HALO - half:POOL_HALO - half + ROW_TILE]
                for o in range(-half + 1, half):
                    acc = acc + x[POOL_HALO + o:POOL_HALO + o + ROW_TILE]
                t = base + lax.broadcasted_iota(jnp.int32, (ROW_TILE, 1), 0)
                cnt = jnp.minimum(t + half, seq_len) - jnp.maximum(t - half, 0)
                diff = acc / cnt.astype(F32) - x[POOL_HALO:POOL_HALO + ROW_TILE]
                y = jnp.dot(diff.astype(BF16), w, preferred_element_type=F32) * scale
                rows = pl.ds(seq_start + base, ROW_TILE)
                o_ref[0, rows, :] = (y * _silu(g_ref[0, rows, :])).astype(o_ref.dtype)
                return carry

            lax.fori_loop(0, seq_len // ROW_TILE, chunk, 0)

    for gi, win in enumerate(POOL_WINDOWS):
        pl.when(grp == gi)(functools.partial(run, win))


def _pool(p_ap, pool_w, pool_scale, n_ctx):
    B, R, _ = p_ap.shape
    groups = len(POOL_WINDOWS)
    return pl.pallas_call(
        functools.partial(_pool_kernel, n_ctx=n_ctx, n_lat=R - n_ctx),
        grid=(B, groups),
        in_specs=[pl.BlockSpec((1, R, POOL_GROUP_DIM), lambda b, g: (b, 0, COL_POOL_U * groups + g)),
                  pl.BlockSpec((1, R, POOL_GROUP_DIM), lambda b, g: (b, 0, COL_POOL_GATE * groups + g)),
                  pl.BlockSpec((1, POOL_GROUP_DIM, POOL_GROUP_DIM), lambda b, g: (g, 0, 0)),
                  pl.BlockSpec((1, POOL_GROUP_DIM), lambda b, g: (0, g))],
        out_specs=pl.BlockSpec((1, R, POOL_GROUP_DIM), lambda b, g: (b, 0, g)),
        out_shape=jax.ShapeDtypeStruct((B, R, W_BRANCH), BF16),
        scratch_shapes=[pltpu.VMEM((R - n_ctx + 2 * POOL_HALO, POOL_GROUP_DIM), F32)],
        compiler_params=_params("parallel", "arbitrary"),
        name="multiscale_pool",
    )(p_ap, p_ap, pool_w, pool_scale)


def _rwkv_feat_kernel(r_ref, rp_ref, rn_ref, k_ref, kp_ref, kn_ref, v_ref, vp_ref, vn_ref, lora_ref,
                      par_ref, w2_ref, a2_ref, tri_ref,
                      vo_ref, kf_ref, bf_ref, af_ref, rf_ref, kb_ref, bb_ref, ab_ref, rb_ref, ptot_ref, bonus_ref,
                      *, n_ctx_tiles, n_tiles):
    i = pl.program_id(1)
    first = (i == 0) | (i == n_ctx_tiles)
    last = (i == n_ctx_tiles - 1) | (i == n_tiles - 1)
    row = lax.broadcasted_iota(jnp.int32, (ROW_TILE, 1), 0)
    par = par_ref[...]

    def prm(p):
        return par[p:p + 1, :]

    def mix(z_ref, prev_ref, next_ref, mu):
        z = z_ref[0]
        prev = jnp.where(first, 0.0, prev_ref[0, 7:8, :])
        nxt = jnp.where(last, 0.0, next_ref[0, 0:1, :])
        z_prev = jnp.where(row == 0, prev, pltpu.roll(z, 1, 0))
        z_next = jnp.where(row == ROW_TILE - 1, nxt, pltpu.roll(z, ROW_TILE - 1, 0))
        return z + mu * (0.5 * (z_prev + z_next) - z)

    r = mix(r_ref, rp_ref, rn_ref, prm(P_MU_R))
    k = mix(k_ref, kp_ref, kn_ref, prm(P_MU_K))
    v = mix(v_ref, vp_ref, vn_ref, prm(P_MU_V))
    vo_ref[0] = v.T

    kk = k * prm(P_K_K)
    kk = kk * jnp.minimum(lax.rsqrt(_head_sum(kk * kk)), 1e12)

    lora = lora_ref[0]
    lane = lax.broadcasted_iota(jnp.int32, (1, LANES), 1)
    lora = jnp.where(lane < RWKV_LORA, jnp.tanh(lora), lora).astype(BF16)
    k_sum = None
    outs = ((kf_ref, bf_ref, af_ref, rf_ref), (kb_ref, bb_ref, ab_ref, rb_ref))
    chunk_decay = []
    for d, (k_out, b_out, a_out, r_out) in enumerate(outs):
        x = prm(P_W0_F + d) + jnp.dot(lora, w2_ref[d], preferred_element_type=F32)
        w_log = -(jnp.maximum(-x, 0.0) + jnp.log(1.0 + jnp.exp(-jnp.abs(x)))) - 0.5
        neg_log_w = jnp.exp(w_log)
        hi, mid, lo = _split3(neg_log_w)
        tri = tri_ref[d]
        cs = (jnp.dot(tri, hi, preferred_element_type=F32) + jnp.dot(tri, mid, preferred_element_type=F32)
              + jnp.dot(tri, lo, preferred_element_type=F32))
        grow = jnp.exp(cs)
        shrink = jnp.exp(-cs)
        a = 0.5 + 0.5 * jnp.tanh(0.5 * (prm(P_A0_F + d) + jnp.dot(lora, a2_ref[d], preferred_element_type=F32)))
        k_d = k * (1.0 + (a - 1.0) * prm(P_K_A))
        for q in range(ROW_TILE // SCAN_STEPS):
            last = q * SCAN_STEPS + (SCAN_STEPS - 1 if d == 0 else 0)
            chunk_decay.append(shrink[last:last + 1])
        k_out[0] = (k_d * grow).T
        b_out[0] = (kk * a * grow).T
        a_out[0] = (-kk * jnp.exp(neg_log_w - cs)).T
        r_out[0] = (r * shrink).T
        k_sum = k_d if k_sum is None else k_sum + k_d
    ptot_ref[0, 0] = jnp.concatenate(chunk_decay, axis=0)
    bonus_ref[0] = _head_sum(r * k_sum * prm(P_R_K)) * v


def _rwkv_features(p_rw, p_lora, par, w2, a2, n_ctx):
    B, R, _ = p_rw.shape
    n_tiles = R // ROW_TILE
    sub = ROW_TILE // 8
    chunks_per_tile = ROW_TILE // SCAN_STEPS
    assert 2 * chunks_per_tile == 8

    def main(c):
        return pl.BlockSpec((1, ROW_TILE, W_BRANCH), lambda b, i: (b, i, c))

    def prev(c):
        return pl.BlockSpec((1, 8, W_BRANCH), lambda b, i: (b, jnp.maximum(i * sub - 1, 0), c))

    def nxt(c):
        return pl.BlockSpec((1, 8, W_BRANCH), lambda b, i: (b, jnp.minimum((i + 1) * sub, n_tiles * sub - 1), c))

    in_specs = []
    for c in (COL_RW_R, COL_RW_K, COL_RW_V):
        in_specs += [main(c), prev(c), nxt(c)]
    in_specs += [pl.BlockSpec((1, ROW_TILE, LANES), lambda b, i: (b, i, 0)),
                 pl.BlockSpec((P_ROWS, W_BRANCH), lambda b, i: (0, 0)),
                 pl.BlockSpec((2, LANES, W_BRANCH), lambda b, i: (0, 0, 0)),
                 pl.BlockSpec((2, LANES, W_BRANCH), lambda b, i: (0, 0, 0)),
                 pl.BlockSpec((2, ROW_TILE, ROW_TILE), lambda b, i: (0, 0, 0))]
    t_idx = np.arange(ROW_TILE)
    same_chunk = t_idx[:, None] // SCAN_STEPS == t_idx[None, :] // SCAN_STEPS
    tri = jnp.asarray(np.stack([same_chunk & (t_idx[None, :] <= t_idx[:, None]),
                                same_chunk & (t_idx[None, :] >= t_idx[:, None])]), BF16)
    out = jax.ShapeDtypeStruct((B, R, W_BRANCH), F32)
    return pl.pallas_call(
        functools.partial(_rwkv_feat_kernel, n_ctx_tiles=n_ctx // ROW_TILE, n_tiles=n_tiles),
        grid=(B, n_tiles),
        in_specs=in_specs,
        out_specs=[pl.BlockSpec((1, W_BRANCH, ROW_TILE), lambda b, i: (b, 0, i))] * 9
        + [pl.BlockSpec((1, 1, 2 * chunks_per_tile, W_BRANCH), lambda b, i: (b, i, 0, 0)),
           pl.BlockSpec((1, ROW_TILE, W_BRANCH), lambda b, i: (b, i, 0))],
        out_shape=[jax.ShapeDtypeStruct((B, W_BRANCH, R), F32)] * 9 + [jax.ShapeDtypeStruct((B, n_tiles, 2 * chunks_per_tile, W_BRANCH), F32), out],
        compiler_params=_params("parallel", "parallel"),
        name="rwkv_features",
    )(*([p_rw] * 9), p_lora, par, w2, a2, tri)


def _scan_kernel(p_ref, k_ref, b_ref, a_ref, r_ref, v_ref, y_ref, s_ref):
    n = HEAD_DIM

    @pl.when(pl.program_id(0) == 0)
    def _():
        s_ref[...] = jnp.zeros_like(s_ref)

    def row(ref, j, t):
        return ref[0, pl.ds(j * SCAN_STEPS + t, 1), :]

    zero = jnp.zeros((n, s_ref.shape[2]), F32)

    def first_sa(jb, sa):
        for jj in range(SCAN_J_UNROLL):
            j = jb * SCAN_J_UNROLL + jj
            sa = sa + s_ref[j] * row(a_ref, j, 0)
        return sa

    def step(t, sa):
        tile_rows = pl.ds(pl.multiple_of(t * n, n), n)
        vt = v_ref[0, tile_rows, :]
        t_next = jnp.minimum(t + 1, SCAN_STEPS - 1)

        def columns(jb, carry):
            y, sa_next = carry
            for jj in range(SCAN_J_UNROLL):
                j = jb * SCAN_J_UNROLL + jj
                sj = s_ref[j] + sa * row(b_ref, j, t) + vt * row(k_ref, j, t)
                s_ref[j] = sj
                y = y + sj * row(r_ref, j, t)
                sa_next = sa_next + sj * row(a_ref, j, t_next)
            return y, sa_next

        y, sa_next = lax.fori_loop(0, n // SCAN_J_UNROLL, columns, (zero, zero))
        y_ref[0, tile_rows, :] = y
        return sa_next

    def rescale(jb, carry):
        for jj in range(SCAN_J_UNROLL):
            j = jb * SCAN_J_UNROLL + jj
            s_ref[j] = s_ref[j] * p_ref[0, pl.ds(j, 1), :]
        return carry

    sa0 = lax.fori_loop(0, n // SCAN_J_UNROLL, first_sa, zero)
    lax.fori_loop(0, SCAN_STEPS, step, sa0)
    lax.fori_loop(0, n // SCAN_J_UNROLL, rescale, 0)


def _chunk_decay_to_scan(ptot, n_ctx):
    B, n_tiles = ptot.shape[:2]
    pt = ptot.reshape(B, n_tiles, 2, ROW_TILE // SCAN_STEPS, HEAD_DIM, N_HEADS)
    n_ctx_chunks = n_ctx // SCAN_STEPS

    def chains(z):
        return z.reshape(B, -1, HEAD_DIM, N_HEADS).transpose(1, 2, 0, 3).reshape(-1, HEAD_DIM, B * N_HEADS)

    fwd, bwd = chains(pt[:, :, 0]), chains(pt[:, :, 1])
    bwd = jnp.concatenate([bwd[:n_ctx_chunks][::-1], bwd[n_ctx_chunks:][::-1]], axis=0)
    return jnp.concatenate([fwd, bwd], axis=-1)


def _wkv_scan(p, k, b, a, r, v):
    n_chunks, rows, chains = k.shape
    spec = pl.BlockSpec((1, rows, chains), lambda s: (s, 0, 0))
    return pl.pallas_call(
        _scan_kernel,
        grid=(n_chunks,),
        in_specs=[pl.BlockSpec((1, HEAD_DIM, chains), lambda s: (s, 0, 0))] + [spec] * 5,
        out_specs=spec,
        out_shape=jax.ShapeDtypeStruct(k.shape, F32),
        scratch_shapes=[pltpu.VMEM((HEAD_DIM, HEAD_DIM, chains), F32)],
        compiler_params=_params("arbitrary"),
        name="wkv_scan",
    )(p, k, b, a, r, v)


def _flip_rows(x, flip):
    hi, mid, lo = _split3(x)
    return (jnp.dot(flip, hi, preferred_element_type=F32) + jnp.dot(flip, mid, preferred_element_type=F32)
            + jnp.dot(flip, lo, preferred_element_type=F32))


def _mirror_chunk(c, n_ctx_chunks, n_chunks):
    return jnp.where(c < n_ctx_chunks, n_ctx_chunks - 1 - c, n_ctx_chunks + n_chunks - 1 - c)


def _reverse_backward(rows, flip, n_fwd):
    hi, mid, lo = _split3(rows[n_fwd:])
    back = (jnp.dot(hi, flip, preferred_element_type=F32) + jnp.dot(mid, flip, preferred_element_type=F32)
            + jnp.dot(lo, flip, preferred_element_type=F32))
    return jnp.concatenate([rows[:n_fwd], back], axis=0)


def _to_scan_kernel(zf_ref, zb_ref, flip_ref, o_ref, *, step_major):
    nb = zf_ref.shape[0]
    flip = flip_ref[...]
    sub = RELAYOUT_ROWS // SCAN_STEPS

    def body(n, carry):
        rows = pl.ds(pl.multiple_of(n * N_HEADS, N_HEADS), N_HEADS)
        slabs = [zf_ref[b, rows, :] for b in range(nb)] + [zb_ref[b, rows, :] for b in range(nb)]
        tile = _reverse_backward(jnp.concatenate(slabs, axis=0), flip, nb * N_HEADS).T
        for q in range(sub):
            if step_major:
                dst = pl.ds(n, SCAN_STEPS, stride=HEAD_DIM)
            else:
                dst = pl.ds(pl.multiple_of(n * SCAN_STEPS, SCAN_STEPS), SCAN_STEPS)
            o_ref[q, dst, :] = tile[q * SCAN_STEPS:(q + 1) * SCAN_STEPS]
        return carry

    lax.fori_loop(0, HEAD_DIM, body, 0, unroll=RELAYOUT_UNROLL)


def _to_scan(z_fwd, z_bwd, flip, n_ctx, step_major=False):
    B, _, R = z_fwd.shape
    n_chunks = R // RELAYOUT_ROWS
    n_ctx_chunks = n_ctx // RELAYOUT_ROWS
    sub = RELAYOUT_ROWS // SCAN_STEPS
    chains = 2 * B * N_HEADS
    return pl.pallas_call(
        functools.partial(_to_scan_kernel, step_major=step_major),
        grid=(n_chunks,),
        in_specs=[pl.BlockSpec((B, W_BRANCH, RELAYOUT_ROWS), lambda c: (0, 0, c)),
                  pl.BlockSpec((B, W_BRANCH, RELAYOUT_ROWS),
                               lambda c: (0, 0, _mirror_chunk(c, n_ctx_chunks, n_chunks))),
                  pl.BlockSpec((RELAYOUT_ROWS, RELAYOUT_ROWS), lambda c: (0, 0))],
        out_specs=pl.BlockSpec((sub, HEAD_DIM * SCAN_STEPS, chains), lambda c: (c, 0, 0)),
        out_shape=jax.ShapeDtypeStruct((R // SCAN_STEPS, HEAD_DIM * SCAN_STEPS, chains), F32),
        compiler_params=_params("parallel"),
        name="to_scan_layout",
    )(z_fwd, z_bwd, flip)


def _from_scan_kernel(y_ref, flip_ref, yf_ref, yb_ref):
    nb = yf_ref.shape[0]
    flip = flip_ref[...]
    sub = RELAYOUT_ROWS // SCAN_STEPS

    def body(n, carry):
        tile = jnp.concatenate([y_ref[q, pl.ds(n, SCAN_STEPS, stride=HEAD_DIM), :] for q in range(sub)], axis=0)
        tile = _reverse_backward(tile.T, flip, nb * N_HEADS)
        rows = pl.ds(pl.multiple_of(n * N_HEADS, N_HEADS), N_HEADS)
        for b in range(nb):
            yf_ref[b, rows, :] = tile[b * N_HEADS:(b + 1) * N_HEADS]
            yb_ref[b, rows, :] = tile[(nb + b) * N_HEADS:(nb + b + 1) * N_HEADS]
        return carry

    lax.fori_loop(0, HEAD_DIM, body, 0, unroll=RELAYOUT_UNROLL)


def _from_scan(y, flip, n_batch, n_ctx):
    R = y.shape[0] * SCAN_STEPS
    n_chunks = R // RELAYOUT_ROWS
    n_ctx_chunks = n_ctx // RELAYOUT_ROWS
    sub = RELAYOUT_ROWS // SCAN_STEPS
    out = jax.ShapeDtypeStruct((n_batch, W_BRANCH, R), F32)
    return pl.pallas_call(
        _from_scan_kernel,
        grid=(n_chunks,),
        in_specs=[pl.BlockSpec((sub, HEAD_DIM * SCAN_STEPS, y.shape[2]), lambda c: (c, 0, 0)),
                  pl.BlockSpec((RELAYOUT_ROWS, RELAYOUT_ROWS), lambda c: (0, 0))],
        out_specs=[pl.BlockSpec((n_batch, W_BRANCH, RELAYOUT_ROWS), lambda c: (0, 0, c)),
                   pl.BlockSpec((n_batch, W_BRANCH, RELAYOUT_ROWS),
                                lambda c: (0, 0, _mirror_chunk(c, n_ctx_chunks, n_chunks)))],
        out_shape=[out, out],
        compiler_params=_params("parallel"),
        name="from_scan_layout",
    )(y, flip)


def _rwkv_readout_kernel(yf_ref, yb_ref, bonus_ref, gate_ref, gb_ref, o_ref):
    y = (yf_ref[0] + yb_ref[0]).T
    mu = _head_sum(y) * (1.0 / HEAD_DIM)
    yc = y - mu
    var = _head_sum(yc * yc) * (1.0 / HEAD_DIM)
    gb = gb_ref[...]
    out = yc * lax.rsqrt(var + LNX_EPS) * gb[0:1, :] + gb[1:2, :] + bonus_ref[0]
    o_ref[0] = (out * _silu(gate_ref[0])).astype(o_ref.dtype)


def _rwkv_readout(y_fwd, y_bwd, bonus, p_rw, lnx_gb):
    B, R, _ = bonus.shape
    tile = pl.BlockSpec((1, ROW_TILE, W_BRANCH), lambda b, i: (b, i, 0))
    tile_t = pl.BlockSpec((1, W_BRANCH, ROW_TILE), lambda b, i: (b, 0, i))
    return pl.pallas_call(
        _rwkv_readout_kernel,
        grid=(B, R // ROW_TILE),
        in_specs=[tile_t, tile_t, tile,
                  pl.BlockSpec((1, ROW_TILE, W_BRANCH), lambda b, i: (b, i, COL_RW_GATE)),
                  pl.BlockSpec((8, W_BRANCH), lambda b, i: (0, 0))],
        out_specs=tile,
        out_shape=jax.ShapeDtypeStruct((B, R, W_BRANCH), BF16),
        compiler_params=_params("parallel", "parallel"),
        name="rwkv_readout",
    )(y_fwd, y_bwd, bonus, p_rw, lnx_gb)


def _merge_kernel(na_ref, pool_ref, rw_ref, lna_ref, lpool_ref, lrw_ref, w_ref, o_ref):
    acc = None
    for br, (x_ref, l_ref) in enumerate(((na_ref, lna_ref), (pool_ref, lpool_ref), (rw_ref, lrw_ref))):
        t = _sigmoid(l_ref[...]) * jnp.dot(x_ref[...], w_ref[br], preferred_element_type=F32)
        acc = t if acc is None else acc + t
    o_ref[...] = acc.astype(o_ref.dtype)


def _merge(b_na, b_pool, b_rw, p_merge, w_branch):
    M = b_na.shape[0]
    tm, tn = min(_row_tile(M), 512), 1024
    nb = D_MODEL // tn
    x_spec = pl.BlockSpec((tm, W_BRANCH), lambda j, i: (i, 0))

    def logit(br):
        return pl.BlockSpec((tm, tn), lambda j, i: (i, br * nb + j))

    return pl.pallas_call(
        _merge_kernel,
        grid=(nb, M // tm),
        in_specs=[x_spec, x_spec, x_spec, logit(0), logit(1), logit(2),
                  pl.BlockSpec((N_BRANCH, W_BRANCH, tn), lambda j, i: (0, 0, j))],
        out_specs=pl.BlockSpec((tm, tn), lambda j, i: (i, j)),
        out_shape=jax.ShapeDtypeStruct((M, D_MODEL), BF16),
        compiler_params=_params("parallel", "parallel"),
        name="branch_merge",
    )(b_na, b_pool, b_rw, p_merge, p_merge, p_merge, w_branch)


def _out_kernel(m_ref, w_ref, c_ref, l_ref, mod_ref, fg_ref, o_ref, *, n_ctx_tiles, tile_offset, final):
    tile = pl.program_id(1) + tile_offset
    gate = _mod_row(mod_ref, tile, n_ctx_tiles)[:, 2 * D_MODEL:]
    x = _stream_tile(c_ref, l_ref, tile, n_ctx_tiles)
    x = x + gate * jnp.dot(m_ref[0], w_ref[...], preferred_element_type=F32)
    o_ref[0] = _rms(x, fg_ref[...]) if final else x


def _out_proj(merged, w_out, stream, mod, final_g, n_ctx, final):
    B, R, _ = merged.shape
    n_ctx_tiles = n_ctx // ROW_TILE
    off = n_ctx_tiles if final else 0
    return pl.pallas_call(
        functools.partial(_out_kernel, n_ctx_tiles=n_ctx_tiles, tile_offset=off, final=final),
        grid=(B, R // ROW_TILE - off),
        in_specs=[pl.BlockSpec((1, ROW_TILE, D_MODEL), lambda b, i: (b, i + off, 0)),
                  pl.BlockSpec((D_MODEL, D_MODEL), lambda b, i: (0, 0))]
        + _stream_specs(stream, n_ctx_tiles, off) + [
            pl.BlockSpec((MOD_ROWS, 3 * D_MODEL), lambda b, i: (0, 0)),
            pl.BlockSpec((1, D_MODEL), lambda b, i: (0, 0))],
        out_specs=pl.BlockSpec((1, ROW_TILE, D_MODEL), lambda b, i: (b, i, 0)),
        out_shape=jax.ShapeDtypeStruct((B, R - off * ROW_TILE, D_MODEL), F32),
        compiler_params=_params("parallel", "parallel"),
        name="out_proj_final" if final else "out_proj",
    )(merged, w_out, stream[0], stream[1], mod, final_g)


def _layer(stream, R, mod, n_ctx, final, final_g, norm_g, w_in_all, layer, na_rpb, pool_w, pool_scale, rw_mu, rw_w0, rw_w2,
           rw_a0, rw_a2, rw_k_k, rw_k_a, rw_r_k, rw_lnx_g, rw_lnx_b, w_branch, w_out):
    B = stream[0].shape[0]
    rows = (R - n_ctx) // GRID_W

    h = _norm_mod(stream, R, norm_g[None], mod, n_ctx).reshape(B * R, D_MODEL)
    lo = N_MAIN + 2 * RWKV_LORA
    w_in = w_in_all[layer]
    p_ap = _matmul(h, w_in_all, F32, "in_proj_attn_pool", n_cols=N_ATTN_POOL, layer=layer)
    p_ap = p_ap.reshape(B, R, N_ATTN_POOL)
    w_rw = _head_major(w_in[:, N_ATTN_POOL:N_MAIN].reshape(D_MODEL, -1, W_BRANCH)).reshape(D_MODEL, -1)
    p_rw = _matmul(h, w_rw.astype(BF16), F32, "in_proj_rwkv").reshape(B, R, N_MAIN - N_ATTN_POOL)
    p_lora = _matmul(h, w_in[:, N_MAIN:lo].astype(BF16), F32, "in_proj_lora").reshape(B, R, 2 * RWKV_LORA)
    p_merge = _matmul(h, w_in_all, F32, "in_proj_merge", n_cols=N_BRANCH * D_MODEL, layer=layer, first_col=lo)

    b_na = _na_attention(p_ap, _na_bias_tables(na_rpb, rows), n_ctx)
    b_pool = _pool(p_ap, pool_w.astype(BF16), pool_scale[None], n_ctx)

    par = jnp.zeros((P_ROWS, W_BRANCH), F32)
    par = par.at[P_MU_R:P_MU_V + 1].set(rw_mu).at[P_W0_F:P_W0_B + 1].set(rw_w0).at[P_A0_F:P_A0_B + 1].set(rw_a0)
    par = _head_major(par.at[P_K_K].set(rw_k_k).at[P_K_A].set(rw_k_a).at[P_R_K].set(rw_r_k.reshape(-1)))
    zeros = jnp.zeros_like(rw_w2)
    w2 = _head_major(jnp.concatenate([rw_w2, zeros], axis=1)).astype(BF16)
    a2 = _head_major(jnp.concatenate([zeros, rw_a2], axis=1)).astype(BF16)
    v, k_f, b_f, a_f, r_f, k_b, b_b, a_b, r_b, ptot, bonus = _rwkv_features(p_rw, p_lora, par, w2, a2, n_ctx)
    flip = jnp.asarray(np.eye(RELAYOUT_ROWS)[::-1], BF16)
    y = _wkv_scan(_chunk_decay_to_scan(ptot, n_ctx),
                  *[_to_scan(zf, zb, flip, n_ctx) for zf, zb in ((k_f, k_b), (b_f, b_b), (a_f, a_b), (r_f, r_b))],
                  _to_scan(v, v, flip, n_ctx, step_major=True))
    y_fwd, y_bwd = _from_scan(y, flip, B, n_ctx)
    lnx_gb = _head_major(jnp.zeros((8, W_BRANCH), F32).at[0].set(rw_lnx_g).at[1].set(rw_lnx_b))
    b_rw = _rwkv_readout(y_fwd, y_bwd, bonus, p_rw, lnx_gb)

    def flat(z):
        return z.reshape(B * R, W_BRANCH)

    w_rw_out = _head_major(w_branch[2].T).T
    w_br = jnp.stack([w_branch[0], w_branch[1], w_rw_out]).astype(BF16)
    merged = _merge(flat(b_na), flat(b_pool), flat(b_rw), p_merge, w_br)
    return _out_proj(merged.reshape(B, R, D_MODEL), w_out.astype(BF16), stream, mod, final_g[None], n_ctx, final)


def kernel(x, c, ctx, c_ctx, norm_g, w_mod, b_mod, w_in, na_rpb, pool_w, pool_scale, rw_mu, rw_w0, rw_w2, rw_a0,
           rw_a2, rw_k_k, rw_k_a, rw_r_k, rw_lnx_g, rw_lnx_b, w_branch, w_out, final_g):
    B, T, _ = x.shape
    n_ctx = ctx.shape[1]
    depth = w_in.shape[0]
    assert B <= CTX_MOD_ROW and n_ctx % ROW_TILE == 0 and T % ROW_TILE == 0
    assert ROW_TILE % RELAYOUT_ROWS == 0 and RELAYOUT_ROWS % SCAN_STEPS == 0

    cond = jnp.zeros((MOD_ROWS, D_MODEL), F32).at[:B].set(c).at[CTX_MOD_ROW].set(c_ctx)
    mods = _modulation(cond, w_mod, b_mod[:, None, :])
    stream = (ctx, x, n_ctx // ROW_TILE)
    for layer in range(depth):
        out = _layer(stream, n_ctx + T, mods[layer], n_ctx, layer == depth - 1, final_g, norm_g[layer], w_in, layer,
                     na_rpb[layer], pool_w[layer], pool_scale[layer], rw_mu[layer], rw_w0[layer], rw_w2[layer],
                     rw_a0[layer], rw_a2[layer], rw_k_k[layer], rw_k_a[layer], rw_r_k[layer], rw_lnx_g[layer],
                     rw_lnx_b[layer], w_branch[layer], w_out[layer])
        stream = (out, out, 0)
    return out
```

```python
import functools

import numpy as np
import jax
import jax.numpy as jnp
from jax import lax
from jax.experimental import pallas as pl
from jax.experimental.pallas import tpu as pltpu

F32 = jnp.float32
BF16 = jnp.bfloat16

D_MODEL = 2048
W_BRANCH = D_MODEL // 2
N_BRANCH = 3
N_HEADS = 16
HEAD_DIM = 64
GRID_W = 64
NA_WIN_H = 8
NA_WIN_W = 16
POOL_WINDOWS = (2, 4, 8, 16)
POOL_GROUP_DIM = W_BRANCH // len(POOL_WINDOWS)
POOL_HALO = max(POOL_WINDOWS) // 2
RWKV_LORA = 64
RMS_EPS = 1e-6
LNX_EPS = 64e-5
NEG_INF = -1e30

LANES = 128
ROW_TILE = 256
NA_Q_ROWS = ROW_TILE // GRID_W
NA_K_ROWS = NA_Q_ROWS + NA_WIN_H
NA_K_TOK = NA_K_ROWS * GRID_W
CTX_MOD_ROW = 4
MOD_ROWS = 8
SCAN_STEPS = 64
SCAN_J_UNROLL = 64
RELAYOUT_UNROLL = 32
RELAYOUT_ROWS = 128
VMEM_LIMIT = 56 << 20

COL_Q, COL_K, COL_V, COL_NA_GATE, COL_POOL_U, COL_POOL_GATE = range(6)
COL_RW_R, COL_RW_K, COL_RW_V, COL_RW_GATE = range(4)
N_ATTN_POOL = 6 * W_BRANCH
N_MAIN = 10 * W_BRANCH

P_MU_R, P_MU_K, P_MU_V, P_W0_F, P_W0_B, P_A0_F, P_A0_B, P_K_K, P_K_A, P_R_K = range(10)
P_ROWS = 16


def _params(*sem):
    return pltpu.CompilerParams(dimension_semantics=sem, vmem_limit_bytes=VMEM_LIMIT)


def _sigmoid(x):
    return 1.0 / (1.0 + jnp.exp(-x))


def _silu(x):
    return x * _sigmoid(x)


def _split3(x):
    hi = x.astype(BF16)
    r1 = x - hi.astype(F32)
    mid = r1.astype(BF16)
    lo = (r1 - mid.astype(F32)).astype(BF16)
    return hi, mid, lo


def _head_major(z):
    lead = z.shape[:-1]
    return z.reshape(lead + (N_HEADS, HEAD_DIM)).swapaxes(-1, -2).reshape(lead + (W_BRANCH,))


def _head_sum(x):
    n_tiles = W_BRANCH // LANES
    part = x[:, :LANES]
    for c in range(1, n_tiles):
        part = part + x[:, c * LANES:(c + 1) * LANES]
    shift = N_HEADS
    while shift < LANES:
        part = part + pltpu.roll(part, shift, 1)
        shift *= 2
    return jnp.concatenate([part] * n_tiles, axis=1)


def _mod_kernel(cond_ref, w_ref, b_ref, o_ref):
    s = _silu(cond_ref[...])
    o_ref[0] = jnp.dot(s.astype(BF16), w_ref[0].astype(BF16), preferred_element_type=F32) + b_ref[0]


def _modulation(cond, w_mod, b_mod):
    n_layers = w_mod.shape[0]
    tn = 3 * D_MODEL // 4
    return pl.pallas_call(
        _mod_kernel,
        grid=(n_layers, 4),
        in_specs=[pl.BlockSpec((MOD_ROWS, D_MODEL), lambda l, j: (0, 0)),
                  pl.BlockSpec((1, D_MODEL, tn), lambda l, j: (l, 0, j)),
                  pl.BlockSpec((1, 1, tn), lambda l, j: (l, 0, j))],
        out_specs=pl.BlockSpec((1, MOD_ROWS, tn), lambda l, j: (l, 0, j)),
        out_shape=jax.ShapeDtypeStruct((n_layers, MOD_ROWS, 3 * D_MODEL), F32),
        compiler_params=_params("arbitrary", "arbitrary"),
        name="adaln_modulation",
    )(cond, w_mod, b_mod)


def _mod_row(mod_ref, tile, n_ctx_tiles):
    row = jnp.where(tile < n_ctx_tiles, CTX_MOD_ROW, pl.program_id(0))
    return mod_ref[pl.ds(row, 1), :]


def _rms(x, g):
    return x * lax.rsqrt(jnp.mean(x * x, axis=-1, keepdims=True) + RMS_EPS) * g


def _stream_specs(stream, n_ctx_tiles, tile_offset=0):
    lat_shift = stream[2]
    return [pl.BlockSpec((1, ROW_TILE, D_MODEL), lambda b, i: (b, jnp.minimum(i + tile_offset, n_ctx_tiles - 1), 0)),
            pl.BlockSpec((1, ROW_TILE, D_MODEL), lambda b, i: (b, jnp.maximum(i + tile_offset - lat_shift, 0), 0))]


def _stream_tile(c_ref, l_ref, tile, n_ctx_tiles):
    return jnp.where(tile < n_ctx_tiles, c_ref[0], l_ref[0])


def _norm_mod_kernel(c_ref, l_ref, g_ref, mod_ref, h_ref, *, n_ctx_tiles):
    tile = pl.program_id(1)
    m = _mod_row(mod_ref, tile, n_ctx_tiles)
    shift = m[:, :D_MODEL]
    scale = m[:, D_MODEL:2 * D_MODEL]
    x = _stream_tile(c_ref, l_ref, tile, n_ctx_tiles)
    h_ref[0] = (_rms(x, g_ref[...]) * (1.0 + scale) + shift).astype(BF16)


def _norm_mod(stream, n_rows, norm_g, mod, n_ctx):
    B = stream[0].shape[0]
    n_ctx_tiles = n_ctx // ROW_TILE
    return pl.pallas_call(
        functools.partial(_norm_mod_kernel, n_ctx_tiles=n_ctx_tiles),
        grid=(B, n_rows // ROW_TILE),
        in_specs=_stream_specs(stream, n_ctx_tiles) + [
            pl.BlockSpec((1, D_MODEL), lambda b, i: (0, 0)),
            pl.BlockSpec((MOD_ROWS, 3 * D_MODEL), lambda b, i: (0, 0))],
        out_specs=pl.BlockSpec((1, ROW_TILE, D_MODEL), lambda b, i: (b, i, 0)),
        out_shape=jax.ShapeDtypeStruct((B, n_rows, D_MODEL), BF16),
        compiler_params=_params("parallel", "parallel"),
        name="norm_modulate",
    )(stream[0], stream[1], norm_g, mod)


def _mm_kernel(a_ref, w_ref, o_ref):
    o_ref[...] = jnp.dot(a_ref[...], w_ref[...], preferred_element_type=F32).astype(o_ref.dtype)


def _row_tile(m):
    for t in (1024, 512, 256):
        if m % t == 0:
            return t
    raise ValueError(f"row count {m} is not a multiple of {ROW_TILE}")


def _mm_cast_kernel(a_ref, w_ref, o_ref, wb_ref):
    @pl.when(pl.program_id(1) == 0)
    def _():
        w = w_ref[...] if len(w_ref.shape) == 2 else w_ref[0]
        wb_ref[...] = w.astype(BF16)

    o_ref[...] = jnp.dot(a_ref[...], wb_ref[...], preferred_element_type=F32).astype(o_ref.dtype)


def _matmul(a, w, out_dtype, name, n_cols=None, layer=None, first_col=0):
    M, K = a.shape
    N = w.shape[-1] if n_cols is None else n_cols
    tm = _row_tile(M)
    tn = min(N, 1024)
    assert first_col % LANES == 0
    cast = w.dtype != BF16
    def col(j):
        return pl.multiple_of(first_col + j * tn, LANES)

    if layer is None:
        w_spec = pl.BlockSpec((pl.Element(K), pl.Element(tn)), lambda j, i: (0, col(j)))
    else:
        w_spec = pl.BlockSpec((pl.Element(1), pl.Element(K), pl.Element(tn)), lambda j, i: (layer, 0, col(j)))
    return pl.pallas_call(
        _mm_cast_kernel if cast else _mm_kernel,
        grid=(N // tn, M // tm),
        in_specs=[pl.BlockSpec((tm, K), lambda j, i: (i, 0)), w_spec],
        out_specs=pl.BlockSpec((tm, tn), lambda j, i: (i, j)),
        out_shape=jax.ShapeDtypeStruct((M, N), out_dtype),
        scratch_shapes=[pltpu.VMEM((K, tn), BF16)] if cast else [],
        compiler_params=_params("parallel", "arbitrary" if cast else "parallel"),
        name=name,
    )(a, w)


def _na_bias_tables(rpb, rows):
    n_blocks = rows // NA_Q_ROWS
    n_off = 2 * NA_WIN_H - 1
    col = np.arange(GRID_W)
    c0 = np.clip(col - NA_WIN_W // 2, 0, GRID_W - NA_WIN_W)
    valid_c = (col[None, :] >= c0[:, None]) & (col[None, :] < c0[:, None] + NA_WIN_W)
    col_off = np.clip(col[None, :] - col[:, None] + NA_WIN_W - 1, 0, 2 * NA_WIN_W - 2)
    pick_c = jnp.asarray(np.eye(2 * NA_WIN_W - 1)[col_off], F32)
    tile = jnp.einsum("hrc,qpc->hrqp", rpb.astype(F32), pick_c, precision=lax.Precision.HIGHEST)
    tile = jnp.where(valid_c, tile, NEG_INF)
    tile = jnp.concatenate([tile, jnp.full((N_HEADS, 1, GRID_W, GRID_W), NEG_INF, F32)], axis=1)
    tile = jnp.concatenate([tile, tile], axis=-1)
    picks = []
    for m in (0, 1, n_blocks - 1):
        q_row = NA_Q_ROWS * m + np.arange(NA_Q_ROWS)
        k_row = int(np.clip(NA_Q_ROWS * m - NA_Q_ROWS, 0, rows - NA_K_ROWS)) + np.arange(NA_K_ROWS)
        r0 = np.clip(q_row - NA_WIN_H // 2, 0, rows - NA_WIN_H)
        valid_r = (k_row[None, :] >= r0[:, None]) & (k_row[None, :] < r0[:, None] + NA_WIN_H)
        row_off = np.clip(k_row[None, :] - q_row[:, None] + NA_WIN_H - 1, 0, n_off - 1)
        picks.append(np.where(valid_r, row_off, n_off))
    picks = np.stack(picks)

    def build(t_ref, o_ref):
        left = lax.broadcasted_iota(jnp.int32, (1, LANES), 1) < GRID_W
        for ty in range(picks.shape[0]):
            for a in range(NA_Q_ROWS):
                for kp in range(NA_K_ROWS // 2):
                    pair = jnp.where(left, t_ref[0, int(picks[ty, a, 2 * kp])], t_ref[0, int(picks[ty, a, 2 * kp + 1])])
                    o_ref[ty, 0, a * GRID_W:(a + 1) * GRID_W, kp * LANES:(kp + 1) * LANES] = pair

    return pl.pallas_call(
        build,
        grid=(N_HEADS,),
        in_specs=[pl.BlockSpec((1, n_off + 1, GRID_W, LANES), lambda h: (h, 0, 0, 0))],
        out_specs=pl.BlockSpec((picks.shape[0], 1, ROW_TILE, NA_K_TOK), lambda h: (0, h, 0, 0)),
        out_shape=jax.ShapeDtypeStruct((picks.shape[0], N_HEADS, ROW_TILE, NA_K_TOK), F32),
        compiler_params=_params("parallel"),
        name="na_bias_tables",
    )(tile)


def _attend(qe, keys, vals, biases):
    dn = (((1,), (1,)), ((), ()))
    scores = []
    for kk, bias in zip(keys, biases):
        s = lax.dot_general(qe, kk, dn, preferred_element_type=F32)
        scores.append(s if bias is None else s + bias)
    m = scores[0].max(axis=-1, keepdims=True)
    for s in scores[1:]:
        m = jnp.maximum(m, s.max(axis=-1, keepdims=True))
    num, den = None, None
    for s, vv in zip(scores, vals):
        p = jnp.exp(s - m)
        l = p.sum(axis=-1, keepdims=True)
        o = jnp.dot(p.astype(BF16), vv, preferred_element_type=F32)
        num = o if num is None else num + o
        den = l if den is None else den + l
    return num / den


def _na_kernel(q_ref, k_ref, v_ref, g_ref, bias_ref, o_ref, *, n_ctx, rows):
    j = pl.program_id(2)
    lane = lax.broadcasted_iota(jnp.int32, (1, LANES), 1)
    in_head = (lane < HEAD_DIM, lane >= HEAD_DIM)
    q = q_ref[0] * (HEAD_DIM ** -0.5)
    kc = k_ref[0, 0:n_ctx, :].astype(BF16)
    vc = v_ref[0, 0:n_ctx, :].astype(BF16)

    def heads(q):
        return [jnp.where(in_head[e], q, 0.0).astype(BF16) for e in range(2)]

    def finish(o0, o1):
        o = jnp.where(in_head[0], o0, o1)
        o_ref[0] = (o * _silu(g_ref[0])).astype(o_ref.dtype)

    @pl.when(j == 0)
    def _():
        finish(*[_attend(qe, [kc], [vc], [None]) for qe in heads(q)])

    @pl.when(j > 0)
    def _():
        k_row = jnp.clip(NA_Q_ROWS * (j - 1) - NA_Q_ROWS, 0, rows - NA_K_ROWS)
        start = pl.multiple_of(n_ctx + k_row * GRID_W, GRID_W)
        kw = k_ref[0, pl.ds(start, NA_K_TOK), :].astype(BF16)
        vw = v_ref[0, pl.ds(start, NA_K_TOK), :].astype(BF16)
        finish(*[_attend(qe, [kw, kc], [vw, vc], [bias_ref[0, e], None])
                 for e, qe in enumerate(heads(q))])


def _na_attention(p_ap, bias_tables, n_ctx):
    B, R, _ = p_ap.shape
    rows = (R - n_ctx) // GRID_W
    n_blocks = rows // NA_Q_ROWS
    pairs = W_BRANCH // LANES
    assert n_ctx == ROW_TILE and rows >= NA_K_ROWS and rows % NA_Q_ROWS == 0

    def col(c):
        return lambda b, hp, j: (b, 0, c * pairs + hp)

    def bias_idx(b, hp, j):
        return (jnp.where(j <= 1, 0, jnp.where(j == n_blocks, 2, 1)), hp, 0, 0)

    return pl.pallas_call(
        functools.partial(_na_kernel, n_ctx=n_ctx, rows=rows),
        grid=(B, pairs, n_blocks + 1),
        in_specs=[pl.BlockSpec((1, ROW_TILE, LANES), lambda b, hp, j: (b, j, COL_Q * pairs + hp)),
                  pl.BlockSpec((1, R, LANES), col(COL_K)),
                  pl.BlockSpec((1, R, LANES), col(COL_V)),
                  pl.BlockSpec((1, ROW_TILE, LANES), lambda b, hp, j: (b, j, COL_NA_GATE * pairs + hp)),
                  pl.BlockSpec((1, 2, ROW_TILE, NA_K_TOK), bias_idx)],
        out_specs=pl.BlockSpec((1, ROW_TILE, LANES), lambda b, hp, j: (b, j, hp)),
        out_shape=jax.ShapeDtypeStruct((B, R, W_BRANCH), BF16),
        compiler_params=_params("parallel", "parallel", "arbitrary"),
        name="neighbourhood_attention",
    )(p_ap, p_ap, p_ap, p_ap, bias_tables)


def _pool_kernel(u_ref, g_ref, w_ref, sc_ref, o_ref, pad_ref, *, n_ctx, n_lat):
    grp = pl.program_id(1)
    w = w_ref[0]
    scale = sc_ref[...]

    def run(win):
        half = win // 2
        for seq_start, seq_len in ((0, n_ctx), (n_ctx, n_lat)):
            zeros = jnp.zeros((POOL_HALO, POOL_GROUP_DIM), F32)
            pad_ref[0:POOL_HALO, :] = zeros
            pad_ref[POOL_HALO:POOL_HALO + seq_len, :] = u_ref[0, seq_start:seq_start + seq_len, :]
            pad_ref[POOL_HALO + seq_len:2 * POOL_HALO + seq_len, :] = zeros

            def chunk(c, carry):
                base = pl.multiple_of(c * ROW_TILE, ROW_TILE)
                x = pad_ref[pl.ds(base, ROW_TILE + 2 * POOL_HALO), :]
                acc = x[POOL_HALO - half:POOL_HALO - half + ROW_TILE]
                for o in range(-half + 1, half):
                    acc = acc + x[POOL_HALO + o:POOL_HALO + o + ROW_TILE]
                t = base + lax.broadcasted_iota(jnp.int32, (ROW_TILE, 1), 0)
                cnt = jnp.minimum(t + half, seq_len) - jnp.maximum(t - half, 0)
                diff = acc / cnt.astype(F32) - x[POOL_HALO:POOL_HALO + ROW_TILE]
                y = jnp.dot(diff.astype(BF16), w, preferred_element_type=F32) * scale
                rows = pl.ds(seq_start + base, ROW_TILE)
                o_ref[0, rows, :] = (y * _silu(g_ref[0, rows, :])).astype(o_ref.dtype)
                return carry

            lax.fori_loop(0, seq_len // ROW_TILE, chunk, 0)

    for gi, win in enumerate(POOL_WINDOWS):
        pl.when(grp == gi)(functools.partial(run, win))


def _pool(p_ap, pool_w, pool_scale, n_ctx):
    B, R, _ = p_ap.shape
    groups = len(POOL_WINDOWS)
    return pl.pallas_call(
        functools.partial(_pool_kernel, n_ctx=n_ctx, n_lat=R - n_ctx),
        grid=(B, groups),
        in_specs=[pl.BlockSpec((1, R, POOL_GROUP_DIM), lambda b, g: (b, 0, COL_POOL_U * groups + g)),
                  pl.BlockSpec((1, R, POOL_GROUP_DIM), lambda b, g: (b, 0, COL_POOL_GATE * groups + g)),
                  pl.BlockSpec((1, POOL_GROUP_DIM, POOL_GROUP_DIM), lambda b, g: (g, 0, 0)),
                  pl.BlockSpec((1, POOL_GROUP_DIM), lambda b, g: (0, g))],
        out_specs=pl.BlockSpec((1, R, POOL_GROUP_DIM), lambda b, g: (b, 0, g)),
        out_shape=jax.ShapeDtypeStruct((B, R, W_BRANCH), BF16),
        scratch_shapes=[pltpu.VMEM((R - n_ctx + 2 * POOL_HALO, POOL_GROUP_DIM), F32)],
        compiler_params=_params("parallel", "arbitrary"),
        name="multiscale_pool",
    )(p_ap, p_ap, pool_w, pool_scale)


def _rwkv_feat_kernel(r_ref, rp_ref, rn_ref, k_ref, kp_ref, kn_ref, v_ref, vp_ref, vn_ref, lora_ref,
                      par_ref, w2_ref, a2_ref, tri_ref,
                      vo_ref, kf_ref, bf_ref, af_ref, rf_ref, kb_ref, bb_ref, ab_ref, rb_ref, ptot_ref, bonus_ref,
                      *, n_ctx_tiles, n_tiles):
    i = pl.program_id(1)
    first = (i == 0) | (i == n_ctx_tiles)
    last = (i == n_ctx_tiles - 1) | (i == n_tiles - 1)
    row = lax.broadcasted_iota(jnp.int32, (ROW_TILE, 1), 0)
    par = par_ref[...]

    def prm(p):
        return par[p:p + 1, :]

    def mix(z_ref, prev_ref, next_ref, mu):
        z = z_ref[0]
        prev = jnp.where(first, 0.0, prev_ref[0, 7:8, :])
        nxt = jnp.where(last, 0.0, next_ref[0, 0:1, :])
        z_prev = jnp.where(row == 0, prev, pltpu.roll(z, 1, 0))
        z_next = jnp.where(row == ROW_TILE - 1, nxt, pltpu.roll(z, ROW_TILE - 1, 0))
        return z + mu * (0.5 * (z_prev + z_next) - z)

    r = mix(r_ref, rp_ref, rn_ref, prm(P_MU_R))
    k = mix(k_ref, kp_ref, kn_ref, prm(P_MU_K))
    v = mix(v_ref, vp_ref, vn_ref, prm(P_MU_V))
    vo_ref[0] = v.T

    kk = k * prm(P_K_K)
    kk = kk * jnp.minimum(lax.rsqrt(_head_sum(kk * kk)), 1e12)

    lora = lora_ref[0]
    lane = lax.broadcasted_iota(jnp.int32, (1, LANES), 1)
    lora = jnp.where(lane < RWKV_LORA, jnp.tanh(lora), lora).astype(BF16)
    k_sum = None
    outs = ((kf_ref, bf_ref, af_ref, rf_ref), (kb_ref, bb_ref, ab_ref, rb_ref))
    chunk_decay = []
    for d, (k_out, b_out, a_out, r_out) in enumerate(outs):
        x = prm(P_W0_F + d) + jnp.dot(lora, w2_ref[d], preferred_element_type=F32)
        w_log = -(jnp.maximum(-x, 0.0) + jnp.log(1.0 + jnp.exp(-jnp.abs(x)))) - 0.5
        neg_log_w = jnp.exp(w_log)
        hi, mid, lo = _split3(neg_log_w)
        tri = tri_ref[d]
        cs = (jnp.dot(tri, hi, preferred_element_type=F32) + jnp.dot(tri, mid, preferred_element_type=F32)
              + jnp.dot(tri, lo, preferred_element_type=F32))
        grow = jnp.exp(cs)
        shrink = jnp.exp(-cs)
        a = 0.5 + 0.5 * jnp.tanh(0.5 * (prm(P_A0_F + d) + jnp.dot(lora, a2_ref[d], preferred_element_type=F32)))
        k_d = k * (1.0 + (a - 1.0) * prm(P_K_A))
        for q in range(ROW_TILE // SCAN_STEPS):
            last = q * SCAN_STEPS + (SCAN_STEPS - 1 if d == 0 else 0)
            chunk_decay.append(shrink[last:last + 1])
        k_out[0] = (k_d * grow).T
        b_out[0] = (kk * a * grow).T
        a_out[0] = (-kk * jnp.exp(neg_log_w - cs)).T
        r_out[0] = (r * shrink).T
        k_sum = k_d if k_sum is None else k_sum + k_d
    ptot_ref[0, 0] = jnp.concatenate(chunk_decay, axis=0)
    bonus_ref[0] = _head_sum(r * k_sum * prm(P_R_K)) * v


def _rwkv_features(p_rw, p_lora, par, w2, a2, n_ctx):
    B, R, _ = p_rw.shape
    n_tiles = R // ROW_TILE
    sub = ROW_TILE // 8
    chunks_per_tile = ROW_TILE // SCAN_STEPS
    assert 2 * chunks_per_tile == 8

    def main(c):
        return pl.BlockSpec((1, ROW_TILE, W_BRANCH), lambda b, i: (b, i, c))

    def prev(c):
        return pl.BlockSpec((1, 8, W_BRANCH), lambda b, i: (b, jnp.maximum(i * sub - 1, 0), c))

    def nxt(c):
        return pl.BlockSpec((1, 8, W_BRANCH), lambda b, i: (b, jnp.minimum((i + 1) * sub, n_tiles * sub - 1), c))

    in_specs = []
    for c in (COL_RW_R, COL_RW_K, COL_RW_V):
        in_specs += [main(c), prev(c), nxt(c)]
    in_specs += [pl.BlockSpec((1, ROW_TILE, LANES), lambda b, i: (b, i, 0)),
                 pl.BlockSpec((P_ROWS, W_BRANCH), lambda b, i: (0, 0)),
                 pl.BlockSpec((2, LANES, W_BRANCH), lambda b, i: (0, 0, 0)),
                 pl.BlockSpec((2, LANES, W_BRANCH), lambda b, i: (0, 0, 0)),
                 pl.BlockSpec((2, ROW_TILE, ROW_TILE), lambda b, i: (0, 0, 0))]
    t_idx = np.arange(ROW_TILE)
    same_chunk = t_idx[:, None] // SCAN_STEPS == t_idx[None, :] // SCAN_STEPS
    tri = jnp.asarray(np.stack([same_chunk & (t_idx[None, :] <= t_idx[:, None]),
                                same_chunk & (t_idx[None, :] >= t_idx[:, None])]), BF16)
    out = jax.ShapeDtypeStruct((B, R, W_BRANCH), F32)
    return pl.pallas_call(
        functools.partial(_rwkv_feat_kernel, n_ctx_tiles=n_ctx // ROW_TILE, n_tiles=n_tiles),
        grid=(B, n_tiles),
        in_specs=in_specs,
        out_specs=[pl.BlockSpec((1, W_BRANCH, ROW_TILE), lambda b, i: (b, 0, i))] * 9
        + [pl.BlockSpec((1, 1, 2 * chunks_per_tile, W_BRANCH), lambda b, i: (b, i, 0, 0)),
           pl.BlockSpec((1, ROW_TILE, W_BRANCH), lambda b, i: (b, i, 0))],
        out_shape=[jax.ShapeDtypeStruct((B, W_BRANCH, R), F32)] * 9 + [jax.ShapeDtypeStruct((B, n_tiles, 2 * chunks_per_tile, W_BRANCH), F32), out],
        compiler_params=_params("parallel", "parallel"),
        name="rwkv_features",
    )(*([p_rw] * 9), p_lora, par, w2, a2, tri)


def _scan_kernel(p_ref, k_ref, b_ref, a_ref, r_ref, v_ref, y_ref, s_ref):
    n = HEAD_DIM

    @pl.when(pl.program_id(0) == 0)
    def _():
        s_ref[...] = jnp.zeros_like(s_ref)

    def row(ref, j, t):
        return ref[0, pl.ds(j * SCAN_STEPS + t, 1), :]

    zero = jnp.zeros((n, s_ref.shape[2]), F32)

    def first_sa(jb, sa):
        for jj in range(SCAN_J_UNROLL):
            j = jb * SCAN_J_UNROLL + jj
            sa = sa + s_ref[j] * row(a_ref, j, 0)
        return sa

    def step(t, sa):
        tile_rows = pl.ds(pl.multiple_of(t * n, n), n)
        vt = v_ref[0, tile_rows, :]
        t_next = jnp.minimum(t + 1, SCAN_STEPS - 1)

        def columns(jb, carry):
            y, sa_next = carry
            for jj in range(SCAN_J_UNROLL):
                j = jb * SCAN_J_UNROLL + jj
                sj = s_ref[j] + sa * row(b_ref, j, t) + vt * row(k_ref, j, t)
                s_ref[j] = sj
                y = y + sj * row(r_ref, j, t)
                sa_next = sa_next + sj * row(a_ref, j, t_next)
            return y, sa_next

        y, sa_next = lax.fori_loop(0, n // SCAN_J_UNROLL, columns, (zero, zero))
        y_ref[0, tile_rows, :] = y
        return sa_next

    def rescale(jb, carry):
        for jj in range(SCAN_J_UNROLL):
            j = jb * SCAN_J_UNROLL + jj
            s_ref[j] = s_ref[j] * p_ref[0, pl.ds(j, 1), :]
        return carry

    sa0 = lax.fori_loop(0, n // SCAN_J_UNROLL, first_sa, zero)
    lax.fori_loop(0, SCAN_STEPS, step, sa0)
    lax.fori_loop(0, n // SCAN_J_UNROLL, rescale, 0)


def _chunk_decay_to_scan(ptot, n_ctx):
    B, n_tiles = ptot.shape[:2]
    pt = ptot.reshape(B, n_tiles, 2, ROW_TILE // SCAN_STEPS, HEAD_DIM, N_HEADS)
    n_ctx_chunks = n_ctx // SCAN_STEPS

    def chains(z):
        return z.reshape(B, -1, HEAD_DIM, N_HEADS).transpose(1, 2, 0, 3).reshape(-1, HEAD_DIM, B * N_HEADS)

    fwd, bwd = chains(pt[:, :, 0]), chains(pt[:, :, 1])
    bwd = jnp.concatenate([bwd[:n_ctx_chunks][::-1], bwd[n_ctx_chunks:][::-1]], axis=0)
    return jnp.concatenate([fwd, bwd], axis=-1)


def _wkv_scan(p, k, b, a, r, v):
    n_chunks, rows, chains = k.shape
    spec = pl.BlockSpec((1, rows, chains), lambda s: (s, 0, 0))
    return pl.pallas_call(
        _scan_kernel,
        grid=(n_chunks,),
        in_specs=[pl.BlockSpec((1, HEAD_DIM, chains), lambda s: (s, 0, 0))] + [spec] * 5,
        out_specs=spec,
        out_shape=jax.ShapeDtypeStruct(k.shape, F32),
        scratch_shapes=[pltpu.VMEM((HEAD_DIM, HEAD_DIM, chains), F32)],
        compiler_params=_params("arbitrary"),
        name="wkv_scan",
    )(p, k, b, a, r, v)


def _flip_rows(x, flip):
    hi, mid, lo = _split3(x)
    return (jnp.dot(flip, hi, preferred_element_type=F32) + jnp.dot(flip, mid, preferred_element_type=F32)
            + jnp.dot(flip, lo, preferred_element_type=F32))


def _mirror_chunk(c, n_ctx_chunks, n_chunks):
    return jnp.where(c < n_ctx_chunks, n_ctx_chunks - 1 - c, n_ctx_chunks + n_chunks - 1 - c)


def _reverse_backward(rows, flip, n_fwd):
    hi, mid, lo = _split3(rows[n_fwd:])
    back = (jnp.dot(hi, flip, preferred_element_type=F32) + jnp.dot(mid, flip, preferred_element_type=F32)
            + jnp.dot(lo, flip, preferred_element_type=F32))
    return jnp.concatenate([rows[:n_fwd], back], axis=0)


def _to_scan_kernel(zf_ref, zb_ref, flip_ref, o_ref, *, step_major):
    nb = zf_ref.shape[0]
    flip = flip_ref[...]
    sub = RELAYOUT_ROWS // SCAN_STEPS

    def body(n, carry):
        rows = pl.ds(pl.multiple_of(n * N_HEADS, N_HEADS), N_HEADS)
        slabs = [zf_ref[b, rows, :] for b in range(nb)] + [zb_ref[b, rows, :] for b in range(nb)]
        tile = _reverse_backward(jnp.concatenate(slabs, axis=0), flip, nb * N_HEADS).T
        for q in range(sub):
            if step_major:
                dst = pl.ds(n, SCAN_STEPS, stride=HEAD_DIM)
            else:
                dst = pl.ds(pl.multiple_of(n * SCAN_STEPS, SCAN_STEPS), SCAN_STEPS)
            o_ref[q, dst, :] = tile[q * SCAN_STEPS:(q + 1) * SCAN_STEPS]
        return carry

    lax.fori_loop(0, HEAD_DIM, body, 0, unroll=RELAYOUT_UNROLL)


def _to_scan(z_fwd, z_bwd, flip, n_ctx, step_major=False):
    B, _, R = z_fwd.shape
    n_chunks = R // RELAYOUT_ROWS
    n_ctx_chunks = n_ctx // RELAYOUT_ROWS
    sub = RELAYOUT_ROWS // SCAN_STEPS
    chains = 2 * B * N_HEADS
    return pl.pallas_call(
        functools.partial(_to_scan_kernel, step_major=step_major),
        grid=(n_chunks,),
        in_specs=[pl.BlockSpec((B, W_BRANCH, RELAYOUT_ROWS), lambda c: (0, 0, c)),
                  pl.BlockSpec((B, W_BRANCH, RELAYOUT_ROWS),
                               lambda c: (0, 0, _mirror_chunk(c, n_ctx_chunks, n_chunks))),
                  pl.BlockSpec((RELAYOUT_ROWS, RELAYOUT_ROWS), lambda c: (0, 0))],
        out_specs=pl.BlockSpec((sub, HEAD_DIM * SCAN_STEPS, chains), lambda c: (c, 0, 0)),
        out_shape=jax.ShapeDtypeStruct((R // SCAN_STEPS, HEAD_DIM * SCAN_STEPS, chains), F32),
        compiler_params=_params("parallel"),
        name="to_scan_layout",
    )(z_fwd, z_bwd, flip)


def _from_scan_kernel(y_ref, flip_ref, yf_ref, yb_ref):
    nb = yf_ref.shape[0]
    flip = flip_ref[...]
    sub = RELAYOUT_ROWS // SCAN_STEPS

    def body(n, carry):
        tile = jnp.concatenate([y_ref[q, pl.ds(n, SCAN_STEPS, stride=HEAD_DIM), :] for q in range(sub)], axis=0)
        tile = _reverse_backward(tile.T, flip, nb * N_HEADS)
        rows = pl.ds(pl.multiple_of(n * N_HEADS, N_HEADS), N_HEADS)
        for b in range(nb):
            yf_ref[b, rows, :] = tile[b * N_HEADS:(b + 1) * N_HEADS]
            yb_ref[b, rows, :] = tile[(nb + b) * N_HEADS:(nb + b + 1) * N_HEADS]
        return carry

    lax.fori_loop(0, HEAD_DIM, body, 0, unroll=RELAYOUT_UNROLL)


def _from_scan(y, flip, n_batch, n_ctx):
    R = y.shape[0] * SCAN_STEPS
    n_chunks = R // RELAYOUT_ROWS
    n_ctx_chunks = n_ctx // RELAYOUT_ROWS
    sub = RELAYOUT_ROWS // SCAN_STEPS
    out = jax.ShapeDtypeStruct((n_batch, W_BRANCH, R), F32)
    return pl.pallas_call(
        _from_scan_kernel,
        grid=(n_chunks,),
        in_specs=[pl.BlockSpec((sub, HEAD_DIM * SCAN_STEPS, y.shape[2]), lambda c: (c, 0, 0)),
                  pl.BlockSpec((RELAYOUT_ROWS, RELAYOUT_ROWS), lambda c: (0, 0))],
        out_specs=[pl.BlockSpec((n_batch, W_BRANCH, RELAYOUT_ROWS), lambda c: (0, 0, c)),
                   pl.BlockSpec((n_batch, W_BRANCH, RELAYOUT_ROWS),
                                lambda c: (0, 0, _mirror_chunk(c, n_ctx_chunks, n_chunks)))],
        out_shape=[out, out],
        compiler_params=_params("parallel"),
        name="from_scan_layout",
    )(y, flip)


def _rwkv_readout_kernel(yf_ref, yb_ref, bonus_ref, gate_ref, gb_ref, o_ref):
    y = (yf_ref[0] + yb_ref[0]).T
    mu = _head_sum(y) * (1.0 / HEAD_DIM)
    yc = y - mu
    var = _head_sum(yc * yc) * (1.0 / HEAD_DIM)
    gb = gb_ref[...]
    out = yc * lax.rsqrt(var + LNX_EPS) * gb[0:1, :] + gb[1:2, :] + bonus_ref[0]
    o_ref[0] = (out * _silu(gate_ref[0])).astype(o_ref.dtype)


def _rwkv_readout(y_fwd, y_bwd, bonus, p_rw, lnx_gb):
    B, R, _ = bonus.shape
    tile = pl.BlockSpec((1, ROW_TILE, W_BRANCH), lambda b, i: (b, i, 0))
    tile_t = pl.BlockSpec((1, W_BRANCH, ROW_TILE), lambda b, i: (b, 0, i))
    return pl.pallas_call(
        _rwkv_readout_kernel,
        grid=(B, R // ROW_TILE),
        in_specs=[tile_t, tile_t, tile,
                  pl.BlockSpec((1, ROW_TILE, W_BRANCH), lambda b, i: (b, i, COL_RW_GATE)),
                  pl.BlockSpec((8, W_BRANCH), lambda b, i: (0, 0))],
        out_specs=tile,
        out_shape=jax.ShapeDtypeStruct((B, R, W_BRANCH), BF16),
        compiler_params=_params("parallel", "parallel"),
        name="rwkv_readout",
    )(y_fwd, y_bwd, bonus, p_rw, lnx_gb)


def _merge_kernel(na_ref, pool_ref, rw_ref, lna_ref, lpool_ref, lrw_ref, w_ref, o_ref):
    acc = None
    for br, (x_ref, l_ref) in enumerate(((na_ref, lna_ref), (pool_ref, lpool_ref), (rw_ref, lrw_ref))):
        t = _sigmoid(l_ref[...]) * jnp.dot(x_ref[...], w_ref[br], preferred_element_type=F32)
        acc = t if acc is None else acc + t
    o_ref[...] = acc.astype(o_ref.dtype)


def _merge(b_na, b_pool, b_rw, p_merge, w_branch):
    M = b_na.shape[0]
    tm, tn = min(_row_tile(M), 512), 1024
    nb = D_MODEL // tn
    x_spec = pl.BlockSpec((tm, W_BRANCH), lambda j, i: (i, 0))

    def logit(br):
        return pl.BlockSpec((tm, tn), lambda j, i: (i, br * nb + j))

    return pl.pallas_call(
        _merge_kernel,
        grid=(nb, M // tm),
        in_specs=[x_spec, x_spec, x_spec, logit(0), logit(1), logit(2),
                  pl.BlockSpec((N_BRANCH, W_BRANCH, tn), lambda j, i: (0, 0, j))],
        out_specs=pl.BlockSpec((tm, tn), lambda j, i: (i, j)),
        out_shape=jax.ShapeDtypeStruct((M, D_MODEL), BF16),
        compiler_params=_params("parallel", "parallel"),
        name="branch_merge",
    )(b_na, b_pool, b_rw, p_merge, p_merge, p_merge, w_branch)


def _out_kernel(m_ref, w_ref, c_ref, l_ref, mod_ref, fg_ref, o_ref, *, n_ctx_tiles, tile_offset, final):
    tile = pl.program_id(1) + tile_offset
    gate = _mod_row(mod_ref, tile, n_ctx_tiles)[:, 2 * D_MODEL:]
    x = _stream_tile(c_ref, l_ref, tile, n_ctx_tiles)
    x = x + gate * jnp.dot(m_ref[0], w_ref[...], preferred_element_type=F32)
    o_ref[0] = _rms(x, fg_ref[...]) if final else x


def _out_proj(merged, w_out, stream, mod, final_g, n_ctx, final):
    B, R, _ = merged.shape
    n_ctx_tiles = n_ctx // ROW_TILE
    off = n_ctx_tiles if final else 0
    return pl.pallas_call(
        functools.partial(_out_kernel, n_ctx_tiles=n_ctx_tiles, tile_offset=off, final=final),
        grid=(B, R // ROW_TILE - off),
        in_specs=[pl.BlockSpec((1, ROW_TILE, D_MODEL), lambda b, i: (b, i + off, 0)),
                  pl.BlockSpec((D_MODEL, D_MODEL), lambda b, i: (0, 0))]
        + _stream_specs(stream, n_ctx_tiles, off) + [
            pl.BlockSpec((MOD_ROWS, 3 * D_MODEL), lambda b, i: (0, 0)),
            pl.BlockSpec((1, D_MODEL), lambda b, i: (0, 0))],
        out_specs=pl.BlockSpec((1, ROW_TILE, D_MODEL), lambda b, i: (b, i, 0)),
        out_shape=jax.ShapeDtypeStruct((B, R - off * ROW_TILE, D_MODEL), F32),
        compiler_params=_params("parallel", "parallel"),
        name="out_proj_final" if final else "out_proj",
    )(merged, w_out, stream[0], stream[1], mod, final_g)


def _layer(stream, R, mod, n_ctx, final, final_g, norm_g, w_in_all, layer, na_rpb, pool_w, pool_scale, rw_mu, rw_w0, rw_w2,
           rw_a0, rw_a2, rw_k_k, rw_k_a, rw_r_k, rw_lnx_g, rw_lnx_b, w_branch, w_out):
    B = stream[0].shape[0]
    rows = (R - n_ctx) // GRID_W

    h = _norm_mod(stream, R, norm_g[None], mod, n_ctx).reshape(B * R, D_MODEL)
    lo = N_MAIN + 2 * RWKV_LORA
    p_ap = _matmul(h, w_in_all, F32, "in_proj_attn_pool", n_cols=N_ATTN_POOL, layer=layer)
    p_ap = p_ap.reshape(B, R, N_ATTN_POOL)
    w_rw = _head_major(w_in_all[layer, :, N_ATTN_POOL:N_MAIN].reshape(D_MODEL, -1, W_BRANCH)).reshape(D_MODEL, -1)
    p_rw = _matmul(h, w_rw.astype(BF16), F32, "in_proj_rwkv").reshape(B, R, N_MAIN - N_ATTN_POOL)
    p_lora = _matmul(h, w_in_all, F32, "in_proj_lora", n_cols=2 * RWKV_LORA, layer=layer, first_col=N_MAIN)
    p_lora = p_lora.reshape(B, R, 2 * RWKV_LORA)
    p_merge = _matmul(h, w_in_all, F32, "in_proj_merge", n_cols=N_BRANCH * D_MODEL, layer=layer, first_col=lo)

    b_na = _na_attention(p_ap, _na_bias_tables(na_rpb, rows), n_ctx)
    b_pool = _pool(p_ap, pool_w.astype(BF16), pool_scale[None], n_ctx)

    par = jnp.zeros((P_ROWS, W_BRANCH), F32)
    par = par.at[P_MU_R:P_MU_V + 1].set(rw_mu).at[P_W0_F:P_W0_B + 1].set(rw_w0).at[P_A0_F:P_A0_B + 1].set(rw_a0)
    par = _head_major(par.at[P_K_K].set(rw_k_k).at[P_K_A].set(rw_k_a).at[P_R_K].set(rw_r_k.reshape(-1)))
    zeros = jnp.zeros_like(rw_w2)
    w2 = _head_major(jnp.concatenate([rw_w2, zeros], axis=1)).astype(BF16)
    a2 = _head_major(jnp.concatenate([zeros, rw_a2], axis=1)).astype(BF16)
    v, k_f, b_f, a_f, r_f, k_b, b_b, a_b, r_b, ptot, bonus = _rwkv_features(p_rw, p_lora, par, w2, a2, n_ctx)
    flip = jnp.asarray(np.eye(RELAYOUT_ROWS)[::-1], BF16)
    y = _wkv_scan(_chunk_decay_to_scan(ptot, n_ctx),
                  *[_to_scan(zf, zb, flip, n_ctx) for zf, zb in ((k_f, k_b), (b_f, b_b), (a_f, a_b), (r_f, r_b))],
                  _to_scan(v, v, flip, n_ctx, step_major=True))
    y_fwd, y_bwd = _from_scan(y, flip, B, n_ctx)
    lnx_gb = _head_major(jnp.zeros((8, W_BRANCH), F32).at[0].set(rw_lnx_g).at[1].set(rw_lnx_b))
    b_rw = _rwkv_readout(y_fwd, y_bwd, bonus, p_rw, lnx_gb)

    def flat(z):
        return z.reshape(B * R, W_BRANCH)

    w_rw_out = w_branch[2].reshape(N_HEADS, HEAD_DIM, D_MODEL).swapaxes(0, 1).reshape(W_BRANCH, D_MODEL)
    w_br = jnp.stack([w_branch[0], w_branch[1], w_rw_out]).astype(BF16)
    merged = _merge(flat(b_na), flat(b_pool), flat(b_rw), p_merge, w_br)
    return _out_proj(merged.reshape(B, R, D_MODEL), w_out.astype(BF16), stream, mod, final_g[None], n_ctx, final)


def kernel(x, c, ctx, c_ctx, norm_g, w_mod, b_mod, w_in, na_rpb, pool_w, pool_scale, rw_mu, rw_w0, rw_w2, rw_a0,
           rw_a2, rw_k_k, rw_k_a, rw_r_k, rw_lnx_g, rw_lnx_b, w_branch, w_out, final_g):
    B, T, _ = x.shape
    n_ctx = ctx.shape[1]
    depth = w_in.shape[0]
    assert B <= CTX_MOD_ROW and n_ctx % ROW_TILE == 0 and T % ROW_TILE == 0
    assert ROW_TILE % RELAYOUT_ROWS == 0 and RELAYOUT_ROWS % SCAN_STEPS == 0

    cond = jnp.zeros((MOD_ROWS, D_MODEL), F32).at[:B].set(c).at[CTX_MOD_ROW].set(c_ctx)
    mods = _modulation(cond, w_mod, b_mod[:, None, :])
    stream = (ctx, x, n_ctx // ROW_TILE)
    for layer in range(depth):
        out = _layer(stream, n_ctx + T, mods[layer], n_ctx, layer == depth - 1, final_g, norm_g[layer], w_in, layer,
                     na_rpb[layer], pool_w[layer], pool_scale[layer], rw_mu[layer], rw_w0[layer], rw_w2[layer],
                     rw_a0[layer], rw_a2[layer], rw_k_k[layer], rw_k_a[layer], rw_r_k[layer], rw_lnx_g[layer],
                     rw_lnx_b[layer], w_branch[layer], w_out[layer])
        stream = (out, out, 0)
    return out
```

```python
import functools

import numpy as np
import jax
import jax.numpy as jnp
from jax import lax
from jax.experimental import pallas as pl
from jax.experimental.pallas import tpu as pltpu

F32 = jnp.float32
BF16 = jnp.bfloat16

D_MODEL = 2048
W_BRANCH = D_MODEL // 2
N_BRANCH = 3
N_HEADS = 16
HEAD_DIM = 64
GRID_W = 64
NA_WIN_H = 8
NA_WIN_W = 16
POOL_WINDOWS = (2, 4, 8, 16)
POOL_GROUP_DIM = W_BRANCH // len(POOL_WINDOWS)
POOL_HALO = max(POOL_WINDOWS) // 2
RWKV_LORA = 64
RMS_EPS = 1e-6
LNX_EPS = 64e-5
NEG_INF = -1e30

LANES = 128
ROW_TILE = 256
NA_Q_ROWS = ROW_TILE // GRID_W
NA_K_ROWS = NA_Q_ROWS + NA_WIN_H
NA_K_TOK = NA_K_ROWS * GRID_W
CTX_MOD_ROW = 4
MOD_ROWS = 8
SCAN_STEPS = 64
SCAN_J_UNROLL = 64
RELAYOUT_UNROLL = 32
RELAYOUT_ROWS = 128
VMEM_LIMIT = 56 << 20

COL_Q, COL_K, COL_V = range(3)
COL_NA_GATE, COL_POOL_U, COL_POOL_GATE = range(3)
N_QKV = 3 * W_BRANCH
COL_RW_R, COL_RW_K, COL_RW_V, COL_RW_GATE = range(4)
N_ATTN_POOL = 6 * W_BRANCH
N_MAIN = 10 * W_BRANCH

P_MU_R, P_MU_K, P_MU_V, P_W0_F, P_W0_B, P_A0_F, P_A0_B, P_K_K, P_K_A, P_R_K = range(10)
P_ROWS = 16


def _params(*sem):
    return pltpu.CompilerParams(dimension_semantics=sem, vmem_limit_bytes=VMEM_LIMIT)


def _sigmoid(x):
    return 1.0 / (1.0 + jnp.exp(-x))


def _silu(x):
    return x * _sigmoid(x)


def _split3(x):
    hi = x.astype(BF16)
    r1 = x - hi.astype(F32)
    mid = r1.astype(BF16)
    lo = (r1 - mid.astype(F32)).astype(BF16)
    return hi, mid, lo


def _head_major(z):
    lead = z.shape[:-1]
    return z.reshape(lead + (N_HEADS, HEAD_DIM)).swapaxes(-1, -2).reshape(lead + (W_BRANCH,))


def _head_sum(x):
    n_tiles = W_BRANCH // LANES
    part = x[:, :LANES]
    for c in range(1, n_tiles):
        part = part + x[:, c * LANES:(c + 1) * LANES]
    shift = N_HEADS
    while shift < LANES:
        part = part + pltpu.roll(part, shift, 1)
        shift *= 2
    return jnp.concatenate([part] * n_tiles, axis=1)


def _mod_kernel(cond_ref, w_ref, b_ref, o_ref):
    s = _silu(cond_ref[...])
    o_ref[0] = jnp.dot(s.astype(BF16), w_ref[0].astype(BF16), preferred_element_type=F32) + b_ref[0]


def _modulation(cond, w_mod, b_mod):
    n_layers = w_mod.shape[0]
    tn = 3 * D_MODEL // 4
    return pl.pallas_call(
        _mod_kernel,
        grid=(n_layers, 4),
        in_specs=[pl.BlockSpec((MOD_ROWS, D_MODEL), lambda l, j: (0, 0)),
                  pl.BlockSpec((1, D_MODEL, tn), lambda l, j: (l, 0, j)),
                  pl.BlockSpec((1, 1, tn), lambda l, j: (l, 0, j))],
        out_specs=pl.BlockSpec((1, MOD_ROWS, tn), lambda l, j: (l, 0, j)),
        out_shape=jax.ShapeDtypeStruct((n_layers, MOD_ROWS, 3 * D_MODEL), F32),
        compiler_params=_params("arbitrary", "arbitrary"),
        name="adaln_modulation",
    )(cond, w_mod, b_mod)


def _mod_row(mod_ref, tile, n_ctx_tiles):
    row = jnp.where(tile < n_ctx_tiles, CTX_MOD_ROW, pl.program_id(0))
    return mod_ref[pl.ds(row, 1), :]


def _rms(x, g):
    return x * lax.rsqrt(jnp.mean(x * x, axis=-1, keepdims=True) + RMS_EPS) * g


def _stream_specs(stream, n_ctx_tiles, tile_offset=0):
    lat_shift = stream[2]
    return [pl.BlockSpec((1, ROW_TILE, D_MODEL), lambda b, i: (b, jnp.minimum(i + tile_offset, n_ctx_tiles - 1), 0)),
            pl.BlockSpec((1, ROW_TILE, D_MODEL), lambda b, i: (b, jnp.maximum(i + tile_offset - lat_shift, 0), 0))]


def _stream_tile(c_ref, l_ref, tile, n_ctx_tiles):
    return jnp.where(tile < n_ctx_tiles, c_ref[0], l_ref[0])


def _norm_mod_kernel(c_ref, l_ref, g_ref, mod_ref, h_ref, *, n_ctx_tiles):
    tile = pl.program_id(1)
    m = _mod_row(mod_ref, tile, n_ctx_tiles)
    shift = m[:, :D_MODEL]
    scale = m[:, D_MODEL:2 * D_MODEL]
    x = _stream_tile(c_ref, l_ref, tile, n_ctx_tiles)
    h_ref[0] = (_rms(x, g_ref[...]) * (1.0 + scale) + shift).astype(BF16)


def _norm_mod(stream, n_rows, norm_g, mod, n_ctx):
    B = stream[0].shape[0]
    n_ctx_tiles = n_ctx // ROW_TILE
    return pl.pallas_call(
        functools.partial(_norm_mod_kernel, n_ctx_tiles=n_ctx_tiles),
        grid=(B, n_rows // ROW_TILE),
        in_specs=_stream_specs(stream, n_ctx_tiles) + [
            pl.BlockSpec((1, D_MODEL), lambda b, i: (0, 0)),
            pl.BlockSpec((MOD_ROWS, 3 * D_MODEL), lambda b, i: (0, 0))],
        out_specs=pl.BlockSpec((1, ROW_TILE, D_MODEL), lambda b, i: (b, i, 0)),
        out_shape=jax.ShapeDtypeStruct((B, n_rows, D_MODEL), BF16),
        compiler_params=_params("parallel", "parallel"),
        name="norm_modulate",
    )(stream[0], stream[1], norm_g, mod)


def _mm_kernel(a_ref, w_ref, o_ref):
    o_ref[...] = jnp.dot(a_ref[...], w_ref[...], preferred_element_type=F32).astype(o_ref.dtype)


def _row_tile(m):
    for t in (1024, 512, 256):
        if m % t == 0:
            return t
    raise ValueError(f"row count {m} is not a multiple of {ROW_TILE}")


def _mm_cast_kernel(a_ref, w_ref, *rest, permute):
    perm_ref, o_ref, wb_ref = rest if permute else (None,) + rest

    @pl.when(pl.program_id(1) == 0)
    def _():
        wb = w_ref[...].astype(BF16)
        if permute:
            wb = jnp.dot(wb, perm_ref[...], preferred_element_type=F32).astype(BF16)
        wb_ref[...] = wb

    o_ref[...] = jnp.dot(a_ref[...], wb_ref[...], preferred_element_type=F32).astype(o_ref.dtype)


def _matmul(a, w, out_dtype, name, n_cols=None, layer=None, first_col=0, perm=None):
    M, K = a.shape
    N = w.shape[-1] if n_cols is None else n_cols
    tm = _row_tile(M)
    tn = min(N, 1024)
    assert first_col % tn == 0
    j0 = first_col // tn
    cast = w.dtype != BF16
    assert cast or perm is None
    if layer is None:
        w_spec = pl.BlockSpec((K, tn), lambda j, i: (0, j0 + j))
    else:
        w_spec = pl.BlockSpec((None, K, tn), lambda j, i: (layer, 0, j0 + j))
    in_specs = [pl.BlockSpec((tm, K), lambda j, i: (i, 0)), w_spec]
    args = [a, w]
    if perm is not None:
        in_specs.append(pl.BlockSpec((tn, tn), lambda j, i: (0, 0)))
        args.append(perm)
    return pl.pallas_call(
        functools.partial(_mm_cast_kernel, permute=perm is not None) if cast else _mm_kernel,
        grid=(N // tn, M // tm),
        in_specs=in_specs,
        out_specs=pl.BlockSpec((tm, tn), lambda j, i: (i, j)),
        out_shape=jax.ShapeDtypeStruct((M, N), out_dtype),
        scratch_shapes=[pltpu.VMEM((K, tn), BF16)] if cast else [],
        compiler_params=_params("parallel", "arbitrary" if cast else "parallel"),
        name=name,
    )(*args)


def _na_bias_tables(rpb, rows):
    n_blocks = rows // NA_Q_ROWS
    n_off = 2 * NA_WIN_H - 1
    col = np.arange(GRID_W)
    c0 = np.clip(col - NA_WIN_W // 2, 0, GRID_W - NA_WIN_W)
    valid_c = (col[None, :] >= c0[:, None]) & (col[None, :] < c0[:, None] + NA_WIN_W)
    col_off = np.clip(col[None, :] - col[:, None] + NA_WIN_W - 1, 0, 2 * NA_WIN_W - 2)
    pick_c = jnp.asarray(np.eye(2 * NA_WIN_W - 1)[col_off], F32)
    tile = jnp.einsum("hrc,qpc->hrqp", rpb.astype(F32), pick_c, precision=lax.Precision.HIGHEST)
    tile = jnp.where(valid_c, tile, NEG_INF)
    tile = jnp.concatenate([tile, jnp.full((N_HEADS, 1, GRID_W, GRID_W), NEG_INF, F32)], axis=1)
    tile = jnp.concatenate([tile, tile], axis=-1)
    picks = []
    for m in (0, 1, n_blocks - 1):
        q_row = NA_Q_ROWS * m + np.arange(NA_Q_ROWS)
        k_row = int(np.clip(NA_Q_ROWS * m - NA_Q_ROWS, 0, rows - NA_K_ROWS)) + np.arange(NA_K_ROWS)
        r0 = np.clip(q_row - NA_WIN_H // 2, 0, rows - NA_WIN_H)
        valid_r = (k_row[None, :] >= r0[:, None]) & (k_row[None, :] < r0[:, None] + NA_WIN_H)
        row_off = np.clip(k_row[None, :] - q_row[:, None] + NA_WIN_H - 1, 0, n_off - 1)
        picks.append(np.where(valid_r, row_off, n_off))
    picks = np.stack(picks)

    def build(t_ref, o_ref):
        left = lax.broadcasted_iota(jnp.int32, (1, LANES), 1) < GRID_W
        for ty in range(picks.shape[0]):
            for a in range(NA_Q_ROWS):
                for kp in range(NA_K_ROWS // 2):
                    pair = jnp.where(left, t_ref[0, int(picks[ty, a, 2 * kp])], t_ref[0, int(picks[ty, a, 2 * kp + 1])])
                    o_ref[ty, 0, a * GRID_W:(a + 1) * GRID_W, kp * LANES:(kp + 1) * LANES] = pair

    return pl.pallas_call(
        build,
        grid=(N_HEADS,),
        in_specs=[pl.BlockSpec((1, n_off + 1, GRID_W, LANES), lambda h: (h, 0, 0, 0))],
        out_specs=pl.BlockSpec((picks.shape[0], 1, ROW_TILE, NA_K_TOK), lambda h: (0, h, 0, 0)),
        out_shape=jax.ShapeDtypeStruct((picks.shape[0], N_HEADS, ROW_TILE, NA_K_TOK), F32),
        compiler_params=_params("parallel"),
        name="na_bias_tables",
    )(tile)


def _attend(qe, keys, vals, biases):
    dn = (((1,), (1,)), ((), ()))
    scores = []
    for kk, bias in zip(keys, biases):
        s = lax.dot_general(qe, kk, dn, preferred_element_type=F32)
        scores.append(s if bias is None else s + bias)
    m = scores[0].max(axis=-1, keepdims=True)
    for s in scores[1:]:
        m = jnp.maximum(m, s.max(axis=-1, keepdims=True))
    num, den = None, None
    for s, vv in zip(scores, vals):
        p = jnp.exp(s - m)
        l = p.sum(axis=-1, keepdims=True)
        o = jnp.dot(p.astype(BF16), vv, preferred_element_type=F32)
        num = o if num is None else num + o
        den = l if den is None else den + l
    return num / den


def _na_kernel(q_ref, k_ref, v_ref, g_ref, bias_ref, o_ref, *, n_ctx, rows):
    j = pl.program_id(2)
    lane = lax.broadcasted_iota(jnp.int32, (1, LANES), 1)
    in_head = (lane < HEAD_DIM, lane >= HEAD_DIM)
    q = q_ref[0] * (HEAD_DIM ** -0.5)
    kc = k_ref[0, 0:n_ctx, :]
    vc = v_ref[0, 0:n_ctx, :]

    def heads(q):
        return [jnp.where(in_head[e], q, jnp.zeros_like(q)) for e in range(2)]

    def finish(o0, o1):
        o = jnp.where(in_head[0], o0, o1)
        o_ref[0] = (o * _silu(g_ref[0])).astype(o_ref.dtype)

    @pl.when(j == 0)
    def _():
        finish(*[_attend(qe, [kc], [vc], [None]) for qe in heads(q)])

    @pl.when(j > 0)
    def _():
        k_row = jnp.clip(NA_Q_ROWS * (j - 1) - NA_Q_ROWS, 0, rows - NA_K_ROWS)
        start = pl.multiple_of(n_ctx + k_row * GRID_W, GRID_W)
        kw = k_ref[0, pl.ds(start, NA_K_TOK), :]
        vw = v_ref[0, pl.ds(start, NA_K_TOK), :]
        finish(*[_attend(qe, [kw, kc], [vw, vc], [bias_ref[0, e], None])
                 for e, qe in enumerate(heads(q))])


def _na_attention(p_qkv, p_gp, bias_tables, n_ctx):
    B, R, _ = p_qkv.shape
    rows = (R - n_ctx) // GRID_W
    n_blocks = rows // NA_Q_ROWS
    pairs = W_BRANCH // LANES
    assert n_ctx == ROW_TILE and rows >= NA_K_ROWS and rows % NA_Q_ROWS == 0

    def col(c):
        return lambda b, hp, j: (b, 0, c * pairs + hp)

    def bias_idx(b, hp, j):
        return (jnp.where(j <= 1, 0, jnp.where(j == n_blocks, 2, 1)), hp, 0, 0)

    return pl.pallas_call(
        functools.partial(_na_kernel, n_ctx=n_ctx, rows=rows),
        grid=(B, pairs, n_blocks + 1),
        in_specs=[pl.BlockSpec((1, ROW_TILE, LANES), lambda b, hp, j: (b, j, COL_Q * pairs + hp)),
                  pl.BlockSpec((1, R, LANES), col(COL_K)),
                  pl.BlockSpec((1, R, LANES), col(COL_V)),
                  pl.BlockSpec((1, ROW_TILE, LANES), lambda b, hp, j: (b, j, COL_NA_GATE * pairs + hp)),
                  pl.BlockSpec((1, 2, ROW_TILE, NA_K_TOK), bias_idx)],
        out_specs=pl.BlockSpec((1, ROW_TILE, LANES), lambda b, hp, j: (b, j, hp)),
        out_shape=jax.ShapeDtypeStruct((B, R, W_BRANCH), BF16),
        compiler_params=_params("parallel", "parallel", "arbitrary"),
        name="neighbourhood_attention",
    )(p_qkv, p_qkv, p_qkv, p_gp, bias_tables)


def _pool_kernel(u_ref, g_ref, w_ref, sc_ref, o_ref, pad_ref, *, n_ctx, n_lat):
    grp = pl.program_id(1)
    w = w_ref[0]
    scale = sc_ref[...]

    def run(win):
        half = win // 2
        for seq_start, seq_len in ((0, n_ctx), (n_ctx, n_lat)):
            zeros = jnp.zeros((POOL_HALO, POOL_GROUP_DIM), F32)
            pad_ref[0:POOL_HALO, :] = zeros
            pad_ref[POOL_HALO:POOL_HALO + seq_len, :] = u_ref[0, seq_start:seq_start + seq_len, :]
            pad_ref[POOL_HALO + seq_len:2 * POOL_HALO + seq_len, :] = zeros

            def chunk(c, carry):
                base = pl.multiple_of(c * ROW_TILE, ROW_TILE)
                x = pad_ref[pl.ds(base, ROW_TILE + 2 * POOL_HALO), :]
                acc = x[POOL_HALO - half:POOL_HALO - half + ROW_TILE]
                for o in range(-half + 1, half):
                    acc = acc + x[POOL_HALO + o:POOL_HALO + o + ROW_TILE]
                t = base + lax.broadcasted_iota(jnp.int32, (ROW_TILE, 1), 0)
                cnt = jnp.minimum(t + half, seq_len) - jnp.maximum(t - half, 0)
                diff = acc / cnt.astype(F32) - x[POOL_HALO:POOL_HALO + ROW_TILE]
                y = jnp.dot(diff.astype(BF16), w, preferred_element_type=F32) * scale
                rows = pl.ds(seq_start + base, ROW_TILE)
                o_ref[0, rows, :] = (y * _silu(g_ref[0, rows, :])).astype(o_ref.dtype)
                return carry

            lax.fori_loop(0, seq_len // ROW_TILE, chunk, 0)

    for gi, win in enumerate(POOL_WINDOWS):
        pl.when(grp == gi)(functools.partial(run, win))


def _pool(p_gp, pool_w, pool_scale, n_ctx):
    B, R, _ = p_gp.shape
    groups = len(POOL_WINDOWS)
    return pl.pallas_call(
        functools.partial(_pool_kernel, n_ctx=n_ctx, n_lat=R - n_ctx),
        grid=(B, groups),
        in_specs=[pl.BlockSpec((1, R, POOL_GROUP_DIM), lambda b, g: (b, 0, COL_POOL_U * groups + g)),
                  pl.BlockSpec((1, R, POOL_GROUP_DIM), lambda b, g: (b, 0, COL_POOL_GATE * groups + g)),
                  pl.BlockSpec((1, POOL_GROUP_DIM, POOL_GROUP_DIM), lambda b, g: (g, 0, 0)),
                  pl.BlockSpec((1, POOL_GROUP_DIM), lambda b, g: (0, g))],
        out_specs=pl.BlockSpec((1, R, POOL_GROUP_DIM), lambda b, g: (b, 0, g)),
        out_shape=jax.ShapeDtypeStruct((B, R, W_BRANCH), BF16),
        scratch_shapes=[pltpu.VMEM((R - n_ctx + 2 * POOL_HALO, POOL_GROUP_DIM), F32)],
        compiler_params=_params("parallel", "arbitrary"),
        name="multiscale_pool",
    )(p_gp, p_gp, pool_w, pool_scale)


def _rwkv_feat_kernel(r_ref, rp_ref, rn_ref, k_ref, kp_ref, kn_ref, v_ref, vp_ref, vn_ref, lora_ref,
                      par_ref, w2_ref, a2_ref, tri_ref,
                      vo_ref, kf_ref, bf_ref, af_ref, rf_ref, kb_ref, bb_ref, ab_ref, rb_ref, ptot_ref, bonus_ref,
                      *, n_ctx_tiles, n_tiles):
    i = pl.program_id(1)
    first = (i == 0) | (i == n_ctx_tiles)
    last = (i == n_ctx_tiles - 1) | (i == n_tiles - 1)
    row = lax.broadcasted_iota(jnp.int32, (ROW_TILE, 1), 0)
    par = par_ref[...]

    def prm(p):
        return par[p:p + 1, :]

    def mix(z_ref, prev_ref, next_ref, mu):
        z = z_ref[0]
        prev = jnp.where(first, 0.0, prev_ref[0, 7:8, :])
        nxt = jnp.where(last, 0.0, next_ref[0, 0:1, :])
        z_prev = jnp.where(row == 0, prev, pltpu.roll(z, 1, 0))
        z_next = jnp.where(row == ROW_TILE - 1, nxt, pltpu.roll(z, ROW_TILE - 1, 0))
        return z + mu * (0.5 * (z_prev + z_next) - z)

    r = mix(r_ref, rp_ref, rn_ref, prm(P_MU_R))
    k = mix(k_ref, kp_ref, kn_ref, prm(P_MU_K))
    v = mix(v_ref, vp_ref, vn_ref, prm(P_MU_V))
    vo_ref[0] = v.T

    kk = k * prm(P_K_K)
    kk = kk * jnp.minimum(lax.rsqrt(_head_sum(kk * kk)), 1e12)

    lora = lora_ref[0]
    lane = lax.broadcasted_iota(jnp.int32, (1, LANES), 1)
    lora = jnp.where(lane < RWKV_LORA, jnp.tanh(lora), lora).astype(BF16)
    k_sum = None
    outs = ((kf_ref, bf_ref, af_ref, rf_ref), (kb_ref, bb_ref, ab_ref, rb_ref))
    chunk_decay = []
    for d, (k_out, b_out, a_out, r_out) in enumerate(outs):
        x = prm(P_W0_F + d) + jnp.dot(lora, w2_ref[d], preferred_element_type=F32)
        w_log = -(jnp.maximum(-x, 0.0) + jnp.log(1.0 + jnp.exp(-jnp.abs(x)))) - 0.5
        neg_log_w = jnp.exp(w_log)
        hi, mid, lo = _split3(neg_log_w)
        tri = tri_ref[d]
        cs = (jnp.dot(tri, hi, preferred_element_type=F32) + jnp.dot(tri, mid, preferred_element_type=F32)
              + jnp.dot(tri, lo, preferred_element_type=F32))
        grow = jnp.exp(cs)
        shrink = jnp.exp(-cs)
        a = 0.5 + 0.5 * jnp.tanh(0.5 * (prm(P_A0_F + d) + jnp.dot(lora, a2_ref[d], preferred_element_type=F32)))
        k_d = k * (1.0 + (a - 1.0) * prm(P_K_A))
        for q in range(ROW_TILE // SCAN_STEPS):
            last = q * SCAN_STEPS + (SCAN_STEPS - 1 if d == 0 else 0)
            chunk_decay.append(shrink[last:last + 1])
        k_out[0] = (k_d * grow).T
        b_out[0] = (kk * a * grow).T
        a_out[0] = (-kk * jnp.exp(neg_log_w - cs)).T
        r_out[0] = (r * shrink).T
        k_sum = k_d if k_sum is None else k_sum + k_d
    ptot_ref[0, 0] = jnp.concatenate(chunk_decay, axis=0)
    bonus_ref[0] = _head_sum(r * k_sum * prm(P_R_K)) * v


def _rwkv_features(p_rw, p_lora, par, w2, a2, n_ctx):
    B, R, _ = p_rw.shape
    n_tiles = R // ROW_TILE
    sub = ROW_TILE // 8
    chunks_per_tile = ROW_TILE // SCAN_STEPS
    assert 2 * chunks_per_tile == 8

    def main(c):
        return pl.BlockSpec((1, ROW_TILE, W_BRANCH), lambda b, i: (b, i, c))

    def prev(c):
        return pl.BlockSpec((1, 8, W_BRANCH), lambda b, i: (b, jnp.maximum(i * sub - 1, 0), c))

    def nxt(c):
        return pl.BlockSpec((1, 8, W_BRANCH), lambda b, i: (b, jnp.minimum((i + 1) * sub, n_tiles * sub - 1), c))

    in_specs = []
    for c in (COL_RW_R, COL_RW_K, COL_RW_V):
        in_specs += [main(c), prev(c), nxt(c)]
    in_specs += [pl.BlockSpec((1, ROW_TILE, LANES), lambda b, i: (b, i, 0)),
                 pl.BlockSpec((P_ROWS, W_BRANCH), lambda b, i: (0, 0)),
                 pl.BlockSpec((2, LANES, W_BRANCH), lambda b, i: (0, 0, 0)),
                 pl.BlockSpec((2, LANES, W_BRANCH), lambda b, i: (0, 0, 0)),
                 pl.BlockSpec((2, ROW_TILE, ROW_TILE), lambda b, i: (0, 0, 0))]
    t_idx = np.arange(ROW_TILE)
    same_chunk = t_idx[:, None] // SCAN_STEPS == t_idx[None, :] // SCAN_STEPS
    tri = jnp.asarray(np.stack([same_chunk & (t_idx[None, :] <= t_idx[:, None]),
                                same_chunk & (t_idx[None, :] >= t_idx[:, None])]), BF16)
    out = jax.ShapeDtypeStruct((B, R, W_BRANCH), F32)
    return pl.pallas_call(
        functools.partial(_rwkv_feat_kernel, n_ctx_tiles=n_ctx // ROW_TILE, n_tiles=n_tiles),
        grid=(B, n_tiles),
        in_specs=in_specs,
        out_specs=[pl.BlockSpec((1, W_BRANCH, ROW_TILE), lambda b, i: (b, 0, i))] * 9
        + [pl.BlockSpec((1, 1, 2 * chunks_per_tile, W_BRANCH), lambda b, i: (b, i, 0, 0)),
           pl.BlockSpec((1, ROW_TILE, W_BRANCH), lambda b, i: (b, i, 0))],
        out_shape=[jax.ShapeDtypeStruct((B, W_BRANCH, R), F32)] * 9 + [jax.ShapeDtypeStruct((B, n_tiles, 2 * chunks_per_tile, W_BRANCH), F32), out],
        compiler_params=_params("parallel", "parallel"),
        name="rwkv_features",
    )(*([p_rw] * 9), p_lora, par, w2, a2, tri)


def _scan_kernel(p_ref, k_ref, b_ref, a_ref, r_ref, v_ref, y_ref, s_ref):
    n = HEAD_DIM

    @pl.when(pl.program_id(0) == 0)
    def _():
        s_ref[...] = jnp.zeros_like(s_ref)

    def row(ref, j, t):
        return ref[0, pl.ds(j * SCAN_STEPS + t, 1), :]

    zero = jnp.zeros((n, s_ref.shape[2]), F32)

    def first_sa(jb, sa):
        for jj in range(SCAN_J_UNROLL):
            j = jb * SCAN_J_UNROLL + jj
            sa = sa + s_ref[j] * row(a_ref, j, 0)
        return sa

    def step(t, sa):
        tile_rows = pl.ds(pl.multiple_of(t * n, n), n)
        vt = v_ref[0, tile_rows, :]
        t_next = jnp.minimum(t + 1, SCAN_STEPS - 1)

        def columns(jb, carry):
            y, sa_next = carry
            for jj in range(SCAN_J_UNROLL):
                j = jb * SCAN_J_UNROLL + jj
                sj = s_ref[j] + sa * row(b_ref, j, t) + vt * row(k_ref, j, t)
                s_ref[j] = sj
                y = y + sj * row(r_ref, j, t)
                sa_next = sa_next + sj * row(a_ref, j, t_next)
            return y, sa_next

        y, sa_next = lax.fori_loop(0, n // SCAN_J_UNROLL, columns, (zero, zero))
        y_ref[0, tile_rows, :] = y
        return sa_next

    def rescale(jb, carry):
        for jj in range(SCAN_J_UNROLL):
            j = jb * SCAN_J_UNROLL + jj
            s_ref[j] = s_ref[j] * p_ref[0, pl.ds(j, 1), :]
        return carry

    sa0 = lax.fori_loop(0, n // SCAN_J_UNROLL, first_sa, zero)
    lax.fori_loop(0, SCAN_STEPS, step, sa0)
    lax.fori_loop(0, n // SCAN_J_UNROLL, rescale, 0)


def _chunk_decay_to_scan(ptot, n_ctx):
    B, n_tiles = ptot.shape[:2]
    pt = ptot.reshape(B, n_tiles, 2, ROW_TILE // SCAN_STEPS, HEAD_DIM, N_HEADS)
    n_ctx_chunks = n_ctx // SCAN_STEPS

    def chains(z):
        return z.reshape(B, -1, HEAD_DIM, N_HEADS).transpose(1, 2, 0, 3).reshape(-1, HEAD_DIM, B * N_HEADS)

    fwd, bwd = chains(pt[:, :, 0]), chains(pt[:, :, 1])
    bwd = jnp.concatenate([bwd[:n_ctx_chunks][::-1], bwd[n_ctx_chunks:][::-1]], axis=0)
    return jnp.concatenate([fwd, bwd], axis=-1)


def _wkv_scan(p, k, b, a, r, v):
    n_chunks, rows, chains = k.shape
    spec = pl.BlockSpec((1, rows, chains), lambda s: (s, 0, 0))
    return pl.pallas_call(
        _scan_kernel,
        grid=(n_chunks,),
        in_specs=[pl.BlockSpec((1, HEAD_DIM, chains), lambda s: (s, 0, 0))] + [spec] * 5,
        out_specs=spec,
        out_shape=jax.ShapeDtypeStruct(k.shape, F32),
        scratch_shapes=[pltpu.VMEM((HEAD_DIM, HEAD_DIM, chains), F32)],
        compiler_params=_params("arbitrary"),
        name="wkv_scan",
    )(p, k, b, a, r, v)


def _flip_rows(x, flip):
    hi, mid, lo = _split3(x)
    return (jnp.dot(flip, hi, preferred_element_type=F32) + jnp.dot(flip, mid, preferred_element_type=F32)
            + jnp.dot(flip, lo, preferred_element_type=F32))


def _mirror_chunk(c, n_ctx_chunks, n_chunks):
    return jnp.where(c < n_ctx_chunks, n_ctx_chunks - 1 - c, n_ctx_chunks + n_chunks - 1 - c)


def _reverse_backward(rows, flip, n_fwd):
    hi, mid, lo = _split3(rows[n_fwd:])
    back = (jnp.dot(hi, flip, preferred_element_type=F32) + jnp.dot(mid, flip, preferred_element_type=F32)
            + jnp.dot(lo, flip, preferred_element_type=F32))
    return jnp.concatenate([rows[:n_fwd], back], axis=0)


def _to_scan_kernel(zf_ref, zb_ref, flip_ref, o_ref, *, step_major):
    nb = zf_ref.shape[0]
    flip = flip_ref[...]
    sub = RELAYOUT_ROWS // SCAN_STEPS

    def body(n, carry):
        rows = pl.ds(pl.multiple_of(n * N_HEADS, N_HEADS), N_HEADS)
        slabs = [zf_ref[b, rows, :] for b in range(nb)] + [zb_ref[b, rows, :] for b in range(nb)]
        tile = _reverse_backward(jnp.concatenate(slabs, axis=0), flip, nb * N_HEADS).T
        for q in range(sub):
            if step_major:
                dst = pl.ds(n, SCAN_STEPS, stride=HEAD_DIM)
            else:
                dst = pl.ds(pl.multiple_of(n * SCAN_STEPS, SCAN_STEPS), SCAN_STEPS)
            o_ref[q, dst, :] = tile[q * SCAN_STEPS:(q + 1) * SCAN_STEPS]
        return carry

    lax.fori_loop(0, HEAD_DIM, body, 0, unroll=RELAYOUT_UNROLL)


def _to_scan(z_fwd, z_bwd, flip, n_ctx, step_major=False):
    B, _, R = z_fwd.shape
    n_chunks = R // RELAYOUT_ROWS
    n_ctx_chunks = n_ctx // RELAYOUT_ROWS
    sub = RELAYOUT_ROWS // SCAN_STEPS
    chains = 2 * B * N_HEADS
    return pl.pallas_call(
        functools.partial(_to_scan_kernel, step_major=step_major),
        grid=(n_chunks,),
        in_specs=[pl.BlockSpec((B, W_BRANCH, RELAYOUT_ROWS), lambda c: (0, 0, c)),
                  pl.BlockSpec((B, W_BRANCH, RELAYOUT_ROWS),
                               lambda c: (0, 0, _mirror_chunk(c, n_ctx_chunks, n_chunks))),
                  pl.BlockSpec((RELAYOUT_ROWS, RELAYOUT_ROWS), lambda c: (0, 0))],
        out_specs=pl.BlockSpec((sub, HEAD_DIM * SCAN_STEPS, chains), lambda c: (c, 0, 0)),
        out_shape=jax.ShapeDtypeStruct((R // SCAN_STEPS, HEAD_DIM * SCAN_STEPS, chains), F32),
        compiler_params=_params("parallel"),
        name="to_scan_layout",
    )(z_fwd, z_bwd, flip)


def _from_scan_kernel(y_ref, flip_ref, yf_ref, yb_ref):
    nb = yf_ref.shape[0]
    flip = flip_ref[...]
    sub = RELAYOUT_ROWS // SCAN_STEPS

    def body(n, carry):
        tile = jnp.concatenate([y_ref[q, pl.ds(n, SCAN_STEPS, stride=HEAD_DIM), :] for q in range(sub)], axis=0)
        tile = _reverse_backward(tile.T, flip, nb * N_HEADS)
        rows = pl.ds(pl.multiple_of(n * N_HEADS, N_HEADS), N_HEADS)
        for b in range(nb):
            yf_ref[b, rows, :] = tile[b * N_HEADS:(b + 1) * N_HEADS]
            yb_ref[b, rows, :] = tile[(nb + b) * N_HEADS:(nb + b + 1) * N_HEADS]
        return carry

    lax.fori_loop(0, HEAD_DIM, body, 0, unroll=RELAYOUT_UNROLL)


def _from_scan(y, flip, n_batch, n_ctx):
    R = y.shape[0] * SCAN_STEPS
    n_chunks = R // RELAYOUT_ROWS
    n_ctx_chunks = n_ctx // RELAYOUT_ROWS
    sub = RELAYOUT_ROWS // SCAN_STEPS
    out = jax.ShapeDtypeStruct((n_batch, W_BRANCH, R), F32)
    return pl.pallas_call(
        _from_scan_kernel,
        grid=(n_chunks,),
        in_specs=[pl.BlockSpec((sub, HEAD_DIM * SCAN_STEPS, y.shape[2]), lambda c: (c, 0, 0)),
                  pl.BlockSpec((RELAYOUT_ROWS, RELAYOUT_ROWS), lambda c: (0, 0))],
        out_specs=[pl.BlockSpec((n_batch, W_BRANCH, RELAYOUT_ROWS), lambda c: (0, 0, c)),
                   pl.BlockSpec((n_batch, W_BRANCH, RELAYOUT_ROWS),
                                lambda c: (0, 0, _mirror_chunk(c, n_ctx_chunks, n_chunks)))],
        out_shape=[out, out],
        compiler_params=_params("parallel"),
        name="from_scan_layout",
    )(y, flip)


def _rwkv_readout_kernel(yf_ref, yb_ref, bonus_ref, gate_ref, gb_ref, o_ref):
    y = (yf_ref[0] + yb_ref[0]).T
    mu = _head_sum(y) * (1.0 / HEAD_DIM)
    yc = y - mu
    var = _head_sum(yc * yc) * (1.0 / HEAD_DIM)
    gb = gb_ref[...]
    out = yc * lax.rsqrt(var + LNX_EPS) * gb[0:1, :] + gb[1:2, :] + bonus_ref[0]
    o_ref[0] = (out * _silu(gate_ref[0])).astype(o_ref.dtype)


def _rwkv_readout(y_fwd, y_bwd, bonus, p_rw, lnx_gb):
    B, R, _ = bonus.shape
    tile = pl.BlockSpec((1, ROW_TILE, W_BRANCH), lambda b, i: (b, i, 0))
    tile_t = pl.BlockSpec((1, W_BRANCH, ROW_TILE), lambda b, i: (b, 0, i))
    return pl.pallas_call(
        _rwkv_readout_kernel,
        grid=(B, R // ROW_TILE),
        in_specs=[tile_t, tile_t, tile,
                  pl.BlockSpec((1, ROW_TILE, W_BRANCH), lambda b, i: (b, i, COL_RW_GATE)),
                  pl.BlockSpec((8, W_BRANCH), lambda b, i: (0, 0))],
        out_specs=tile,
        out_shape=jax.ShapeDtypeStruct((B, R, W_BRANCH), BF16),
        compiler_params=_params("parallel", "parallel"),
        name="rwkv_readout",
    )(y_fwd, y_bwd, bonus, p_rw, lnx_gb)


def _merge_kernel(na_ref, pool_ref, rw_ref, lna_ref, lpool_ref, lrw_ref, w_ref, o_ref):
    acc = None
    for br, (x_ref, l_ref) in enumerate(((na_ref, lna_ref), (pool_ref, lpool_ref), (rw_ref, lrw_ref))):
        t = _sigmoid(l_ref[...]) * jnp.dot(x_ref[...], w_ref[br], preferred_element_type=F32)
        acc = t if acc is None else acc + t
    o_ref[...] = acc.astype(o_ref.dtype)


def _merge(b_na, b_pool, b_rw, p_merge, w_branch):
    M = b_na.shape[0]
    tm, tn = min(_row_tile(M), 512), 1024
    nb = D_MODEL // tn
    x_spec = pl.BlockSpec((tm, W_BRANCH), lambda j, i: (i, 0))

    def logit(br):
        return pl.BlockSpec((tm, tn), lambda j, i: (i, br * nb + j))

    return pl.pallas_call(
        _merge_kernel,
        grid=(nb, M // tm),
        in_specs=[x_spec, x_spec, x_spec, logit(0), logit(1), logit(2),
                  pl.BlockSpec((N_BRANCH, W_BRANCH, tn), lambda j, i: (0, 0, j))],
        out_specs=pl.BlockSpec((tm, tn), lambda j, i: (i, j)),
        out_shape=jax.ShapeDtypeStruct((M, D_MODEL), BF16),
        compiler_params=_params("parallel", "parallel"),
        name="branch_merge",
    )(b_na, b_pool, b_rw, p_merge, p_merge, p_merge, w_branch)


def _out_kernel(m_ref, w_ref, c_ref, l_ref, mod_ref, fg_ref, o_ref, *, n_ctx_tiles, tile_offset, final):
    tile = pl.program_id(1) + tile_offset
    gate = _mod_row(mod_ref, tile, n_ctx_tiles)[:, 2 * D_MODEL:]
    x = _stream_tile(c_ref, l_ref, tile, n_ctx_tiles)
    x = x + gate * jnp.dot(m_ref[0], w_ref[...], preferred_element_type=F32)
    o_ref[0] = _rms(x, fg_ref[...]) if final else x


def _out_proj(merged, w_out, stream, mod, final_g, n_ctx, final):
    B, R, _ = merged.shape
    n_ctx_tiles = n_ctx // ROW_TILE
    off = n_ctx_tiles if final else 0
    return pl.pallas_call(
        functools.partial(_out_kernel, n_ctx_tiles=n_ctx_tiles, tile_offset=off, final=final),
        grid=(B, R // ROW_TILE - off),
        in_specs=[pl.BlockSpec((1, ROW_TILE, D_MODEL), lambda b, i: (b, i + off, 0)),
                  pl.BlockSpec((D_MODEL, D_MODEL), lambda b, i: (0, 0))]
        + _stream_specs(stream, n_ctx_tiles, off) + [
            pl.BlockSpec((MOD_ROWS, 3 * D_MODEL), lambda b, i: (0, 0)),
            pl.BlockSpec((1, D_MODEL), lambda b, i: (0, 0))],
        out_specs=pl.BlockSpec((1, ROW_TILE, D_MODEL), lambda b, i: (b, i, 0)),
        out_shape=jax.ShapeDtypeStruct((B, R - off * ROW_TILE, D_MODEL), F32),
        compiler_params=_params("parallel", "parallel"),
        name="out_proj_final" if final else "out_proj",
    )(merged, w_out, stream[0], stream[1], mod, final_g)


def _layer(stream, R, mod, n_ctx, final, final_g, norm_g, w_in_all, layer, na_rpb, pool_w, pool_scale, rw_mu, rw_w0, rw_w2,
           rw_a0, rw_a2, rw_k_k, rw_k_a, rw_r_k, rw_lnx_g, rw_lnx_b, w_branch, w_out):
    B = stream[0].shape[0]
    rows = (R - n_ctx) // GRID_W

    h = _norm_mod(stream, R, norm_g[None], mod, n_ctx).reshape(B * R, D_MODEL)
    lo = N_MAIN + 2 * RWKV_LORA
    w_in = w_in_all[layer]
    p_qkv = _matmul(h, w_in_all, BF16, "in_proj_qkv", n_cols=N_QKV, layer=layer).reshape(B, R, N_QKV)
    p_gp = _matmul(h, w_in_all, F32, "in_proj_gate_pool", n_cols=N_ATTN_POOL - N_QKV, layer=layer, first_col=N_QKV)
    p_gp = p_gp.reshape(B, R, N_ATTN_POOL - N_QKV)
    head_major = jnp.asarray(_head_major(np.eye(W_BRANCH, dtype=np.float32)), BF16)
    p_rw = _matmul(h, w_in_all, F32, "in_proj_rwkv", n_cols=N_MAIN - N_ATTN_POOL, layer=layer,
                   first_col=N_ATTN_POOL, perm=head_major).reshape(B, R, N_MAIN - N_ATTN_POOL)
    p_lora = _matmul(h, w_in[:, N_MAIN:lo].astype(BF16), F32, "in_proj_lora").reshape(B, R, 2 * RWKV_LORA)
    p_merge = _matmul(h, w_in[:, lo:].astype(BF16), F32, "in_proj_merge")

    b_na = _na_attention(p_qkv, p_gp, _na_bias_tables(na_rpb, rows), n_ctx)
    b_pool = _pool(p_gp, pool_w.astype(BF16), pool_scale[None], n_ctx)

    par = jnp.zeros((P_ROWS, W_BRANCH), F32)
    par = par.at[P_MU_R:P_MU_V + 1].set(rw_mu).at[P_W0_F:P_W0_B + 1].set(rw_w0).at[P_A0_F:P_A0_B + 1].set(rw_a0)
    par = _head_major(par.at[P_K_K].set(rw_k_k).at[P_K_A].set(rw_k_a).at[P_R_K].set(rw_r_k.reshape(-1)))
    zeros = jnp.zeros_like(rw_w2)
    w2 = _head_major(jnp.concatenate([rw_w2, zeros], axis=1)).astype(BF16)
    a2 = _head_major(jnp.concatenate([zeros, rw_a2], axis=1)).astype(BF16)
    v, k_f, b_f, a_f, r_f, k_b, b_b, a_b, r_b, ptot, bonus = _rwkv_features(p_rw, p_lora, par, w2, a2, n_ctx)
    flip = jnp.asarray(np.eye(RELAYOUT_ROWS)[::-1], BF16)
    y = _wkv_scan(_chunk_decay_to_scan(ptot, n_ctx),
                  *[_to_scan(zf, zb, flip, n_ctx) for zf, zb in ((k_f, k_b), (b_f, b_b), (a_f, a_b), (r_f, r_b))],
                  _to_scan(v, v, flip, n_ctx, step_major=True))
    y_fwd, y_bwd = _from_scan(y, flip, B, n_ctx)
    lnx_gb = _head_major(jnp.zeros((8, W_BRANCH), F32).at[0].set(rw_lnx_g).at[1].set(rw_lnx_b))
    b_rw = _rwkv_readout(y_fwd, y_bwd, bonus, p_rw, lnx_gb)

    def flat(z):
        return z.reshape(B * R, W_BRANCH)

    w_rw_out = _head_major(w_branch[2].T).T
    w_br = jnp.stack([w_branch[0], w_branch[1], w_rw_out]).astype(BF16)
    merged = _merge(flat(b_na), flat(b_pool), flat(b_rw), p_merge, w_br)
    return _out_proj(merged.reshape(B, R, D_MODEL), w_out.astype(BF16), stream, mod, final_g[None], n_ctx, final)


def kernel(x, c, ctx, c_ctx, norm_g, w_mod, b_mod, w_in, na_rpb, pool_w, pool_scale, rw_mu, rw_w0, rw_w2, rw_a0,
           rw_a2, rw_k_k, rw_k_a, rw_r_k, rw_lnx_g, rw_lnx_b, w_branch, w_out, final_g):
    B, T, _ = x.shape
    n_ctx = ctx.shape[1]
    depth = w_in.shape[0]
    assert B <= CTX_MOD_ROW and n_ctx % ROW_TILE == 0 and T % ROW_TILE == 0
    assert ROW_TILE % RELAYOUT_ROWS == 0 and RELAYOUT_ROWS % SCAN_STEPS == 0

    cond = jnp.zeros((MOD_ROWS, D_MODEL), F32).at[:B].set(c).at[CTX_MOD_ROW].set(c_ctx)
    mods = _modulation(cond, w_mod, b_mod[:, None, :])
    stream = (ctx, x, n_ctx // ROW_TILE)
    for layer in range(depth):
        out = _layer(stream, n_ctx + T, mods[layer], n_ctx, layer == depth - 1, final_g, norm_g[layer], w_in, layer,
                     na_rpb[layer], pool_w[layer], pool_scale[layer], rw_mu[layer], rw_w0[layer], rw_w2[layer],
                     rw_a0[layer], rw_a2[layer], rw_k_k[layer], rw_k_a[layer], rw_r_k[layer], rw_lnx_g[layer],
                     rw_lnx_b[layer], w_branch[layer], w_out[layer])
        stream = (out, out, 0)
    return out
```

```python
import functools

import numpy as np
import jax
import jax.numpy as jnp
from jax import lax
from jax.experimental import pallas as pl
from jax.experimental.pallas import tpu as pltpu

F32 = jnp.float32
BF16 = jnp.bfloat16

D_MODEL = 2048
W_BRANCH = D_MODEL // 2
N_BRANCH = 3
N_HEADS = 16
HEAD_DIM = 64
GRID_W = 64
NA_WIN_H = 8
NA_WIN_W = 16
POOL_WINDOWS = (2, 4, 8, 16)
POOL_GROUP_DIM = W_BRANCH // len(POOL_WINDOWS)
POOL_HALO = max(POOL_WINDOWS) // 2
RWKV_LORA = 64
RMS_EPS = 1e-6
LNX_EPS = 64e-5
NEG_INF = -1e30

LANES = 128
SUBLANES = 8
ROW_TILE = 256
NA_Q_ROWS = ROW_TILE // GRID_W
NA_K_ROWS = NA_Q_ROWS + NA_WIN_H
NA_K_TOK = NA_K_ROWS * GRID_W
CTX_MOD_ROW = 4
MOD_ROWS = SUBLANES
SCAN_STEPS = 64
SCAN_J_UNROLL = 64
RELAYOUT_UNROLL = 32
RELAYOUT_ROWS = 128
VMEM_LIMIT = 56 << 20

COL_Q, COL_K, COL_V = range(3)
COL_NA_GATE, COL_POOL_U, COL_POOL_GATE = range(3)
N_QKV = 3 * W_BRANCH
COL_RW_R, COL_RW_K, COL_RW_V, COL_RW_GATE = range(4)
N_ATTN_POOL = 6 * W_BRANCH
N_MAIN = 10 * W_BRANCH

P_MU_R, P_MU_K, P_MU_V, P_W0_F, P_W0_B, P_A0_F, P_A0_B, P_K_K, P_K_A, P_R_K = range(10)
P_ROWS = 16


def _params(*sem):
    return pltpu.CompilerParams(dimension_semantics=sem, vmem_limit_bytes=VMEM_LIMIT)


def _sigmoid(x):
    return 1.0 / (1.0 + jnp.exp(-x))


def _silu(x):
    return x * _sigmoid(x)


def _split3(x):
    hi = x.astype(BF16)
    r1 = x - hi.astype(F32)
    mid = r1.astype(BF16)
    lo = (r1 - mid.astype(F32)).astype(BF16)
    return hi, mid, lo


def _head_major(z):
    lead = z.shape[:-1]
    return z.reshape(lead + (N_HEADS, HEAD_DIM)).swapaxes(-1, -2).reshape(lead + (W_BRANCH,))


def _head_sum(x):
    n_tiles = W_BRANCH // LANES
    part = x[:, :LANES]
    for c in range(1, n_tiles):
        part = part + x[:, c * LANES:(c + 1) * LANES]
    shift = N_HEADS
    while shift < LANES:
        part = part + pltpu.roll(part, shift, 1)
        shift *= 2
    return jnp.concatenate([part] * n_tiles, axis=1)


def _mod_kernel(cond_ref, w_ref, b_ref, o_ref):
    s = _silu(cond_ref[...])
    o_ref[0] = jnp.dot(s.astype(BF16), w_ref[0].astype(BF16), preferred_element_type=F32) + b_ref[0]


def _modulation(cond, w_mod, b_mod):
    n_layers = w_mod.shape[0]
    tn = 3 * D_MODEL // 4
    return pl.pallas_call(
        _mod_kernel,
        grid=(n_layers, 4),
        in_specs=[pl.BlockSpec((MOD_ROWS, D_MODEL), lambda l, j: (0, 0)),
                  pl.BlockSpec((1, D_MODEL, tn), lambda l, j: (l, 0, j)),
                  pl.BlockSpec((1, 1, tn), lambda l, j: (l, 0, j))],
        out_specs=pl.BlockSpec((1, MOD_ROWS, tn), lambda l, j: (l, 0, j)),
        out_shape=jax.ShapeDtypeStruct((n_layers, MOD_ROWS, 3 * D_MODEL), F32),
        compiler_params=_params("arbitrary", "arbitrary"),
        name="adaln_modulation",
    )(cond, w_mod, b_mod)


def _mod_row(mod_ref, tile, n_ctx_tiles):
    row = jnp.where(tile < n_ctx_tiles, CTX_MOD_ROW, pl.program_id(0))
    return mod_ref[pl.ds(row, 1), :]


def _rms(x, g):
    return x * lax.rsqrt(jnp.mean(x * x, axis=-1, keepdims=True) + RMS_EPS) * g


def _stream_specs(stream, n_ctx_tiles, tile_offset=0):
    lat_shift = stream[2]
    return [pl.BlockSpec((1, ROW_TILE, D_MODEL), lambda b, i: (b, jnp.minimum(i + tile_offset, n_ctx_tiles - 1), 0)),
            pl.BlockSpec((1, ROW_TILE, D_MODEL), lambda b, i: (b, jnp.maximum(i + tile_offset - lat_shift, 0), 0))]


def _stream_tile(c_ref, l_ref, tile, n_ctx_tiles):
    return jnp.where(tile < n_ctx_tiles, c_ref[0], l_ref[0])


def _norm_mod_kernel(c_ref, l_ref, g_ref, mod_ref, h_ref, *, n_ctx_tiles):
    tile = pl.program_id(1)
    m = _mod_row(mod_ref, tile, n_ctx_tiles)
    shift = m[:, :D_MODEL]
    scale = m[:, D_MODEL:2 * D_MODEL]
    x = _stream_tile(c_ref, l_ref, tile, n_ctx_tiles)
    h_ref[0] = (_rms(x, g_ref[...]) * (1.0 + scale) + shift).astype(BF16)


def _norm_mod(stream, n_rows, norm_g, mod, n_ctx):
    B = stream[0].shape[0]
    n_ctx_tiles = n_ctx // ROW_TILE
    return pl.pallas_call(
        functools.partial(_norm_mod_kernel, n_ctx_tiles=n_ctx_tiles),
        grid=(B, n_rows // ROW_TILE),
        in_specs=_stream_specs(stream, n_ctx_tiles) + [
            pl.BlockSpec((1, D_MODEL), lambda b, i: (0, 0)),
            pl.BlockSpec((MOD_ROWS, 3 * D_MODEL), lambda b, i: (0, 0))],
        out_specs=pl.BlockSpec((1, ROW_TILE, D_MODEL), lambda b, i: (b, i, 0)),
        out_shape=jax.ShapeDtypeStruct((B, n_rows, D_MODEL), BF16),
        compiler_params=_params("parallel", "parallel"),
        name="norm_modulate",
    )(stream[0], stream[1], norm_g, mod)


def _mm_kernel(a_ref, w_ref, o_ref):
    o_ref[...] = jnp.dot(a_ref[...], w_ref[...], preferred_element_type=F32).astype(o_ref.dtype)


def _row_tile(m):
    for t in (1024, 512, 256):
        if m % t == 0:
            return t
    raise ValueError(f"row count {m} is not a multiple of {ROW_TILE}")


def _mm_cast_kernel(a_ref, w_ref, *rest, permute):
    perm_ref, o_ref, wb_ref = rest if permute else (None,) + rest

    @pl.when(pl.program_id(1) == 0)
    def _():
        wb = w_ref[...].astype(BF16)
        if permute:
            wb = jnp.dot(wb, perm_ref[...], preferred_element_type=F32).astype(BF16)
        wb_ref[...] = wb

    o_ref[...] = jnp.dot(a_ref[...], wb_ref[...], preferred_element_type=F32).astype(o_ref.dtype)


def _matmul(a, w, out_dtype, name, n_cols=None, layer=None, first_col=0, perm=None):
    M, K = a.shape
    N = w.shape[-1] if n_cols is None else n_cols
    tm = _row_tile(M)
    tn = min(N, 1024)
    assert first_col % tn == 0
    j0 = first_col // tn
    cast = w.dtype != BF16
    assert cast or perm is None
    if layer is None:
        w_spec = pl.BlockSpec((K, tn), lambda j, i: (0, j0 + j))
    else:
        w_spec = pl.BlockSpec((None, K, tn), lambda j, i: (layer, 0, j0 + j))
    in_specs = [pl.BlockSpec((tm, K), lambda j, i: (i, 0)), w_spec]
    args = [a, w]
    if perm is not None:
        in_specs.append(pl.BlockSpec((tn, tn), lambda j, i: (0, 0)))
        args.append(perm)
    return pl.pallas_call(
        functools.partial(_mm_cast_kernel, permute=perm is not None) if cast else _mm_kernel,
        grid=(N // tn, M // tm),
        in_specs=in_specs,
        out_specs=pl.BlockSpec((tm, tn), lambda j, i: (i, j)),
        out_shape=jax.ShapeDtypeStruct((M, N), out_dtype),
        scratch_shapes=[pltpu.VMEM((K, tn), BF16)] if cast else [],
        compiler_params=_params("parallel", "arbitrary" if cast else "parallel"),
        name=name,
    )(*args)


def _na_bias_tables(rpb, rows):
    n_blocks = rows // NA_Q_ROWS
    n_off = 2 * NA_WIN_H - 1
    col = np.arange(GRID_W)
    c0 = np.clip(col - NA_WIN_W // 2, 0, GRID_W - NA_WIN_W)
    valid_c = (col[None, :] >= c0[:, None]) & (col[None, :] < c0[:, None] + NA_WIN_W)
    col_off = np.clip(col[None, :] - col[:, None] + NA_WIN_W - 1, 0, 2 * NA_WIN_W - 2)
    pick_c = jnp.asarray(np.eye(2 * NA_WIN_W - 1)[col_off], F32)
    tile = jnp.einsum("hrc,qpc->hrqp", rpb.astype(F32), pick_c, precision=lax.Precision.HIGHEST)
    tile = jnp.where(valid_c, tile, NEG_INF)
    tile = jnp.concatenate([tile, jnp.full((N_HEADS, 1, GRID_W, GRID_W), NEG_INF, F32)], axis=1)
    tile = jnp.concatenate([tile, tile], axis=-1)
    picks = []
    for m in (0, 1, n_blocks - 1):
        q_row = NA_Q_ROWS * m + np.arange(NA_Q_ROWS)
        k_row = int(np.clip(NA_Q_ROWS * m - NA_Q_ROWS, 0, rows - NA_K_ROWS)) + np.arange(NA_K_ROWS)
        r0 = np.clip(q_row - NA_WIN_H // 2, 0, rows - NA_WIN_H)
        valid_r = (k_row[None, :] >= r0[:, None]) & (k_row[None, :] < r0[:, None] + NA_WIN_H)
        row_off = np.clip(k_row[None, :] - q_row[:, None] + NA_WIN_H - 1, 0, n_off - 1)
        picks.append(np.where(valid_r, row_off, n_off))
    picks = np.stack(picks)

    def build(t_ref, o_ref):
        left = lax.broadcasted_iota(jnp.int32, (1, LANES), 1) < GRID_W
        for ty in range(picks.shape[0]):
            for a in range(NA_Q_ROWS):
                for kp in range(NA_K_ROWS // 2):
                    pair = jnp.where(left, t_ref[0, int(picks[ty, a, 2 * kp])], t_ref[0, int(picks[ty, a, 2 * kp + 1])])
                    o_ref[ty, 0, a * GRID_W:(a + 1) * GRID_W, kp * LANES:(kp + 1) * LANES] = pair

    return pl.pallas_call(
        build,
        grid=(N_HEADS,),
        in_specs=[pl.BlockSpec((1, n_off + 1, GRID_W, LANES), lambda h: (h, 0, 0, 0))],
        out_specs=pl.BlockSpec((picks.shape[0], 1, ROW_TILE, NA_K_TOK), lambda h: (0, h, 0, 0)),
        out_shape=jax.ShapeDtypeStruct((picks.shape[0], N_HEADS, ROW_TILE, NA_K_TOK), F32),
        compiler_params=_params("parallel"),
        name="na_bias_tables",
    )(tile)


def _attend(qe, keys, vals, biases):
    dn = (((1,), (1,)), ((), ()))
    scores = []
    for kk, bias in zip(keys, biases):
        s = lax.dot_general(qe, kk, dn, preferred_element_type=F32)
        scores.append(s if bias is None else s + bias)
    m = scores[0].max(axis=-1, keepdims=True)
    for s in scores[1:]:
        m = jnp.maximum(m, s.max(axis=-1, keepdims=True))
    num, den = None, None
    for s, vv in zip(scores, vals):
        p = jnp.exp(s - m)
        l = p.sum(axis=-1, keepdims=True)
        o = jnp.dot(p.astype(BF16), vv, preferred_element_type=F32)
        num = o if num is None else num + o
        den = l if den is None else den + l
    return num / den


def _na_kernel(q_ref, k_ref, v_ref, g_ref, bias_ref, o_ref, *, n_ctx, rows):
    j = pl.program_id(2)
    lane = lax.broadcasted_iota(jnp.int32, (1, LANES), 1)
    in_head = (lane < HEAD_DIM, lane >= HEAD_DIM)
    q = q_ref[0] * (HEAD_DIM ** -0.5)
    kc = k_ref[0, 0:n_ctx, :]
    vc = v_ref[0, 0:n_ctx, :]

    def heads(q):
        return [jnp.where(in_head[e], q, jnp.zeros_like(q)) for e in range(2)]

    def finish(o0, o1):
        o = jnp.where(in_head[0], o0, o1)
        o_ref[0] = (o * _silu(g_ref[0])).astype(o_ref.dtype)

    @pl.when(j == 0)
    def _():
        finish(*[_attend(qe, [kc], [vc], [None]) for qe in heads(q)])

    @pl.when(j > 0)
    def _():
        k_row = jnp.clip(NA_Q_ROWS * (j - 1) - NA_Q_ROWS, 0, rows - NA_K_ROWS)
        start = pl.multiple_of(n_ctx + k_row * GRID_W, GRID_W)
        kw = k_ref[0, pl.ds(start, NA_K_TOK), :]
        vw = v_ref[0, pl.ds(start, NA_K_TOK), :]
        finish(*[_attend(qe, [kw, kc], [vw, vc], [bias_ref[0, e], None])
                 for e, qe in enumerate(heads(q))])


def _na_attention(p_qkv, p_gp, bias_tables, n_ctx):
    B, R, _ = p_qkv.shape
    rows = (R - n_ctx) // GRID_W
    n_blocks = rows // NA_Q_ROWS
    pairs = W_BRANCH // LANES
    assert n_ctx == ROW_TILE and rows >= NA_K_ROWS and rows % NA_Q_ROWS == 0

    def col(c):
        return lambda b, hp, j: (b, 0, c * pairs + hp)

    def bias_idx(b, hp, j):
        return (jnp.where(j <= 1, 0, jnp.where(j == n_blocks, 2, 1)), hp, 0, 0)

    return pl.pallas_call(
        functools.partial(_na_kernel, n_ctx=n_ctx, rows=rows),
        grid=(B, pairs, n_blocks + 1),
        in_specs=[pl.BlockSpec((1, ROW_TILE, LANES), lambda b, hp, j: (b, j, COL_Q * pairs + hp)),
                  pl.BlockSpec((1, R, LANES), col(COL_K)),
                  pl.BlockSpec((1, R, LANES), col(COL_V)),
                  pl.BlockSpec((1, ROW_TILE, LANES), lambda b, hp, j: (b, j, COL_NA_GATE * pairs + hp)),
                  pl.BlockSpec((1, 2, ROW_TILE, NA_K_TOK), bias_idx)],
        out_specs=pl.BlockSpec((1, ROW_TILE, LANES), lambda b, hp, j: (b, j, hp)),
        out_shape=jax.ShapeDtypeStruct((B, R, W_BRANCH), BF16),
        compiler_params=_params("parallel", "parallel", "arbitrary"),
        name="neighbourhood_attention",
    )(p_qkv, p_qkv, p_qkv, p_gp, bias_tables)


def _pool_kernel(u_ref, g_ref, w_ref, sc_ref, o_ref, pad_ref, *, n_ctx, n_lat):
    grp = pl.program_id(1)
    w = w_ref[0]
    scale = sc_ref[...]

    def run(win):
        half = win // 2
        for seq_start, seq_len in ((0, n_ctx), (n_ctx, n_lat)):
            zeros = jnp.zeros((POOL_HALO, POOL_GROUP_DIM), F32)
            pad_ref[0:POOL_HALO, :] = zeros
            pad_ref[POOL_HALO:POOL_HALO + seq_len, :] = u_ref[0, seq_start:seq_start + seq_len, :]
            pad_ref[POOL_HALO + seq_len:2 * POOL_HALO + seq_len, :] = zeros

            def chunk(c, carry):
                base = pl.multiple_of(c * ROW_TILE, ROW_TILE)
                x = pad_ref[pl.ds(base, ROW_TILE + 2 * POOL_HALO), :]
                acc = x[POOL_HALO - half:POOL_HALO - half + ROW_TILE]
                for o in range(-half + 1, half):
                    acc = acc + x[POOL_HALO + o:POOL_HALO + o + ROW_TILE]
                t = base + lax.broadcasted_iota(jnp.int32, (ROW_TILE, 1), 0)
                cnt = jnp.minimum(t + half, seq_len) - jnp.maximum(t - half, 0)
                diff = acc / cnt.astype(F32) - x[POOL_HALO:POOL_HALO + ROW_TILE]
                y = jnp.dot(diff.astype(BF16), w, preferred_element_type=F32) * scale
                rows = pl.ds(seq_start + base, ROW_TILE)
                o_ref[0, rows, :] = (y * _silu(g_ref[0, rows, :])).astype(o_ref.dtype)
                return carry

            lax.fori_loop(0, seq_len // ROW_TILE, chunk, 0)

    for gi, win in enumerate(POOL_WINDOWS):
        pl.when(grp == gi)(functools.partial(run, win))


def _pool(p_gp, pool_w, pool_scale, n_ctx):
    B, R, _ = p_gp.shape
    groups = len(POOL_WINDOWS)
    return pl.pallas_call(
        functools.partial(_pool_kernel, n_ctx=n_ctx, n_lat=R - n_ctx),
        grid=(B, groups),
        in_specs=[pl.BlockSpec((1, R, POOL_GROUP_DIM), lambda b, g: (b, 0, COL_POOL_U * groups + g)),
                  pl.BlockSpec((1, R, POOL_GROUP_DIM), lambda b, g: (b, 0, COL_POOL_GATE * groups + g)),
                  pl.BlockSpec((1, POOL_GROUP_DIM, POOL_GROUP_DIM), lambda b, g: (g, 0, 0)),
                  pl.BlockSpec((1, POOL_GROUP_DIM), lambda b, g: (0, g))],
        out_specs=pl.BlockSpec((1, R, POOL_GROUP_DIM), lambda b, g: (b, 0, g)),
        out_shape=jax.ShapeDtypeStruct((B, R, W_BRANCH), BF16),
        scratch_shapes=[pltpu.VMEM((R - n_ctx + 2 * POOL_HALO, POOL_GROUP_DIM), F32)],
        compiler_params=_params("parallel", "arbitrary"),
        name="multiscale_pool",
    )(p_gp, p_gp, pool_w, pool_scale)


def _rwkv_feat_kernel(r_ref, rp_ref, rn_ref, k_ref, kp_ref, kn_ref, v_ref, vp_ref, vn_ref, lora_ref,
                      par_ref, w2_ref, a2_ref, tri_ref,
                      vo_ref, kf_ref, bf_ref, af_ref, rf_ref, kb_ref, bb_ref, ab_ref, rb_ref, ptot_ref, bonus_ref,
                      *, n_ctx_tiles, n_tiles):
    i = pl.program_id(1)
    first = (i == 0) | (i == n_ctx_tiles)
    last = (i == n_ctx_tiles - 1) | (i == n_tiles - 1)
    row = lax.broadcasted_iota(jnp.int32, (ROW_TILE, 1), 0)
    par = par_ref[...]

    def prm(p):
        return par[p:p + 1, :]

    def mix(z_ref, prev_ref, next_ref, mu):
        z = z_ref[0]
        prev = jnp.where(first, 0.0, prev_ref[0, SUBLANES - 1:SUBLANES, :])
        nxt = jnp.where(last, 0.0, next_ref[0, 0:1, :])
        z_prev = jnp.where(row == 0, prev, pltpu.roll(z, 1, 0))
        z_next = jnp.where(row == ROW_TILE - 1, nxt, pltpu.roll(z, ROW_TILE - 1, 0))
        return z + mu * (0.5 * (z_prev + z_next) - z)

    r = mix(r_ref, rp_ref, rn_ref, prm(P_MU_R))
    k = mix(k_ref, kp_ref, kn_ref, prm(P_MU_K))
    v = mix(v_ref, vp_ref, vn_ref, prm(P_MU_V))
    vo_ref[0] = v.T

    kk = k * prm(P_K_K)
    kk = kk * jnp.minimum(lax.rsqrt(_head_sum(kk * kk)), 1e12)

    lora = lora_ref[0]
    lane = lax.broadcasted_iota(jnp.int32, (1, LANES), 1)
    lora = jnp.where(lane < RWKV_LORA, jnp.tanh(lora), lora).astype(BF16)
    k_sum = None
    outs = ((kf_ref, bf_ref, af_ref, rf_ref), (kb_ref, bb_ref, ab_ref, rb_ref))
    chunk_decay = []
    for d, (k_out, b_out, a_out, r_out) in enumerate(outs):
        x = prm(P_W0_F + d) + jnp.dot(lora, w2_ref[d], preferred_element_type=F32)
        w_log = -(jnp.maximum(-x, 0.0) + jnp.log(1.0 + jnp.exp(-jnp.abs(x)))) - 0.5
        neg_log_w = jnp.exp(w_log)
        hi, mid, lo = _split3(neg_log_w)
        tri = tri_ref[d]
        cs = (jnp.dot(tri, hi, preferred_element_type=F32) + jnp.dot(tri, mid, preferred_element_type=F32)
              + jnp.dot(tri, lo, preferred_element_type=F32))
        grow = jnp.exp(cs)
        shrink = jnp.exp(-cs)
        a = 0.5 + 0.5 * jnp.tanh(0.5 * (prm(P_A0_F + d) + jnp.dot(lora, a2_ref[d], preferred_element_type=F32)))
        k_d = k * (1.0 + (a - 1.0) * prm(P_K_A))
        for q in range(ROW_TILE // SCAN_STEPS):
            last = q * SCAN_STEPS + (SCAN_STEPS - 1 if d == 0 else 0)
            chunk_decay.append(shrink[last:last + 1])
        k_out[0] = (k_d * grow).T
        b_out[0] = (kk * a * grow).T
        a_out[0] = (-kk * jnp.exp(neg_log_w - cs)).T
        r_out[0] = (r * shrink).T
        k_sum = k_d if k_sum is None else k_sum + k_d
    ptot_ref[0, 0] = jnp.concatenate(chunk_decay, axis=0)
    bonus_ref[0] = _head_sum(r * k_sum * prm(P_R_K)) * v


def _rwkv_features(p_rw, p_lora, par, w2, a2, n_ctx):
    B, R, _ = p_rw.shape
    n_tiles = R // ROW_TILE
    sub = ROW_TILE // SUBLANES
    chunks_per_tile = ROW_TILE // SCAN_STEPS
    assert 2 * chunks_per_tile == SUBLANES

    def main(c):
        return pl.BlockSpec((1, ROW_TILE, W_BRANCH), lambda b, i: (b, i, c))

    def prev(c):
        return pl.BlockSpec((1, SUBLANES, W_BRANCH), lambda b, i: (b, jnp.maximum(i * sub - 1, 0), c))

    def nxt(c):
        return pl.BlockSpec((1, SUBLANES, W_BRANCH),
                            lambda b, i: (b, jnp.minimum((i + 1) * sub, n_tiles * sub - 1), c))

    in_specs = []
    for c in (COL_RW_R, COL_RW_K, COL_RW_V):
        in_specs += [main(c), prev(c), nxt(c)]
    in_specs += [pl.BlockSpec((1, ROW_TILE, LANES), lambda b, i: (b, i, 0)),
                 pl.BlockSpec((P_ROWS, W_BRANCH), lambda b, i: (0, 0)),
                 pl.BlockSpec((2, LANES, W_BRANCH), lambda b, i: (0, 0, 0)),
                 pl.BlockSpec((2, LANES, W_BRANCH), lambda b, i: (0, 0, 0)),
                 pl.BlockSpec((2, ROW_TILE, ROW_TILE), lambda b, i: (0, 0, 0))]
    t_idx = np.arange(ROW_TILE)
    same_chunk = t_idx[:, None] // SCAN_STEPS == t_idx[None, :] // SCAN_STEPS
    tri = jnp.asarray(np.stack([same_chunk & (t_idx[None, :] <= t_idx[:, None]),
                                same_chunk & (t_idx[None, :] >= t_idx[:, None])]), BF16)
    out = jax.ShapeDtypeStruct((B, R, W_BRANCH), F32)
    return pl.pallas_call(
        functools.partial(_rwkv_feat_kernel, n_ctx_tiles=n_ctx // ROW_TILE, n_tiles=n_tiles),
        grid=(B, n_tiles),
        in_specs=in_specs,
        out_specs=[pl.BlockSpec((1, W_BRANCH, ROW_TILE), lambda b, i: (b, 0, i))] * 9
        + [pl.BlockSpec((1, 1, 2 * chunks_per_tile, W_BRANCH), lambda b, i: (b, i, 0, 0)),
           pl.BlockSpec((1, ROW_TILE, W_BRANCH), lambda b, i: (b, i, 0))],
        out_shape=[jax.ShapeDtypeStruct((B, W_BRANCH, R), F32)] * 9
        + [jax.ShapeDtypeStruct((B, n_tiles, 2 * chunks_per_tile, W_BRANCH), F32), out],
        compiler_params=_params("parallel", "parallel"),
        name="rwkv_features",
    )(*([p_rw] * 9), p_lora, par, w2, a2, tri)


def _scan_kernel(p_ref, k_ref, b_ref, a_ref, r_ref, v_ref, y_ref, s_ref):
    n = HEAD_DIM

    @pl.when(pl.program_id(0) == 0)
    def _():
        s_ref[...] = jnp.zeros_like(s_ref)

    def row(ref, j, t):
        return ref[0, pl.ds(j * SCAN_STEPS + t, 1), :]

    zero = jnp.zeros((n, s_ref.shape[2]), F32)

    def first_sa(jb, sa):
        for jj in range(SCAN_J_UNROLL):
            j = jb * SCAN_J_UNROLL + jj
            sa = sa + s_ref[j] * row(a_ref, j, 0)
        return sa

    def step(t, sa):
        tile_rows = pl.ds(pl.multiple_of(t * n, n), n)
        vt = v_ref[0, tile_rows, :]
        t_next = jnp.minimum(t + 1, SCAN_STEPS - 1)

        def columns(jb, carry):
            y, sa_next = carry
            for jj in range(SCAN_J_UNROLL):
                j = jb * SCAN_J_UNROLL + jj
                sj = s_ref[j] + sa * row(b_ref, j, t) + vt * row(k_ref, j, t)
                s_ref[j] = sj
                y = y + sj * row(r_ref, j, t)
                sa_next = sa_next + sj * row(a_ref, j, t_next)
            return y, sa_next

        y, sa_next = lax.fori_loop(0, n // SCAN_J_UNROLL, columns, (zero, zero))
        y_ref[0, tile_rows, :] = y
        return sa_next

    def rescale(jb, carry):
        for jj in range(SCAN_J_UNROLL):
            j = jb * SCAN_J_UNROLL + jj
            s_ref[j] = s_ref[j] * p_ref[0, pl.ds(j, 1), :]
        return carry

    sa0 = lax.fori_loop(0, n // SCAN_J_UNROLL, first_sa, zero)
    lax.fori_loop(0, SCAN_STEPS, step, sa0)
    lax.fori_loop(0, n // SCAN_J_UNROLL, rescale, 0)


def _chunk_decay_to_scan(ptot, n_ctx):
    B, n_tiles = ptot.shape[:2]
    pt = ptot.reshape(B, n_tiles, 2, ROW_TILE // SCAN_STEPS, HEAD_DIM, N_HEADS)
    n_ctx_chunks = n_ctx // SCAN_STEPS

    def chains(z):
        return z.reshape(B, -1, HEAD_DIM, N_HEADS).transpose(1, 2, 0, 3).reshape(-1, HEAD_DIM, B * N_HEADS)

    fwd, bwd = chains(pt[:, :, 0]), chains(pt[:, :, 1])
    bwd = jnp.concatenate([bwd[:n_ctx_chunks][::-1], bwd[n_ctx_chunks:][::-1]], axis=0)
    return jnp.concatenate([fwd, bwd], axis=-1)


def _wkv_scan(p, k, b, a, r, v):
    n_chunks, rows, chains = k.shape
    spec = pl.BlockSpec((1, rows, chains), lambda s: (s, 0, 0))
    return pl.pallas_call(
        _scan_kernel,
        grid=(n_chunks,),
        in_specs=[pl.BlockSpec((1, HEAD_DIM, chains), lambda s: (s, 0, 0))] + [spec] * 5,
        out_specs=spec,
        out_shape=jax.ShapeDtypeStruct(k.shape, F32),
        scratch_shapes=[pltpu.VMEM((HEAD_DIM, HEAD_DIM, chains), F32)],
        compiler_params=_params("arbitrary"),
        name="wkv_scan",
    )(p, k, b, a, r, v)


def _mirror_chunk(c, n_ctx_chunks, n_chunks):
    return jnp.where(c < n_ctx_chunks, n_ctx_chunks - 1 - c, n_ctx_chunks + n_chunks - 1 - c)


def _reverse_backward(rows, flip, n_fwd):
    hi, mid, lo = _split3(rows[n_fwd:])
    back = (jnp.dot(hi, flip, preferred_element_type=F32) + jnp.dot(mid, flip, preferred_element_type=F32)
            + jnp.dot(lo, flip, preferred_element_type=F32))
    return jnp.concatenate([rows[:n_fwd], back], axis=0)


def _to_scan_kernel(zf_ref, zb_ref, flip_ref, o_ref, *, step_major):
    nb = zf_ref.shape[0]
    flip = flip_ref[...]
    sub = RELAYOUT_ROWS // SCAN_STEPS

    def body(n, carry):
        rows = pl.ds(pl.multiple_of(n * N_HEADS, N_HEADS), N_HEADS)
        slabs = [zf_ref[b, rows, :] for b in range(nb)] + [zb_ref[b, rows, :] for b in range(nb)]
        tile = _reverse_backward(jnp.concatenate(slabs, axis=0), flip, nb * N_HEADS).T
        for q in range(sub):
            if step_major:
                dst = pl.ds(n, SCAN_STEPS, stride=HEAD_DIM)
            else:
                dst = pl.ds(pl.multiple_of(n * SCAN_STEPS, SCAN_STEPS), SCAN_STEPS)
            o_ref[q, dst, :] = tile[q * SCAN_STEPS:(q + 1) * SCAN_STEPS]
        return carry

    lax.fori_loop(0, HEAD_DIM, body, 0, unroll=RELAYOUT_UNROLL)


def _to_scan(z_fwd, z_bwd, flip, n_ctx, step_major=False):
    B, _, R = z_fwd.shape
    n_chunks = R // RELAYOUT_ROWS
    n_ctx_chunks = n_ctx // RELAYOUT_ROWS
    sub = RELAYOUT_ROWS // SCAN_STEPS
    chains = 2 * B * N_HEADS
    return pl.pallas_call(
        functools.partial(_to_scan_kernel, step_major=step_major),
        grid=(n_chunks,),
        in_specs=[pl.BlockSpec((B, W_BRANCH, RELAYOUT_ROWS), lambda c: (0, 0, c)),
                  pl.BlockSpec((B, W_BRANCH, RELAYOUT_ROWS),
                               lambda c: (0, 0, _mirror_chunk(c, n_ctx_chunks, n_chunks))),
                  pl.BlockSpec((RELAYOUT_ROWS, RELAYOUT_ROWS), lambda c: (0, 0))],
        out_specs=pl.BlockSpec((sub, HEAD_DIM * SCAN_STEPS, chains), lambda c: (c, 0, 0)),
        out_shape=jax.ShapeDtypeStruct((R // SCAN_STEPS, HEAD_DIM * SCAN_STEPS, chains), F32),
        compiler_params=_params("parallel"),
        name="to_scan_layout",
    )(z_fwd, z_bwd, flip)


def _from_scan_kernel(y_ref, flip_ref, yf_ref, yb_ref):
    nb = yf_ref.shape[0]
    flip = flip_ref[...]
    sub = RELAYOUT_ROWS // SCAN_STEPS

    def body(n, carry):
        tile = jnp.concatenate([y_ref[q, pl.ds(n, SCAN_STEPS, stride=HEAD_DIM), :] for q in range(sub)], axis=0)
        tile = _reverse_backward(tile.T, flip, nb * N_HEADS)
        rows = pl.ds(pl.multiple_of(n * N_HEADS, N_HEADS), N_HEADS)
        for b in range(nb):
            yf_ref[b, rows, :] = tile[b * N_HEADS:(b + 1) * N_HEADS]
            yb_ref[b, rows, :] = tile[(nb + b) * N_HEADS:(nb + b + 1) * N_HEADS]
        return carry

    lax.fori_loop(0, HEAD_DIM, body, 0, unroll=RELAYOUT_UNROLL)


def _from_scan(y, flip, n_batch, n_ctx):
    R = y.shape[0] * SCAN_STEPS
    n_chunks = R // RELAYOUT_ROWS
    n_ctx_chunks = n_ctx // RELAYOUT_ROWS
    sub = RELAYOUT_ROWS // SCAN_STEPS
    out = jax.ShapeDtypeStruct((n_batch, W_BRANCH, R), F32)
    return pl.pallas_call(
        _from_scan_kernel,
        grid=(n_chunks,),
        in_specs=[pl.BlockSpec((sub, HEAD_DIM * SCAN_STEPS, y.shape[2]), lambda c: (c, 0, 0)),
                  pl.BlockSpec((RELAYOUT_ROWS, RELAYOUT_ROWS), lambda c: (0, 0))],
        out_specs=[pl.BlockSpec((n_batch, W_BRANCH, RELAYOUT_ROWS), lambda c: (0, 0, c)),
                   pl.BlockSpec((n_batch, W_BRANCH, RELAYOUT_ROWS),
                                lambda c: (0, 0, _mirror_chunk(c, n_ctx_chunks, n_chunks)))],
        out_shape=[out, out],
        compiler_params=_params("parallel"),
        name="from_scan_layout",
    )(y, flip)


def _rwkv_readout_kernel(yf_ref, yb_ref, bonus_ref, gate_ref, gb_ref, o_ref):
    y = (yf_ref[0] + yb_ref[0]).T
    mu = _head_sum(y) * (1.0 / HEAD_DIM)
    yc = y - mu
    var = _head_sum(yc * yc) * (1.0 / HEAD_DIM)
    gb = gb_ref[...]
    out = yc * lax.rsqrt(var + LNX_EPS) * gb[0:1, :] + gb[1:2, :] + bonus_ref[0]
    o_ref[0] = (out * _silu(gate_ref[0])).astype(o_ref.dtype)


def _rwkv_readout(y_fwd, y_bwd, bonus, p_rw, lnx_gb):
    B, R, _ = bonus.shape
    tile = pl.BlockSpec((1, ROW_TILE, W_BRANCH), lambda b, i: (b, i, 0))
    tile_t = pl.BlockSpec((1, W_BRANCH, ROW_TILE), lambda b, i: (b, 0, i))
    return pl.pallas_call(
        _rwkv_readout_kernel,
        grid=(B, R // ROW_TILE),
        in_specs=[tile_t, tile_t, tile,
                  pl.BlockSpec((1, ROW_TILE, W_BRANCH), lambda b, i: (b, i, COL_RW_GATE)),
                  pl.BlockSpec((SUBLANES, W_BRANCH), lambda b, i: (0, 0))],
        out_specs=tile,
        out_shape=jax.ShapeDtypeStruct((B, R, W_BRANCH), BF16),
        compiler_params=_params("parallel", "parallel"),
        name="rwkv_readout",
    )(y_fwd, y_bwd, bonus, p_rw, lnx_gb)


def _merge_kernel(na_ref, pool_ref, rw_ref, lna_ref, lpool_ref, lrw_ref, w_ref, o_ref):
    acc = None
    for br, (x_ref, l_ref) in enumerate(((na_ref, lna_ref), (pool_ref, lpool_ref), (rw_ref, lrw_ref))):
        t = _sigmoid(l_ref[...]) * jnp.dot(x_ref[...], w_ref[br], preferred_element_type=F32)
        acc = t if acc is None else acc + t
    o_ref[...] = acc.astype(o_ref.dtype)


def _merge(b_na, b_pool, b_rw, p_merge, w_branch):
    M = b_na.shape[0]
    tm, tn = min(_row_tile(M), 512), 1024
    nb = D_MODEL // tn
    x_spec = pl.BlockSpec((tm, W_BRANCH), lambda j, i: (i, 0))

    def logit(br):
        return pl.BlockSpec((tm, tn), lambda j, i: (i, br * nb + j))

    return pl.pallas_call(
        _merge_kernel,
        grid=(nb, M // tm),
        in_specs=[x_spec, x_spec, x_spec, logit(0), logit(1), logit(2),
                  pl.BlockSpec((N_BRANCH, W_BRANCH, tn), lambda j, i: (0, 0, j))],
        out_specs=pl.BlockSpec((tm, tn), lambda j, i: (i, j)),
        out_shape=jax.ShapeDtypeStruct((M, D_MODEL), BF16),
        compiler_params=_params("parallel", "parallel"),
        name="branch_merge",
    )(b_na, b_pool, b_rw, p_merge, p_merge, p_merge, w_branch)


def _out_kernel(m_ref, w_ref, c_ref, l_ref, mod_ref, fg_ref, o_ref, *, n_ctx_tiles, tile_offset, final):
    tile = pl.program_id(1) + tile_offset
    gate = _mod_row(mod_ref, tile, n_ctx_tiles)[:, 2 * D_MODEL:]
    x = _stream_tile(c_ref, l_ref, tile, n_ctx_tiles)
    x = x + gate * jnp.dot(m_ref[0], w_ref[...], preferred_element_type=F32)
    o_ref[0] = _rms(x, fg_ref[...]) if final else x


def _out_proj(merged, w_out, stream, mod, final_g, n_ctx, final):
    B, R, _ = merged.shape
    n_ctx_tiles = n_ctx // ROW_TILE
    off = n_ctx_tiles if final else 0
    return pl.pallas_call(
        functools.partial(_out_kernel, n_ctx_tiles=n_ctx_tiles, tile_offset=off, final=final),
        grid=(B, R // ROW_TILE - off),
        in_specs=[pl.BlockSpec((1, ROW_TILE, D_MODEL), lambda b, i: (b, i + off, 0)),
                  pl.BlockSpec((D_MODEL, D_MODEL), lambda b, i: (0, 0))]
        + _stream_specs(stream, n_ctx_tiles, off) + [
            pl.BlockSpec((MOD_ROWS, 3 * D_MODEL), lambda b, i: (0, 0)),
            pl.BlockSpec((1, D_MODEL), lambda b, i: (0, 0))],
        out_specs=pl.BlockSpec((1, ROW_TILE, D_MODEL), lambda b, i: (b, i, 0)),
        out_shape=jax.ShapeDtypeStruct((B, R - off * ROW_TILE, D_MODEL), F32),
        compiler_params=_params("parallel", "parallel"),
        name="out_proj_final" if final else "out_proj",
    )(merged, w_out, stream[0], stream[1], mod, final_g)


def _layer(stream, R, mod, n_ctx, final, final_g, norm_g, w_in_all, layer, na_rpb, pool_w, pool_scale, rw_mu, rw_w0,
           rw_w2, rw_a0, rw_a2, rw_k_k, rw_k_a, rw_r_k, rw_lnx_g, rw_lnx_b, w_branch, w_out):
    B = stream[0].shape[0]
    rows = (R - n_ctx) // GRID_W

    h = _norm_mod(stream, R, norm_g[None], mod, n_ctx).reshape(B * R, D_MODEL)
    lo = N_MAIN + 2 * RWKV_LORA
    w_in = w_in_all[layer]
    p_qkv = _matmul(h, w_in_all, BF16, "in_proj_qkv", n_cols=N_QKV, layer=layer).reshape(B, R, N_QKV)
    p_gp = _matmul(h, w_in_all, F32, "in_proj_gate_pool", n_cols=N_ATTN_POOL - N_QKV, layer=layer, first_col=N_QKV)
    p_gp = p_gp.reshape(B, R, N_ATTN_POOL - N_QKV)
    head_major = jnp.asarray(_head_major(np.eye(W_BRANCH, dtype=np.float32)), BF16)
    p_rw = _matmul(h, w_in_all, F32, "in_proj_rwkv", n_cols=N_MAIN - N_ATTN_POOL, layer=layer,
                   first_col=N_ATTN_POOL, perm=head_major).reshape(B, R, N_MAIN - N_ATTN_POOL)
    p_lora = _matmul(h, w_in[:, N_MAIN:lo].astype(BF16), F32, "in_proj_lora").reshape(B, R, 2 * RWKV_LORA)
    p_merge = _matmul(h, w_in[:, lo:].astype(BF16), F32, "in_proj_merge")

    b_na = _na_attention(p_qkv, p_gp, _na_bias_tables(na_rpb, rows), n_ctx)
    b_pool = _pool(p_gp, pool_w.astype(BF16), pool_scale[None], n_ctx)

    par = jnp.zeros((P_ROWS, W_BRANCH), F32)
    par = par.at[P_MU_R:P_MU_V + 1].set(rw_mu).at[P_W0_F:P_W0_B + 1].set(rw_w0).at[P_A0_F:P_A0_B + 1].set(rw_a0)
    par = _head_major(par.at[P_K_K].set(rw_k_k).at[P_K_A].set(rw_k_a).at[P_R_K].set(rw_r_k.reshape(-1)))
    zeros = jnp.zeros_like(rw_w2)
    w2 = _head_major(jnp.concatenate([rw_w2, zeros], axis=1)).astype(BF16)
    a2 = _head_major(jnp.concatenate([zeros, rw_a2], axis=1)).astype(BF16)
    v, k_f, b_f, a_f, r_f, k_b, b_b, a_b, r_b, ptot, bonus = _rwkv_features(p_rw, p_lora, par, w2, a2, n_ctx)
    flip = jnp.asarray(np.eye(RELAYOUT_ROWS)[::-1], BF16)
    y = _wkv_scan(_chunk_decay_to_scan(ptot, n_ctx),
                  *[_to_scan(zf, zb, flip, n_ctx) for zf, zb in ((k_f, k_b), (b_f, b_b), (a_f, a_b), (r_f, r_b))],
                  _to_scan(v, v, flip, n_ctx, step_major=True))
    y_fwd, y_bwd = _from_scan(y, flip, B, n_ctx)
    lnx_gb = _head_major(jnp.zeros((SUBLANES, W_BRANCH), F32).at[0].set(rw_lnx_g).at[1].set(rw_lnx_b))
    b_rw = _rwkv_readout(y_fwd, y_bwd, bonus, p_rw, lnx_gb)

    def flat(z):
        return z.reshape(B * R, W_BRANCH)

    w_rw_out = _head_major(w_branch[2].T).T
    w_br = jnp.stack([w_branch[0], w_branch[1], w_rw_out]).astype(BF16)
    merged = _merge(flat(b_na), flat(b_pool), flat(b_rw), p_merge, w_br)
    return _out_proj(merged.reshape(B, R, D_MODEL), w_out.astype(BF16), stream, mod, final_g[None], n_ctx, final)


def kernel(x, c, ctx, c_ctx, norm_g, w_mod, b_mod, w_in, na_rpb, pool_w, pool_scale, rw_mu, rw_w0, rw_w2, rw_a0,
           rw_a2, rw_k_k, rw_k_a, rw_r_k, rw_lnx_g, rw_lnx_b, w_branch, w_out, final_g):
    B, T, _ = x.shape
    n_ctx = ctx.shape[1]
    depth = w_in.shape[0]
    assert B <= CTX_MOD_ROW and n_ctx % ROW_TILE == 0 and T % ROW_TILE == 0
    assert ROW_TILE % RELAYOUT_ROWS == 0 and RELAYOUT_ROWS % SCAN_STEPS == 0

    cond = jnp.zeros((MOD_ROWS, D_MODEL), F32).at[:B].set(c).at[CTX_MOD_ROW].set(c_ctx)
    mods = _modulation(cond, w_mod, b_mod[:, None, :])
    stream = (ctx, x, n_ctx // ROW_TILE)
    for layer in range(depth):
        out = _layer(stream, n_ctx + T, mods[layer], n_ctx, layer == depth - 1, final_g, norm_g[layer], w_in, layer,
                     na_rpb[layer], pool_w[layer], pool_scale[layer], rw_mu[layer], rw_w0[layer], rw_w2[layer],
                     rw_a0[layer], rw_a2[layer], rw_k_k[layer], rw_k_a[layer], rw_r_k[layer], rw_lnx_g[layer],
                     rw_lnx_b[layer], w_branch[layer], w_out[layer])
        stream = (out, out, 0)
    return out
```

```python
import functools

import numpy as np
import jax
import jax.numpy as jnp
from jax import lax
from jax.experimental import pallas as pl
from jax.experimental.pallas import tpu as pltpu

F32 = jnp.float32
BF16 = jnp.bfloat16

D_MODEL = 2048
W_BRANCH = D_MODEL // 2
N_BRANCH = 3
N_HEADS = 16
HEAD_DIM = 64
GRID_W = 64
NA_WIN_H = 8
NA_WIN_W = 16
POOL_WINDOWS = (2, 4, 8, 16)
POOL_GROUP_DIM = W_BRANCH // len(POOL_WINDOWS)
POOL_HALO = max(POOL_WINDOWS) // 2
RWKV_LORA = 64
RMS_EPS = 1e-6
LNX_EPS = 64e-5
NEG_INF = -1e30

LANES = 128
SUBLANES = 8
ROW_TILE = 256
NA_Q_ROWS = ROW_TILE // GRID_W
NA_K_ROWS = NA_Q_ROWS + NA_WIN_H
NA_K_TOK = NA_K_ROWS * GRID_W
CTX_MOD_ROW = 4
MOD_ROWS = SUBLANES
SCAN_STEPS = 64
SCAN_J_UNROLL = 64
RELAYOUT_UNROLL = 32
RELAYOUT_ROWS = 128
VMEM_LIMIT = 56 << 20

COL_Q, COL_K, COL_V = range(3)
COL_NA_GATE, COL_POOL_U, COL_POOL_GATE = range(3)
N_QKV = 3 * W_BRANCH
COL_RW_R, COL_RW_K, COL_RW_V, COL_RW_GATE = range(4)
N_ATTN_POOL = 6 * W_BRANCH
N_MAIN = 10 * W_BRANCH

P_MU_R, P_MU_K, P_MU_V, P_W0_F, P_W0_B, P_A0_F, P_A0_B, P_K_K, P_K_A, P_R_K = range(10)
P_ROWS = 16


def _params(*sem):
    return pltpu.CompilerParams(dimension_semantics=sem, vmem_limit_bytes=VMEM_LIMIT)


def _sigmoid(x):
    return 1.0 / (1.0 + jnp.exp(-x))


def _silu(x):
    return x * _sigmoid(x)


def _split3(x):
    hi = x.astype(BF16)
    r1 = x - hi.astype(F32)
    mid = r1.astype(BF16)
    lo = (r1 - mid.astype(F32)).astype(BF16)
    return hi, mid, lo


def _head_major(z):
    lead = z.shape[:-1]
    return z.reshape(lead + (N_HEADS, HEAD_DIM)).swapaxes(-1, -2).reshape(lead + (W_BRANCH,))


def _head_sum(x):
    n_tiles = W_BRANCH // LANES
    part = x[:, :LANES]
    for c in range(1, n_tiles):
        part = part + x[:, c * LANES:(c + 1) * LANES]
    shift = N_HEADS
    while shift < LANES:
        part = part + pltpu.roll(part, shift, 1)
        shift *= 2
    return jnp.concatenate([part] * n_tiles, axis=1)


def _mod_kernel(cond_ref, w_ref, b_ref, o_ref):
    s = _silu(cond_ref[...])
    o_ref[0] = jnp.dot(s.astype(BF16), w_ref[0].astype(BF16), preferred_element_type=F32) + b_ref[0]


def _modulation(cond, w_mod, b_mod):
    n_layers = w_mod.shape[0]
    tn = 3 * D_MODEL // 4
    return pl.pallas_call(
        _mod_kernel,
        grid=(n_layers, 4),
        in_specs=[pl.BlockSpec((MOD_ROWS, D_MODEL), lambda l, j: (0, 0)),
                  pl.BlockSpec((1, D_MODEL, tn), lambda l, j: (l, 0, j)),
                  pl.BlockSpec((1, 1, tn), lambda l, j: (l, 0, j))],
        out_specs=pl.BlockSpec((1, MOD_ROWS, tn), lambda l, j: (l, 0, j)),
        out_shape=jax.ShapeDtypeStruct((n_layers, MOD_ROWS, 3 * D_MODEL), F32),
        compiler_params=_params("arbitrary", "arbitrary"),
        name="adaln_modulation",
    )(cond, w_mod, b_mod)


def _mod_row(mod_ref, tile, n_ctx_tiles):
    row = jnp.where(tile < n_ctx_tiles, CTX_MOD_ROW, pl.program_id(0))
    return mod_ref[pl.ds(row, 1), :]


def _rms(x, g):
    return x * lax.rsqrt(jnp.mean(x * x, axis=-1, keepdims=True) + RMS_EPS) * g


def _stream_specs(stream, n_ctx_tiles, tile_offset=0):
    lat_shift = stream[2]
    return [pl.BlockSpec((1, ROW_TILE, D_MODEL), lambda b, i: (b, jnp.minimum(i + tile_offset, n_ctx_tiles - 1), 0)),
            pl.BlockSpec((1, ROW_TILE, D_MODEL), lambda b, i: (b, jnp.maximum(i + tile_offset - lat_shift, 0), 0))]


def _stream_tile(c_ref, l_ref, tile, n_ctx_tiles):
    return jnp.where(tile < n_ctx_tiles, c_ref[0], l_ref[0])


def _norm_mod_kernel(c_ref, l_ref, g_ref, mod_ref, h_ref, *, n_ctx_tiles):
    tile = pl.program_id(1)
    m = _mod_row(mod_ref, tile, n_ctx_tiles)
    shift = m[:, :D_MODEL]
    scale = m[:, D_MODEL:2 * D_MODEL]
    x = _stream_tile(c_ref, l_ref, tile, n_ctx_tiles)
    h_ref[0] = (_rms(x, g_ref[...]) * (1.0 + scale) + shift).astype(BF16)


def _norm_mod(stream, n_rows, norm_g, mod, n_ctx):
    B = stream[0].shape[0]
    n_ctx_tiles = n_ctx // ROW_TILE
    return pl.pallas_call(
        functools.partial(_norm_mod_kernel, n_ctx_tiles=n_ctx_tiles),
        grid=(B, n_rows // ROW_TILE),
        in_specs=_stream_specs(stream, n_ctx_tiles) + [
            pl.BlockSpec((1, D_MODEL), lambda b, i: (0, 0)),
            pl.BlockSpec((MOD_ROWS, 3 * D_MODEL), lambda b, i: (0, 0))],
        out_specs=pl.BlockSpec((1, ROW_TILE, D_MODEL), lambda b, i: (b, i, 0)),
        out_shape=jax.ShapeDtypeStruct((B, n_rows, D_MODEL), BF16),
        compiler_params=_params("parallel", "parallel"),
        name="norm_modulate",
    )(stream[0], stream[1], norm_g, mod)


def _mm_kernel(a_ref, w_ref, o_ref):
    o_ref[...] = jnp.dot(a_ref[...], w_ref[...], preferred_element_type=F32).astype(o_ref.dtype)


def _row_tile(m):
    for t in (1024, 512, 256):
        if m % t == 0:
            return t
    raise ValueError(f"row count {m} is not a multiple of {ROW_TILE}")


def _mm_cast_kernel(a_ref, w_ref, *rest, permute):
    perm_ref, o_ref, wb_ref = rest if permute else (None,) + rest

    @pl.when(pl.program_id(1) == 0)
    def _():
        wb = w_ref[...].astype(BF16)
        if permute:
            wb = jnp.dot(wb, perm_ref[...], preferred_element_type=F32).astype(BF16)
        wb_ref[...] = wb

    o_ref[...] = jnp.dot(a_ref[...], wb_ref[...], preferred_element_type=F32).astype(o_ref.dtype)


def _matmul(a, w, out_dtype, name, n_cols=None, layer=None, first_col=0, perm=None):
    M, K = a.shape
    N = w.shape[-1] if n_cols is None else n_cols
    tm = _row_tile(M)
    tn = min(N, 1024)
    assert first_col % tn == 0
    j0 = first_col // tn
    cast = w.dtype != BF16
    assert cast or perm is None
    if layer is None:
        w_spec = pl.BlockSpec((K, tn), lambda j, i: (0, j0 + j))
    else:
        w_spec = pl.BlockSpec((None, K, tn), lambda j, i: (layer, 0, j0 + j))
    in_specs = [pl.BlockSpec((tm, K), lambda j, i: (i, 0)), w_spec]
    args = [a, w]
    if perm is not None:
        in_specs.append(pl.BlockSpec((tn, tn), lambda j, i: (0, 0)))
        args.append(perm)
    return pl.pallas_call(
        functools.partial(_mm_cast_kernel, permute=perm is not None) if cast else _mm_kernel,
        grid=(N // tn, M // tm),
        in_specs=in_specs,
        out_specs=pl.BlockSpec((tm, tn), lambda j, i: (i, j)),
        out_shape=jax.ShapeDtypeStruct((M, N), out_dtype),
        scratch_shapes=[pltpu.VMEM((K, tn), BF16)] if cast else [],
        compiler_params=_params("parallel", "arbitrary" if cast else "parallel"),
        name=name,
    )(*args)


def _na_bias_tables(rpb, rows):
    n_blocks = rows // NA_Q_ROWS
    n_off = 2 * NA_WIN_H - 1
    col = np.arange(GRID_W)
    c0 = np.clip(col - NA_WIN_W // 2, 0, GRID_W - NA_WIN_W)
    valid_c = (col[None, :] >= c0[:, None]) & (col[None, :] < c0[:, None] + NA_WIN_W)
    col_off = np.clip(col[None, :] - col[:, None] + NA_WIN_W - 1, 0, 2 * NA_WIN_W - 2)
    pick_c = jnp.asarray(np.eye(2 * NA_WIN_W - 1)[col_off], F32)
    tile = jnp.einsum("hrc,qpc->hrqp", rpb.astype(F32), pick_c, precision=lax.Precision.HIGHEST)
    tile = jnp.where(valid_c, tile, NEG_INF)
    tile = jnp.concatenate([tile, jnp.full((N_HEADS, 1, GRID_W, GRID_W), NEG_INF, F32)], axis=1)
    tile = jnp.concatenate([tile, tile], axis=-1)
    picks = []
    for m in (0, 1, n_blocks - 1):
        q_row = NA_Q_ROWS * m + np.arange(NA_Q_ROWS)
        k_row = int(np.clip(NA_Q_ROWS * m - NA_Q_ROWS, 0, rows - NA_K_ROWS)) + np.arange(NA_K_ROWS)
        r0 = np.clip(q_row - NA_WIN_H // 2, 0, rows - NA_WIN_H)
        valid_r = (k_row[None, :] >= r0[:, None]) & (k_row[None, :] < r0[:, None] + NA_WIN_H)
        row_off = np.clip(k_row[None, :] - q_row[:, None] + NA_WIN_H - 1, 0, n_off - 1)
        picks.append(np.where(valid_r, row_off, n_off))
    picks = np.stack(picks)

    def build(t_ref, o_ref):
        left = lax.broadcasted_iota(jnp.int32, (1, LANES), 1) < GRID_W
        for ty in range(picks.shape[0]):
            for a in range(NA_Q_ROWS):
                for kp in range(NA_K_ROWS // 2):
                    pair = jnp.where(left, t_ref[0, int(picks[ty, a, 2 * kp])], t_ref[0, int(picks[ty, a, 2 * kp + 1])])
                    o_ref[ty, 0, a * GRID_W:(a + 1) * GRID_W, kp * LANES:(kp + 1) * LANES] = pair

    return pl.pallas_call(
        build,
        grid=(N_HEADS,),
        in_specs=[pl.BlockSpec((1, n_off + 1, GRID_W, LANES), lambda h: (h, 0, 0, 0))],
        out_specs=pl.BlockSpec((picks.shape[0], 1, ROW_TILE, NA_K_TOK), lambda h: (0, h, 0, 0)),
        out_shape=jax.ShapeDtypeStruct((picks.shape[0], N_HEADS, ROW_TILE, NA_K_TOK), F32),
        compiler_params=_params("parallel"),
        name="na_bias_tables",
    )(tile)


def _attend(qe, keys, vals, biases):
    dn = (((1,), (1,)), ((), ()))
    scores = []
    for kk, bias in zip(keys, biases):
        s = lax.dot_general(qe, kk, dn, preferred_element_type=F32)
        scores.append(s if bias is None else s + bias)
    m = scores[0].max(axis=-1, keepdims=True)
    for s in scores[1:]:
        m = jnp.maximum(m, s.max(axis=-1, keepdims=True))
    num, den = None, None
    for s, vv in zip(scores, vals):
        p = jnp.exp(s - m)
        l = p.sum(axis=-1, keepdims=True)
        o = jnp.dot(p.astype(BF16), vv, preferred_element_type=F32)
        num = o if num is None else num + o
        den = l if den is None else den + l
    return num / den


def _na_kernel(q_ref, k_ref, v_ref, g_ref, bias_ref, o_ref, *, n_ctx, rows):
    j = pl.program_id(2)
    lane = lax.broadcasted_iota(jnp.int32, (1, LANES), 1)
    in_head = (lane < HEAD_DIM, lane >= HEAD_DIM)
    q = q_ref[0] * (HEAD_DIM ** -0.5)
    kc = k_ref[0, 0:n_ctx, :]
    vc = v_ref[0, 0:n_ctx, :]

    def heads(q):
        return [jnp.where(in_head[e], q, jnp.zeros_like(q)) for e in range(2)]

    def finish(o0, o1):
        o = jnp.where(in_head[0], o0, o1)
        o_ref[0] = (o * _silu(g_ref[0])).astype(o_ref.dtype)

    @pl.when(j == 0)
    def _():
        finish(*[_attend(qe, [kc], [vc], [None]) for qe in heads(q)])

    @pl.when(j > 0)
    def _():
        k_row = jnp.clip(NA_Q_ROWS * (j - 1) - NA_Q_ROWS, 0, rows - NA_K_ROWS)
        start = pl.multiple_of(n_ctx + k_row * GRID_W, GRID_W)
        kw = k_ref[0, pl.ds(start, NA_K_TOK), :]
        vw = v_ref[0, pl.ds(start, NA_K_TOK), :]
        finish(*[_attend(qe, [kw, kc], [vw, vc], [bias_ref[0, e], None])
                 for e, qe in enumerate(heads(q))])


def _na_attention(p_qkv, p_gp, bias_tables, n_ctx):
    B, R, _ = p_qkv.shape
    rows = (R - n_ctx) // GRID_W
    n_blocks = rows // NA_Q_ROWS
    pairs = W_BRANCH // LANES
    assert n_ctx == ROW_TILE and rows >= NA_K_ROWS and rows % NA_Q_ROWS == 0

    def col(c):
        return lambda b, hp, j: (b, 0, c * pairs + hp)

    def bias_idx(b, hp, j):
        return (jnp.where(j <= 1, 0, jnp.where(j == n_blocks, 2, 1)), hp, 0, 0)

    return pl.pallas_call(
        functools.partial(_na_kernel, n_ctx=n_ctx, rows=rows),
        grid=(B, pairs, n_blocks + 1),
        in_specs=[pl.BlockSpec((1, ROW_TILE, LANES), lambda b, hp, j: (b, j, COL_Q * pairs + hp)),
                  pl.BlockSpec((1, R, LANES), col(COL_K)),
                  pl.BlockSpec((1, R, LANES), col(COL_V)),
                  pl.BlockSpec((1, ROW_TILE, LANES), lambda b, hp, j: (b, j, COL_NA_GATE * pairs + hp)),
                  pl.BlockSpec((1, 2, ROW_TILE, NA_K_TOK), bias_idx)],
        out_specs=pl.BlockSpec((1, ROW_TILE, LANES), lambda b, hp, j: (b, j, hp)),
        out_shape=jax.ShapeDtypeStruct((B, R, W_BRANCH), BF16),
        compiler_params=_params("parallel", "parallel", "arbitrary"),
        name="neighbourhood_attention",
    )(p_qkv, p_qkv, p_qkv, p_gp, bias_tables)


def _pool_kernel(u_ref, g_ref, w_ref, sc_ref, o_ref, pad_ref, *, n_ctx, n_lat):
    grp = pl.program_id(1)
    w = w_ref[0]
    scale = sc_ref[...]

    def run(win):
        half = win // 2
        for seq_start, seq_len in ((0, n_ctx), (n_ctx, n_lat)):
            zeros = jnp.zeros((POOL_HALO, POOL_GROUP_DIM), F32)
            pad_ref[0:POOL_HALO, :] = zeros
            pad_ref[POOL_HALO:POOL_HALO + seq_len, :] = u_ref[0, seq_start:seq_start + seq_len, :]
            pad_ref[POOL_HALO + seq_len:2 * POOL_HALO + seq_len, :] = zeros

            def chunk(c, carry):
                base = pl.multiple_of(c * ROW_TILE, ROW_TILE)
                x = pad_ref[pl.ds(base, ROW_TILE + 2 * POOL_HALO), :]
                acc = x[POOL_HALO - half:POOL_HALO - half + ROW_TILE]
                for o in range(-half + 1, half):
                    acc = acc + x[POOL_HALO + o:POOL_HALO + o + ROW_TILE]
                t = base + lax.broadcasted_iota(jnp.int32, (ROW_TILE, 1), 0)
                cnt = jnp.minimum(t + half, seq_len) - jnp.maximum(t - half, 0)
                diff = acc / cnt.astype(F32) - x[POOL_HALO:POOL_HALO + ROW_TILE]
                y = jnp.dot(diff.astype(BF16), w, preferred_element_type=F32) * scale
                rows = pl.ds(seq_start + base, ROW_TILE)
                o_ref[0, rows, :] = (y * _silu(g_ref[0, rows, :])).astype(o_ref.dtype)
                return carry

            lax.fori_loop(0, seq_len // ROW_TILE, chunk, 0)

    for gi, win in enumerate(POOL_WINDOWS):
        pl.when(grp == gi)(functools.partial(run, win))


def _pool(p_gp, pool_w, pool_scale, n_ctx):
    B, R, _ = p_gp.shape
    groups = len(POOL_WINDOWS)
    return pl.pallas_call(
        functools.partial(_pool_kernel, n_ctx=n_ctx, n_lat=R - n_ctx),
        grid=(B, groups),
        in_specs=[pl.BlockSpec((1, R, POOL_GROUP_DIM), lambda b, g: (b, 0, COL_POOL_U * groups + g)),
                  pl.BlockSpec((1, R, POOL_GROUP_DIM), lambda b, g: (b, 0, COL_POOL_GATE * groups + g)),
                  pl.BlockSpec((1, POOL_GROUP_DIM, POOL_GROUP_DIM), lambda b, g: (g, 0, 0)),
                  pl.BlockSpec((1, POOL_GROUP_DIM), lambda b, g: (0, g))],
        out_specs=pl.BlockSpec((1, R, POOL_GROUP_DIM), lambda b, g: (b, 0, g)),
        out_shape=jax.ShapeDtypeStruct((B, R, W_BRANCH), BF16),
        scratch_shapes=[pltpu.VMEM((R - n_ctx + 2 * POOL_HALO, POOL_GROUP_DIM), F32)],
        compiler_params=_params("parallel", "arbitrary"),
        name="multiscale_pool",
    )(p_gp, p_gp, pool_w, pool_scale)


def _rwkv_feat_kernel(r_ref, rp_ref, rn_ref, k_ref, kp_ref, kn_ref, v_ref, vp_ref, vn_ref, lora_ref,
                      par_ref, w2_ref, a2_ref, tri_ref,
                      vo_ref, kf_ref, bf_ref, af_ref, rf_ref, kb_ref, bb_ref, ab_ref, rb_ref, ptot_ref, bonus_ref,
                      *, n_ctx_tiles, n_tiles):
    i = pl.program_id(1)
    first = (i == 0) | (i == n_ctx_tiles)
    last = (i == n_ctx_tiles - 1) | (i == n_tiles - 1)
    row = lax.broadcasted_iota(jnp.int32, (ROW_TILE, 1), 0)
    par = par_ref[...]

    def prm(p):
        return par[p:p + 1, :]

    def mix(z_ref, prev_ref, next_ref, mu):
        z = z_ref[0]
        prev = jnp.where(first, 0.0, prev_ref[0, SUBLANES - 1:SUBLANES, :])
        nxt = jnp.where(last, 0.0, next_ref[0, 0:1, :])
        z_prev = jnp.where(row == 0, prev, pltpu.roll(z, 1, 0))
        z_next = jnp.where(row == ROW_TILE - 1, nxt, pltpu.roll(z, ROW_TILE - 1, 0))
        return z + mu * (0.5 * (z_prev + z_next) - z)

    r = mix(r_ref, rp_ref, rn_ref, prm(P_MU_R))
    k = mix(k_ref, kp_ref, kn_ref, prm(P_MU_K))
    v = mix(v_ref, vp_ref, vn_ref, prm(P_MU_V))
    vo_ref[0] = v.T

    kk = k * prm(P_K_K)
    kk = kk * jnp.minimum(lax.rsqrt(_head_sum(kk * kk)), 1e12)

    lora = lora_ref[0]
    lane = lax.broadcasted_iota(jnp.int32, (1, LANES), 1)
    lora = jnp.where(lane < RWKV_LORA, jnp.tanh(lora), lora).astype(BF16)
    k_sum = None
    outs = ((kf_ref, bf_ref, af_ref, rf_ref), (kb_ref, bb_ref, ab_ref, rb_ref))
    chunk_decay = []
    for d, (k_out, b_out, a_out, r_out) in enumerate(outs):
        x = prm(P_W0_F + d) + jnp.dot(lora, w2_ref[d], preferred_element_type=F32)
        w_log = -(jnp.maximum(-x, 0.0) + jnp.log(1.0 + jnp.exp(-jnp.abs(x)))) - 0.5
        neg_log_w = jnp.exp(w_log)
        hi, mid, lo = _split3(neg_log_w)
        tri = tri_ref[d]
        cs = (jnp.dot(tri, hi, preferred_element_type=F32) + jnp.dot(tri, mid, preferred_element_type=F32)
              + jnp.dot(tri, lo, preferred_element_type=F32))
        grow = jnp.exp(cs)
        shrink = jnp.exp(-cs)
        a = 0.5 + 0.5 * jnp.tanh(0.5 * (prm(P_A0_F + d) + jnp.dot(lora, a2_ref[d], preferred_element_type=F32)))
        k_d = k * (1.0 + (a - 1.0) * prm(P_K_A))
        for q in range(ROW_TILE // SCAN_STEPS):
            last = q * SCAN_STEPS + (SCAN_STEPS - 1 if d == 0 else 0)
            chunk_decay.append(shrink[last:last + 1])
        k_out[0] = (k_d * grow).T
        b_out[0] = (kk * a * grow).T
        a_out[0] = (-kk * jnp.exp(neg_log_w - cs)).T
        r_out[0] = (r * shrink).T
        k_sum = k_d if k_sum is None else k_sum + k_d
    ptot_ref[0, 0] = jnp.concatenate(chunk_decay, axis=0)
    bonus_ref[0] = _head_sum(r * k_sum * prm(P_R_K)) * v


def _rwkv_features(p_rw, p_lora, par, w2, a2, n_ctx):
    B, R, _ = p_rw.shape
    n_tiles = R // ROW_TILE
    sub = ROW_TILE // SUBLANES
    chunks_per_tile = ROW_TILE // SCAN_STEPS
    assert 2 * chunks_per_tile == SUBLANES

    def main(c):
        return pl.BlockSpec((1, ROW_TILE, W_BRANCH), lambda b, i: (b, i, c))

    def prev(c):
        return pl.BlockSpec((1, SUBLANES, W_BRANCH), lambda b, i: (b, jnp.maximum(i * sub - 1, 0), c))

    def nxt(c):
        return pl.BlockSpec((1, SUBLANES, W_BRANCH),
                            lambda b, i: (b, jnp.minimum((i + 1) * sub, n_tiles * sub - 1), c))

    in_specs = []
    for c in (COL_RW_R, COL_RW_K, COL_RW_V):
        in_specs += [main(c), prev(c), nxt(c)]
    in_specs += [pl.BlockSpec((1, ROW_TILE, LANES), lambda b, i: (b, i, 0)),
                 pl.BlockSpec((P_ROWS, W_BRANCH), lambda b, i: (0, 0)),
                 pl.BlockSpec((2, LANES, W_BRANCH), lambda b, i: (0, 0, 0)),
                 pl.BlockSpec((2, LANES, W_BRANCH), lambda b, i: (0, 0, 0)),
                 pl.BlockSpec((2, ROW_TILE, ROW_TILE), lambda b, i: (0, 0, 0))]
    t_idx = np.arange(ROW_TILE)
    same_chunk = t_idx[:, None] // SCAN_STEPS == t_idx[None, :] // SCAN_STEPS
    tri = jnp.asarray(np.stack([same_chunk & (t_idx[None, :] <= t_idx[:, None]),
                                same_chunk & (t_idx[None, :] >= t_idx[:, None])]), BF16)
    out = jax.ShapeDtypeStruct((B, R, W_BRANCH), F32)
    return pl.pallas_call(
        functools.partial(_rwkv_feat_kernel, n_ctx_tiles=n_ctx // ROW_TILE, n_tiles=n_tiles),
        grid=(B, n_tiles),
        in_specs=in_specs,
        out_specs=[pl.BlockSpec((1, W_BRANCH, ROW_TILE), lambda b, i: (b, 0, i))] * 9
        + [pl.BlockSpec((1, 1, 2 * chunks_per_tile, W_BRANCH), lambda b, i: (b, i, 0, 0)),
           pl.BlockSpec((1, ROW_TILE, W_BRANCH), lambda b, i: (b, i, 0))],
        out_shape=[jax.ShapeDtypeStruct((B, W_BRANCH, R), F32)] * 9
        + [jax.ShapeDtypeStruct((B, n_tiles, 2 * chunks_per_tile, W_BRANCH), F32), out],
        compiler_params=_params("parallel", "parallel"),
        name="rwkv_features",
    )(*([p_rw] * 9), p_lora, par, w2, a2, tri)


def _scan_kernel(p_ref, k_ref, b_ref, a_ref, r_ref, v_ref, y_ref, s_ref):
    n = HEAD_DIM

    @pl.when(pl.program_id(0) == 0)
    def _():
        s_ref[...] = jnp.zeros_like(s_ref)

    def row(ref, j, t):
        return ref[0, pl.ds(j * SCAN_STEPS + t, 1), :]

    zero = jnp.zeros((n, s_ref.shape[2]), F32)

    def first_sa(jb, sa):
        for jj in range(SCAN_J_UNROLL):
            j = jb * SCAN_J_UNROLL + jj
            sa = sa + s_ref[j] * row(a_ref, j, 0)
        return sa

    def step(t, sa):
        tile_rows = pl.ds(pl.multiple_of(t * n, n), n)
        vt = v_ref[0, tile_rows, :]
        t_next = jnp.minimum(t + 1, SCAN_STEPS - 1)

        def columns(jb, carry):
            y, sa_next = carry
            for jj in range(SCAN_J_UNROLL):
                j = jb * SCAN_J_UNROLL + jj
                sj = s_ref[j] + sa * row(b_ref, j, t) + vt * row(k_ref, j, t)
                s_ref[j] = sj
                y = y + sj * row(r_ref, j, t)
                sa_next = sa_next + sj * row(a_ref, j, t_next)
            return y, sa_next

        y, sa_next = lax.fori_loop(0, n // SCAN_J_UNROLL, columns, (zero, zero))
        y_ref[0, tile_rows, :] = y
        return sa_next

    def rescale(jb, carry):
        for jj in range(SCAN_J_UNROLL):
            j = jb * SCAN_J_UNROLL + jj
            s_ref[j] = s_ref[j] * p_ref[0, pl.ds(j, 1), :]
        return carry

    sa0 = lax.fori_loop(0, n // SCAN_J_UNROLL, first_sa, zero)
    lax.fori_loop(0, SCAN_STEPS, step, sa0)
    lax.fori_loop(0, n // SCAN_J_UNROLL, rescale, 0)


def _chunk_decay_to_scan(ptot, n_ctx):
    B, n_tiles = ptot.shape[:2]
    pt = ptot.reshape(B, n_tiles, 2, ROW_TILE // SCAN_STEPS, HEAD_DIM, N_HEADS)
    n_ctx_chunks = n_ctx // SCAN_STEPS

    def chains(z):
        return z.reshape(B, -1, HEAD_DIM, N_HEADS).transpose(1, 2, 0, 3).reshape(-1, HEAD_DIM, B * N_HEADS)

    fwd, bwd = chains(pt[:, :, 0]), chains(pt[:, :, 1])
    bwd = jnp.concatenate([bwd[:n_ctx_chunks][::-1], bwd[n_ctx_chunks:][::-1]], axis=0)
    return jnp.concatenate([fwd, bwd], axis=-1)


def _wkv_scan(p, k, b, a, r, v):
    n_chunks, rows, chains = k.shape
    spec = pl.BlockSpec((1, rows, chains), lambda s: (s, 0, 0))
    return pl.pallas_call(
        _scan_kernel,
        grid=(n_chunks,),
        in_specs=[pl.BlockSpec((1, HEAD_DIM, chains), lambda s: (s, 0, 0))] + [spec] * 5,
        out_specs=spec,
        out_shape=jax.ShapeDtypeStruct(k.shape, F32),
        scratch_shapes=[pltpu.VMEM((HEAD_DIM, HEAD_DIM, chains), F32)],
        compiler_params=_params("arbitrary"),
        name="wkv_scan",
    )(p, k, b, a, r, v)


def _mirror_chunk(c, n_ctx_chunks, n_chunks):
    return jnp.where(c < n_ctx_chunks, n_ctx_chunks - 1 - c, n_ctx_chunks + n_chunks - 1 - c)


def _reverse_backward(rows, flip, n_fwd):
    hi, mid, lo = _split3(rows[n_fwd:])
    back = (jnp.dot(hi, flip, preferred_element_type=F32) + jnp.dot(mid, flip, preferred_element_type=F32)
            + jnp.dot(lo, flip, preferred_element_type=F32))
    return jnp.concatenate([rows[:n_fwd], back], axis=0)


def _to_scan_kernel(zf_ref, zb_ref, flip_ref, o_ref, *, step_major):
    nb = zf_ref.shape[0]
    flip = flip_ref[...]
    sub = RELAYOUT_ROWS // SCAN_STEPS

    def body(n, carry):
        rows = pl.ds(pl.multiple_of(n * N_HEADS, N_HEADS), N_HEADS)
        slabs = [zf_ref[b, rows, :] for b in range(nb)] + [zb_ref[b, rows, :] for b in range(nb)]
        tile = _reverse_backward(jnp.concatenate(slabs, axis=0), flip, nb * N_HEADS).T
        for q in range(sub):
            if step_major:
                dst = pl.ds(n, SCAN_STEPS, stride=HEAD_DIM)
            else:
                dst = pl.ds(pl.multiple_of(n * SCAN_STEPS, SCAN_STEPS), SCAN_STEPS)
            o_ref[q, dst, :] = tile[q * SCAN_STEPS:(q + 1) * SCAN_STEPS]
        return carry

    lax.fori_loop(0, HEAD_DIM, body, 0, unroll=RELAYOUT_UNROLL)


def _to_scan(z_fwd, z_bwd, flip, n_ctx, step_major=False):
    B, _, R = z_fwd.shape
    n_chunks = R // RELAYOUT_ROWS
    n_ctx_chunks = n_ctx // RELAYOUT_ROWS
    sub = RELAYOUT_ROWS // SCAN_STEPS
    chains = 2 * B * N_HEADS
    deep = pl.Buffered(3)
    stream = pltpu.emit_pipeline(
        functools.partial(_to_scan_kernel, step_major=step_major),
        grid=(n_chunks,),
        in_specs=[pl.BlockSpec((B, W_BRANCH, RELAYOUT_ROWS), lambda c: (0, 0, c), pipeline_mode=deep),
                  pl.BlockSpec((B, W_BRANCH, RELAYOUT_ROWS),
                               lambda c: (0, 0, _mirror_chunk(c, n_ctx_chunks, n_chunks)), pipeline_mode=deep),
                  pl.BlockSpec((RELAYOUT_ROWS, RELAYOUT_ROWS), lambda c: (0, 0))],
        out_specs=[pl.BlockSpec((sub, HEAD_DIM * SCAN_STEPS, chains), lambda c: (c, 0, 0))],
    )

    def whole(zf_hbm, zb_hbm, flip_hbm, o_hbm):
        stream(zf_hbm, zb_hbm, flip_hbm, o_hbm)

    any_spec = pl.BlockSpec(memory_space=pl.ANY)
    return pl.pallas_call(
        whole,
        in_specs=[any_spec, any_spec, any_spec],
        out_specs=any_spec,
        out_shape=jax.ShapeDtypeStruct((R // SCAN_STEPS, HEAD_DIM * SCAN_STEPS, chains), F32),
        compiler_params=pltpu.CompilerParams(vmem_limit_bytes=VMEM_LIMIT),
        name="to_scan_layout",
    )(z_fwd, z_bwd, flip)


def _from_scan_kernel(y_ref, flip_ref, yf_ref, yb_ref):
    nb = yf_ref.shape[0]
    flip = flip_ref[...]
    sub = RELAYOUT_ROWS // SCAN_STEPS

    def body(n, carry):
        tile = jnp.concatenate([y_ref[q, pl.ds(n, SCAN_STEPS, stride=HEAD_DIM), :] for q in range(sub)], axis=0)
        tile = _reverse_backward(tile.T, flip, nb * N_HEADS)
        rows = pl.ds(pl.multiple_of(n * N_HEADS, N_HEADS), N_HEADS)
        for b in range(nb):
            yf_ref[b, rows, :] = tile[b * N_HEADS:(b + 1) * N_HEADS]
            yb_ref[b, rows, :] = tile[(nb + b) * N_HEADS:(nb + b + 1) * N_HEADS]
        return carry

    lax.fori_loop(0, HEAD_DIM, body, 0, unroll=RELAYOUT_UNROLL)


def _from_scan(y, flip, n_batch, n_ctx):
    R = y.shape[0] * SCAN_STEPS
    n_chunks = R // RELAYOUT_ROWS
    n_ctx_chunks = n_ctx // RELAYOUT_ROWS
    sub = RELAYOUT_ROWS // SCAN_STEPS
    out = jax.ShapeDtypeStruct((n_batch, W_BRANCH, R), F32)
    return pl.pallas_call(
        _from_scan_kernel,
        grid=(n_chunks,),
        in_specs=[pl.BlockSpec((sub, HEAD_DIM * SCAN_STEPS, y.shape[2]), lambda c: (c, 0, 0)),
                  pl.BlockSpec((RELAYOUT_ROWS, RELAYOUT_ROWS), lambda c: (0, 0))],
        out_specs=[pl.BlockSpec((n_batch, W_BRANCH, RELAYOUT_ROWS), lambda c: (0, 0, c)),
                   pl.BlockSpec((n_batch, W_BRANCH, RELAYOUT_ROWS),
                                lambda c: (0, 0, _mirror_chunk(c, n_ctx_chunks, n_chunks)))],
        out_shape=[out, out],
        compiler_params=_params("parallel"),
        name="from_scan_layout",
    )(y, flip)


def _rwkv_readout_kernel(yf_ref, yb_ref, bonus_ref, gate_ref, gb_ref, o_ref):
    y = (yf_ref[0] + yb_ref[0]).T
    mu = _head_sum(y) * (1.0 / HEAD_DIM)
    yc = y - mu
    var = _head_sum(yc * yc) * (1.0 / HEAD_DIM)
    gb = gb_ref[...]
    out = yc * lax.rsqrt(var + LNX_EPS) * gb[0:1, :] + gb[1:2, :] + bonus_ref[0]
    o_ref[0] = (out * _silu(gate_ref[0])).astype(o_ref.dtype)


def _rwkv_readout(y_fwd, y_bwd, bonus, p_rw, lnx_gb):
    B, R, _ = bonus.shape
    tile = pl.BlockSpec((1, ROW_TILE, W_BRANCH), lambda b, i: (b, i, 0))
    tile_t = pl.BlockSpec((1, W_BRANCH, ROW_TILE), lambda b, i: (b, 0, i))
    return pl.pallas_call(
        _rwkv_readout_kernel,
        grid=(B, R // ROW_TILE),
        in_specs=[tile_t, tile_t, tile,
                  pl.BlockSpec((1, ROW_TILE, W_BRANCH), lambda b, i: (b, i, COL_RW_GATE)),
                  pl.BlockSpec((SUBLANES, W_BRANCH), lambda b, i: (0, 0))],
        out_specs=tile,
        out_shape=jax.ShapeDtypeStruct((B, R, W_BRANCH), BF16),
        compiler_params=_params("parallel", "parallel"),
        name="rwkv_readout",
    )(y_fwd, y_bwd, bonus, p_rw, lnx_gb)


def _merge_kernel(na_ref, pool_ref, rw_ref, lna_ref, lpool_ref, lrw_ref, w_ref, o_ref):
    acc = None
    for br, (x_ref, l_ref) in enumerate(((na_ref, lna_ref), (pool_ref, lpool_ref), (rw_ref, lrw_ref))):
        t = _sigmoid(l_ref[...]) * jnp.dot(x_ref[...], w_ref[br], preferred_element_type=F32)
        acc = t if acc is None else acc + t
    o_ref[...] = acc.astype(o_ref.dtype)


def _merge(b_na, b_pool, b_rw, p_merge, w_branch):
    M = b_na.shape[0]
    tm, tn = min(_row_tile(M), 512), 1024
    nb = D_MODEL // tn
    x_spec = pl.BlockSpec((tm, W_BRANCH), lambda j, i: (i, 0))

    def logit(br):
        return pl.BlockSpec((tm, tn), lambda j, i: (i, br * nb + j))

    return pl.pallas_call(
        _merge_kernel,
        grid=(nb, M // tm),
        in_specs=[x_spec, x_spec, x_spec, logit(0), logit(1), logit(2),
                  pl.BlockSpec((N_BRANCH, W_BRANCH, tn), lambda j, i: (0, 0, j))],
        out_specs=pl.BlockSpec((tm, tn), lambda j, i: (i, j)),
        out_shape=jax.ShapeDtypeStruct((M, D_MODEL), BF16),
        compiler_params=_params("parallel", "parallel"),
        name="branch_merge",
    )(b_na, b_pool, b_rw, p_merge, p_merge, p_merge, w_branch)


def _out_kernel(m_ref, w_ref, c_ref, l_ref, mod_ref, fg_ref, o_ref, *, n_ctx_tiles, tile_offset, final):
    tile = pl.program_id(1) + tile_offset
    gate = _mod_row(mod_ref, tile, n_ctx_tiles)[:, 2 * D_MODEL:]
    x = _stream_tile(c_ref, l_ref, tile, n_ctx_tiles)
    x = x + gate * jnp.dot(m_ref[0], w_ref[...], preferred_element_type=F32)
    o_ref[0] = _rms(x, fg_ref[...]) if final else x


def _out_proj(merged, w_out, stream, mod, final_g, n_ctx, final):
    B, R, _ = merged.shape
    n_ctx_tiles = n_ctx // ROW_TILE
    off = n_ctx_tiles if final else 0
    return pl.pallas_call(
        functools.partial(_out_kernel, n_ctx_tiles=n_ctx_tiles, tile_offset=off, final=final),
        grid=(B, R // ROW_TILE - off),
        in_specs=[pl.BlockSpec((1, ROW_TILE, D_MODEL), lambda b, i: (b, i + off, 0)),
                  pl.BlockSpec((D_MODEL, D_MODEL), lambda b, i: (0, 0))]
        + _stream_specs(stream, n_ctx_tiles, off) + [
            pl.BlockSpec((MOD_ROWS, 3 * D_MODEL), lambda b, i: (0, 0)),
            pl.BlockSpec((1, D_MODEL), lambda b, i: (0, 0))],
        out_specs=pl.BlockSpec((1, ROW_TILE, D_MODEL), lambda b, i: (b, i, 0)),
        out_shape=jax.ShapeDtypeStruct((B, R - off * ROW_TILE, D_MODEL), F32),
        compiler_params=_params("parallel", "parallel"),
        name="out_proj_final" if final else "out_proj",
    )(merged, w_out, stream[0], stream[1], mod, final_g)


def _layer(stream, R, mod, n_ctx, final, final_g, norm_g, w_in_all, layer, na_rpb, pool_w, pool_scale, rw_mu, rw_w0,
           rw_w2, rw_a0, rw_a2, rw_k_k, rw_k_a, rw_r_k, rw_lnx_g, rw_lnx_b, w_branch, w_out):
    B = stream[0].shape[0]
    rows = (R - n_ctx) // GRID_W

    h = _norm_mod(stream, R, norm_g[None], mod, n_ctx).reshape(B * R, D_MODEL)
    lo = N_MAIN + 2 * RWKV_LORA
    w_in = w_in_all[layer]
    p_qkv = _matmul(h, w_in_all, BF16, "in_proj_qkv", n_cols=N_QKV, layer=layer).reshape(B, R, N_QKV)
    p_gp = _matmul(h, w_in_all, F32, "in_proj_gate_pool", n_cols=N_ATTN_POOL - N_QKV, layer=layer, first_col=N_QKV)
    p_gp = p_gp.reshape(B, R, N_ATTN_POOL - N_QKV)
    head_major = jnp.asarray(_head_major(np.eye(W_BRANCH, dtype=np.float32)), BF16)
    p_rw = _matmul(h, w_in_all, F32, "in_proj_rwkv", n_cols=N_MAIN - N_ATTN_POOL, layer=layer,
                   first_col=N_ATTN_POOL, perm=head_major).reshape(B, R, N_MAIN - N_ATTN_POOL)
    p_lora = _matmul(h, w_in[:, N_MAIN:lo].astype(BF16), F32, "in_proj_lora").reshape(B, R, 2 * RWKV_LORA)
    p_merge = _matmul(h, w_in[:, lo:].astype(BF16), F32, "in_proj_merge")

    b_na = _na_attention(p_qkv, p_gp, _na_bias_tables(na_rpb, rows), n_ctx)
    b_pool = _pool(p_gp, pool_w.astype(BF16), pool_scale[None], n_ctx)

    par = jnp.zeros((P_ROWS, W_BRANCH), F32)
    par = par.at[P_MU_R:P_MU_V + 1].set(rw_mu).at[P_W0_F:P_W0_B + 1].set(rw_w0).at[P_A0_F:P_A0_B + 1].set(rw_a0)
    par = _head_major(par.at[P_K_K].set(rw_k_k).at[P_K_A].set(rw_k_a).at[P_R_K].set(rw_r_k.reshape(-1)))
    zeros = jnp.zeros_like(rw_w2)
    w2 = _head_major(jnp.concatenate([rw_w2, zeros], axis=1)).astype(BF16)
    a2 = _head_major(jnp.concatenate([zeros, rw_a2], axis=1)).astype(BF16)
    v, k_f, b_f, a_f, r_f, k_b, b_b, a_b, r_b, ptot, bonus = _rwkv_features(p_rw, p_lora, par, w2, a2, n_ctx)
    flip = jnp.asarray(np.eye(RELAYOUT_ROWS)[::-1], BF16)
    y = _wkv_scan(_chunk_decay_to_scan(ptot, n_ctx),
                  *[_to_scan(zf, zb, flip, n_ctx) for zf, zb in ((k_f, k_b), (b_f, b_b), (a_f, a_b), (r_f, r_b))],
                  _to_scan(v, v, flip, n_ctx, step_major=True))
    y_fwd, y_bwd = _from_scan(y, flip, B, n_ctx)
    lnx_gb = _head_major(jnp.zeros((SUBLANES, W_BRANCH), F32).at[0].set(rw_lnx_g).at[1].set(rw_lnx_b))
    b_rw = _rwkv_readout(y_fwd, y_bwd, bonus, p_rw, lnx_gb)

    def flat(z):
        return z.reshape(B * R, W_BRANCH)

    w_rw_out = _head_major(w_branch[2].T).T
    w_br = jnp.stack([w_branch[0], w_branch[1], w_rw_out]).astype(BF16)
    merged = _merge(flat(b_na), flat(b_pool), flat(b_rw), p_merge, w_br)
    return _out_proj(merged.reshape(B, R, D_MODEL), w_out.astype(BF16), stream, mod, final_g[None], n_ctx, final)


def kernel(x, c, ctx, c_ctx, norm_g, w_mod, b_mod, w_in, na_rpb, pool_w, pool_scale, rw_mu, rw_w0, rw_w2, rw_a0,
           rw_a2, rw_k_k, rw_k_a, rw_r_k, rw_lnx_g, rw_lnx_b, w_branch, w_out, final_g):
    B, T, _ = x.shape
    n_ctx = ctx.shape[1]
    depth = w_in.shape[0]
    assert B <= CTX_MOD_ROW and n_ctx % ROW_TILE == 0 and T % ROW_TILE == 0
    assert ROW_TILE % RELAYOUT_ROWS == 0 and RELAYOUT_ROWS % SCAN_STEPS == 0

    cond = jnp.zeros((MOD_ROWS, D_MODEL), F32).at[:B].set(c).at[CTX_MOD_ROW].set(c_ctx)
    mods = _modulation(cond, w_mod, b_mod[:, None, :])
    stream = (ctx, x, n_ctx // ROW_TILE)
    for layer in range(depth):
        out = _layer(stream, n_ctx + T, mods[layer], n_ctx, layer == depth - 1, final_g, norm_g[layer], w_in, layer,
                     na_rpb[layer], pool_w[layer], pool_scale[layer], rw_mu[layer], rw_w0[layer], rw_w2[layer],
                     rw_a0[layer], rw_a2[layer], rw_k_k[layer], rw_k_a[layer], rw_r_k[layer], rw_lnx_g[layer],
                     rw_lnx_b[layer], w_branch[layer], w_out[layer])
        stream = (out, out, 0)
    return out
```
